```python
import math
import jax, jax.numpy as jnp
from jax import lax
import numpy as np

D_MODEL = 1024
BATCH = 2
SEQ = 16384
DEPTH = 4

GRID_W = 64
CTX_LEN = 256
Q_BLOCK = 128
EPS = 1e-6
ROPE_BASE = 10000.0

DIFF_HEADS = 4
DIFF_DIM = 64
DIFF_VDIM = 2 * DIFF_DIM

MLSTM_HEADS = 4
MLSTM_DIM = 128
MLSTM_CHUNK = 64
MLSTM_CONV = 3

GQA_HEADS = 8
GQA_KV_HEADS = 2
GQA_DIM = 64

HYENA_CH = 512
HYENA_ORDER = 2
HYENA_CONV = 3
FILTER_BANDS = 16
FILTER_EMB = 2 * FILTER_BANDS + 1
FILTER_HIDDEN = 64
FILTER_SHIFT = 0.05
DECAY_TARGET = 1e-2
FAST_DECAY_PCT = 0.3
SLOW_DECAY_PCT = 1.5

N_BRANCH = 4
BRANCH_W = 512

MOE_GROUPS = 4
MOE_EXPERTS_PER_GROUP = 8
MOE_EXPERTS = MOE_GROUPS * MOE_EXPERTS_PER_GROUP
MOE_TOP_K = 2
EXPERT_HIDDEN = 512
MOE_BLOCK = 128

kernel_name = 'hybrid_diff_mlstm_gqa_hyena_hmoe_dit'


def in_sizes():
    return (DIFF_HEADS * 2 * DIFF_DIM,
            DIFF_HEADS * 2 * DIFF_DIM,
            DIFF_HEADS * DIFF_VDIM,
            3 * MLSTM_HEADS * MLSTM_DIM,
            MLSTM_HEADS * MLSTM_DIM,
            4 * MLSTM_HEADS,
            GQA_HEADS * GQA_DIM,
            2 * GQA_KV_HEADS * GQA_DIM,
            (HYENA_ORDER + 1) * HYENA_CH,
            N_BRANCH * D_MODEL)


def split_cols(p):
    idx = np.cumsum(in_sizes())[:-1].tolist()
    return jnp.split(p, idx, axis=-1)


def rmsnorm(x, g):
    xf = x.astype(jnp.float32)
    y = xf * lax.rsqrt(jnp.mean(xf * xf, axis=-1, keepdims=True) + EPS)
    return y.astype(x.dtype) * g


def modulate(h, shift, scale):
    return h * (1 + scale) + shift


def axial_rope_tables(n_tok, dim):
    rows = n_tok // GRID_W
    row = jnp.repeat(jnp.arange(rows, dtype=jnp.float32), GRID_W)
    col = jnp.broadcast_to(jnp.arange(GRID_W, dtype=jnp.float32), (rows, GRID_W)).reshape(-1)
    n_freq = dim // 4
    inv = ROPE_BASE ** (-jnp.arange(n_freq, dtype=jnp.float32) / n_freq)
    ar = row[:, None] * inv
    ac = col[:, None] * inv
    ang = jnp.concatenate([ar, ar, ac, ac], axis=-1)
    return jnp.cos(ang), jnp.sin(ang)


def apply_rope(x, cos, sin):
    r1, r2, c1, c2 = jnp.split(x, 4, axis=-1)
    rot = jnp.concatenate([-r2, r1, -c2, c1], axis=-1)
    shape = (x.shape[1],) + (1,) * (x.ndim - 3) + (x.shape[-1],)
    return x * cos.reshape(shape).astype(x.dtype) + rot * sin.reshape(shape).astype(x.dtype)


def sweep_blocks(fn, q):
    b, t = q.shape[:2]
    nb = t // Q_BLOCK
    qb = jnp.swapaxes(q.reshape((b, nb, Q_BLOCK) + q.shape[2:]), 0, 1)
    out = jnp.swapaxes(lax.map(fn, qb), 0, 1)
    return out.reshape((b, t) + out.shape[3:])


def short_conv(x, w, bias):
    k, ch = w.shape
    y = lax.conv_general_dilated(x, w[:, None, :].astype(x.dtype), window_strides=(1,),
                                 padding=((k // 2, k // 2),), dimension_numbers=('NWC', 'WIO', 'NWC'),
                                 feature_group_count=ch)
    return y + bias


def diff_attention(q_c, k_c, v_c, q_l, k_l, v_l, lam_p, norm_g, layer, rope, need_ctx):
    lam_init = 0.8 - 0.6 * math.exp(-0.3 * layer)
    lam_p = lam_p.astype(jnp.float32)
    lam = jnp.exp(jnp.sum(lam_p[0] * lam_p[1])) - jnp.exp(jnp.sum(lam_p[2] * lam_p[3])) + lam_init
    scale = DIFF_DIM ** -0.5

    def heads(q, k, v):
        b, t = q.shape[:2]
        return (q.reshape(b, t, DIFF_HEADS, 2, DIFF_DIM), k.reshape(b, t, DIFF_HEADS, 2, DIFF_DIM),
                v.reshape(b, t, DIFF_HEADS, DIFF_VDIM))

    def attend(q, k, v):
        def block(qb):
            s = jnp.einsum('bqhcd,bkhcd->bhcqk', qb, k).astype(jnp.float32) * scale
            p = jax.nn.softmax(s, axis=-1)
            pd = p[:, :, 0] - lam * p[:, :, 1]
            return jnp.einsum('bhqk,bkhe->bqhe', pd.astype(v.dtype), v)
        o = rmsnorm(sweep_blocks(block, q), norm_g) * (1.0 - lam_init)
        return o.reshape(o.shape[:2] + (DIFF_HEADS * DIFF_VDIM,))

    qc, kc, vc = heads(q_c, k_c, v_c)
    ql, kl, vl = heads(q_l, k_l, v_l)
    cos, sin = rope
    ql = apply_rope(ql, cos, sin)
    kl = apply_rope(kl, cos, sin)
    y_lat = attend(ql, jnp.concatenate([kc, kl], axis=1), jnp.concatenate([vc, vl], axis=1))
    y_ctx = attend(qc, kc, vc) if need_ctx else None
    return y_ctx, y_lat


def gqa_attention(q_c, kv_c, q_l, kv_l, q_g, k_g, rope, need_ctx):
    rep = GQA_HEADS // GQA_KV_HEADS
    scale = GQA_DIM ** -0.5

    def heads(q, kv):
        b, t = q.shape[:2]
        q = rmsnorm(q.reshape(b, t, GQA_KV_HEADS, rep, GQA_DIM), q_g)
        kv = kv.reshape(b, t, 2, GQA_KV_HEADS, GQA_DIM)
        return q, rmsnorm(kv[:, :, 0], k_g), kv[:, :, 1]

    def attend(q, k, v):
        def block(qb):
            s = jnp.einsum('bqgrd,bkgd->bgrqk', qb, k).astype(jnp.float32) * scale
            p = jax.nn.softmax(s, axis=-1)
            return jnp.einsum('bgrqk,bkgd->bqgrd', p.astype(v.dtype), v)
        o = sweep_blocks(block, q)
        return o.reshape(o.shape[:2] + (GQA_HEADS * GQA_DIM,))

    qc, kc, vc = heads(q_c, kv_c)
    ql, kl, vl = heads(q_l, kv_l)
    cos, sin = rope
    ql = apply_rope(ql, cos, sin)
    kl = apply_rope(kl, cos, sin)
    y_lat = attend(ql, jnp.concatenate([kc, kl], axis=1), jnp.concatenate([vc, vl], axis=1))
    y_ctx = attend(qc, kc, vc) if need_ctx else None
    return y_ctx, y_lat


def mlstm_scan(q, k, v, log_i, log_f, state):
    b, t, h, d = q.shape
    nc = t // MLSTM_CHUNK

    def chunks(a):
        a = a.astype(jnp.float32).reshape((b, nc, MLSTM_CHUNK) + a.shape[2:])
        return jnp.swapaxes(jnp.moveaxis(a, 1, 0), 2, 3)

    causal = jnp.tril(jnp.ones((MLSTM_CHUNK, MLSTM_CHUNK), dtype=bool))

    def step(carry, inp):
        c_mat, n_vec, m = carry
        qc, kc, vc, li, lf = inp
        bcum = jnp.cumsum(lf, axis=-1)
        dmat = jnp.where(causal, bcum[..., :, None] - bcum[..., None, :] + li[..., None, :], -jnp.inf)
        inter = bcum + m[..., None]
        m_t = jnp.maximum(inter, jnp.max(dmat, axis=-1))
        w_intra = jnp.exp(dmat - m_t[..., None])
        w_inter = jnp.exp(inter - m_t)
        s = jnp.einsum('bhtd,bhsd->bhts', qc, kc) * w_intra
        num = jnp.einsum('bhts,bhse->bhte', s, vc) + w_inter[..., None] * jnp.einsum('bhtd,bhde->bhte', qc, c_mat)
        den = jnp.sum(s, axis=-1) + w_inter * jnp.einsum('bhtd,bhd->bht', qc, n_vec)
        h_out = num / jnp.maximum(jnp.abs(den), jnp.exp(-m_t))[..., None]
        b_last = bcum[..., -1]
        g = b_last[..., None] - bcum + li
        m_new = jnp.maximum(b_last + m, jnp.max(g, axis=-1))
        kw = kc * jnp.exp(g - m_new[..., None])[..., None]
        wc = jnp.exp(b_last + m - m_new)
        c_new = wc[..., None, None] * c_mat + jnp.einsum('bhsd,bhse->bhde', kw, vc)
        n_new = wc[..., None] * n_vec + jnp.sum(kw, axis=2)
        return (c_new, n_new, m_new), h_out

    state, hs = lax.scan(step, state, (chunks(q), chunks(k), chunks(v), chunks(log_i), chunks(log_f)))
    hs = jnp.moveaxis(jnp.swapaxes(hs, 2, 3), 0, 1).reshape(b, t, h, d)
    return hs.astype(q.dtype), state


def mlstm_branch(qkv_c, o_c, g_c, qkv_l, o_l, g_l, conv_w, conv_b, gate_b, norm_g, need_ctx):
    w = MLSTM_HEADS * MLSTM_DIM

    def prep(qkv, g):
        b, t = qkv.shape[:2]
        qk = jax.nn.silu(short_conv(qkv[..., :2 * w], conv_w, conv_b))
        q = qk[..., :w].reshape(b, t, MLSTM_HEADS, MLSTM_DIM)
        k = qk[..., w:].reshape(b, t, MLSTM_HEADS, MLSTM_DIM) * MLSTM_DIM ** -0.5
        v = qkv[..., 2 * w:].reshape(b, t, MLSTM_HEADS, MLSTM_DIM)
        gates = g.reshape(b, t, 2, 2, MLSTM_HEADS).astype(jnp.float32) + gate_b
        return q, k, v, gates

    def flip(a):
        return a[:, ::-1]

    def run(inp, direction, state):
        q, k, v, gates = inp
        log_i = gates[:, :, direction, 0]
        log_f = jax.nn.log_sigmoid(gates[:, :, direction, 1])
        if direction == 1:
            h, st = mlstm_scan(flip(q), flip(k), flip(v), flip(log_i), flip(log_f), state)
            return flip(h), st
        return mlstm_scan(q, k, v, log_i, log_f, state)

    ctx_in = prep(qkv_c, g_c)
    lat_in = prep(qkv_l, g_l)
    b = qkv_l.shape[0]
    zero = (jnp.zeros((b, MLSTM_HEADS, MLSTM_DIM, MLSTM_DIM), jnp.float32),
            jnp.zeros((b, MLSTM_HEADS, MLSTM_DIM), jnp.float32),
            jnp.zeros((b, MLSTM_HEADS), jnp.float32))
    hc_f, st_f = run(ctx_in, 0, zero)
    hc_b, st_b = run(ctx_in, 1, zero)
    hl = run(lat_in, 0, st_f)[0] + run(lat_in, 1, st_b)[0]

    def out(h, o):
        h = rmsnorm(h, norm_g) * jax.nn.sigmoid(o).reshape(h.shape)
        return h.reshape(h.shape[:2] + (w,))

    y_lat = out(hl, o_l)
    y_ctx = out(hc_f + hc_b, o_c) if need_ctx else None
    return y_ctx, y_lat


def hyena_filters(length, w1, b1, w2, b2, w3, freq):
    t = jnp.arange(length, dtype=jnp.float32) / length
    bands = jnp.arange(1, FILTER_BANDS + 1, dtype=jnp.float32)
    ang = 2.0 * math.pi * t[:, None] * bands
    emb = jnp.concatenate([t[:, None], jnp.cos(ang), jnp.sin(ang)], axis=-1)
    a = jnp.sin(freq[0] * (emb @ w1 + b1))
    a = jnp.sin(freq[1] * (a @ w2 + b2))
    filt = (a @ w3).astype(jnp.float32).reshape(length, HYENA_ORDER, 2, HYENA_CH)
    alpha = jnp.linspace(abs(math.log(DECAY_TARGET)) / SLOW_DECAY_PCT,
                         abs(math.log(DECAY_TARGET)) / FAST_DECAY_PCT, HYENA_CH)
    window = jnp.exp(-t[:, None] * alpha) + FILTER_SHIFT
    filt = filt * window[:, None, None, :]
    k = jnp.concatenate([filt[:, :, 0], jnp.zeros((1, HYENA_ORDER, HYENA_CH), jnp.float32),
                         filt[:0:-1, :, 1]], axis=0)
    return k * lax.rsqrt(jnp.sum(k * k, axis=0, keepdims=True) + EPS)


def fft_long_conv(z, k):
    length = z.shape[1]
    zf = jnp.fft.rfft(z.astype(jnp.float32), n=2 * length, axis=1)
    kf = jnp.fft.rfft(k, axis=0)
    y = jnp.fft.irfft(zf * kf[None], n=2 * length, axis=1)[:, :length]
    return y.astype(z.dtype)


def hyena_operator(p, conv_w, conv_b, w1, b1, w2, b2, w3, freq, bias):
    length = p.shape[1]
    u = short_conv(p, conv_w, conv_b)
    parts = jnp.split(u, HYENA_ORDER + 1, axis=-1)
    k = hyena_filters(length, w1, b1, w2, b2, w3, freq)
    z = parts[0]
    for n in range(HYENA_ORDER):
        z = parts[n + 1] * (fft_long_conv(z, k[:, n]) + bias[n] * z)
    return z


def merge_branches(ys, gate_cols, w_branch, w_out):
    y = jnp.stack(ys, axis=2)
    g = jax.nn.sigmoid(gate_cols.reshape(y.shape[:3] + (D_MODEL,)))
    merged = jnp.sum(g * jnp.einsum('btnc,ncd->btnd', y, w_branch), axis=2)
    return merged @ w_out


def hier_moe(h, w_group, b_group, w_router, b_router, w_gate, w_up, w_down):
    n_tok, d = h.shape
    g_logits = (h @ w_group + b_group).astype(jnp.float32)
    grp = jnp.argmax(g_logits, axis=-1)
    p_grp = jnp.max(jax.nn.softmax(g_logits, axis=-1), axis=-1, keepdims=True)
    e_logits = (h @ w_router + b_router).astype(jnp.float32).reshape(n_tok, MOE_GROUPS, MOE_EXPERTS_PER_GROUP)
    e_in = e_logits[jnp.arange(n_tok), grp]
    top_v, top_i = lax.top_k(e_in, MOE_TOP_K)
    gates = p_grp * jax.nn.softmax(top_v, axis=-1)
    e_flat = (grp[:, None] * MOE_EXPERTS_PER_GROUP + top_i).reshape(-1)
    m_slots = n_tok * MOE_TOP_K
    order = jnp.argsort(e_flat)
    e_s = e_flat[order]
    tok_s = order // MOE_TOP_K
    gate_s = gates.reshape(-1)[order]
    counts = jnp.bincount(e_flat, length=MOE_EXPERTS)
    padded = (counts + MOE_BLOCK - 1) // MOE_BLOCK * MOE_BLOCK
    start = jnp.cumsum(counts) - counts
    p_end = jnp.cumsum(padded)
    dest = (p_end - padded)[e_s] + jnp.arange(m_slots) - start[e_s]
    n_blocks = -(-(m_slots + MOE_EXPERTS * (MOE_BLOCK - 1)) // MOE_BLOCK)
    buf_tok = jnp.zeros((n_blocks * MOE_BLOCK,), jnp.int32).at[dest].set(tok_s.astype(jnp.int32))
    blk_exp = jnp.minimum(jnp.searchsorted(p_end, jnp.arange(n_blocks) * MOE_BLOCK, side='right'), MOE_EXPERTS - 1)
    xb = h[buf_tok].reshape(n_blocks, MOE_BLOCK, d)

    def expert_block(args):
        xblk, e = args
        return (jax.nn.silu(xblk @ w_gate[e]) * (xblk @ w_up[e])) @ w_down[e]

    yb = lax.map(expert_block, (xb, blk_exp)).reshape(-1, d)
    y_s = yb[dest] * gate_s[:, None].astype(h.dtype)
    return jnp.zeros_like(h).at[tok_s].add(y_s)


def setup_inputs(seed: int = 0) -> dict:
    key = jax.random.key(seed)
    keys = iter(jax.random.split(key, 48))

    def nrm(shape, scale):
        return scale * jax.random.normal(next(keys), shape, jnp.float32)

    d, L = D_MODEL, DEPTH
    n_in = sum(in_sizes())
    w_ml = MLSTM_HEADS * MLSTM_DIM
    inp = {}
    inp['x'] = nrm((BATCH, SEQ, d), 1.0)
    inp['c'] = nrm((BATCH, d), 1.0)
    inp['ctx'] = nrm((BATCH, CTX_LEN, d), 1.0)
    inp['c_ctx'] = nrm((d,), 1.0)
    inp['w_ada'] = nrm((L, d, 6 * d), 0.5 * d ** -0.5)
    inp['b_ada'] = nrm((L, 6 * d), 0.02)
    inp['norm1_g'] = 1.0 + nrm((L, d), 0.05)
    inp['norm2_g'] = 1.0 + nrm((L, d), 0.05)
    inp['w_in'] = nrm((L, d, n_in), d ** -0.5)
    inp['diff_lam'] = nrm((L, 4, DIFF_DIM), 0.1)
    inp['diff_norm_g'] = 1.0 + nrm((L, DIFF_VDIM), 0.05)
    inp['ml_conv_w'] = nrm((L, MLSTM_CONV, 2 * w_ml), MLSTM_CONV ** -0.5)
    inp['ml_conv_b'] = nrm((L, 2 * w_ml), 0.02)
    i_bias = nrm((L, 2, MLSTM_HEADS), 0.1)
    f_bias = jnp.linspace(3.0, 6.0, MLSTM_HEADS) + nrm((L, 2, MLSTM_HEADS), 0.1)
    inp['ml_gate_b'] = jnp.stack([i_bias, f_bias], axis=2)
    inp['ml_norm_g'] = 1.0 + nrm((L, MLSTM_DIM), 0.05)
    inp['gqa_qnorm_g'] = 1.0 + nrm((L, GQA_DIM), 0.05)
    inp['gqa_knorm_g'] = 1.0 + nrm((L, GQA_DIM), 0.05)
    inp['hy_conv_w'] = nrm((L, HYENA_CONV, (HYENA_ORDER + 1) * HYENA_CH), HYENA_CONV ** -0.5)
    inp['hy_conv_b'] = nrm((L, (HYENA_ORDER + 1) * HYENA_CH), 0.02)
    inp['hy_f_w1'] = nrm((L, FILTER_EMB, FILTER_HIDDEN), FILTER_EMB ** -0.5)
    inp['hy_f_b1'] = nrm((L, FILTER_HIDDEN), 0.1)
    inp['hy_f_w2'] = nrm((L, FILTER_HIDDEN, FILTER_HIDDEN), FILTER_HIDDEN ** -0.5)
    inp['hy_f_b2'] = nrm((L, FILTER_HIDDEN), 0.1)
    inp['hy_f_w3'] = nrm((L, FILTER_HIDDEN, HYENA_ORDER * 2 * HYENA_CH), FILTER_HIDDEN ** -0.5)
    inp['hy_f_freq'] = 1.0 + nrm((L, 2, FILTER_HIDDEN), 0.1)
    inp['hy_bias'] = nrm((L, HYENA_ORDER, HYENA_CH), 0.5)
    inp['w_branch'] = nrm((L, N_BRANCH, BRANCH_W, d), BRANCH_W ** -0.5)
    inp['w_out'] = nrm((L, d, d), d ** -0.5)
    inp['moe_w_group'] = nrm((L, d, MOE_GROUPS), d ** -0.5)
    inp['moe_b_group'] = nrm((L, MOE_GROUPS), 0.01)
    inp['moe_w_router'] = nrm((L, d, MOE_EXPERTS), d ** -0.5)
    inp['moe_b_router'] = nrm((L, MOE_EXPERTS), 0.01)
    inp['moe_w_gate'] = nrm((L, MOE_EXPERTS, d, EXPERT_HIDDEN), d ** -0.5)
    inp['moe_w_up'] = nrm((L, MOE_EXPERTS, d, EXPERT_HIDDEN), d ** -0.5)
    inp['moe_w_down'] = nrm((L, MOE_EXPERTS, EXPERT_HIDDEN, d), EXPERT_HIDDEN ** -0.5)
    inp['final_norm_g'] = 1.0 + nrm((d,), 0.05)
    return inp


def reference(x, c, ctx, c_ctx, w_ada, b_ada, norm1_g, norm2_g, w_in, diff_lam, diff_norm_g,
              ml_conv_w, ml_conv_b, ml_gate_b, ml_norm_g, gqa_qnorm_g, gqa_knorm_g,
              hy_conv_w, hy_conv_b, hy_f_w1, hy_f_b1, hy_f_w2, hy_f_b2, hy_f_w3, hy_f_freq, hy_bias,
              w_branch, w_out, moe_w_group, moe_b_group, moe_w_router, moe_b_router,
              moe_w_gate, moe_w_up, moe_w_down, final_norm_g):
    b, n, d = x.shape
    n_ctx = ctx.shape[1]
    rope_d = axial_rope_tables(n, DIFF_DIM)
    rope_g = axial_rope_tables(n, GQA_DIM)
    sc = jax.nn.silu(c)[:, None, :]
    scx = jax.nn.silu(c_ctx)
    xs, cs = x, ctx
    for l in range(DEPTH):
        need_ctx = l < DEPTH - 1
        mod_l = jnp.split(sc @ w_ada[l] + b_ada[l], 6, axis=-1)
        mod_c = jnp.split(scx @ w_ada[l] + b_ada[l], 6, axis=-1)
        hl = modulate(rmsnorm(xs, norm1_g[l]), mod_l[0], mod_l[1])
        hc = modulate(rmsnorm(cs, norm1_g[l]), mod_c[0], mod_c[1])
        pl = split_cols(hl @ w_in[l])
        pc = split_cols(hc @ w_in[l])
        yc_a, yl_a = diff_attention(pc[0], pc[1], pc[2], pl[0], pl[1], pl[2], diff_lam[l], diff_norm_g[l],
                                    l, rope_d, need_ctx)
        yc_b, yl_b = mlstm_branch(pc[3], pc[4], pc[5], pl[3], pl[4], pl[5], ml_conv_w[l], ml_conv_b[l],
                                  ml_gate_b[l], ml_norm_g[l], need_ctx)
        yc_c, yl_c = gqa_attention(pc[6], pc[7], pl[6], pl[7], gqa_qnorm_g[l], gqa_knorm_g[l], rope_g, need_ctx)
        hy_args = (hy_conv_w[l], hy_conv_b[l], hy_f_w1[l], hy_f_b1[l], hy_f_w2[l], hy_f_b2[l],
                   hy_f_w3[l], hy_f_freq[l], hy_bias[l])
        yl_d = hyena_operator(pl[8], *hy_args)
        xs = xs + mod_l[2] * merge_branches((yl_a, yl_b, yl_c, yl_d), pl[9], w_branch[l], w_out[l])
        hl2 = modulate(rmsnorm(xs, norm2_g[l]), mod_l[3], mod_l[4])
        moe_args = (moe_w_group[l], moe_b_group[l], moe_w_router[l], moe_b_router[l],
                    moe_w_gate[l], moe_w_up[l], moe_w_down[l])
        if need_ctx:
            yc_d = hyena_operator(pc[8], *hy_args)
            cs = cs + mod_c[2] * merge_branches((yc_a, yc_b, yc_c, yc_d), pc[9], w_branch[l], w_out[l])
            hc2 = modulate(rmsnorm(cs, norm2_g[l]), mod_c[3], mod_c[4])
            f = hier_moe(jnp.concatenate([hc2.reshape(-1, d), hl2.reshape(-1, d)], axis=0), *moe_args)
            cs = cs + mod_c[5] * f[:b * n_ctx].reshape(b, n_ctx, d)
            f_lat = f[b * n_ctx:]
        else:
            f_lat = hier_moe(hl2.reshape(-1, d), *moe_args)
        xs = xs + mod_l[5] * f_lat.reshape(b, n, d)
    return rmsnorm(xs, final_norm_g)
```

```python
import functools
import math

import jax
import jax.numpy as jnp
import numpy as np
from jax import lax
from jax.experimental import pallas as pl
from jax.experimental.pallas import tpu as pltpu

F32 = jnp.float32
BF16 = jnp.bfloat16

EPS = 1e-6
ROPE_BASE = 10000.0
GRID_W = 64

DIFF_HEADS = 4
DIFF_DIM = 64
DIFF_VDIM = 128
ML_HEADS = 4
ML_DIM = 128
ML_CHUNK = 64
GQA_HEADS = 8
GQA_KV = 2
GQA_DIM = 64
HY_CH = 512
HY_ORDER = 2
FILTER_BANDS = 16
FILTER_SHIFT = 0.05
DECAY_TARGET = 1e-2
FAST_DECAY_PCT = 0.3
SLOW_DECAY_PCT = 1.5
N_BRANCH = 4
MOE_GROUPS = 4
MOE_EPG = 8
MOE_EXPERTS = MOE_GROUPS * MOE_EPG
MOE_TOP_K = 2
MOE_BLOCK = 128

LANES = 128
VMEM_LIMIT = 48 * 1024 * 1024

C_GATE = 0
C_DQ = 4096
C_DK = 4608
C_DV = 5120
C_MLQ = 5632
C_MLK = 6144
C_MLV = 6656
C_MLO = 7168
C_GQ = 7680
C_HY = 8192
C_GK = 9728
C_GV = 9856
C_MLG = 9984
N_P = 10240

QSCALE = (DIFF_DIM ** -0.5) * math.log2(math.e)


def _cparams(sem):
    return pltpu.CompilerParams(dimension_semantics=sem, vmem_limit_bytes=VMEM_LIMIT)


def _tile(n, target):
    if n <= target:
        return n
    for t in range(target, 7, -1):
        if n % t == 0 and t % 8 == 0:
            return t
    return n


def _norm_mod_kernel(x_ref, g_ref, sh_ref, sc_ref, o_ref):
    x = x_ref[0]
    y = x * lax.rsqrt(jnp.mean(x * x, axis=-1, keepdims=True) + EPS)
    y = y * g_ref[...]
    o_ref[0] = (y * (1.0 + sc_ref[0]) + sh_ref[0]).astype(o_ref.dtype)


def norm_mod(x, g, shift, scale, out_dtype):
    b, t, d = x.shape
    tt = _tile(t, 512)
    return pl.pallas_call(
        _norm_mod_kernel,
        grid=(b, t // tt),
        in_specs=[pl.BlockSpec((1, tt, d), lambda i, j: (i, j, 0)),
                  pl.BlockSpec((1, d), lambda i, j: (0, 0)),
                  pl.BlockSpec((1, 1, d), lambda i, j: (i, 0, 0)),
                  pl.BlockSpec((1, 1, d), lambda i, j: (i, 0, 0))],
        out_specs=pl.BlockSpec((1, tt, d), lambda i, j: (i, j, 0)),
        out_shape=jax.ShapeDtypeStruct((b, t, d), out_dtype),
        compiler_params=_cparams(("parallel", "parallel")),
        name="norm_mod",
    )(x, g.reshape(1, d), shift.reshape(b, 1, d), scale.reshape(b, 1, d))


def _mm_kernel(a_ref, w_ref, o_ref):
    o_ref[...] = jnp.dot(a_ref[...], w_ref[...], preferred_element_type=F32).astype(o_ref.dtype)


def matmul(a, w, out_dtype, tm=512, tn=1024):
    m, k = a.shape
    n = w.shape[1]
    tm = _tile(m, tm)
    tn = _tile(n, tn)
    return pl.pallas_call(
        _mm_kernel,
        grid=(m // tm, n // tn),
        in_specs=[pl.BlockSpec((tm, k), lambda i, j: (i, 0)),
                  pl.BlockSpec((k, tn), lambda i, j: (0, j))],
        out_specs=pl.BlockSpec((tm, tn), lambda i, j: (i, j)),
        out_shape=jax.ShapeDtypeStruct((m, n), out_dtype),
        compiler_params=_cparams(("parallel", "parallel")),
        name="matmul",
    )(a, w)


def _rope(x, cos, sa, sb):
    xa = pltpu.roll(x, LANES - 16, axis=1)
    xb = pltpu.roll(x, 16, axis=1)
    return x * cos + xa * sa + xb * sb


def _seg_rmsnorm(x, g):
    lane = lax.broadcasted_iota(jnp.int32, x.shape, 1)
    lo = lane < GQA_DIM
    ss = x * x
    s_lo = jnp.sum(jnp.where(lo, ss, 0.0), axis=-1, keepdims=True)
    s_hi = jnp.sum(jnp.where(lo, 0.0, ss), axis=-1, keepdims=True)
    r = jnp.where(lo, lax.rsqrt(s_lo * (1.0 / GQA_DIM) + EPS), lax.rsqrt(s_hi * (1.0 / GQA_DIM) + EPS))
    return x * r * g


def _prep_kernel(dq_ref, dk_ref, dv_ref, gq_ref, gkv_ref, cos_ref, sa_ref, sb_ref, qg_ref, kg_ref,
                 dq_o, dkt_o, dv_o, gq_o, gkt_o, gv_o, *, rope):
    if rope:
        cos, sa, sb = cos_ref[...], sa_ref[...], sb_ref[...]
        rot = lambda x: _rope(x, cos, sa, sb)
    else:
        rot = lambda x: x
    qg = qg_ref[...]
    kg = kg_ref[...]
    for j in range(DIFF_HEADS):
        sl = slice(j * LANES, (j + 1) * LANES)
        dq_o[0, :, sl] = (rot(dq_ref[0, :, sl]) * QSCALE).astype(BF16)
        kt = rot(dk_ref[0, :, sl]).T
        dkt_o[0, j, 0, 0] = kt[:DIFF_DIM].astype(BF16)
        dkt_o[0, j, 1, 0] = kt[DIFF_DIM:].astype(BF16)
        gq_o[0, :, sl] = (rot(_seg_rmsnorm(gq_ref[0, :, sl], qg)) * QSCALE).astype(BF16)
    dv_o[0] = dv_ref[0].astype(BF16)
    kt = rot(_seg_rmsnorm(gkv_ref[0, :, :LANES], kg)).T
    gkt_o[0, 0, 0] = kt[:GQA_DIM].astype(BF16)
    gkt_o[0, 1, 0] = kt[GQA_DIM:].astype(BF16)
    v = gkv_ref[0, :, LANES:].astype(BF16)
    gv_o[0, 0] = v[:, :GQA_DIM]
    gv_o[0, 1] = v[:, GQA_DIM:]


def attn_prep(p, tables, q_g, k_g, tk):
    b, t, _ = p.shape
    rope = tables is not None
    if rope:
        cos, sa, sb = tables
    else:
        cos = sa = sb = jnp.zeros((t, LANES), F32)
    nck = t // tk
    w512 = lambda c: pl.BlockSpec((1, tk, 512), lambda i, j, c=c: (i, j, c // 512))
    tab = pl.BlockSpec((tk, LANES), lambda i, j: (j, 0))
    vec = pl.BlockSpec((1, LANES), lambda i, j: (0, 0))
    outs = pl.pallas_call(
        functools.partial(_prep_kernel, rope=rope),
        grid=(b, nck),
        in_specs=[w512(C_DQ), w512(C_DK), w512(C_DV), w512(C_GQ),
                  pl.BlockSpec((1, tk, 256), lambda i, j: (i, j, C_GK // 256)),
                  tab, tab, tab, vec, vec],
        out_specs=[pl.BlockSpec((1, tk, 512), lambda i, j: (i, j, 0)),
                   pl.BlockSpec((1, DIFF_HEADS, 2, 1, DIFF_DIM, tk), lambda i, j: (i, 0, 0, j, 0, 0)),
                   pl.BlockSpec((1, tk, 512), lambda i, j: (i, j, 0)),
                   pl.BlockSpec((1, tk, 512), lambda i, j: (i, j, 0)),
                   pl.BlockSpec((1, GQA_KV, 1, GQA_DIM, tk), lambda i, j: (i, 0, j, 0, 0)),
                   pl.BlockSpec((1, GQA_KV, tk, GQA_DIM), lambda i, j: (i, 0, j, 0))],
        out_shape=[jax.ShapeDtypeStruct((b, t, 512), BF16),
                   jax.ShapeDtypeStruct((b, DIFF_HEADS, 2, nck, DIFF_DIM, tk), BF16),
                   jax.ShapeDtypeStruct((b, t, 512), BF16),
                   jax.ShapeDtypeStruct((b, t, 512), BF16),
                   jax.ShapeDtypeStruct((b, GQA_KV, nck, GQA_DIM, tk), BF16),
                   jax.ShapeDtypeStruct((b, GQA_KV, t, GQA_DIM), BF16)],
        compiler_params=_cparams(("parallel", "parallel")),
        name="attn_prep",
    )(p, p, p, p, p, cos, sa, sb,
      jnp.tile(q_g, 2).reshape(1, LANES), jnp.tile(k_g, 2).reshape(1, LANES))
    return outs


def _flash_step(q, kt, v, m_ref, l_ref, acc_ref):
    s = jnp.dot(q, kt, preferred_element_type=F32)
    tk = s.shape[1]
    m_prev = m_ref[...]
    m_next = jnp.maximum(m_prev, jnp.max(s, axis=1, keepdims=True))
    alpha = jnp.exp2(m_prev - m_next)
    p = jnp.exp2(s - jnp.concatenate([m_next] * (tk // LANES), axis=1))
    l_ref[...] = alpha * l_ref[...] + jnp.sum(p, axis=1, keepdims=True)
    dv = acc_ref.shape[-1]
    acc_ref[...] = acc_ref[...] * alpha[:, :dv] + jnp.dot(p.astype(BF16), v, preferred_element_type=F32)
    m_ref[...] = m_next


def _attn_body(q_ref, ktc, vc, ktl, vl, m_sc, l_sc, acc_sc, *, n_lat, tk):
    m_sc[...] = jnp.full(m_sc.shape, -jnp.inf, F32)
    l_sc[...] = jnp.zeros(l_sc.shape, F32)
    acc_sc[...] = jnp.zeros(acc_sc.shape, F32)
    q = q_ref[0]
    qs = (q[:, :DIFF_DIM], q[:, DIFF_DIM:])
    for c in range(2):
        _flash_step(qs[c], ktc(c), vc(c), m_sc.at[c], l_sc.at[c], acc_sc.at[c])
    if n_lat:
        def body(i, carry):
            for c in range(2):
                _flash_step(qs[c], ktl(c, i), vl(c, i), m_sc.at[c], l_sc.at[c], acc_sc.at[c])
            return carry
        lax.fori_loop(0, n_lat, body, 0)


def _diff_attn_kernel(*refs, n_lat, tk, out_scale):
    if n_lat:
        q_ref, ktc_ref, vc_ref, ktl_ref, vl_ref, lam_ref, g_ref, o_ref, m_sc, l_sc, acc_sc = refs
        ktl = lambda c, i: ktl_ref[0, 0, c, i]
        vl = lambda c, i: vl_ref[0, pl.ds(pl.multiple_of(i * tk, tk), tk), :]
    else:
        q_ref, ktc_ref, vc_ref, lam_ref, g_ref, o_ref, m_sc, l_sc, acc_sc = refs
        ktl = vl = None
    _attn_body(q_ref, lambda c: ktc_ref[0, 0, c, 0], lambda c: vc_ref[0], ktl, vl,
               m_sc, l_sc, acc_sc, n_lat=n_lat, tk=tk)
    o0 = acc_sc[0] / l_sc[0]
    o1 = acc_sc[1] / l_sc[1]
    o = o0 - lam_ref[...] * o1
    o = o * lax.rsqrt(jnp.mean(o * o, axis=-1, keepdims=True) + EPS)
    o_ref[0] = (o * g_ref[...] * out_scale).astype(o_ref.dtype)


def diff_attention(q, ktc, vc, ktl, vl, lam, norm_g, out_scale, tq):
    b, t, _ = q.shape
    sc = vc.shape[1]
    tq = _tile(t, tq)
    n_lat, tk = (ktl.shape[3], ktl.shape[5]) if ktl is not None else (0, 0)
    in_specs = [pl.BlockSpec((1, tq, LANES), lambda i, h, j: (i, j, h)),
                pl.BlockSpec((1, 1, 2, 1, DIFF_DIM, sc), lambda i, h, j: (i, h, 0, 0, 0, 0)),
                pl.BlockSpec((1, sc, LANES), lambda i, h, j: (i, 0, h))]
    args = [q, ktc, vc]
    if n_lat:
        in_specs += [pl.BlockSpec((1, 1, 2, n_lat, DIFF_DIM, tk), lambda i, h, j: (i, h, 0, 0, 0, 0)),
                     pl.BlockSpec((1, n_lat * tk, LANES), lambda i, h, j: (i, 0, h))]
        args += [ktl, vl]
    vec = pl.BlockSpec((1, LANES), lambda i, h, j: (0, 0))
    in_specs += [vec, vec]
    args += [jnp.full((1, LANES), lam, F32), norm_g.reshape(1, LANES)]
    return pl.pallas_call(
        functools.partial(_diff_attn_kernel, n_lat=n_lat, tk=tk, out_scale=out_scale),
        grid=(b, DIFF_HEADS, t // tq),
        in_specs=in_specs,
        out_specs=pl.BlockSpec((1, tq, LANES), lambda i, h, j: (i, j, h)),
        out_shape=jax.ShapeDtypeStruct((b, t, DIFF_HEADS * DIFF_VDIM), BF16),
        scratch_shapes=[pltpu.VMEM((2, tq, LANES), F32), pltpu.VMEM((2, tq, LANES), F32),
                        pltpu.VMEM((2, tq, DIFF_VDIM), F32)],
        compiler_params=_cparams(("parallel", "parallel", "parallel")),
        name="diff_attn",
    )(*args)


def _gqa_attn_kernel(*refs, n_lat, tk):
    if n_lat:
        q_ref, ktc_ref, vc_ref, ktl_ref, vl_ref, o_ref, m_sc, l_sc, acc_sc = refs
        ktl = lambda c, i: ktl_ref[0, 0, i]
        vl = lambda c, i: vl_ref[0, 0, pl.ds(pl.multiple_of(i * tk, tk), tk), :]
    else:
        q_ref, ktc_ref, vc_ref, o_ref, m_sc, l_sc, acc_sc = refs
        ktl = vl = None
    _attn_body(q_ref, lambda c: ktc_ref[0, 0, 0], lambda c: vc_ref[0, 0], ktl, vl,
               m_sc, l_sc, acc_sc, n_lat=n_lat, tk=tk)
    o0 = acc_sc[0] / l_sc[0][:, :GQA_DIM]
    o1 = acc_sc[1] / l_sc[1][:, :GQA_DIM]
    o_ref[0] = jnp.concatenate([o0, o1], axis=-1).astype(o_ref.dtype)


def gqa_attention(q, ktc, vc, ktl, vl, tq):
    b, t, _ = q.shape
    sc = vc.shape[2]
    tq = _tile(t, tq)
    n_lat, tk = (ktl.shape[2], ktl.shape[4]) if ktl is not None else (0, 0)
    pairs = GQA_HEADS // 2
    grp = lambda h: h // (pairs // GQA_KV)
    in_specs = [pl.BlockSpec((1, tq, LANES), lambda i, h, j: (i, j, h)),
                pl.BlockSpec((1, 1, 1, GQA_DIM, sc), lambda i, h, j: (i, grp(h), 0, 0, 0)),
                pl.BlockSpec((1, 1, sc, GQA_DIM), lambda i, h, j: (i, grp(h), 0, 0))]
    args = [q, ktc, vc]
    if n_lat:
        in_specs += [pl.BlockSpec((1, 1, n_lat, GQA_DIM, tk), lambda i, h, j: (i, grp(h), 0, 0, 0)),
                     pl.BlockSpec((1, 1, n_lat * tk, GQA_DIM), lambda i, h, j: (i, grp(h), 0, 0))]
        args += [ktl, vl]
    return pl.pallas_call(
        functools.partial(_gqa_attn_kernel, n_lat=n_lat, tk=tk),
        grid=(b, pairs, t // tq),
        in_specs=in_specs,
        out_specs=pl.BlockSpec((1, tq, LANES), lambda i, h, j: (i, j, h)),
        out_shape=jax.ShapeDtypeStruct((b, t, GQA_HEADS * GQA_DIM), BF16),
        scratch_shapes=[pltpu.VMEM((2, tq, LANES), F32), pltpu.VMEM((2, tq, LANES), F32),
                        pltpu.VMEM((2, tq, GQA_DIM), F32)],
        compiler_params=_cparams(("parallel", "parallel", "parallel")),
        name="gqa_attn",
    )(*args)


def _merge_kernel(ya_ref, yb_ref, yc_ref, yd_ref, g_ref, wb_ref, wo_ref, x_ref, gate_ref, o_ref):
    d = x_ref.shape[-1]
    acc = None
    for n, y_ref in enumerate((ya_ref, yb_ref, yc_ref, yd_ref)):
        t = jnp.dot(y_ref[0].astype(BF16), wb_ref[n], preferred_element_type=F32)
        t = jax.nn.sigmoid(g_ref[0, :, n * d:(n + 1) * d]) * t
        acc = t if acc is None else acc + t
    z = jnp.dot(acc.astype(BF16), wo_ref[...], preferred_element_type=F32)
    o_ref[0] = x_ref[0] + gate_ref[0] * z


def merge(ys, p, w_branch, w_out, x, gate):
    b, t, d = x.shape
    tm = _tile(t, 256)
    ysp = pl.BlockSpec((1, tm, 512), lambda i, j: (i, j, 0))
    return pl.pallas_call(
        _merge_kernel,
        grid=(b, t // tm),
        in_specs=[ysp, ysp, ysp, ysp,
                  pl.BlockSpec((1, tm, N_BRANCH * d), lambda i, j: (i, j, C_GATE // (N_BRANCH * d))),
                  pl.BlockSpec((N_BRANCH, 512, d), lambda i, j: (0, 0, 0)),
                  pl.BlockSpec((d, d), lambda i, j: (0, 0)),
                  pl.BlockSpec((1, tm, d), lambda i, j: (i, j, 0)),
                  pl.BlockSpec((1, 1, d), lambda i, j: (i, 0, 0))],
        out_specs=pl.BlockSpec((1, tm, d), lambda i, j: (i, j, 0)),
        out_shape=jax.ShapeDtypeStruct((b, t, d), F32),
        compiler_params=_cparams(("parallel", "parallel")),
        name="merge",
    )(*ys, p, w_branch, w_out, x, gate.reshape(b, 1, d))


def _expert_kernel(be_ref, x_ref, wg_ref, wu_ref, wd_ref, o_ref):
    x = x_ref[...]
    a = jnp.dot(x, wg_ref[0], preferred_element_type=F32)
    u = jnp.dot(x, wu_ref[0], preferred_element_type=F32)
    h = (a * jax.nn.sigmoid(a)) * u
    o_ref[...] = jnp.dot(h.astype(BF16), wd_ref[0], preferred_element_type=F32)


def expert_blocks(xb, blk_exp, w_gate, w_up, w_down):
    m, d = xb.shape
    hdim = w_gate.shape[-1]
    n_blocks = m // MOE_BLOCK
    grid_spec = pltpu.PrefetchScalarGridSpec(
        num_scalar_prefetch=1,
        grid=(n_blocks,),
        in_specs=[pl.BlockSpec((MOE_BLOCK, d), lambda i, be: (i, 0)),
                  pl.BlockSpec((1, d, hdim), lambda i, be: (be[i], 0, 0)),
                  pl.BlockSpec((1, d, hdim), lambda i, be: (be[i], 0, 0)),
                  pl.BlockSpec((1, hdim, d), lambda i, be: (be[i], 0, 0))],
        out_specs=pl.BlockSpec((MOE_BLOCK, d), lambda i, be: (i, 0)),
    )
    return pl.pallas_call(
        _expert_kernel,
        grid_spec=grid_spec,
        out_shape=jax.ShapeDtypeStruct((m, d), F32),
        compiler_params=_cparams(("arbitrary",)),
        name="moe_experts",
    )(blk_exp, xb, w_gate, w_up, w_down)


def _rope_tables(n_tok):
    rows = n_tok // GRID_W
    row = jnp.repeat(jnp.arange(rows, dtype=F32), GRID_W)
    col = jnp.broadcast_to(jnp.arange(GRID_W, dtype=F32), (rows, GRID_W)).reshape(-1)
    n_freq = DIFF_DIM // 4
    inv = ROPE_BASE ** (-jnp.arange(n_freq, dtype=F32) / n_freq)
    ar = row[:, None] * inv
    ac = col[:, None] * inv
    ang = jnp.concatenate([ar, ar, ac, ac], axis=-1)
    cos, sin = jnp.cos(ang), jnp.sin(ang)
    first = (jnp.arange(DIFF_DIM) % 32) < 16
    sa = jnp.where(first, -sin, 0.0)
    sb = jnp.where(first, 0.0, sin)
    return tuple(jnp.tile(a, (1, 2)) for a in (cos, sa, sb))


def _short_conv(x, w, bias):
    k, ch = w.shape
    y = lax.conv_general_dilated(x, w[:, None, :].astype(x.dtype), window_strides=(1,),
                                 padding=((k // 2, k // 2),), dimension_numbers=('NWC', 'WIO', 'NWC'),
                                 feature_group_count=ch)
    return y + bias


def _rmsnorm(x, g):
    y = x * lax.rsqrt(jnp.mean(x * x, axis=-1, keepdims=True) + EPS)
    return y * g


def _mlstm_scan(q, k, v, log_i, log_f, state):
    b, t, h, d = q.shape
    nc = t // ML_CHUNK

    def chunks(a):
        a = a.astype(F32).reshape((b, nc, ML_CHUNK) + a.shape[2:])
        return jnp.swapaxes(jnp.moveaxis(a, 1, 0), 2, 3)

    causal = jnp.tril(jnp.ones((ML_CHUNK, ML_CHUNK), dtype=bool))

    def step(carry, inp):
        c_mat, n_vec, m = carry
        qc, kc, vc, li, lf = inp
        bcum = jnp.cumsum(lf, axis=-1)
        dmat = jnp.where(causal, bcum[..., :, None] - bcum[..., None, :] + li[..., None, :], -jnp.inf)
        inter = bcum + m[..., None]
        m_t = jnp.maximum(inter, jnp.max(dmat, axis=-1))
        w_intra = jnp.exp(dmat - m_t[..., None])
        w_inter = jnp.exp(inter - m_t)
        s = jnp.einsum('bhtd,bhsd->bhts', qc, kc) * w_intra
        num = jnp.einsum('bhts,bhse->bhte', s, vc) + w_inter[..., None] * jnp.einsum('bhtd,bhde->bhte', qc, c_mat)
        den = jnp.sum(s, axis=-1) + w_inter * jnp.einsum('bhtd,bhd->bht', qc, n_vec)
        h_out = num / jnp.maximum(jnp.abs(den), jnp.exp(-m_t))[..., None]
        b_last = bcum[..., -1]
        g = b_last[..., None] - bcum + li
        m_new = jnp.maximum(b_last + m, jnp.max(g, axis=-1))
        kw = kc * jnp.exp(g - m_new[..., None])[..., None]
        wc = jnp.exp(b_last + m - m_new)
        c_new = wc[..., None, None] * c_mat + jnp.einsum('bhsd,bhse->bhde', kw, vc)
        n_new = wc[..., None] * n_vec + jnp.sum(kw, axis=2)
        return (c_new, n_new, m_new), h_out

    state, hs = lax.scan(step, state, (chunks(q), chunks(k), chunks(v), chunks(log_i), chunks(log_f)))
    hs = jnp.moveaxis(jnp.swapaxes(hs, 2, 3), 0, 1).reshape(b, t, h, d)
    return hs, state


def _mlstm_branch(p_c, p_l, conv_w, conv_b, gate_b, norm_g, need_ctx):
    w = ML_HEADS * ML_DIM

    def prep(p):
        b, t = p.shape[:2]
        qk = jax.nn.silu(_short_conv(p[..., C_MLQ:C_MLQ + 2 * w], conv_w, conv_b))
        q = qk[..., :w].reshape(b, t, ML_HEADS, ML_DIM)
        k = qk[..., w:].reshape(b, t, ML_HEADS, ML_DIM) * ML_DIM ** -0.5
        v = p[..., C_MLV:C_MLV + w].reshape(b, t, ML_HEADS, ML_DIM)
        gates = p[..., C_MLG:C_MLG + 4 * ML_HEADS].reshape(b, t, 2, 2, ML_HEADS) + gate_b
        return q, k, v, gates

    flip = lambda a: a[:, ::-1]

    def run(inp, direction, state):
        q, k, v, gates = inp
        log_i = gates[:, :, direction, 0]
        log_f = jax.nn.log_sigmoid(gates[:, :, direction, 1])
        if direction == 1:
            h, st = _mlstm_scan(flip(q), flip(k), flip(v), flip(log_i), flip(log_f), state)
            return flip(h), st
        return _mlstm_scan(q, k, v, log_i, log_f, state)

    ctx_in = prep(p_c)
    lat_in = prep(p_l)
    b = p_l.shape[0]
    zero = (jnp.zeros((b, ML_HEADS, ML_DIM, ML_DIM), F32), jnp.zeros((b, ML_HEADS, ML_DIM), F32),
            jnp.zeros((b, ML_HEADS), F32))
    hc_f, st_f = run(ctx_in, 0, zero)
    hc_b, st_b = run(ctx_in, 1, zero)
    hl = run(lat_in, 0, st_f)[0] + run(lat_in, 1, st_b)[0]

    def out(h, o):
        h = _rmsnorm(h, norm_g) * jax.nn.sigmoid(o).reshape(h.shape)
        return h.reshape(h.shape[:2] + (w,))

    y_lat = out(hl, p_l[..., C_MLO:C_MLO + w])
    y_ctx = out(hc_f + hc_b, p_c[..., C_MLO:C_MLO + w]) if need_ctx else None
    return y_ctx, y_lat


def _hyena_filters(length, w1, b1, w2, b2, w3, freq):
    t = jnp.arange(length, dtype=F32) / length
    bands = jnp.arange(1, FILTER_BANDS + 1, dtype=F32)
    ang = 2.0 * math.pi * t[:, None] * bands
    emb = jnp.concatenate([t[:, None], jnp.cos(ang), jnp.sin(ang)], axis=-1)
    a = jnp.sin(freq[0] * (emb @ w1 + b1))
    a = jnp.sin(freq[1] * (a @ w2 + b2))
    filt = (a @ w3).astype(F32).reshape(length, HY_ORDER, 2, HY_CH)
    alpha = jnp.linspace(abs(math.log(DECAY_TARGET)) / SLOW_DECAY_PCT,
                         abs(math.log(DECAY_TARGET)) / FAST_DECAY_PCT, HY_CH)
    window = jnp.exp(-t[:, None] * alpha) + FILTER_SHIFT
    filt = filt * window[:, None, None, :]
    k = jnp.concatenate([filt[:, :, 0], jnp.zeros((1, HY_ORDER, HY_CH), F32), filt[:0:-1, :, 1]], axis=0)
    return k * lax.rsqrt(jnp.sum(k * k, axis=0, keepdims=True) + EPS)


def _fft_long_conv(z, k):
    length = z.shape[1]
    zf = jnp.fft.rfft(z.astype(F32), n=2 * length, axis=1)
    kf = jnp.fft.rfft(k, axis=0)
    return jnp.fft.irfft(zf * kf[None], n=2 * length, axis=1)[:, :length]


def _hyena(p, conv_w, conv_b, w1, b1, w2, b2, w3, freq, bias):
    length = p.shape[1]
    u = _short_conv(p, conv_w, conv_b)
    parts = jnp.split(u, HY_ORDER + 1, axis=-1)
    k = _hyena_filters(length, w1, b1, w2, b2, w3, freq)
    z = parts[0]
    for n in range(HY_ORDER):
        z = parts[n + 1] * (_fft_long_conv(z, k[:, n]) + bias[n] * z)
    return z


def _hier_moe(h, w_group, b_group, w_router, b_router, w_gate, w_up, w_down):
    n_tok, d = h.shape
    g_logits = (h @ w_group + b_group).astype(F32)
    grp = jnp.argmax(g_logits, axis=-1)
    p_grp = jnp.max(jax.nn.softmax(g_logits, axis=-1), axis=-1, keepdims=True)
    e_logits = (h @ w_router + b_router).astype(F32).reshape(n_tok, MOE_GROUPS, MOE_EPG)
    e_in = e_logits[jnp.arange(n_tok), grp]
    top_v, top_i = lax.top_k(e_in, MOE_TOP_K)
    gates = p_grp * jax.nn.softmax(top_v, axis=-1)
    e_flat = (grp[:, None] * MOE_EPG + top_i).reshape(-1)
    m_slots = n_tok * MOE_TOP_K
    order = jnp.argsort(e_flat)
    e_s = e_flat[order]
    tok_s = order // MOE_TOP_K
    gate_s = gates.reshape(-1)[order]
    counts = jnp.bincount(e_flat, length=MOE_EXPERTS)
    padded = (counts + MOE_BLOCK - 1) // MOE_BLOCK * MOE_BLOCK
    start = jnp.cumsum(counts) - counts
    p_end = jnp.cumsum(padded)
    dest = (p_end - padded)[e_s] + jnp.arange(m_slots) - start[e_s]
    n_blocks = -(-(m_slots + MOE_EXPERTS * (MOE_BLOCK - 1)) // MOE_BLOCK)
    buf_tok = jnp.zeros((n_blocks * MOE_BLOCK,), jnp.int32).at[dest].set(tok_s.astype(jnp.int32))
    blk_exp = jnp.minimum(jnp.searchsorted(p_end, jnp.arange(n_blocks) * MOE_BLOCK, side='right'),
                          MOE_EXPERTS - 1).astype(jnp.int32)
    xb = h.astype(BF16)[buf_tok]
    yb = expert_blocks(xb, blk_exp, w_gate.astype(BF16), w_up.astype(BF16), w_down.astype(BF16))
    y_s = yb[dest] * gate_s[:, None]
    return jnp.zeros_like(h).at[tok_s].add(y_s)


def _permute_w_in(w):
    d = w.shape[0]
    sizes = (512, 512, 512, 1536, 512, 16, 512, 256, 1536, 4096)
    offs = np.cumsum((0,) + sizes)
    dq, dk, dv, mlqkv, mlo, mlg, gq, gkv, hy, gate = [w[:, offs[i]:offs[i + 1]] for i in range(10)]
    pad = jnp.zeros((d, N_P - C_MLG - 16), w.dtype)
    return jnp.concatenate([gate, dq, dk, dv, mlqkv, mlo, gq, hy, gkv, mlg, pad], axis=1)


def kernel(x, c, ctx, c_ctx, w_ada, b_ada, norm1_g, norm2_g, w_in, diff_lam, diff_norm_g, ml_conv_w, ml_conv_b, ml_gate_b, ml_norm_g, gqa_qnorm_g, gqa_knorm_g, hy_conv_w, hy_conv_b, hy_f_w1, hy_f_b1, hy_f_w2, hy_f_b2, hy_f_w3, hy_f_freq, hy_bias, w_branch, w_out, moe_w_group, moe_b_group, moe_w_router, moe_b_router, moe_w_gate, moe_w_up, moe_w_down, final_norm_g):
    b, n, d = x.shape
    n_ctx = ctx.shape[1]
    depth = w_in.shape[0]
    tk = _tile(n, 512)
    tables = _rope_tables(n)
    sc = jax.nn.silu(c)
    scx = jax.nn.silu(c_ctx)
    xs, cs = x, ctx
    for l in range(depth):
        need_ctx = l < depth - 1
        mod_l = jnp.split(sc @ w_ada[l] + b_ada[l], 6, axis=-1)
        mod_c = [jnp.broadcast_to(m, (b, d)) for m in jnp.split(scx @ w_ada[l] + b_ada[l], 6, axis=-1)]
        w_p = _permute_w_in(w_in[l]).astype(BF16)
        hl = norm_mod(xs, norm1_g[l], mod_l[0], mod_l[1], BF16)
        hc = norm_mod(cs, norm1_g[l], mod_c[0], mod_c[1], BF16)
        p_l = matmul(hl.reshape(b * n, d), w_p, F32).reshape(b, n, N_P)
        p_c = matmul(hc.reshape(b * n_ctx, d), w_p, F32).reshape(b, n_ctx, N_P)

        dq_l, dkt_l, dv_l, gq_l, gkt_l, gv_l = attn_prep(p_l, tables, gqa_qnorm_g[l], gqa_knorm_g[l], tk)
        dq_c, dkt_c, dv_c, gq_c, gkt_c, gv_c = attn_prep(p_c, None, gqa_qnorm_g[l], gqa_knorm_g[l], n_ctx)
        lam_init = 0.8 - 0.6 * math.exp(-0.3 * l)
        lp = diff_lam[l].astype(F32)
        lam = jnp.exp(jnp.sum(lp[0] * lp[1])) - jnp.exp(jnp.sum(lp[2] * lp[3])) + lam_init
        yl_a = diff_attention(dq_l, dkt_c, dv_c, dkt_l, dv_l, lam, diff_norm_g[l], 1.0 - lam_init, 512)
        yl_c = gqa_attention(gq_l, gkt_c, gv_c, gkt_l, gv_l, 512)
        if need_ctx:
            yc_a = diff_attention(dq_c, dkt_c, dv_c, None, None, lam, diff_norm_g[l], 1.0 - lam_init, 256)
            yc_c = gqa_attention(gq_c, gkt_c, gv_c, None, None, 256)

        yc_b, yl_b = _mlstm_branch(p_c, p_l, ml_conv_w[l], ml_conv_b[l], ml_gate_b[l], ml_norm_g[l], need_ctx)
        hy_args = (hy_conv_w[l], hy_conv_b[l], hy_f_w1[l], hy_f_b1[l], hy_f_w2[l], hy_f_b2[l],
                   hy_f_w3[l], hy_f_freq[l], hy_bias[l])
        yl_d = _hyena(p_l[..., C_HY:C_HY + 3 * HY_CH], *hy_args)

        wb = w_branch[l].astype(BF16)
        wo = w_out[l].astype(BF16)
        xs = merge((yl_a, yl_b, yl_c, yl_d), p_l, wb, wo, xs, mod_l[2])
        hl2 = norm_mod(xs, norm2_g[l], mod_l[3], mod_l[4], F32)
        moe_args = (moe_w_group[l], moe_b_group[l], moe_w_router[l], moe_b_router[l],
                    moe_w_gate[l], moe_w_up[l], moe_w_down[l])
        if need_ctx:
            yc_d = _hyena(p_c[..., C_HY:C_HY + 3 * HY_CH], *hy_args)
            cs = merge((yc_a, yc_b, yc_c, yc_d), p_c, wb, wo, cs, mod_c[2])
            hc2 = norm_mod(cs, norm2_g[l], mod_c[3], mod_c[4], F32)
            f = _hier_moe(jnp.concatenate([hc2.reshape(-1, d), hl2.reshape(-1, d)], axis=0), *moe_args)
            cs = cs + mod_c[5][:, None, :] * f[:b * n_ctx].reshape(b, n_ctx, d)
            f_lat = f[b * n_ctx:]
        else:
            f_lat = _hier_moe(hl2.reshape(-1, d), *moe_args)
        xs = xs + mod_l[5][:, None, :] * f_lat.reshape(b, n, d)
    zero = jnp.zeros((b, d), F32)
    return norm_mod(xs, final_norm_g, zero, zero, F32)
```

```python
import functools
import math

import jax
import jax.numpy as jnp
import numpy as np
from jax import lax
from jax.experimental import pallas as pl
from jax.experimental.pallas import tpu as pltpu

F32 = jnp.float32
BF16 = jnp.bfloat16

EPS = 1e-6
ROPE_BASE = 10000.0
GRID_W = 64

DIFF_HEADS = 4
DIFF_DIM = 64
DIFF_VDIM = 128
ML_HEADS = 4
ML_DIM = 128
ML_CHUNK = 64
GQA_HEADS = 8
GQA_KV = 2
GQA_DIM = 64
HY_CH = 512
HY_ORDER = 2
FILTER_BANDS = 16
FILTER_SHIFT = 0.05
DECAY_TARGET = 1e-2
FAST_DECAY_PCT = 0.3
SLOW_DECAY_PCT = 1.5
N_BRANCH = 4
MOE_GROUPS = 4
MOE_EPG = 8
MOE_EXPERTS = MOE_GROUPS * MOE_EPG
MOE_TOP_K = 2
MOE_BLOCK = 128

LANES = 128
VMEM_LIMIT = 48 * 1024 * 1024

C_GATE = 0
C_DQ = 4096
C_DK = 4608
C_DV = 5120
C_MLQ = 5632
C_MLK = 6144
C_MLV = 6656
C_MLO = 7168
C_GQ = 7680
C_HY = 8192
C_GK = 9728
C_GV = 9856
C_MLG = 9984
N_P = 10240

QSCALE = (DIFF_DIM ** -0.5) * math.log2(math.e)


def _cparams(sem):
    return pltpu.CompilerParams(dimension_semantics=sem, vmem_limit_bytes=VMEM_LIMIT)


def _tile(n, target):
    if n <= target:
        return n
    for t in range(target, 7, -1):
        if n % t == 0 and t % 8 == 0:
            return t
    return n


def _norm_mod_kernel(x_ref, g_ref, sh_ref, sc_ref, o_ref):
    x = x_ref[0]
    y = x * lax.rsqrt(jnp.mean(x * x, axis=-1, keepdims=True) + EPS)
    y = y * g_ref[...]
    o_ref[0] = (y * (1.0 + sc_ref[0]) + sh_ref[0]).astype(o_ref.dtype)


def norm_mod(x, g, shift, scale, out_dtype):
    b, t, d = x.shape
    tt = _tile(t, 512)
    return pl.pallas_call(
        _norm_mod_kernel,
        grid=(b, t // tt),
        in_specs=[pl.BlockSpec((1, tt, d), lambda i, j: (i, j, 0)),
                  pl.BlockSpec((1, d), lambda i, j: (0, 0)),
                  pl.BlockSpec((1, 1, d), lambda i, j: (i, 0, 0)),
                  pl.BlockSpec((1, 1, d), lambda i, j: (i, 0, 0))],
        out_specs=pl.BlockSpec((1, tt, d), lambda i, j: (i, j, 0)),
        out_shape=jax.ShapeDtypeStruct((b, t, d), out_dtype),
        compiler_params=_cparams(("parallel", "parallel")),
        name="norm_mod",
    )(x, g.reshape(1, d), shift.reshape(b, 1, d), scale.reshape(b, 1, d))


def _mm_kernel(a_ref, w_ref, o_ref):
    o_ref[...] = jnp.dot(a_ref[...], w_ref[...], preferred_element_type=F32).astype(o_ref.dtype)


def matmul(a, w, out_dtype, tm=512, tn=1024):
    m, k = a.shape
    n = w.shape[1]
    tm = _tile(m, tm)
    tn = _tile(n, tn)
    return pl.pallas_call(
        _mm_kernel,
        grid=(m // tm, n // tn),
        in_specs=[pl.BlockSpec((tm, k), lambda i, j: (i, 0)),
                  pl.BlockSpec((k, tn), lambda i, j: (0, j))],
        out_specs=pl.BlockSpec((tm, tn), lambda i, j: (i, j)),
        out_shape=jax.ShapeDtypeStruct((m, n), out_dtype),
        compiler_params=_cparams(("parallel", "parallel")),
        name="matmul",
    )(a, w)


def _rope(x, cos, sa, sb):
    xa = pltpu.roll(x, LANES - 16, axis=1)
    xb = pltpu.roll(x, 16, axis=1)
    return x * cos + xa * sa + xb * sb


def _seg_rmsnorm(x, g):
    lane = lax.broadcasted_iota(jnp.int32, x.shape, 1)
    lo = lane < GQA_DIM
    ss = x * x
    s_lo = jnp.sum(jnp.where(lo, ss, 0.0), axis=-1, keepdims=True)
    s_hi = jnp.sum(jnp.where(lo, 0.0, ss), axis=-1, keepdims=True)
    r = jnp.where(lo, lax.rsqrt(s_lo * (1.0 / GQA_DIM) + EPS), lax.rsqrt(s_hi * (1.0 / GQA_DIM) + EPS))
    return x * r * g


def _prep_kernel(dq_ref, dk_ref, dv_ref, gq_ref, gkv_ref, cos_ref, sa_ref, sb_ref, qg_ref, kg_ref,
                 dq_o, dkt_o, dv_o, gq_o, gkt_o, gv_o, *, rope):
    if rope:
        cos, sa, sb = cos_ref[...], sa_ref[...], sb_ref[...]
        rot = lambda x: _rope(x, cos, sa, sb)
    else:
        rot = lambda x: x
    qg = qg_ref[...]
    kg = kg_ref[...]
    for j in range(DIFF_HEADS):
        sl = slice(j * LANES, (j + 1) * LANES)
        dq_o[0, :, sl] = (rot(dq_ref[0, :, sl]) * QSCALE).astype(BF16)
        kt = rot(dk_ref[0, :, sl]).T
        dkt_o[0, j, 0, 0] = kt[:DIFF_DIM].astype(BF16)
        dkt_o[0, j, 1, 0] = kt[DIFF_DIM:].astype(BF16)
        gq_o[0, :, sl] = (rot(_seg_rmsnorm(gq_ref[0, :, sl], qg)) * QSCALE).astype(BF16)
    dv_o[0] = dv_ref[0].astype(BF16)
    kt = rot(_seg_rmsnorm(gkv_ref[0, :, :LANES], kg)).T
    gkt_o[0, 0, 0] = kt[:GQA_DIM].astype(BF16)
    gkt_o[0, 1, 0] = kt[GQA_DIM:].astype(BF16)
    v = gkv_ref[0, :, LANES:].astype(BF16)
    gv_o[0, 0] = v[:, :GQA_DIM]
    gv_o[0, 1] = v[:, GQA_DIM:]


def attn_prep(p, tables, q_g, k_g, tk):
    b, t, _ = p.shape
    rope = tables is not None
    if rope:
        cos, sa, sb = tables
    else:
        cos = sa = sb = jnp.zeros((t, LANES), F32)
    nck = t // tk
    tp = _tile(tk, 512)
    sub = tk // tp
    w512 = lambda c: pl.BlockSpec((1, tp, 512), lambda i, j, c=c: (i, j, c // 512))
    tab = pl.BlockSpec((tp, LANES), lambda i, j: (j, 0))
    vec = pl.BlockSpec((1, LANES), lambda i, j: (0, 0))
    outs = pl.pallas_call(
        functools.partial(_prep_kernel, rope=rope),
        grid=(b, t // tp),
        in_specs=[w512(C_DQ), w512(C_DK), w512(C_DV), w512(C_GQ),
                  pl.BlockSpec((1, tp, 256), lambda i, j: (i, j, C_GK // 256)),
                  tab, tab, tab, vec, vec],
        out_specs=[pl.BlockSpec((1, tp, 512), lambda i, j: (i, j, 0)),
                   pl.BlockSpec((1, DIFF_HEADS, 2, 1, DIFF_DIM, tp), lambda i, j: (i, 0, 0, j // sub, 0, j % sub)),
                   pl.BlockSpec((1, tp, 512), lambda i, j: (i, j, 0)),
                   pl.BlockSpec((1, tp, 512), lambda i, j: (i, j, 0)),
                   pl.BlockSpec((1, GQA_KV, 1, GQA_DIM, tp), lambda i, j: (i, 0, j // sub, 0, j % sub)),
                   pl.BlockSpec((1, GQA_KV, tp, GQA_DIM), lambda i, j: (i, 0, j, 0))],
        out_shape=[jax.ShapeDtypeStruct((b, t, 512), BF16),
                   jax.ShapeDtypeStruct((b, DIFF_HEADS, 2, nck, DIFF_DIM, tk), BF16),
                   jax.ShapeDtypeStruct((b, t, 512), BF16),
                   jax.ShapeDtypeStruct((b, t, 512), BF16),
                   jax.ShapeDtypeStruct((b, GQA_KV, nck, GQA_DIM, tk), BF16),
                   jax.ShapeDtypeStruct((b, GQA_KV, t, GQA_DIM), BF16)],
        compiler_params=_cparams(("parallel", "parallel")),
        name="attn_prep",
    )(p, p, p, p, p, cos, sa, sb,
      jnp.tile(q_g, 2).reshape(1, LANES), jnp.tile(k_g, 2).reshape(1, LANES))
    return outs


def _flash_step(q, kt, v, m_ref, l_ref, acc_ref):
    s = jnp.dot(q, kt, preferred_element_type=F32)
    tk = s.shape[1]
    m_prev = m_ref[...]
    m_next = jnp.maximum(m_prev, jnp.max(s, axis=1, keepdims=True))
    alpha = jnp.exp2(m_prev - m_next)
    p = jnp.exp2(s - jnp.concatenate([m_next] * (tk // LANES), axis=1))
    l_ref[...] = alpha * l_ref[...] + jnp.sum(p, axis=1, keepdims=True)
    dv = acc_ref.shape[-1]
    acc_ref[...] = acc_ref[...] * alpha[:, :dv] + jnp.dot(p.astype(BF16), v, preferred_element_type=F32)
    m_ref[...] = m_next


def _attn_body(q_ref, ktc, vc, ktl, vl, m_sc, l_sc, acc_sc, *, n_lat, tk):
    m_sc[...] = jnp.full(m_sc.shape, -jnp.inf, F32)
    l_sc[...] = jnp.zeros(l_sc.shape, F32)
    acc_sc[...] = jnp.zeros(acc_sc.shape, F32)
    q = q_ref[0]
    qs = (q[:, :DIFF_DIM], q[:, DIFF_DIM:])
    for c in range(2):
        _flash_step(qs[c], ktc(c), vc(c), m_sc.at[c], l_sc.at[c], acc_sc.at[c])
    if n_lat:
        def body(i, carry):
            for c in range(2):
                _flash_step(qs[c], ktl(c, i), vl(c, i), m_sc.at[c], l_sc.at[c], acc_sc.at[c])
            return carry
        lax.fori_loop(0, n_lat, body, 0)


def _diff_attn_kernel(*refs, n_lat, tk, out_scale):
    if n_lat:
        q_ref, ktc_ref, vc_ref, ktl_ref, vl_ref, lam_ref, g_ref, o_ref, m_sc, l_sc, acc_sc = refs
        ktl = lambda c, i: ktl_ref[0, 0, c, i]
        vl = lambda c, i: vl_ref[0, pl.ds(pl.multiple_of(i * tk, tk), tk), :]
    else:
        q_ref, ktc_ref, vc_ref, lam_ref, g_ref, o_ref, m_sc, l_sc, acc_sc = refs
        ktl = vl = None
    _attn_body(q_ref, lambda c: ktc_ref[0, 0, c, 0], lambda c: vc_ref[0], ktl, vl,
               m_sc, l_sc, acc_sc, n_lat=n_lat, tk=tk)
    o0 = acc_sc[0] / l_sc[0]
    o1 = acc_sc[1] / l_sc[1]
    o = o0 - lam_ref[...] * o1
    o = o * lax.rsqrt(jnp.mean(o * o, axis=-1, keepdims=True) + EPS)
    o_ref[0] = (o * g_ref[...] * out_scale).astype(o_ref.dtype)


def diff_attention(q, ktc, vc, ktl, vl, lam, norm_g, out_scale, tq):
    b, t, _ = q.shape
    sc = vc.shape[1]
    tq = _tile(t, tq)
    n_lat, tk = (ktl.shape[3], ktl.shape[5]) if ktl is not None else (0, 0)
    in_specs = [pl.BlockSpec((1, tq, LANES), lambda i, h, j: (i, j, h)),
                pl.BlockSpec((1, 1, 2, 1, DIFF_DIM, sc), lambda i, h, j: (i, h, 0, 0, 0, 0)),
                pl.BlockSpec((1, sc, LANES), lambda i, h, j: (i, 0, h))]
    args = [q, ktc, vc]
    if n_lat:
        in_specs += [pl.BlockSpec((1, 1, 2, n_lat, DIFF_DIM, tk), lambda i, h, j: (i, h, 0, 0, 0, 0)),
                     pl.BlockSpec((1, n_lat * tk, LANES), lambda i, h, j: (i, 0, h))]
        args += [ktl, vl]
    vec = pl.BlockSpec((1, LANES), lambda i, h, j: (0, 0))
    in_specs += [vec, vec]
    args += [jnp.full((1, LANES), lam, F32), norm_g.reshape(1, LANES)]
    return pl.pallas_call(
        functools.partial(_diff_attn_kernel, n_lat=n_lat, tk=tk, out_scale=out_scale),
        grid=(b, DIFF_HEADS, t // tq),
        in_specs=in_specs,
        out_specs=pl.BlockSpec((1, tq, LANES), lambda i, h, j: (i, j, h)),
        out_shape=jax.ShapeDtypeStruct((b, t, DIFF_HEADS * DIFF_VDIM), BF16),
        scratch_shapes=[pltpu.VMEM((2, tq, LANES), F32), pltpu.VMEM((2, tq, LANES), F32),
                        pltpu.VMEM((2, tq, DIFF_VDIM), F32)],
        compiler_params=_cparams(("parallel", "parallel", "parallel")),
        name="diff_attn",
    )(*args)


def _gqa_attn_kernel(*refs, n_lat, tk):
    if n_lat:
        q_ref, ktc_ref, vc_ref, ktl_ref, vl_ref, o_ref, m_sc, l_sc, acc_sc = refs
        ktl = lambda c, i: ktl_ref[0, 0, i]
        vl = lambda c, i: vl_ref[0, 0, pl.ds(pl.multiple_of(i * tk, tk), tk), :]
    else:
        q_ref, ktc_ref, vc_ref, o_ref, m_sc, l_sc, acc_sc = refs
        ktl = vl = None
    _attn_body(q_ref, lambda c: ktc_ref[0, 0, 0], lambda c: vc_ref[0, 0], ktl, vl,
               m_sc, l_sc, acc_sc, n_lat=n_lat, tk=tk)
    o0 = acc_sc[0] / l_sc[0][:, :GQA_DIM]
    o1 = acc_sc[1] / l_sc[1][:, :GQA_DIM]
    o_ref[0] = jnp.concatenate([o0, o1], axis=-1).astype(o_ref.dtype)


def gqa_attention(q, ktc, vc, ktl, vl, tq):
    b, t, _ = q.shape
    sc = vc.shape[2]
    tq = _tile(t, tq)
    n_lat, tk = (ktl.shape[2], ktl.shape[4]) if ktl is not None else (0, 0)
    pairs = GQA_HEADS // 2
    grp = lambda h: h // (pairs // GQA_KV)
    in_specs = [pl.BlockSpec((1, tq, LANES), lambda i, h, j: (i, j, h)),
                pl.BlockSpec((1, 1, 1, GQA_DIM, sc), lambda i, h, j: (i, grp(h), 0, 0, 0)),
                pl.BlockSpec((1, 1, sc, GQA_DIM), lambda i, h, j: (i, grp(h), 0, 0))]
    args = [q, ktc, vc]
    if n_lat:
        in_specs += [pl.BlockSpec((1, 1, n_lat, GQA_DIM, tk), lambda i, h, j: (i, grp(h), 0, 0, 0)),
                     pl.BlockSpec((1, 1, n_lat * tk, GQA_DIM), lambda i, h, j: (i, grp(h), 0, 0))]
        args += [ktl, vl]
    return pl.pallas_call(
        functools.partial(_gqa_attn_kernel, n_lat=n_lat, tk=tk),
        grid=(b, pairs, t // tq),
        in_specs=in_specs,
        out_specs=pl.BlockSpec((1, tq, LANES), lambda i, h, j: (i, j, h)),
        out_shape=jax.ShapeDtypeStruct((b, t, GQA_HEADS * GQA_DIM), BF16),
        scratch_shapes=[pltpu.VMEM((2, tq, LANES), F32), pltpu.VMEM((2, tq, LANES), F32),
                        pltpu.VMEM((2, tq, GQA_DIM), F32)],
        compiler_params=_cparams(("parallel", "parallel", "parallel")),
        name="gqa_attn",
    )(*args)


ML_STEP = 256
HALO = 8


def _halo_rows(prev_ref, next_ref, j, nblk):
    prev = jnp.where(j > 0, prev_ref[0, HALO - 1:HALO, :], 0.0)
    nxt = jnp.where(j < nblk - 1, next_ref[0, 0:1, :], 0.0)
    return prev, nxt


def _conv3(x, prev, nxt, w, bias):
    tt = x.shape[0]
    row = lax.broadcasted_iota(jnp.int32, x.shape, 0)
    xm = jnp.where(row == 0, prev, pltpu.roll(x, 1, axis=0))
    xp = jnp.where(row == tt - 1, nxt, pltpu.roll(x, tt - 1, axis=0))
    return xm * w[0:1] + x * w[1:2] + xp * w[2:3] + bias


def _halo_specs(tt, width, col_block, t):
    nb = t // HALO
    prev = pl.BlockSpec((1, HALO, width), lambda i, j: (i, jnp.maximum(j * (tt // HALO) - 1, 0), col_block))
    nxt = pl.BlockSpec((1, HALO, width), lambda i, j: (i, jnp.minimum((j + 1) * (tt // HALO), nb - 1), col_block))
    return prev, nxt


def _ml_prep_kernel(q_ref, qp_ref, qn_ref, k_ref, kp_ref, kn_ref, g_ref, w_ref, b_ref, gb_ref, q_o, k_o, g_o):
    j = pl.program_id(1)
    nblk = pl.num_programs(1)
    w = ML_HEADS * ML_DIM
    qp, qn = _halo_rows(qp_ref, qn_ref, j, nblk)
    kp, kn = _halo_rows(kp_ref, kn_ref, j, nblk)
    q = _conv3(q_ref[0], qp, qn, w_ref[:, :w], b_ref[:, :w])
    k = _conv3(k_ref[0], kp, kn, w_ref[:, w:], b_ref[:, w:])
    q_o[0] = q * jax.nn.sigmoid(q)
    k_o[0] = (k * jax.nn.sigmoid(k)) * ML_DIM ** -0.5
    x = g_ref[0] + gb_ref[...]
    lane = lax.broadcasted_iota(jnp.int32, x.shape, 1)
    log_sig = jnp.minimum(x, 0.0) - jnp.log(1.0 + jnp.exp(-jnp.abs(x)))
    g_o[0] = jnp.where((lane % 8) >= ML_HEADS, log_sig, x)


def ml_prep(p, conv_w, conv_b, gate_b):
    b, t, _ = p.shape
    tt = _tile(t, 512)
    w = ML_HEADS * ML_DIM
    blk = lambda c: pl.BlockSpec((1, tt, w), lambda i, j, c=c: (i, j, c // w))
    qp, qn = _halo_specs(tt, w, C_MLQ // w, t)
    kp, kn = _halo_specs(tt, w, C_MLK // w, t)
    gb = jnp.zeros((1, LANES), F32).at[0, :4 * ML_HEADS].set(gate_b.reshape(-1))
    return pl.pallas_call(
        _ml_prep_kernel,
        grid=(b, t // tt),
        in_specs=[blk(C_MLQ), qp, qn, blk(C_MLK), kp, kn,
                  pl.BlockSpec((1, tt, LANES), lambda i, j: (i, j, C_MLG // LANES)),
                  pl.BlockSpec((3, 2 * w), lambda i, j: (0, 0)),
                  pl.BlockSpec((1, 2 * w), lambda i, j: (0, 0)),
                  pl.BlockSpec((1, LANES), lambda i, j: (0, 0))],
        out_specs=[pl.BlockSpec((1, tt, w), lambda i, j: (i, j, 0)),
                   pl.BlockSpec((1, tt, w), lambda i, j: (i, j, 0)),
                   pl.BlockSpec((1, tt, LANES), lambda i, j: (i, j, 0))],
        out_shape=[jax.ShapeDtypeStruct((b, t, w), F32), jax.ShapeDtypeStruct((b, t, w), F32),
                   jax.ShapeDtypeStruct((b, t, LANES), F32)],
        compiler_params=_cparams(("parallel", "parallel")),
        name="ml_prep",
    )(p, p, p, p, p, p, p, conv_w, conv_b.reshape(1, 2 * w), gb)


def _ml_chunk_head(q, k, v, li_c, bc_c, li_r, bc_r, b_last, tri, c_ref, n_ref, m_ref):
    m = m_ref[:, 0:1]
    c_mat = c_ref[...]
    n_row = n_ref[...]
    dmat = jnp.where(tri, bc_c - bc_r + li_r, -jnp.inf)
    inter = bc_c + m
    m_t = jnp.maximum(inter, jnp.max(dmat, axis=1, keepdims=True))
    w_intra = jnp.exp(dmat - m_t)
    w_inter = jnp.exp(inter - m_t)
    qb = q.astype(BF16)
    s = lax.dot_general(qb, k.astype(BF16), (((1,), (1,)), ((), ())), preferred_element_type=F32) * w_intra
    num = (jnp.dot(s.astype(BF16), v.astype(BF16), preferred_element_type=F32)
           + w_inter * jnp.dot(qb, c_mat.astype(BF16), preferred_element_type=F32))
    den = jnp.sum(s, axis=1, keepdims=True) + w_inter * jnp.sum(q * n_row, axis=1, keepdims=True)
    h = num / jnp.maximum(jnp.abs(den), jnp.exp(-m_t))
    g = b_last - bc_c + li_c
    m_new = jnp.maximum(b_last + m, jnp.max(g, axis=0, keepdims=True))
    kw = k * jnp.exp(g - m_new)
    wc = jnp.exp(b_last + m - m_new)
    c_ref[...] = wc * c_mat + jnp.dot(kw.T.astype(BF16), v.astype(BF16), preferred_element_type=F32)
    n_ref[...] = wc * n_row + jnp.sum(kw, axis=0, keepdims=True)
    m_ref[...] = jnp.broadcast_to(m_new, m_ref.shape)
    return h


def _ml_scan_kernel(q_ref, k_ref, v_ref, g_ref, c0_ref, n0_ref, m0_ref, h_o, c1_o, n1_o, m1_o,
                    c_sc, n_sc, m_sc, *, direction):
    j = pl.program_id(1)

    @pl.when(j == 0)
    def _():
        c_sc[...] = c0_ref[0]
        n_sc[...] = n0_ref[0]
        m_sc[...] = m0_ref[0]

    r = lax.broadcasted_iota(jnp.int32, (ML_CHUNK, ML_CHUNK), 0)
    s = lax.broadcasted_iota(jnp.int32, (ML_CHUNK, ML_CHUNK), 1)
    tri = (s >= r) if direction else (s <= r)
    tri_f = tri.astype(F32)
    n_chunks = q_ref.shape[1] // ML_CHUNK
    order = range(n_chunks - 1, -1, -1) if direction else range(n_chunks)
    last = 0 if direction else ML_CHUNK - 1
    for c in order:
        rows = slice(c * ML_CHUNK, (c + 1) * ML_CHUNK)
        gch = g_ref[0, rows, :]
        bc = jnp.dot(tri_f, gch, preferred_element_type=F32, precision=lax.Precision.HIGHEST)
        gt = gch.T
        bct = bc.T
        for hd in range(ML_HEADS):
            ci = 2 * ML_HEADS * direction + hd
            cf = ci + ML_HEADS
            cols = slice(hd * ML_DIM, (hd + 1) * ML_DIM)
            h = _ml_chunk_head(q_ref[0, rows, cols], k_ref[0, rows, cols], v_ref[0, rows, cols],
                               gch[:, ci:ci + 1], bc[:, cf:cf + 1], gt[ci:ci + 1, :], bct[cf:cf + 1, :],
                               bc[last:last + 1, cf:cf + 1], tri, c_sc.at[hd], n_sc.at[hd], m_sc.at[hd])
            h_o[0, rows, cols] = h

    @pl.when(j == pl.num_programs(1) - 1)
    def _():
        c1_o[0] = c_sc[...]
        n1_o[0] = n_sc[...]
        m1_o[0] = m_sc[...]


def ml_scan(q, k, p, g, state, direction):
    b, t, w = q.shape
    ts = _tile(t, ML_STEP)
    nst = t // ts
    tok = (lambda i, j: (i, nst - 1 - j, 0)) if direction else (lambda i, j: (i, j, 0))
    tokv = (lambda i, j: (i, nst - 1 - j, C_MLV // w)) if direction else (lambda i, j: (i, j, C_MLV // w))
    st_specs = [pl.BlockSpec((1, ML_HEADS, ML_DIM, ML_DIM), lambda i, j: (i, 0, 0, 0)),
                pl.BlockSpec((1, ML_HEADS, 1, ML_DIM), lambda i, j: (i, 0, 0, 0)),
                pl.BlockSpec((1, ML_HEADS, 1, ML_DIM), lambda i, j: (i, 0, 0, 0))]
    st_shapes = [jax.ShapeDtypeStruct((b, ML_HEADS, ML_DIM, ML_DIM), F32),
                 jax.ShapeDtypeStruct((b, ML_HEADS, 1, ML_DIM), F32),
                 jax.ShapeDtypeStruct((b, ML_HEADS, 1, ML_DIM), F32)]
    h, c1, n1, m1 = pl.pallas_call(
        functools.partial(_ml_scan_kernel, direction=direction),
        grid=(b, nst),
        in_specs=[pl.BlockSpec((1, ts, w), tok), pl.BlockSpec((1, ts, w), tok), pl.BlockSpec((1, ts, w), tokv),
                  pl.BlockSpec((1, ts, LANES), tok)] + st_specs,
        out_specs=[pl.BlockSpec((1, ts, w), tok)] + st_specs,
        out_shape=[jax.ShapeDtypeStruct((b, t, w), F32)] + st_shapes,
        scratch_shapes=[pltpu.VMEM((ML_HEADS, ML_DIM, ML_DIM), F32), pltpu.VMEM((ML_HEADS, 1, ML_DIM), F32),
                        pltpu.VMEM((ML_HEADS, 1, ML_DIM), F32)],
        compiler_params=_cparams(("parallel", "arbitrary")),
        name="ml_scan",
    )(q, k, p, g, *state)
    return h, (c1, n1, m1)


def mlstm_branch(p_c, p_l, conv_w, conv_b, gate_b, need_ctx):
    b = p_l.shape[0]
    qc, kc, gc = ml_prep(p_c, conv_w, conv_b, gate_b)
    ql, kl, gl = ml_prep(p_l, conv_w, conv_b, gate_b)
    zero = (jnp.zeros((b, ML_HEADS, ML_DIM, ML_DIM), F32), jnp.zeros((b, ML_HEADS, 1, ML_DIM), F32),
            jnp.zeros((b, ML_HEADS, 1, ML_DIM), F32))
    hs_c, hs_l = [], []
    for direction in (0, 1):
        hc, st = ml_scan(qc, kc, p_c, gc, zero, direction)
        hl, _ = ml_scan(ql, kl, p_l, gl, st, direction)
        hs_c.append(hc)
        hs_l.append(hl)
    return (tuple(hs_c) if need_ctx else None), tuple(hs_l)


def _merge_kernel(ya_ref, hf_ref, hb_ref, yc_ref, yd_ref, og_ref, g_ref, mg_ref, wb_ref, wo_ref, x_ref, gate_ref,
                  o_ref):
    d = x_ref.shape[-1]
    mg = mg_ref[...]
    yb = []
    for hd in range(ML_HEADS):
        cols = slice(hd * ML_DIM, (hd + 1) * ML_DIM)
        h = hf_ref[0, :, cols] + hb_ref[0, :, cols]
        h = h * lax.rsqrt(jnp.mean(h * h, axis=-1, keepdims=True) + EPS) * mg
        yb.append((h * jax.nn.sigmoid(og_ref[0, :, cols])).astype(BF16))
    ys = (ya_ref[0].astype(BF16), jnp.concatenate(yb, axis=-1), yc_ref[0].astype(BF16), yd_ref[0].astype(BF16))
    acc = None
    for n, y in enumerate(ys):
        t = jnp.dot(y, wb_ref[n], preferred_element_type=F32)
        t = jax.nn.sigmoid(g_ref[0, :, n * d:(n + 1) * d]) * t
        acc = t if acc is None else acc + t
    z = jnp.dot(acc.astype(BF16), wo_ref[...], preferred_element_type=F32)
    o_ref[0] = x_ref[0] + gate_ref[0] * z


def merge(ya, hs, yc, yd, p, ml_norm_g, w_branch, w_out, x, gate):
    b, t, d = x.shape
    tm = _tile(t, 256)
    w = ML_HEADS * ML_DIM
    ysp = pl.BlockSpec((1, tm, w), lambda i, j: (i, j, 0))
    return pl.pallas_call(
        _merge_kernel,
        grid=(b, t // tm),
        in_specs=[ysp, ysp, ysp, ysp, ysp,
                  pl.BlockSpec((1, tm, w), lambda i, j: (i, j, C_MLO // w)),
                  pl.BlockSpec((1, tm, N_BRANCH * d), lambda i, j: (i, j, C_GATE // (N_BRANCH * d))),
                  pl.BlockSpec((1, ML_DIM), lambda i, j: (0, 0)),
                  pl.BlockSpec((N_BRANCH, w, d), lambda i, j: (0, 0, 0)),
                  pl.BlockSpec((d, d), lambda i, j: (0, 0)),
                  pl.BlockSpec((1, tm, d), lambda i, j: (i, j, 0)),
                  pl.BlockSpec((1, 1, d), lambda i, j: (i, 0, 0))],
        out_specs=pl.BlockSpec((1, tm, d), lambda i, j: (i, j, 0)),
        out_shape=jax.ShapeDtypeStruct((b, t, d), F32),
        compiler_params=_cparams(("parallel", "parallel")),
        name="merge",
    )(ya, hs[0], hs[1], yc, yd, p, p, ml_norm_g.reshape(1, ML_DIM), w_branch, w_out, x, gate.reshape(b, 1, d))


def _expert_kernel(be_ref, x_ref, wg_ref, wu_ref, wd_ref, o_ref):
    x = x_ref[...]
    a = jnp.dot(x, wg_ref[0], preferred_element_type=F32)
    u = jnp.dot(x, wu_ref[0], preferred_element_type=F32)
    h = (a * jax.nn.sigmoid(a)) * u
    o_ref[...] = jnp.dot(h.astype(BF16), wd_ref[0], preferred_element_type=F32)


def expert_blocks(xb, blk_exp, w_gate, w_up, w_down):
    m, d = xb.shape
    hdim = w_gate.shape[-1]
    n_blocks = m // MOE_BLOCK
    grid_spec = pltpu.PrefetchScalarGridSpec(
        num_scalar_prefetch=1,
        grid=(n_blocks,),
        in_specs=[pl.BlockSpec((MOE_BLOCK, d), lambda i, be: (i, 0)),
                  pl.BlockSpec((1, d, hdim), lambda i, be: (be[i], 0, 0)),
                  pl.BlockSpec((1, d, hdim), lambda i, be: (be[i], 0, 0)),
                  pl.BlockSpec((1, hdim, d), lambda i, be: (be[i], 0, 0))],
        out_specs=pl.BlockSpec((MOE_BLOCK, d), lambda i, be: (i, 0)),
    )
    return pl.pallas_call(
        _expert_kernel,
        grid_spec=grid_spec,
        out_shape=jax.ShapeDtypeStruct((m, d), F32),
        compiler_params=_cparams(("arbitrary",)),
        name="moe_experts",
    )(blk_exp, xb, w_gate, w_up, w_down)


def _rope_tables(n_tok):
    rows = n_tok // GRID_W
    row = jnp.repeat(jnp.arange(rows, dtype=F32), GRID_W)
    col = jnp.broadcast_to(jnp.arange(GRID_W, dtype=F32), (rows, GRID_W)).reshape(-1)
    n_freq = DIFF_DIM // 4
    inv = ROPE_BASE ** (-jnp.arange(n_freq, dtype=F32) / n_freq)
    ar = row[:, None] * inv
    ac = col[:, None] * inv
    ang = jnp.concatenate([ar, ar, ac, ac], axis=-1)
    cos, sin = jnp.cos(ang), jnp.sin(ang)
    first = (jnp.arange(DIFF_DIM) % 32) < 16
    sa = jnp.where(first, -sin, 0.0)
    sb = jnp.where(first, 0.0, sin)
    return tuple(jnp.tile(a, (1, 2)) for a in (cos, sa, sb))


def _short_conv(x, w, bias):
    k, ch = w.shape
    y = lax.conv_general_dilated(x, w[:, None, :].astype(x.dtype), window_strides=(1,),
                                 padding=((k // 2, k // 2),), dimension_numbers=('NWC', 'WIO', 'NWC'),
                                 feature_group_count=ch)
    return y + bias


def _rmsnorm(x, g):
    y = x * lax.rsqrt(jnp.mean(x * x, axis=-1, keepdims=True) + EPS)
    return y * g


def _hyena_filters(length, w1, b1, w2, b2, w3, freq):
    t = jnp.arange(length, dtype=F32) / length
    bands = jnp.arange(1, FILTER_BANDS + 1, dtype=F32)
    ang = 2.0 * math.pi * t[:, None] * bands
    emb = jnp.concatenate([t[:, None], jnp.cos(ang), jnp.sin(ang)], axis=-1)
    a = jnp.sin(freq[0] * (emb @ w1 + b1))
    a = jnp.sin(freq[1] * (a @ w2 + b2))
    filt = (a @ w3).astype(F32).reshape(length, HY_ORDER, 2, HY_CH)
    alpha = jnp.linspace(abs(math.log(DECAY_TARGET)) / SLOW_DECAY_PCT,
                         abs(math.log(DECAY_TARGET)) / FAST_DECAY_PCT, HY_CH)
    window = jnp.exp(-t[:, None] * alpha) + FILTER_SHIFT
    filt = filt * window[:, None, None, :]
    k = jnp.concatenate([filt[:, :, 0], jnp.zeros((1, HY_ORDER, HY_CH), F32), filt[:0:-1, :, 1]], axis=0)
    return k * lax.rsqrt(jnp.sum(k * k, axis=0, keepdims=True) + EPS)


def _fft_long_conv(z, k):
    length = z.shape[1]
    zf = jnp.fft.rfft(z.astype(F32), n=2 * length, axis=1)
    kf = jnp.fft.rfft(k, axis=0)
    return jnp.fft.irfft(zf * kf[None], n=2 * length, axis=1)[:, :length]


def _hyena(p, conv_w, conv_b, w1, b1, w2, b2, w3, freq, bias):
    length = p.shape[1]
    u = _short_conv(p, conv_w, conv_b)
    parts = jnp.split(u, HY_ORDER + 1, axis=-1)
    k = _hyena_filters(length, w1, b1, w2, b2, w3, freq)
    z = parts[0]
    for n in range(HY_ORDER):
        z = parts[n + 1] * (_fft_long_conv(z, k[:, n]) + bias[n] * z)
    return z


def _hier_moe(h, w_group, b_group, w_router, b_router, w_gate, w_up, w_down):
    n_tok, d = h.shape
    g_logits = (h @ w_group + b_group).astype(F32)
    grp = jnp.argmax(g_logits, axis=-1)
    p_grp = jnp.max(jax.nn.softmax(g_logits, axis=-1), axis=-1, keepdims=True)
    e_logits = (h @ w_router + b_router).astype(F32).reshape(n_tok, MOE_GROUPS, MOE_EPG)
    e_in = e_logits[jnp.arange(n_tok), grp]
    top_v, top_i = lax.top_k(e_in, MOE_TOP_K)
    gates = p_grp * jax.nn.softmax(top_v, axis=-1)
    e_flat = (grp[:, None] * MOE_EPG + top_i).reshape(-1)
    m_slots = n_tok * MOE_TOP_K
    order = jnp.argsort(e_flat)
    e_s = e_flat[order]
    tok_s = order // MOE_TOP_K
    gate_s = gates.reshape(-1)[order]
    counts = jnp.bincount(e_flat, length=MOE_EXPERTS)
    padded = (counts + MOE_BLOCK - 1) // MOE_BLOCK * MOE_BLOCK
    start = jnp.cumsum(counts) - counts
    p_end = jnp.cumsum(padded)
    dest = (p_end - padded)[e_s] + jnp.arange(m_slots) - start[e_s]
    n_blocks = -(-(m_slots + MOE_EXPERTS * (MOE_BLOCK - 1)) // MOE_BLOCK)
    buf_tok = jnp.zeros((n_blocks * MOE_BLOCK,), jnp.int32).at[dest].set(tok_s.astype(jnp.int32))
    blk_exp = jnp.minimum(jnp.searchsorted(p_end, jnp.arange(n_blocks) * MOE_BLOCK, side='right'),
                          MOE_EXPERTS - 1).astype(jnp.int32)
    xb = h.astype(BF16)[buf_tok]
    yb = expert_blocks(xb, blk_exp, w_gate.astype(BF16), w_up.astype(BF16), w_down.astype(BF16))
    y_s = yb[dest] * gate_s[:, None]
    return jnp.zeros_like(h).at[tok_s].add(y_s)


def _permute_w_in(w):
    d = w.shape[0]
    sizes = (512, 512, 512, 1536, 512, 16, 512, 256, 1536, 4096)
    offs = np.cumsum((0,) + sizes)
    dq, dk, dv, mlqkv, mlo, mlg, gq, gkv, hy, gate = [w[:, offs[i]:offs[i + 1]] for i in range(10)]
    pad = jnp.zeros((d, N_P - C_MLG - 16), w.dtype)
    return jnp.concatenate([gate, dq, dk, dv, mlqkv, mlo, gq, hy, gkv, mlg, pad], axis=1)


def kernel(x, c, ctx, c_ctx, w_ada, b_ada, norm1_g, norm2_g, w_in, diff_lam, diff_norm_g, ml_conv_w, ml_conv_b, ml_gate_b, ml_norm_g, gqa_qnorm_g, gqa_knorm_g, hy_conv_w, hy_conv_b, hy_f_w1, hy_f_b1, hy_f_w2, hy_f_b2, hy_f_w3, hy_f_freq, hy_bias, w_branch, w_out, moe_w_group, moe_b_group, moe_w_router, moe_b_router, moe_w_gate, moe_w_up, moe_w_down, final_norm_g):
    b, n, d = x.shape
    n_ctx = ctx.shape[1]
    depth = w_in.shape[0]
    tk = _tile(n, 1024)
    tables = _rope_tables(n)
    sc = jax.nn.silu(c)
    scx = jax.nn.silu(c_ctx)
    xs, cs = x, ctx
    for l in range(depth):
        need_ctx = l < depth - 1
        mod_l = jnp.split(sc @ w_ada[l] + b_ada[l], 6, axis=-1)
        mod_c = [jnp.broadcast_to(m, (b, d)) for m in jnp.split(scx @ w_ada[l] + b_ada[l], 6, axis=-1)]
        w_p = _permute_w_in(w_in[l]).astype(BF16)
        hl = norm_mod(xs, norm1_g[l], mod_l[0], mod_l[1], BF16)
        hc = norm_mod(cs, norm1_g[l], mod_c[0], mod_c[1], BF16)
        p_l = matmul(hl.reshape(b * n, d), w_p, F32).reshape(b, n, N_P)
        p_c = matmul(hc.reshape(b * n_ctx, d), w_p, F32).reshape(b, n_ctx, N_P)

        dq_l, dkt_l, dv_l, gq_l, gkt_l, gv_l = attn_prep(p_l, tables, gqa_qnorm_g[l], gqa_knorm_g[l], tk)
        dq_c, dkt_c, dv_c, gq_c, gkt_c, gv_c = attn_prep(p_c, None, gqa_qnorm_g[l], gqa_knorm_g[l], n_ctx)
        lam_init = 0.8 - 0.6 * math.exp(-0.3 * l)
        lp = diff_lam[l].astype(F32)
        lam = jnp.exp(jnp.sum(lp[0] * lp[1])) - jnp.exp(jnp.sum(lp[2] * lp[3])) + lam_init
        yl_a = diff_attention(dq_l, dkt_c, dv_c, dkt_l, dv_l, lam, diff_norm_g[l], 1.0 - lam_init, 1024)
        yl_c = gqa_attention(gq_l, gkt_c, gv_c, gkt_l, gv_l, 1024)
        if need_ctx:
            yc_a = diff_attention(dq_c, dkt_c, dv_c, None, None, lam, diff_norm_g[l], 1.0 - lam_init, 256)
            yc_c = gqa_attention(gq_c, gkt_c, gv_c, None, None, 256)

        hs_c, hs_l = mlstm_branch(p_c, p_l, ml_conv_w[l], ml_conv_b[l], ml_gate_b[l], need_ctx)
        hy_args = (hy_conv_w[l], hy_conv_b[l], hy_f_w1[l], hy_f_b1[l], hy_f_w2[l], hy_f_b2[l],
                   hy_f_w3[l], hy_f_freq[l], hy_bias[l])
        yl_d = _hyena(p_l[..., C_HY:C_HY + 3 * HY_CH], *hy_args)

        wb = w_branch[l].astype(BF16)
        wo = w_out[l].astype(BF16)
        xs = merge(yl_a, hs_l, yl_c, yl_d, p_l, ml_norm_g[l], wb, wo, xs, mod_l[2])
        hl2 = norm_mod(xs, norm2_g[l], mod_l[3], mod_l[4], F32)
        moe_args = (moe_w_group[l], moe_b_group[l], moe_w_router[l], moe_b_router[l],
                    moe_w_gate[l], moe_w_up[l], moe_w_down[l])
        if need_ctx:
            yc_d = _hyena(p_c[..., C_HY:C_HY + 3 * HY_CH], *hy_args)
            cs = merge(yc_a, hs_c, yc_c, yc_d, p_c, ml_norm_g[l], wb, wo, cs, mod_c[2])
            hc2 = norm_mod(cs, norm2_g[l], mod_c[3], mod_c[4], F32)
            f = _hier_moe(jnp.concatenate([hc2.reshape(-1, d), hl2.reshape(-1, d)], axis=0), *moe_args)
            cs = cs + mod_c[5][:, None, :] * f[:b * n_ctx].reshape(b, n_ctx, d)
            f_lat = f[b * n_ctx:]
        else:
            f_lat = _hier_moe(hl2.reshape(-1, d), *moe_args)
        xs = xs + mod_l[5][:, None, :] * f_lat.reshape(b, n, d)
    zero = jnp.zeros((b, d), F32)
    return norm_mod(xs, final_norm_g, zero, zero, F32)
```

```python
import functools
import math

import jax
import jax.numpy as jnp
import numpy as np
from jax import lax
from jax.experimental import pallas as pl
from jax.experimental.pallas import tpu as pltpu

F32 = jnp.float32
BF16 = jnp.bfloat16

EPS = 1e-6
ROPE_BASE = 10000.0
GRID_W = 64

DIFF_HEADS = 4
DIFF_DIM = 64
DIFF_VDIM = 128
ML_HEADS = 4
ML_DIM = 128
ML_CHUNK = 64
GQA_HEADS = 8
GQA_KV = 2
GQA_DIM = 64
HY_CH = 512
HY_ORDER = 2
FILTER_BANDS = 16
FILTER_SHIFT = 0.05
DECAY_TARGET = 1e-2
FAST_DECAY_PCT = 0.3
SLOW_DECAY_PCT = 1.5
N_BRANCH = 4
MOE_GROUPS = 4
MOE_EPG = 8
MOE_EXPERTS = MOE_GROUPS * MOE_EPG
MOE_TOP_K = 2
MOE_BLOCK = 128

LANES = 128
VMEM_LIMIT = 48 * 1024 * 1024

C_GATE = 0
C_DQ = 4096
C_DK = 4608
C_DV = 5120
C_MLQ = 5632
C_MLK = 6144
C_MLV = 6656
C_MLO = 7168
C_GQ = 7680
C_HY = 8192
C_GK = 9728
C_GV = 9856
C_MLG = 9984
N_P = 10240

QSCALE = (DIFF_DIM ** -0.5) * math.log2(math.e)


def _cparams(sem):
    return pltpu.CompilerParams(dimension_semantics=sem, vmem_limit_bytes=VMEM_LIMIT)


def _tile(n, target):
    if n <= target:
        return n
    for t in range(target, 7, -1):
        if n % t == 0 and t % 8 == 0:
            return t
    return n


def _norm_mod_kernel(x_ref, g_ref, sh_ref, sc_ref, o_ref):
    x = x_ref[0]
    y = x * lax.rsqrt(jnp.mean(x * x, axis=-1, keepdims=True) + EPS)
    y = y * g_ref[...]
    o_ref[0] = (y * (1.0 + sc_ref[0]) + sh_ref[0]).astype(o_ref.dtype)


def norm_mod(x, g, shift, scale, out_dtype):
    b, t, d = x.shape
    tt = _tile(t, 512)
    return pl.pallas_call(
        _norm_mod_kernel,
        grid=(b, t // tt),
        in_specs=[pl.BlockSpec((1, tt, d), lambda i, j: (i, j, 0)),
                  pl.BlockSpec((1, d), lambda i, j: (0, 0)),
                  pl.BlockSpec((1, 1, d), lambda i, j: (i, 0, 0)),
                  pl.BlockSpec((1, 1, d), lambda i, j: (i, 0, 0))],
        out_specs=pl.BlockSpec((1, tt, d), lambda i, j: (i, j, 0)),
        out_shape=jax.ShapeDtypeStruct((b, t, d), out_dtype),
        compiler_params=_cparams(("parallel", "parallel")),
        name="norm_mod",
    )(x, g.reshape(1, d), shift.reshape(b, 1, d), scale.reshape(b, 1, d))


def _mm_kernel(a_ref, w_ref, o_ref):
    o_ref[...] = jnp.dot(a_ref[...], w_ref[...], preferred_element_type=F32).astype(o_ref.dtype)


def matmul(a, w, out_dtype, tm=512, tn=1024):
    m, k = a.shape
    n = w.shape[1]
    tm = _tile(m, tm)
    tn = _tile(n, tn)
    return pl.pallas_call(
        _mm_kernel,
        grid=(m // tm, n // tn),
        in_specs=[pl.BlockSpec((tm, k), lambda i, j: (i, 0)),
                  pl.BlockSpec((k, tn), lambda i, j: (0, j))],
        out_specs=pl.BlockSpec((tm, tn), lambda i, j: (i, j)),
        out_shape=jax.ShapeDtypeStruct((m, n), out_dtype),
        compiler_params=_cparams(("parallel", "parallel")),
        name="matmul",
    )(a, w)


def _rope(x, cos, sa, sb):
    xa = pltpu.roll(x, LANES - 16, axis=1)
    xb = pltpu.roll(x, 16, axis=1)
    return x * cos + xa * sa + xb * sb


def _seg_rmsnorm(x, g):
    lane = lax.broadcasted_iota(jnp.int32, x.shape, 1)
    lo = lane < GQA_DIM
    ss = x * x
    s_lo = jnp.sum(jnp.where(lo, ss, 0.0), axis=-1, keepdims=True)
    s_hi = jnp.sum(jnp.where(lo, 0.0, ss), axis=-1, keepdims=True)
    r = jnp.where(lo, lax.rsqrt(s_lo * (1.0 / GQA_DIM) + EPS), lax.rsqrt(s_hi * (1.0 / GQA_DIM) + EPS))
    return x * r * g


def _prep_kernel(dq_ref, dk_ref, dv_ref, gq_ref, gkv_ref, cos_ref, sa_ref, sb_ref, qg_ref, kg_ref,
                 dq_o, dkt_o, dv_o, gq_o, gkt_o, gv_o, *, rope):
    if rope:
        cos, sa, sb = cos_ref[...], sa_ref[...], sb_ref[...]
        rot = lambda x: _rope(x, cos, sa, sb)
    else:
        rot = lambda x: x
    qg = qg_ref[...]
    kg = kg_ref[...]
    for j in range(DIFF_HEADS):
        sl = slice(j * LANES, (j + 1) * LANES)
        dq_o[0, :, sl] = (rot(dq_ref[0, :, sl]) * QSCALE).astype(BF16)
        kt = rot(dk_ref[0, :, sl]).T
        dkt_o[0, j, 0, 0] = kt[:DIFF_DIM].astype(BF16)
        dkt_o[0, j, 1, 0] = kt[DIFF_DIM:].astype(BF16)
        gq_o[0, :, sl] = (rot(_seg_rmsnorm(gq_ref[0, :, sl], qg)) * QSCALE).astype(BF16)
    dv_o[0] = dv_ref[0].astype(BF16)
    kt = rot(_seg_rmsnorm(gkv_ref[0, :, :LANES], kg)).T
    gkt_o[0, 0, 0] = kt[:GQA_DIM].astype(BF16)
    gkt_o[0, 1, 0] = kt[GQA_DIM:].astype(BF16)
    v = gkv_ref[0, :, LANES:].astype(BF16)
    gv_o[0, 0] = v[:, :GQA_DIM]
    gv_o[0, 1] = v[:, GQA_DIM:]


def attn_prep(p, tables, q_g, k_g, tk):
    b, t, _ = p.shape
    rope = tables is not None
    if rope:
        cos, sa, sb = tables
    else:
        cos = sa = sb = jnp.zeros((t, LANES), F32)
    nck = t // tk
    tp = _tile(tk, 512)
    sub = tk // tp
    w512 = lambda c: pl.BlockSpec((1, tp, 512), lambda i, j, c=c: (i, j, c // 512))
    tab = pl.BlockSpec((tp, LANES), lambda i, j: (j, 0))
    vec = pl.BlockSpec((1, LANES), lambda i, j: (0, 0))
    outs = pl.pallas_call(
        functools.partial(_prep_kernel, rope=rope),
        grid=(b, t // tp),
        in_specs=[w512(C_DQ), w512(C_DK), w512(C_DV), w512(C_GQ),
                  pl.BlockSpec((1, tp, 256), lambda i, j: (i, j, C_GK // 256)),
                  tab, tab, tab, vec, vec],
        out_specs=[pl.BlockSpec((1, tp, 512), lambda i, j: (i, j, 0)),
                   pl.BlockSpec((1, DIFF_HEADS, 2, 1, DIFF_DIM, tp), lambda i, j: (i, 0, 0, j // sub, 0, j % sub)),
                   pl.BlockSpec((1, tp, 512), lambda i, j: (i, j, 0)),
                   pl.BlockSpec((1, tp, 512), lambda i, j: (i, j, 0)),
                   pl.BlockSpec((1, GQA_KV, 1, GQA_DIM, tp), lambda i, j: (i, 0, j // sub, 0, j % sub)),
                   pl.BlockSpec((1, GQA_KV, tp, GQA_DIM), lambda i, j: (i, 0, j, 0))],
        out_shape=[jax.ShapeDtypeStruct((b, t, 512), BF16),
                   jax.ShapeDtypeStruct((b, DIFF_HEADS, 2, nck, DIFF_DIM, tk), BF16),
                   jax.ShapeDtypeStruct((b, t, 512), BF16),
                   jax.ShapeDtypeStruct((b, t, 512), BF16),
                   jax.ShapeDtypeStruct((b, GQA_KV, nck, GQA_DIM, tk), BF16),
                   jax.ShapeDtypeStruct((b, GQA_KV, t, GQA_DIM), BF16)],
        compiler_params=_cparams(("parallel", "parallel")),
        name="attn_prep",
    )(p, p, p, p, p, cos, sa, sb,
      jnp.tile(q_g, 2).reshape(1, LANES), jnp.tile(k_g, 2).reshape(1, LANES))
    return outs


def _flash_step(q, kt, v, m_ref, l_ref, acc_ref):
    s = jnp.dot(q, kt, preferred_element_type=F32)
    tk = s.shape[1]
    m_prev = m_ref[...]
    m_next = jnp.maximum(m_prev, jnp.max(s, axis=1, keepdims=True))
    alpha = jnp.exp2(m_prev - m_next)
    p = jnp.exp2(s - jnp.concatenate([m_next] * (tk // LANES), axis=1))
    l_ref[...] = alpha * l_ref[...] + jnp.sum(p, axis=1, keepdims=True)
    dv = acc_ref.shape[-1]
    acc_ref[...] = acc_ref[...] * alpha[:, :dv] + jnp.dot(p.astype(BF16), v, preferred_element_type=F32)
    m_ref[...] = m_next


def _attn_body(q_ref, ktc, vc, ktl, vl, m_sc, l_sc, acc_sc, *, n_lat, tk):
    m_sc[...] = jnp.full(m_sc.shape, -jnp.inf, F32)
    l_sc[...] = jnp.zeros(l_sc.shape, F32)
    acc_sc[...] = jnp.zeros(acc_sc.shape, F32)
    q = q_ref[0]
    qs = (q[:, :DIFF_DIM], q[:, DIFF_DIM:])
    for c in range(2):
        _flash_step(qs[c], ktc(c), vc(c), m_sc.at[c], l_sc.at[c], acc_sc.at[c])
    if n_lat:
        def body(i, carry):
            for c in range(2):
                _flash_step(qs[c], ktl(c, i), vl(c, i), m_sc.at[c], l_sc.at[c], acc_sc.at[c])
            return carry
        lax.fori_loop(0, n_lat, body, 0)


def _diff_attn_kernel(*refs, n_lat, tk, out_scale):
    if n_lat:
        q_ref, ktc_ref, vc_ref, ktl_ref, vl_ref, lam_ref, g_ref, o_ref, m_sc, l_sc, acc_sc = refs
        ktl = lambda c, i: ktl_ref[0, 0, c, i]
        vl = lambda c, i: vl_ref[0, pl.ds(pl.multiple_of(i * tk, tk), tk), :]
    else:
        q_ref, ktc_ref, vc_ref, lam_ref, g_ref, o_ref, m_sc, l_sc, acc_sc = refs
        ktl = vl = None
    _attn_body(q_ref, lambda c: ktc_ref[0, 0, c, 0], lambda c: vc_ref[0], ktl, vl,
               m_sc, l_sc, acc_sc, n_lat=n_lat, tk=tk)
    o0 = acc_sc[0] / l_sc[0]
    o1 = acc_sc[1] / l_sc[1]
    o = o0 - lam_ref[...] * o1
    o = o * lax.rsqrt(jnp.mean(o * o, axis=-1, keepdims=True) + EPS)
    o_ref[0] = (o * g_ref[...] * out_scale).astype(o_ref.dtype)


def diff_attention(q, ktc, vc, ktl, vl, lam, norm_g, out_scale, tq):
    b, t, _ = q.shape
    sc = vc.shape[1]
    tq = _tile(t, tq)
    n_lat, tk = (ktl.shape[3], ktl.shape[5]) if ktl is not None else (0, 0)
    in_specs = [pl.BlockSpec((1, tq, LANES), lambda i, h, j: (i, j, h)),
                pl.BlockSpec((1, 1, 2, 1, DIFF_DIM, sc), lambda i, h, j: (i, h, 0, 0, 0, 0)),
                pl.BlockSpec((1, sc, LANES), lambda i, h, j: (i, 0, h))]
    args = [q, ktc, vc]
    if n_lat:
        in_specs += [pl.BlockSpec((1, 1, 2, n_lat, DIFF_DIM, tk), lambda i, h, j: (i, h, 0, 0, 0, 0)),
                     pl.BlockSpec((1, n_lat * tk, LANES), lambda i, h, j: (i, 0, h))]
        args += [ktl, vl]
    vec = pl.BlockSpec((1, LANES), lambda i, h, j: (0, 0))
    in_specs += [vec, vec]
    args += [jnp.full((1, LANES), lam, F32), norm_g.reshape(1, LANES)]
    return pl.pallas_call(
        functools.partial(_diff_attn_kernel, n_lat=n_lat, tk=tk, out_scale=out_scale),
        grid=(b, DIFF_HEADS, t // tq),
        in_specs=in_specs,
        out_specs=pl.BlockSpec((1, tq, LANES), lambda i, h, j: (i, j, h)),
        out_shape=jax.ShapeDtypeStruct((b, t, DIFF_HEADS * DIFF_VDIM), BF16),
        scratch_shapes=[pltpu.VMEM((2, tq, LANES), F32), pltpu.VMEM((2, tq, LANES), F32),
                        pltpu.VMEM((2, tq, DIFF_VDIM), F32)],
        compiler_params=_cparams(("parallel", "parallel", "parallel")),
        name="diff_attn",
    )(*args)


def _gqa_attn_kernel(*refs, n_lat, tk):
    if n_lat:
        q_ref, ktc_ref, vc_ref, ktl_ref, vl_ref, o_ref, m_sc, l_sc, acc_sc = refs
        ktl = lambda c, i: ktl_ref[0, 0, i]
        vl = lambda c, i: vl_ref[0, 0, pl.ds(pl.multiple_of(i * tk, tk), tk), :]
    else:
        q_ref, ktc_ref, vc_ref, o_ref, m_sc, l_sc, acc_sc = refs
        ktl = vl = None
    _attn_body(q_ref, lambda c: ktc_ref[0, 0, 0], lambda c: vc_ref[0, 0], ktl, vl,
               m_sc, l_sc, acc_sc, n_lat=n_lat, tk=tk)
    o0 = acc_sc[0] / l_sc[0][:, :GQA_DIM]
    o1 = acc_sc[1] / l_sc[1][:, :GQA_DIM]
    o_ref[0] = jnp.concatenate([o0, o1], axis=-1).astype(o_ref.dtype)


def gqa_attention(q, ktc, vc, ktl, vl, tq):
    b, t, _ = q.shape
    sc = vc.shape[2]
    tq = _tile(t, tq)
    n_lat, tk = (ktl.shape[2], ktl.shape[4]) if ktl is not None else (0, 0)
    pairs = GQA_HEADS // 2
    grp = lambda h: h // (pairs // GQA_KV)
    in_specs = [pl.BlockSpec((1, tq, LANES), lambda i, h, j: (i, j, h)),
                pl.BlockSpec((1, 1, 1, GQA_DIM, sc), lambda i, h, j: (i, grp(h), 0, 0, 0)),
                pl.BlockSpec((1, 1, sc, GQA_DIM), lambda i, h, j: (i, grp(h), 0, 0))]
    args = [q, ktc, vc]
    if n_lat:
        in_specs += [pl.BlockSpec((1, 1, n_lat, GQA_DIM, tk), lambda i, h, j: (i, grp(h), 0, 0, 0)),
                     pl.BlockSpec((1, 1, n_lat * tk, GQA_DIM), lambda i, h, j: (i, grp(h), 0, 0))]
        args += [ktl, vl]
    return pl.pallas_call(
        functools.partial(_gqa_attn_kernel, n_lat=n_lat, tk=tk),
        grid=(b, pairs, t // tq),
        in_specs=in_specs,
        out_specs=pl.BlockSpec((1, tq, LANES), lambda i, h, j: (i, j, h)),
        out_shape=jax.ShapeDtypeStruct((b, t, GQA_HEADS * GQA_DIM), BF16),
        scratch_shapes=[pltpu.VMEM((2, tq, LANES), F32), pltpu.VMEM((2, tq, LANES), F32),
                        pltpu.VMEM((2, tq, GQA_DIM), F32)],
        compiler_params=_cparams(("parallel", "parallel", "parallel")),
        name="gqa_attn",
    )(*args)


ML_STEP = 256
HALO = 8


def _halo_rows(prev_ref, next_ref, j, nblk):
    prev = jnp.where(j > 0, prev_ref[0, HALO - 1:HALO, :], 0.0)
    nxt = jnp.where(j < nblk - 1, next_ref[0, 0:1, :], 0.0)
    return prev, nxt


def _conv3(x, prev, nxt, w, bias):
    tt = x.shape[0]
    row = lax.broadcasted_iota(jnp.int32, x.shape, 0)
    xm = jnp.where(row == 0, prev, pltpu.roll(x, 1, axis=0))
    xp = jnp.where(row == tt - 1, nxt, pltpu.roll(x, tt - 1, axis=0))
    return xm * w[0:1] + x * w[1:2] + xp * w[2:3] + bias


def _halo_specs(tt, width, col_block, t):
    nb = t // HALO
    prev = pl.BlockSpec((1, HALO, width), lambda i, j: (i, jnp.maximum(j * (tt // HALO) - 1, 0), col_block))
    nxt = pl.BlockSpec((1, HALO, width), lambda i, j: (i, jnp.minimum((j + 1) * (tt // HALO), nb - 1), col_block))
    return prev, nxt


def _ml_prep_kernel(q_ref, qp_ref, qn_ref, k_ref, kp_ref, kn_ref, g_ref, w_ref, b_ref, gb_ref, q_o, k_o, g_o):
    j = pl.program_id(1)
    nblk = pl.num_programs(1)
    w = ML_HEADS * ML_DIM
    qp, qn = _halo_rows(qp_ref, qn_ref, j, nblk)
    kp, kn = _halo_rows(kp_ref, kn_ref, j, nblk)
    q = _conv3(q_ref[0], qp, qn, w_ref[:, :w], b_ref[:, :w])
    k = _conv3(k_ref[0], kp, kn, w_ref[:, w:], b_ref[:, w:])
    q_o[0] = q * jax.nn.sigmoid(q)
    k_o[0] = (k * jax.nn.sigmoid(k)) * ML_DIM ** -0.5
    x = g_ref[0] + gb_ref[...]
    lane = lax.broadcasted_iota(jnp.int32, x.shape, 1)
    log_sig = jnp.minimum(x, 0.0) - jnp.log(1.0 + jnp.exp(-jnp.abs(x)))
    g_o[0] = jnp.where((lane % 8) >= ML_HEADS, log_sig, x)


def ml_prep(p, conv_w, conv_b, gate_b):
    b, t, _ = p.shape
    tt = _tile(t, 512)
    w = ML_HEADS * ML_DIM
    blk = lambda c: pl.BlockSpec((1, tt, w), lambda i, j, c=c: (i, j, c // w))
    qp, qn = _halo_specs(tt, w, C_MLQ // w, t)
    kp, kn = _halo_specs(tt, w, C_MLK // w, t)
    gb = jnp.zeros((1, LANES), F32).at[0, :4 * ML_HEADS].set(gate_b.reshape(-1))
    return pl.pallas_call(
        _ml_prep_kernel,
        grid=(b, t // tt),
        in_specs=[blk(C_MLQ), qp, qn, blk(C_MLK), kp, kn,
                  pl.BlockSpec((1, tt, LANES), lambda i, j: (i, j, C_MLG // LANES)),
                  pl.BlockSpec((3, 2 * w), lambda i, j: (0, 0)),
                  pl.BlockSpec((1, 2 * w), lambda i, j: (0, 0)),
                  pl.BlockSpec((1, LANES), lambda i, j: (0, 0))],
        out_specs=[pl.BlockSpec((1, tt, w), lambda i, j: (i, j, 0)),
                   pl.BlockSpec((1, tt, w), lambda i, j: (i, j, 0)),
                   pl.BlockSpec((1, tt, LANES), lambda i, j: (i, j, 0))],
        out_shape=[jax.ShapeDtypeStruct((b, t, w), F32), jax.ShapeDtypeStruct((b, t, w), F32),
                   jax.ShapeDtypeStruct((b, t, LANES), F32)],
        compiler_params=_cparams(("parallel", "parallel")),
        name="ml_prep",
    )(p, p, p, p, p, p, p, conv_w, conv_b.reshape(1, 2 * w), gb)


def _ml_chunk_head(q, k, v, li_c, bc_c, li_r, bc_r, b_last, tri, c_ref, n_ref, m_ref):
    m = m_ref[:, 0:1]
    c_mat = c_ref[...]
    n_row = n_ref[...]
    dmat = jnp.where(tri, bc_c - bc_r + li_r, -jnp.inf)
    inter = bc_c + m
    m_t = jnp.maximum(inter, jnp.max(dmat, axis=1, keepdims=True))
    w_intra = jnp.exp(dmat - m_t)
    w_inter = jnp.exp(inter - m_t)
    qb = q.astype(BF16)
    s = lax.dot_general(qb, k.astype(BF16), (((1,), (1,)), ((), ())), preferred_element_type=F32) * w_intra
    num = (jnp.dot(s.astype(BF16), v.astype(BF16), preferred_element_type=F32)
           + w_inter * jnp.dot(qb, c_mat.astype(BF16), preferred_element_type=F32))
    den = jnp.sum(s, axis=1, keepdims=True) + w_inter * jnp.sum(q * n_row, axis=1, keepdims=True)
    h = num / jnp.maximum(jnp.abs(den), jnp.exp(-m_t))
    g = b_last - bc_c + li_c
    m_new = jnp.maximum(b_last + m, jnp.max(g, axis=0, keepdims=True))
    kw = k * jnp.exp(g - m_new)
    wc = jnp.exp(b_last + m - m_new)
    c_ref[...] = wc * c_mat + jnp.dot(kw.T.astype(BF16), v.astype(BF16), preferred_element_type=F32)
    n_ref[...] = wc * n_row + jnp.sum(kw, axis=0, keepdims=True)
    m_ref[...] = jnp.broadcast_to(m_new, m_ref.shape)
    return h


def _ml_scan_kernel(q_ref, k_ref, v_ref, g_ref, c0_ref, n0_ref, m0_ref, h_o, c1_o, n1_o, m1_o,
                    c_sc, n_sc, m_sc, *, direction):
    j = pl.program_id(1)

    @pl.when(j == 0)
    def _():
        c_sc[...] = c0_ref[0]
        n_sc[...] = n0_ref[0]
        m_sc[...] = m0_ref[0]

    r = lax.broadcasted_iota(jnp.int32, (ML_CHUNK, ML_CHUNK), 0)
    s = lax.broadcasted_iota(jnp.int32, (ML_CHUNK, ML_CHUNK), 1)
    tri = (s >= r) if direction else (s <= r)
    tri_f = tri.astype(F32)
    n_chunks = q_ref.shape[1] // ML_CHUNK
    order = range(n_chunks - 1, -1, -1) if direction else range(n_chunks)
    last = 0 if direction else ML_CHUNK - 1
    for c in order:
        rows = slice(c * ML_CHUNK, (c + 1) * ML_CHUNK)
        gch = g_ref[0, rows, :]
        bc = jnp.dot(tri_f, gch, preferred_element_type=F32, precision=lax.Precision.HIGHEST)
        gt = gch.T
        bct = bc.T
        for hd in range(ML_HEADS):
            ci = 2 * ML_HEADS * direction + hd
            cf = ci + ML_HEADS
            cols = slice(hd * ML_DIM, (hd + 1) * ML_DIM)
            h = _ml_chunk_head(q_ref[0, rows, cols], k_ref[0, rows, cols], v_ref[0, rows, cols],
                               gch[:, ci:ci + 1], bc[:, cf:cf + 1], gt[ci:ci + 1, :], bct[cf:cf + 1, :],
                               bc[last:last + 1, cf:cf + 1], tri, c_sc.at[hd], n_sc.at[hd], m_sc.at[hd])
            h_o[0, rows, cols] = h

    @pl.when(j == pl.num_programs(1) - 1)
    def _():
        c1_o[0] = c_sc[...]
        n1_o[0] = n_sc[...]
        m1_o[0] = m_sc[...]


def ml_scan(q, k, p, g, state, direction):
    b, t, w = q.shape
    ts = _tile(t, ML_STEP)
    nst = t // ts
    tok = (lambda i, j: (i, nst - 1 - j, 0)) if direction else (lambda i, j: (i, j, 0))
    tokv = (lambda i, j: (i, nst - 1 - j, C_MLV // w)) if direction else (lambda i, j: (i, j, C_MLV // w))
    st_specs = [pl.BlockSpec((1, ML_HEADS, ML_DIM, ML_DIM), lambda i, j: (i, 0, 0, 0)),
                pl.BlockSpec((1, ML_HEADS, 1, ML_DIM), lambda i, j: (i, 0, 0, 0)),
                pl.BlockSpec((1, ML_HEADS, 1, ML_DIM), lambda i, j: (i, 0, 0, 0))]
    st_shapes = [jax.ShapeDtypeStruct((b, ML_HEADS, ML_DIM, ML_DIM), F32),
                 jax.ShapeDtypeStruct((b, ML_HEADS, 1, ML_DIM), F32),
                 jax.ShapeDtypeStruct((b, ML_HEADS, 1, ML_DIM), F32)]
    h, c1, n1, m1 = pl.pallas_call(
        functools.partial(_ml_scan_kernel, direction=direction),
        grid=(b, nst),
        in_specs=[pl.BlockSpec((1, ts, w), tok), pl.BlockSpec((1, ts, w), tok), pl.BlockSpec((1, ts, w), tokv),
                  pl.BlockSpec((1, ts, LANES), tok)] + st_specs,
        out_specs=[pl.BlockSpec((1, ts, w), tok)] + st_specs,
        out_shape=[jax.ShapeDtypeStruct((b, t, w), F32)] + st_shapes,
        scratch_shapes=[pltpu.VMEM((ML_HEADS, ML_DIM, ML_DIM), F32), pltpu.VMEM((ML_HEADS, 1, ML_DIM), F32),
                        pltpu.VMEM((ML_HEADS, 1, ML_DIM), F32)],
        compiler_params=_cparams(("parallel", "arbitrary")),
        name="ml_scan",
    )(q, k, p, g, *state)
    return h, (c1, n1, m1)


def mlstm_branch(p_c, p_l, conv_w, conv_b, gate_b, need_ctx):
    b = p_l.shape[0]
    qc, kc, gc = ml_prep(p_c, conv_w, conv_b, gate_b)
    ql, kl, gl = ml_prep(p_l, conv_w, conv_b, gate_b)
    zero = (jnp.zeros((b, ML_HEADS, ML_DIM, ML_DIM), F32), jnp.zeros((b, ML_HEADS, 1, ML_DIM), F32),
            jnp.zeros((b, ML_HEADS, 1, ML_DIM), F32))
    hs_c, hs_l = [], []
    for direction in (0, 1):
        hc, st = ml_scan(qc, kc, p_c, gc, zero, direction)
        hl, _ = ml_scan(ql, kl, p_l, gl, st, direction)
        hs_c.append(hc)
        hs_l.append(hl)
    return (tuple(hs_c) if need_ctx else None), tuple(hs_l)


def _merge_kernel(ya_ref, hf_ref, hb_ref, yc_ref, yd_ref, og_ref, g_ref, mg_ref, wb_ref, wo_ref, x_ref, gate_ref,
                  o_ref):
    d = x_ref.shape[-1]
    mg = mg_ref[...]
    yb = []
    for hd in range(ML_HEADS):
        cols = slice(hd * ML_DIM, (hd + 1) * ML_DIM)
        h = hf_ref[0, :, cols] + hb_ref[0, :, cols]
        h = h * lax.rsqrt(jnp.mean(h * h, axis=-1, keepdims=True) + EPS) * mg
        yb.append((h * jax.nn.sigmoid(og_ref[0, :, cols])).astype(BF16))
    ys = (ya_ref[0].astype(BF16), jnp.concatenate(yb, axis=-1), yc_ref[0].astype(BF16), yd_ref[0].astype(BF16))
    acc = None
    for n, y in enumerate(ys):
        t = jnp.dot(y, wb_ref[n], preferred_element_type=F32)
        t = jax.nn.sigmoid(g_ref[0, :, n * d:(n + 1) * d]) * t
        acc = t if acc is None else acc + t
    z = jnp.dot(acc.astype(BF16), wo_ref[...], preferred_element_type=F32)
    o_ref[0] = x_ref[0] + gate_ref[0] * z


def merge(ya, hs, yc, yd, p, ml_norm_g, w_branch, w_out, x, gate):
    b, t, d = x.shape
    tm = _tile(t, 256)
    w = ML_HEADS * ML_DIM
    ysp = pl.BlockSpec((1, tm, w), lambda i, j: (i, j, 0))
    return pl.pallas_call(
        _merge_kernel,
        grid=(b, t // tm),
        in_specs=[ysp, ysp, ysp, ysp, ysp,
                  pl.BlockSpec((1, tm, w), lambda i, j: (i, j, C_MLO // w)),
                  pl.BlockSpec((1, tm, N_BRANCH * d), lambda i, j: (i, j, C_GATE // (N_BRANCH * d))),
                  pl.BlockSpec((1, ML_DIM), lambda i, j: (0, 0)),
                  pl.BlockSpec((N_BRANCH, w, d), lambda i, j: (0, 0, 0)),
                  pl.BlockSpec((d, d), lambda i, j: (0, 0)),
                  pl.BlockSpec((1, tm, d), lambda i, j: (i, j, 0)),
                  pl.BlockSpec((1, 1, d), lambda i, j: (i, 0, 0))],
        out_specs=pl.BlockSpec((1, tm, d), lambda i, j: (i, j, 0)),
        out_shape=jax.ShapeDtypeStruct((b, t, d), F32),
        compiler_params=_cparams(("parallel", "parallel")),
        name="merge",
    )(ya, hs[0], hs[1], yc, yd, p, p, ml_norm_g.reshape(1, ML_DIM), w_branch, w_out, x, gate.reshape(b, 1, d))


def _expert_kernel(be_ref, x_ref, wg_ref, wu_ref, wd_ref, o_ref):
    x = x_ref[...]
    a = jnp.dot(x, wg_ref[0], preferred_element_type=F32)
    u = jnp.dot(x, wu_ref[0], preferred_element_type=F32)
    h = (a * jax.nn.sigmoid(a)) * u
    o_ref[...] = jnp.dot(h.astype(BF16), wd_ref[0], preferred_element_type=F32)


def expert_blocks(xb, blk_exp, w_gate, w_up, w_down):
    m, d = xb.shape
    hdim = w_gate.shape[-1]
    n_blocks = m // MOE_BLOCK
    grid_spec = pltpu.PrefetchScalarGridSpec(
        num_scalar_prefetch=1,
        grid=(n_blocks,),
        in_specs=[pl.BlockSpec((MOE_BLOCK, d), lambda i, be: (i, 0)),
                  pl.BlockSpec((1, d, hdim), lambda i, be: (be[i], 0, 0)),
                  pl.BlockSpec((1, d, hdim), lambda i, be: (be[i], 0, 0)),
                  pl.BlockSpec((1, hdim, d), lambda i, be: (be[i], 0, 0))],
        out_specs=pl.BlockSpec((MOE_BLOCK, d), lambda i, be: (i, 0)),
    )
    return pl.pallas_call(
        _expert_kernel,
        grid_spec=grid_spec,
        out_shape=jax.ShapeDtypeStruct((m, d), F32),
        compiler_params=_cparams(("arbitrary",)),
        name="moe_experts",
    )(blk_exp, xb, w_gate, w_up, w_down)


def _rope_tables(n_tok):
    rows = n_tok // GRID_W
    row = jnp.repeat(jnp.arange(rows, dtype=F32), GRID_W)
    col = jnp.broadcast_to(jnp.arange(GRID_W, dtype=F32), (rows, GRID_W)).reshape(-1)
    n_freq = DIFF_DIM // 4
    inv = ROPE_BASE ** (-jnp.arange(n_freq, dtype=F32) / n_freq)
    ar = row[:, None] * inv
    ac = col[:, None] * inv
    ang = jnp.concatenate([ar, ar, ac, ac], axis=-1)
    cos, sin = jnp.cos(ang), jnp.sin(ang)
    first = (jnp.arange(DIFF_DIM) % 32) < 16
    sa = jnp.where(first, -sin, 0.0)
    sb = jnp.where(first, 0.0, sin)
    return tuple(jnp.tile(a, (1, 2)) for a in (cos, sa, sb))


DFT_N2 = 256
DFT_J = 8
DFT_P = 4


def _hy_prep_kernel(*refs):
    ins, outs = refs[:9], refs[11:]
    w_ref, b_ref = refs[9], refs[10]
    j = pl.program_id(1)
    nblk = pl.num_programs(1)
    for n in range(HY_ORDER + 1):
        x_ref, p_ref, n_ref = ins[3 * n:3 * n + 3]
        cols = slice(n * HY_CH, (n + 1) * HY_CH)
        prev, nxt = _halo_rows(p_ref, n_ref, j, nblk)
        outs[n][0] = _conv3(x_ref[0], prev, nxt, w_ref[:, cols], b_ref[:, cols])


def hy_prep(p, conv_w, conv_b):
    b, t, _ = p.shape
    tt = _tile(t, 512)
    in_specs, args = [], []
    for n in range(HY_ORDER + 1):
        cb = C_HY // HY_CH + n
        prev, nxt = _halo_specs(tt, HY_CH, cb, t)
        in_specs += [pl.BlockSpec((1, tt, HY_CH), lambda i, j, cb=cb: (i, j, cb)), prev, nxt]
        args += [p, p, p]
    nch = (HY_ORDER + 1) * HY_CH
    in_specs += [pl.BlockSpec((3, nch), lambda i, j: (0, 0)), pl.BlockSpec((1, nch), lambda i, j: (0, 0))]
    osp = pl.BlockSpec((1, tt, HY_CH), lambda i, j: (i, j, 0))
    return pl.pallas_call(
        _hy_prep_kernel,
        grid=(b, t // tt),
        in_specs=in_specs,
        out_specs=[osp] * (HY_ORDER + 1),
        out_shape=[jax.ShapeDtypeStruct((b, t, HY_CH), F32)] * (HY_ORDER + 1),
        compiler_params=_cparams(("parallel", "parallel")),
        name="hy_prep",
    )(*args, conv_w, conv_b.reshape(1, nch))


def _hy_filter_kernel(emb_ref, w1_ref, b1_ref, w2_ref, b2_ref, w3_ref, fr_ref, al_ref, f_o, ss_o, *, length):
    j = pl.program_id(0)
    tt = emb_ref.shape[0]
    a = jnp.dot(emb_ref[...].astype(BF16), w1_ref[...].astype(BF16), preferred_element_type=F32) + b1_ref[...]
    a = jnp.sin(fr_ref[0:1, :] * a)
    a = jnp.dot(a.astype(BF16), w2_ref[...].astype(BF16), preferred_element_type=F32) + b2_ref[...]
    a = jnp.sin(fr_ref[1:2, :] * a)
    filt = jnp.dot(a.astype(BF16), w3_ref[...].astype(BF16), preferred_element_type=F32)
    row = lax.broadcasted_iota(jnp.int32, (tt, HY_CH), 0) + j * tt
    window = jnp.exp(-(row.astype(F32) / length) * al_ref[...]) + FILTER_SHIFT

    @pl.when(j == 0)
    def _():
        ss_o[...] = jnp.zeros(ss_o.shape, F32)

    for o in range(HY_ORDER):
        for d in range(2):
            idx = 2 * o + d
            f = filt[:, idx * HY_CH:(idx + 1) * HY_CH] * window
            if d == 1:
                f = jnp.where(row == 0, 0.0, f)
            f_o[idx] = f
            ss_o[o:o + 1, :] += jnp.sum(f * f, axis=0, keepdims=True)


def hy_filters(length, w1, b1, w2, b2, w3, freq):
    t = jnp.arange(length, dtype=F32) / length
    bands = jnp.arange(1, FILTER_BANDS + 1, dtype=F32)
    ang = 2.0 * math.pi * t[:, None] * bands
    emb = jnp.concatenate([t[:, None], jnp.cos(ang), jnp.sin(ang)], axis=-1)
    pad = LANES - emb.shape[1]
    emb = jnp.pad(emb, ((0, 0), (0, pad)))
    w1 = jnp.pad(w1, ((0, pad), (0, 0)))
    ne, nh = emb.shape[1], w1.shape[1]
    alpha = jnp.linspace(abs(math.log(DECAY_TARGET)) / SLOW_DECAY_PCT,
                         abs(math.log(DECAY_TARGET)) / FAST_DECAY_PCT, HY_CH).reshape(1, HY_CH)
    tt = _tile(length, 512)
    full = lambda shape: pl.BlockSpec(shape, lambda j: (0,) * len(shape))
    f, ss = pl.pallas_call(
        functools.partial(_hy_filter_kernel, length=length),
        grid=(length // tt,),
        in_specs=[pl.BlockSpec((tt, ne), lambda j: (j, 0)), full((ne, nh)), full((1, nh)), full((nh, nh)),
                  full((1, nh)), full((nh, 2 * HY_ORDER * HY_CH)), full((2, nh)), full((1, HY_CH))],
        out_specs=[pl.BlockSpec((2 * HY_ORDER, tt, HY_CH), lambda j: (0, j, 0)), full((HY_ORDER, HY_CH))],
        out_shape=[jax.ShapeDtypeStruct((2 * HY_ORDER, length, HY_CH), F32),
                   jax.ShapeDtypeStruct((HY_ORDER, HY_CH), F32)],
        compiler_params=_cparams(("arbitrary",)),
        name="hy_filter",
    )(emb, w1, b1.reshape(1, nh), w2, b2.reshape(1, nh), w3, freq, alpha)
    return f, lax.rsqrt(ss + EPS)


def _dft_tables(length):
    n = 2 * length
    n1 = n // DFT_N2
    half = n1 // 2
    k1 = jnp.arange(n1, dtype=jnp.int32)[None, :, None]
    tn = (DFT_N2 * jnp.arange(half, dtype=jnp.int32)[None, None, :]
          + jnp.arange(DFT_N2, dtype=jnp.int32)[:, None, None])
    th = (2.0 * math.pi / n) * ((k1 * tn) % n).astype(F32)
    ga = jnp.concatenate([jnp.cos(th), -jnp.sin(th)], axis=1).astype(BF16)
    thi = jnp.swapaxes(th, 1, 2)
    gi = jnp.concatenate([jnp.cos(thi), -jnp.sin(thi)], axis=2).astype(BF16)
    kk = jnp.arange(DFT_N2, dtype=jnp.int32)
    t2 = (2.0 * math.pi / DFT_N2) * ((kk[:, None] * kk[None, :]) % DFT_N2).astype(F32)
    c2, s2 = jnp.cos(t2), jnp.sin(t2)
    mf = jnp.block([[c2, s2], [-s2, c2]]).astype(BF16)
    mi = jnp.block([[c2, -s2], [s2, c2]]).astype(BF16)
    return ga, gi, mf, mi


def _dft_a_kernel(z_ref, g_ref, o_ref):
    c = HY_CH
    for j in range(DFT_J):
        slab = z_ref[0, :, j * c:(j + 1) * c].astype(BF16)
        r = jnp.dot(g_ref[j], slab, preferred_element_type=F32)
        o_ref[0, :, :, j * c:(j + 1) * c] = r.reshape(2, r.shape[0] // 2, c).astype(o_ref.dtype)


def dft_a(z, ga):
    bz, half, wid = z.shape
    n1 = ga.shape[1] // 2
    jc = DFT_J * HY_CH
    return pl.pallas_call(
        _dft_a_kernel,
        grid=(bz, DFT_N2 // DFT_J),
        in_specs=[pl.BlockSpec((1, half, jc), lambda i, j: (i, 0, j)),
                  pl.BlockSpec((DFT_J, 2 * n1, half), lambda i, j: (j, 0, 0))],
        out_specs=pl.BlockSpec((1, 2, n1, jc), lambda i, j: (i, 0, 0, j)),
        out_shape=jax.ShapeDtypeStruct((bz, 2, n1, wid), BF16),
        compiler_params=_cparams(("parallel", "parallel")),
        name="dft_a",
    )(z, ga)


def _stack_ri(ref, b, k):
    return jnp.concatenate([ref[b, 0, k], ref[b, 1, k]], axis=0)


def _spec_filter_kernel(f_ref, mf_ref, sc_ref, h_o):
    sc = sc_ref[0]
    for k in range(DFT_P):
        xf = jnp.dot(mf_ref[...], _stack_ri(f_ref, 0, k), preferred_element_type=F32)
        xb = jnp.dot(mf_ref[...], _stack_ri(f_ref, 1, k), preferred_element_type=F32)
        h_o[0, k, 0] = ((xf[:DFT_N2] + xb[:DFT_N2]) * sc).astype(h_o.dtype)
        h_o[0, k, 1] = ((xf[DFT_N2:] - xb[DFT_N2:]) * sc).astype(h_o.dtype)


def spec_filter(fa, mf, scale):
    nb, _, n1, _, c = fa.shape
    order = nb // 2
    return pl.pallas_call(
        _spec_filter_kernel,
        grid=(order, n1 // DFT_P),
        in_specs=[pl.BlockSpec((2, 2, DFT_P, DFT_N2, c), lambda o, k: (o, 0, k, 0, 0)),
                  pl.BlockSpec((2 * DFT_N2, 2 * DFT_N2), lambda o, k: (0, 0)),
                  pl.BlockSpec((1, 1, c), lambda o, k: (o, 0, 0))],
        out_specs=pl.BlockSpec((1, DFT_P, 2, DFT_N2, c), lambda o, k: (o, k, 0, 0, 0)),
        out_shape=jax.ShapeDtypeStruct((order, n1, 2, DFT_N2, c), BF16),
        compiler_params=_cparams(("parallel", "parallel")),
        name="spec_filter",
    )(fa, mf, scale.reshape(order, 1, c))


def _spec_conv_kernel(a_ref, h_ref, mf_ref, mi_ref, o_ref):
    for k in range(DFT_P):
        x = jnp.dot(mf_ref[...], _stack_ri(a_ref, 0, k), preferred_element_type=F32)
        xr, xi = x[:DFT_N2], x[DFT_N2:]
        hr, hi = h_ref[0, k, 0].astype(F32), h_ref[0, k, 1].astype(F32)
        y = jnp.concatenate([xr * hr - xi * hi, xr * hi + xi * hr], axis=0).astype(BF16)
        z = jnp.dot(mi_ref[...], y, preferred_element_type=F32)
        o_ref[0, 0, k] = z[:DFT_N2].astype(o_ref.dtype)
        o_ref[0, 1, k] = z[DFT_N2:].astype(o_ref.dtype)


def spec_conv(a, h, order, mf, mi):
    b, _, n1, _, c = a.shape
    blk = pl.BlockSpec((1, 2, DFT_P, DFT_N2, c), lambda i, k: (i, 0, k, 0, 0))
    mat = pl.BlockSpec((2 * DFT_N2, 2 * DFT_N2), lambda i, k: (0, 0))
    return pl.pallas_call(
        _spec_conv_kernel,
        grid=(b, n1 // DFT_P),
        in_specs=[blk, pl.BlockSpec((1, DFT_P, 2, DFT_N2, c), lambda i, k: (order, k, 0, 0, 0)), mat, mat],
        out_specs=blk,
        out_shape=jax.ShapeDtypeStruct(a.shape, BF16),
        compiler_params=_cparams(("parallel", "parallel")),
        name="spec_conv",
    )(a, h, mf, mi)


def _dft_ainv_kernel(z_ref, g_ref, xg_ref, zin_ref, bias_ref, o_ref):
    c = HY_CH
    n1 = z_ref.shape[2]
    for j in range(DFT_J):
        cols = slice(j * c, (j + 1) * c)
        zz = z_ref[0, :, :, cols].reshape(2 * n1, c)
        y = jnp.dot(g_ref[j], zz, preferred_element_type=F32)
        o_ref[0, :, cols] = xg_ref[0, :, cols] * (y + bias_ref[...] * zin_ref[0, :, cols])


def dft_ainv(z, gi, xg, zin, bias):
    b, _, n1, wid = z.shape
    half = gi.shape[1]
    jc = DFT_J * HY_CH
    tok = pl.BlockSpec((1, half, jc), lambda i, j: (i, 0, j))
    return pl.pallas_call(
        _dft_ainv_kernel,
        grid=(b, DFT_N2 // DFT_J),
        in_specs=[pl.BlockSpec((1, 2, n1, jc), lambda i, j: (i, 0, 0, j)),
                  pl.BlockSpec((DFT_J, half, 2 * n1), lambda i, j: (j, 0, 0)),
                  tok, tok, pl.BlockSpec((1, HY_CH), lambda i, j: (0, 0))],
        out_specs=tok,
        out_shape=jax.ShapeDtypeStruct((b, half, wid), F32),
        compiler_params=_cparams(("parallel", "parallel")),
        name="dft_ainv",
    )(z, gi, xg, zin, bias.reshape(1, HY_CH))


def _ctx_conv_kernel(z_ref, xg_ref, f_ref, mf_ref, mi_ref, sc_ref, bias_ref, o_ref):
    nf = mf_ref.shape[0] // 2
    mf = mf_ref[...]
    xf = jnp.dot(mf, f_ref[0].astype(BF16), preferred_element_type=F32)
    xb = jnp.dot(mf, f_ref[1].astype(BF16), preferred_element_type=F32)
    sc = sc_ref[0]
    hr = (xf[:nf] + xb[:nf]) * sc
    hi = (xf[nf:] - xb[nf:]) * sc
    z = z_ref[0]
    x = jnp.dot(mf, z.astype(BF16), preferred_element_type=F32)
    xr, xi = x[:nf], x[nf:]
    y = jnp.concatenate([xr * hr - xi * hi, xr * hi + xi * hr], axis=0).astype(BF16)
    o_ref[0] = xg_ref[0] * (jnp.dot(mi_ref[...], y, preferred_element_type=F32) + bias_ref[...] * z)


def ctx_conv(z, xg, f, order, scale, bias):
    b, length, c = z.shape
    n = 2 * length
    kk = jnp.arange(n, dtype=jnp.int32)[:, None]
    tn = jnp.arange(length, dtype=jnp.int32)[None, :]
    th = (2.0 * math.pi / n) * ((kk * tn) % n).astype(F32)
    mf = jnp.concatenate([jnp.cos(th), -jnp.sin(th)], axis=0).astype(BF16)
    mi = jnp.concatenate([jnp.cos(th.T), -jnp.sin(th.T)], axis=1).astype(BF16)
    tok = pl.BlockSpec((1, length, c), lambda i: (i, 0, 0))
    return pl.pallas_call(
        _ctx_conv_kernel,
        grid=(b,),
        in_specs=[tok, tok, pl.BlockSpec((2, length, c), lambda i: (order, 0, 0)),
                  pl.BlockSpec((2 * n, length), lambda i: (0, 0)), pl.BlockSpec((length, 2 * n), lambda i: (0, 0)),
                  pl.BlockSpec((1, 1, c), lambda i: (order, 0, 0)), pl.BlockSpec((1, c), lambda i: (0, 0))],
        out_specs=tok,
        out_shape=jax.ShapeDtypeStruct((b, length, c), F32),
        compiler_params=_cparams(("parallel",)),
        name="ctx_conv",
    )(z, xg, f, mf, mi, scale.reshape(-1, 1, c), bias.reshape(1, c))


def hyena(p, conv_w, conv_b, w1, b1, w2, b2, w3, freq, bias, tables):
    b, length, _ = p.shape
    parts = hy_prep(p, conv_w, conv_b)
    f, rnorm = hy_filters(length, w1, b1, w2, b2, w3, freq)
    scale = rnorm / (2 * length)
    z = parts[0]
    if tables is None:
        for o in range(HY_ORDER):
            z = ctx_conv(z, parts[o + 1], f, o, scale, bias[o])
        return z
    ga, gi, mf, mi = tables
    half = ga.shape[2]
    wid = DFT_N2 * HY_CH
    fa = dft_a(f.reshape(2 * HY_ORDER, half, wid), ga)
    n1 = fa.shape[2]
    h = spec_filter(fa.reshape(2 * HY_ORDER, 2, n1, DFT_N2, HY_CH), mf, scale)
    for o in range(HY_ORDER):
        a = dft_a(z.reshape(b, half, wid), ga).reshape(b, 2, n1, DFT_N2, HY_CH)
        zc = spec_conv(a, h, o, mf, mi).reshape(b, 2, n1, wid)
        z = dft_ainv(zc, gi, parts[o + 1].reshape(b, half, wid), z.reshape(b, half, wid), bias[o])
        z = z.reshape(b, length, HY_CH)
    return z


def _hier_moe(h, w_group, b_group, w_router, b_router, w_gate, w_up, w_down):
    n_tok, d = h.shape
    g_logits = (h @ w_group + b_group).astype(F32)
    grp = jnp.argmax(g_logits, axis=-1)
    p_grp = jnp.max(jax.nn.softmax(g_logits, axis=-1), axis=-1, keepdims=True)
    e_logits = (h @ w_router + b_router).astype(F32).reshape(n_tok, MOE_GROUPS, MOE_EPG)
    e_in = e_logits[jnp.arange(n_tok), grp]
    top_v, top_i = lax.top_k(e_in, MOE_TOP_K)
    gates = p_grp * jax.nn.softmax(top_v, axis=-1)
    e_flat = (grp[:, None] * MOE_EPG + top_i).reshape(-1).astype(jnp.int32)
    m_slots = n_tok * MOE_TOP_K
    onehot = (e_flat[:, None] == jnp.arange(MOE_EXPERTS, dtype=jnp.int32)[None, :]).astype(jnp.int32)
    csum = jnp.cumsum(onehot, axis=0)
    rank = jnp.sum(onehot * csum, axis=1) - 1
    counts = csum[-1]
    padded = (counts + MOE_BLOCK - 1) // MOE_BLOCK * MOE_BLOCK
    p_end = jnp.cumsum(padded)
    dest = (p_end - padded)[e_flat] + rank
    n_blocks = -(-(m_slots + MOE_EXPERTS * (MOE_BLOCK - 1)) // MOE_BLOCK)
    slot_tok = jnp.arange(m_slots, dtype=jnp.int32) // MOE_TOP_K
    buf_tok = jnp.zeros((n_blocks * MOE_BLOCK,), jnp.int32).at[dest].set(slot_tok)
    blk_exp = jnp.minimum(jnp.searchsorted(p_end, jnp.arange(n_blocks) * MOE_BLOCK, side='right'),
                          MOE_EXPERTS - 1).astype(jnp.int32)
    xb = h.astype(BF16)[buf_tok]
    yb = expert_blocks(xb, blk_exp, w_gate.astype(BF16), w_up.astype(BF16), w_down.astype(BF16))
    return yb, dest.reshape(n_tok, MOE_TOP_K), gates


def _moe_combine_kernel(x_ref, y0_ref, y1_ref, g_ref, m_ref, o_ref):
    g = g_ref[...]
    f = g[:, 0:1] * y0_ref[...] + g[:, 1:2] * y1_ref[...]
    o_ref[0] = x_ref[0] + m_ref[0] * f


def moe_combine(x, yb, dest, gates, mod):
    b, t, d = x.shape
    y0 = yb[dest[:, 0]]
    y1 = yb[dest[:, 1]]
    tt = _tile(t, 512)
    nt = t // tt
    row = pl.BlockSpec((tt, d), lambda i, j: (i * nt + j, 0))
    return pl.pallas_call(
        _moe_combine_kernel,
        grid=(b, nt),
        in_specs=[pl.BlockSpec((1, tt, d), lambda i, j: (i, j, 0)), row, row,
                  pl.BlockSpec((tt, MOE_TOP_K), lambda i, j: (i * nt + j, 0)),
                  pl.BlockSpec((1, 1, d), lambda i, j: (i, 0, 0))],
        out_specs=pl.BlockSpec((1, tt, d), lambda i, j: (i, j, 0)),
        out_shape=jax.ShapeDtypeStruct((b, t, d), F32),
        compiler_params=_cparams(("parallel", "parallel")),
        name="moe_combine",
    )(x, y0, y1, gates, mod.reshape(b, 1, d))


def _permute_w_in(w):
    d = w.shape[0]
    sizes = (512, 512, 512, 1536, 512, 16, 512, 256, 1536, 4096)
    offs = np.cumsum((0,) + sizes)
    dq, dk, dv, mlqkv, mlo, mlg, gq, gkv, hy, gate = [w[:, offs[i]:offs[i + 1]] for i in range(10)]
    pad = jnp.zeros((d, N_P - C_MLG - 16), w.dtype)
    return jnp.concatenate([gate, dq, dk, dv, mlqkv, mlo, gq, hy, gkv, mlg, pad], axis=1)


def kernel(x, c, ctx, c_ctx, w_ada, b_ada, norm1_g, norm2_g, w_in, diff_lam, diff_norm_g, ml_conv_w, ml_conv_b, ml_gate_b, ml_norm_g, gqa_qnorm_g, gqa_knorm_g, hy_conv_w, hy_conv_b, hy_f_w1, hy_f_b1, hy_f_w2, hy_f_b2, hy_f_w3, hy_f_freq, hy_bias, w_branch, w_out, moe_w_group, moe_b_group, moe_w_router, moe_b_router, moe_w_gate, moe_w_up, moe_w_down, final_norm_g):
    b, n, d = x.shape
    n_ctx = ctx.shape[1]
    depth = w_in.shape[0]
    tk = _tile(n, 1024)
    tables = _rope_tables(n)
    dft_tables = _dft_tables(n)
    sc = jax.nn.silu(c)
    scx = jax.nn.silu(c_ctx)
    xs, cs = x, ctx
    for l in range(depth):
        need_ctx = l < depth - 1
        mod_l = jnp.split(sc @ w_ada[l] + b_ada[l], 6, axis=-1)
        mod_c = [jnp.broadcast_to(m, (b, d)) for m in jnp.split(scx @ w_ada[l] + b_ada[l], 6, axis=-1)]
        w_p = _permute_w_in(w_in[l]).astype(BF16)
        hl = norm_mod(xs, norm1_g[l], mod_l[0], mod_l[1], BF16)
        hc = norm_mod(cs, norm1_g[l], mod_c[0], mod_c[1], BF16)
        p_l = matmul(hl.reshape(b * n, d), w_p, F32).reshape(b, n, N_P)
        p_c = matmul(hc.reshape(b * n_ctx, d), w_p, F32).reshape(b, n_ctx, N_P)

        dq_l, dkt_l, dv_l, gq_l, gkt_l, gv_l = attn_prep(p_l, tables, gqa_qnorm_g[l], gqa_knorm_g[l], tk)
        dq_c, dkt_c, dv_c, gq_c, gkt_c, gv_c = attn_prep(p_c, None, gqa_qnorm_g[l], gqa_knorm_g[l], n_ctx)
        lam_init = 0.8 - 0.6 * math.exp(-0.3 * l)
        lp = diff_lam[l].astype(F32)
        lam = jnp.exp(jnp.sum(lp[0] * lp[1])) - jnp.exp(jnp.sum(lp[2] * lp[3])) + lam_init
        yl_a = diff_attention(dq_l, dkt_c, dv_c, dkt_l, dv_l, lam, diff_norm_g[l], 1.0 - lam_init, 1024)
        yl_c = gqa_attention(gq_l, gkt_c, gv_c, gkt_l, gv_l, 1024)
        if need_ctx:
            yc_a = diff_attention(dq_c, dkt_c, dv_c, None, None, lam, diff_norm_g[l], 1.0 - lam_init, 256)
            yc_c = gqa_attention(gq_c, gkt_c, gv_c, None, None, 256)

        hs_c, hs_l = mlstm_branch(p_c, p_l, ml_conv_w[l], ml_conv_b[l], ml_gate_b[l], need_ctx)
        hy_args = (hy_conv_w[l], hy_conv_b[l], hy_f_w1[l], hy_f_b1[l], hy_f_w2[l], hy_f_b2[l],
                   hy_f_w3[l], hy_f_freq[l], hy_bias[l])
        yl_d = hyena(p_l, *hy_args, dft_tables)

        wb = w_branch[l].astype(BF16)
        wo = w_out[l].astype(BF16)
        xs = merge(yl_a, hs_l, yl_c, yl_d, p_l, ml_norm_g[l], wb, wo, xs, mod_l[2])
        hl2 = norm_mod(xs, norm2_g[l], mod_l[3], mod_l[4], F32)
        moe_args = (moe_w_group[l], moe_b_group[l], moe_w_router[l], moe_b_router[l],
                    moe_w_gate[l], moe_w_up[l], moe_w_down[l])
        if need_ctx:
            yc_d = hyena(p_c, *hy_args, None)
            cs = merge(yc_a, hs_c, yc_c, yc_d, p_c, ml_norm_g[l], wb, wo, cs, mod_c[2])
            hc2 = norm_mod(cs, norm2_g[l], mod_c[3], mod_c[4], F32)
            yb, dest, gates = _hier_moe(jnp.concatenate([hc2.reshape(-1, d), hl2.reshape(-1, d)], axis=0), *moe_args)
            nc = b * n_ctx
            cs = moe_combine(cs, yb, dest[:nc], gates[:nc], mod_c[5])
            dest, gates = dest[nc:], gates[nc:]
        else:
            yb, dest, gates = _hier_moe(hl2.reshape(-1, d), *moe_args)
        xs = moe_combine(xs, yb, dest, gates, mod_l[5])
    zero = jnp.zeros((b, d), F32)
    return norm_mod(xs, final_norm_g, zero, zero, F32)
```

```python
import functools
import math

import jax
import jax.numpy as jnp
import numpy as np
from jax import lax
from jax.experimental import pallas as pl
from jax.experimental.pallas import tpu as pltpu

F32 = jnp.float32
BF16 = jnp.bfloat16

EPS = 1e-6
ROPE_BASE = 10000.0
GRID_W = 64

DIFF_HEADS = 4
DIFF_DIM = 64
DIFF_VDIM = 128
ML_HEADS = 4
ML_DIM = 128
ML_CHUNK = 64
GQA_HEADS = 8
GQA_KV = 2
GQA_DIM = 64
HY_CH = 512
HY_ORDER = 2
FILTER_BANDS = 16
FILTER_SHIFT = 0.05
DECAY_TARGET = 1e-2
FAST_DECAY_PCT = 0.3
SLOW_DECAY_PCT = 1.5
N_BRANCH = 4
MOE_GROUPS = 4
MOE_EPG = 8
MOE_EXPERTS = MOE_GROUPS * MOE_EPG
MOE_TOP_K = 2
MOE_BLOCK = 256

LANES = 128
VMEM_LIMIT = 48 * 1024 * 1024

C_GATE = 0
C_DQ = 4096
C_DK = 4608
C_DV = 5120
C_MLQ = 5632
C_MLK = 6144
C_MLV = 6656
C_MLO = 7168
C_GQ = 7680
C_HY = 8192
C_GK = 9728
C_GV = 9856
C_MLG = 9984
N_P = 10240

QSCALE = (DIFF_DIM ** -0.5) * math.log2(math.e)


def _cparams(sem):
    return pltpu.CompilerParams(dimension_semantics=sem, vmem_limit_bytes=VMEM_LIMIT)


def _tile(n, target):
    if n <= target:
        return n
    for t in range(target, 7, -1):
        if n % t == 0 and t % 8 == 0:
            return t
    return n


def _norm_mod_kernel(x_ref, g_ref, sh_ref, sc_ref, o_ref):
    x = x_ref[0]
    y = x * lax.rsqrt(jnp.mean(x * x, axis=-1, keepdims=True) + EPS)
    y = y * g_ref[...]
    o_ref[0] = (y * (1.0 + sc_ref[0]) + sh_ref[0]).astype(o_ref.dtype)


def norm_mod(x, g, shift, scale, out_dtype):
    b, t, d = x.shape
    tt = _tile(t, 512)
    return pl.pallas_call(
        _norm_mod_kernel,
        grid=(b, t // tt),
        in_specs=[pl.BlockSpec((1, tt, d), lambda i, j: (i, j, 0)),
                  pl.BlockSpec((1, d), lambda i, j: (0, 0)),
                  pl.BlockSpec((1, 1, d), lambda i, j: (i, 0, 0)),
                  pl.BlockSpec((1, 1, d), lambda i, j: (i, 0, 0))],
        out_specs=pl.BlockSpec((1, tt, d), lambda i, j: (i, j, 0)),
        out_shape=jax.ShapeDtypeStruct((b, t, d), out_dtype),
        compiler_params=_cparams(("parallel", "parallel")),
        name="norm_mod",
    )(x, g.reshape(1, d), shift.reshape(b, 1, d), scale.reshape(b, 1, d))


def _mm_kernel(a_ref, w_ref, o_ref):
    o_ref[...] = jnp.dot(a_ref[...], w_ref[...], preferred_element_type=F32).astype(o_ref.dtype)


def matmul(a, w, out_dtype, tm=512, tn=1024):
    m, k = a.shape
    n = w.shape[1]
    tm = _tile(m, tm)
    tn = _tile(n, tn)
    return pl.pallas_call(
        _mm_kernel,
        grid=(m // tm, n // tn),
        in_specs=[pl.BlockSpec((tm, k), lambda i, j: (i, 0)),
                  pl.BlockSpec((k, tn), lambda i, j: (0, j))],
        out_specs=pl.BlockSpec((tm, tn), lambda i, j: (i, j)),
        out_shape=jax.ShapeDtypeStruct((m, n), out_dtype),
        compiler_params=_cparams(("parallel", "parallel")),
        name="matmul",
    )(a, w)


def _rope(x, cos, sa, sb):
    xa = pltpu.roll(x, LANES - 16, axis=1)
    xb = pltpu.roll(x, 16, axis=1)
    return x * cos + xa * sa + xb * sb


def _seg_rmsnorm(x, g):
    lane = lax.broadcasted_iota(jnp.int32, x.shape, 1)
    lo = lane < GQA_DIM
    ss = x * x
    s_lo = jnp.sum(jnp.where(lo, ss, 0.0), axis=-1, keepdims=True)
    s_hi = jnp.sum(jnp.where(lo, 0.0, ss), axis=-1, keepdims=True)
    r = jnp.where(lo, lax.rsqrt(s_lo * (1.0 / GQA_DIM) + EPS), lax.rsqrt(s_hi * (1.0 / GQA_DIM) + EPS))
    return x * r * g


def _prep_kernel(dq_ref, dk_ref, dv_ref, gq_ref, gkv_ref, cos_ref, sa_ref, sb_ref, qg_ref, kg_ref,
                 dq_o, dkt_o, dv_o, gq_o, gkt_o, gv_o, *, rope):
    if rope:
        cos, sa, sb = cos_ref[...], sa_ref[...], sb_ref[...]
        rot = lambda x: _rope(x, cos, sa, sb)
    else:
        rot = lambda x: x
    qg = qg_ref[...]
    kg = kg_ref[...]
    for j in range(DIFF_HEADS):
        sl = slice(j * LANES, (j + 1) * LANES)
        dq_o[0, :, sl] = (rot(dq_ref[0, :, sl].astype(F32)) * QSCALE).astype(BF16)
        kt = rot(dk_ref[0, :, sl].astype(F32)).T
        dkt_o[0, j, 0, 0] = kt[:DIFF_DIM].astype(BF16)
        dkt_o[0, j, 1, 0] = kt[DIFF_DIM:].astype(BF16)
        gq_o[0, :, sl] = (rot(_seg_rmsnorm(gq_ref[0, :, sl].astype(F32), qg)) * QSCALE).astype(BF16)
    dv_o[0] = dv_ref[0].astype(BF16)
    kt = rot(_seg_rmsnorm(gkv_ref[0, :, :LANES].astype(F32), kg)).T
    gkt_o[0, 0, 0] = kt[:GQA_DIM].astype(BF16)
    gkt_o[0, 1, 0] = kt[GQA_DIM:].astype(BF16)
    v = gkv_ref[0, :, LANES:].astype(BF16)
    gv_o[0, 0] = v[:, :GQA_DIM]
    gv_o[0, 1] = v[:, GQA_DIM:]


def attn_prep(p, tables, q_g, k_g, tk):
    b, t, _ = p.shape
    rope = tables is not None
    if rope:
        cos, sa, sb = tables
    else:
        cos = sa = sb = jnp.zeros((t, LANES), F32)
    nck = t // tk
    tp = _tile(tk, 512)
    sub = tk // tp
    w512 = lambda c: pl.BlockSpec((1, tp, 512), lambda i, j, c=c: (i, j, c // 512))
    tab = pl.BlockSpec((tp, LANES), lambda i, j: (j, 0))
    vec = pl.BlockSpec((1, LANES), lambda i, j: (0, 0))
    outs = pl.pallas_call(
        functools.partial(_prep_kernel, rope=rope),
        grid=(b, t // tp),
        in_specs=[w512(C_DQ), w512(C_DK), w512(C_DV), w512(C_GQ),
                  pl.BlockSpec((1, tp, 256), lambda i, j: (i, j, C_GK // 256)),
                  tab, tab, tab, vec, vec],
        out_specs=[pl.BlockSpec((1, tp, 512), lambda i, j: (i, j, 0)),
                   pl.BlockSpec((1, DIFF_HEADS, 2, 1, DIFF_DIM, tp), lambda i, j: (i, 0, 0, j // sub, 0, j % sub)),
                   pl.BlockSpec((1, tp, 512), lambda i, j: (i, j, 0)),
                   pl.BlockSpec((1, tp, 512), lambda i, j: (i, j, 0)),
                   pl.BlockSpec((1, GQA_KV, 1, GQA_DIM, tp), lambda i, j: (i, 0, j // sub, 0, j % sub)),
                   pl.BlockSpec((1, GQA_KV, tp, GQA_DIM), lambda i, j: (i, 0, j, 0))],
        out_shape=[jax.ShapeDtypeStruct((b, t, 512), BF16),
                   jax.ShapeDtypeStruct((b, DIFF_HEADS, 2, nck, DIFF_DIM, tk), BF16),
                   jax.ShapeDtypeStruct((b, t, 512), BF16),
                   jax.ShapeDtypeStruct((b, t, 512), BF16),
                   jax.ShapeDtypeStruct((b, GQA_KV, nck, GQA_DIM, tk), BF16),
                   jax.ShapeDtypeStruct((b, GQA_KV, t, GQA_DIM), BF16)],
        compiler_params=_cparams(("parallel", "parallel")),
        name="attn_prep",
    )(p, p, p, p, p, cos, sa, sb,
      jnp.tile(q_g, 2).reshape(1, LANES), jnp.tile(k_g, 2).reshape(1, LANES))
    return outs


def _flash_step(q, kt, v, m_ref, l_ref, acc_ref):
    s = jnp.dot(q, kt, preferred_element_type=F32)
    tk = s.shape[1]
    m_prev = m_ref[...]
    m_next = jnp.maximum(m_prev, jnp.max(s, axis=1, keepdims=True))
    alpha = jnp.exp2(m_prev - m_next)
    p = jnp.exp2(s - jnp.concatenate([m_next] * (tk // LANES), axis=1))
    l_ref[...] = alpha * l_ref[...] + jnp.sum(p, axis=1, keepdims=True)
    dv = acc_ref.shape[-1]
    acc_ref[...] = acc_ref[...] * alpha[:, :dv] + jnp.dot(p.astype(BF16), v, preferred_element_type=F32)
    m_ref[...] = m_next


def _attn_body(q_ref, ktc, vc, ktl, vl, m_sc, l_sc, acc_sc, *, n_lat, tk):
    m_sc[...] = jnp.full(m_sc.shape, -jnp.inf, F32)
    l_sc[...] = jnp.zeros(l_sc.shape, F32)
    acc_sc[...] = jnp.zeros(acc_sc.shape, F32)
    q = q_ref[0]
    qs = (q[:, :DIFF_DIM], q[:, DIFF_DIM:])
    for c in range(2):
        _flash_step(qs[c], ktc(c), vc(c), m_sc.at[c], l_sc.at[c], acc_sc.at[c])
    if n_lat:
        def body(i, carry):
            for c in range(2):
                _flash_step(qs[c], ktl(c, i), vl(c, i), m_sc.at[c], l_sc.at[c], acc_sc.at[c])
            return carry
        lax.fori_loop(0, n_lat, body, 0)


def _diff_attn_kernel(*refs, n_lat, tk, out_scale):
    if n_lat:
        q_ref, ktc_ref, vc_ref, ktl_ref, vl_ref, lam_ref, g_ref, o_ref, m_sc, l_sc, acc_sc = refs
        ktl = lambda c, i: ktl_ref[0, 0, c, i]
        vl = lambda c, i: vl_ref[0, pl.ds(pl.multiple_of(i * tk, tk), tk), :]
    else:
        q_ref, ktc_ref, vc_ref, lam_ref, g_ref, o_ref, m_sc, l_sc, acc_sc = refs
        ktl = vl = None
    _attn_body(q_ref, lambda c: ktc_ref[0, 0, c, 0], lambda c: vc_ref[0], ktl, vl,
               m_sc, l_sc, acc_sc, n_lat=n_lat, tk=tk)
    o0 = acc_sc[0] / l_sc[0]
    o1 = acc_sc[1] / l_sc[1]
    o = o0 - lam_ref[...] * o1
    o = o * lax.rsqrt(jnp.mean(o * o, axis=-1, keepdims=True) + EPS)
    o_ref[0] = (o * g_ref[...] * out_scale).astype(o_ref.dtype)


def diff_attention(q, ktc, vc, ktl, vl, lam, norm_g, out_scale, tq):
    b, t, _ = q.shape
    sc = vc.shape[1]
    tq = _tile(t, tq)
    n_lat, tk = (ktl.shape[3], ktl.shape[5]) if ktl is not None else (0, 0)
    in_specs = [pl.BlockSpec((1, tq, LANES), lambda i, h, j: (i, j, h)),
                pl.BlockSpec((1, 1, 2, 1, DIFF_DIM, sc), lambda i, h, j: (i, h, 0, 0, 0, 0)),
                pl.BlockSpec((1, sc, LANES), lambda i, h, j: (i, 0, h))]
    args = [q, ktc, vc]
    if n_lat:
        in_specs += [pl.BlockSpec((1, 1, 2, n_lat, DIFF_DIM, tk), lambda i, h, j: (i, h, 0, 0, 0, 0)),
                     pl.BlockSpec((1, n_lat * tk, LANES), lambda i, h, j: (i, 0, h))]
        args += [ktl, vl]
    vec = pl.BlockSpec((1, LANES), lambda i, h, j: (0, 0))
    in_specs += [vec, vec]
    args += [jnp.full((1, LANES), lam, F32), norm_g.reshape(1, LANES)]
    return pl.pallas_call(
        functools.partial(_diff_attn_kernel, n_lat=n_lat, tk=tk, out_scale=out_scale),
        grid=(b, DIFF_HEADS, t // tq),
        in_specs=in_specs,
        out_specs=pl.BlockSpec((1, tq, LANES), lambda i, h, j: (i, j, h)),
        out_shape=jax.ShapeDtypeStruct((b, t, DIFF_HEADS * DIFF_VDIM), BF16),
        scratch_shapes=[pltpu.VMEM((2, tq, LANES), F32), pltpu.VMEM((2, tq, LANES), F32),
                        pltpu.VMEM((2, tq, DIFF_VDIM), F32)],
        compiler_params=_cparams(("parallel", "parallel", "parallel")),
        name="diff_attn",
    )(*args)


def _gqa_attn_kernel(*refs, n_lat, tk):
    if n_lat:
        q_ref, ktc_ref, vc_ref, ktl_ref, vl_ref, o_ref, m_sc, l_sc, acc_sc = refs
        ktl = lambda c, i: ktl_ref[0, 0, i]
        vl = lambda c, i: vl_ref[0, 0, pl.ds(pl.multiple_of(i * tk, tk), tk), :]
    else:
        q_ref, ktc_ref, vc_ref, o_ref, m_sc, l_sc, acc_sc = refs
        ktl = vl = None
    _attn_body(q_ref, lambda c: ktc_ref[0, 0, 0], lambda c: vc_ref[0, 0], ktl, vl,
               m_sc, l_sc, acc_sc, n_lat=n_lat, tk=tk)
    o0 = acc_sc[0] / l_sc[0][:, :GQA_DIM]
    o1 = acc_sc[1] / l_sc[1][:, :GQA_DIM]
    o_ref[0] = jnp.concatenate([o0, o1], axis=-1).astype(o_ref.dtype)


def gqa_attention(q, ktc, vc, ktl, vl, tq):
    b, t, _ = q.shape
    sc = vc.shape[2]
    tq = _tile(t, tq)
    n_lat, tk = (ktl.shape[2], ktl.shape[4]) if ktl is not None else (0, 0)
    pairs = GQA_HEADS // 2
    grp = lambda h: h // (pairs // GQA_KV)
    in_specs = [pl.BlockSpec((1, tq, LANES), lambda i, h, j: (i, j, h)),
                pl.BlockSpec((1, 1, 1, GQA_DIM, sc), lambda i, h, j: (i, grp(h), 0, 0, 0)),
                pl.BlockSpec((1, 1, sc, GQA_DIM), lambda i, h, j: (i, grp(h), 0, 0))]
    args = [q, ktc, vc]
    if n_lat:
        in_specs += [pl.BlockSpec((1, 1, n_lat, GQA_DIM, tk), lambda i, h, j: (i, grp(h), 0, 0, 0)),
                     pl.BlockSpec((1, 1, n_lat * tk, GQA_DIM), lambda i, h, j: (i, grp(h), 0, 0))]
        args += [ktl, vl]
    return pl.pallas_call(
        functools.partial(_gqa_attn_kernel, n_lat=n_lat, tk=tk),
        grid=(b, pairs, t // tq),
        in_specs=in_specs,
        out_specs=pl.BlockSpec((1, tq, LANES), lambda i, h, j: (i, j, h)),
        out_shape=jax.ShapeDtypeStruct((b, t, GQA_HEADS * GQA_DIM), BF16),
        scratch_shapes=[pltpu.VMEM((2, tq, LANES), F32), pltpu.VMEM((2, tq, LANES), F32),
                        pltpu.VMEM((2, tq, GQA_DIM), F32)],
        compiler_params=_cparams(("parallel", "parallel", "parallel")),
        name="gqa_attn",
    )(*args)


ML_STEP = 256
HALO = 16


def _halo_rows(prev_ref, next_ref, j, nblk):
    prev = jnp.where(j > 0, prev_ref[0, HALO - 1:HALO, :].astype(F32), 0.0)
    nxt = jnp.where(j < nblk - 1, next_ref[0, 0:1, :].astype(F32), 0.0)
    return prev, nxt


def _conv3(x, prev, nxt, w, bias):
    tt = x.shape[0]
    row = lax.broadcasted_iota(jnp.int32, x.shape, 0)
    xm = jnp.where(row == 0, prev, pltpu.roll(x, 1, axis=0))
    xp = jnp.where(row == tt - 1, nxt, pltpu.roll(x, tt - 1, axis=0))
    return xm * w[0:1] + x * w[1:2] + xp * w[2:3] + bias


def _halo_specs(tt, width, col_block, t):
    nb = t // HALO
    prev = pl.BlockSpec((1, HALO, width), lambda i, j: (i, jnp.maximum(j * (tt // HALO) - 1, 0), col_block))
    nxt = pl.BlockSpec((1, HALO, width), lambda i, j: (i, jnp.minimum((j + 1) * (tt // HALO), nb - 1), col_block))
    return prev, nxt


def _ml_prep_kernel(q_ref, qp_ref, qn_ref, k_ref, kp_ref, kn_ref, g_ref, w_ref, b_ref, gb_ref, q_o, k_o, g_o):
    j = pl.program_id(1)
    nblk = pl.num_programs(1)
    w = ML_HEADS * ML_DIM
    qp, qn = _halo_rows(qp_ref, qn_ref, j, nblk)
    kp, kn = _halo_rows(kp_ref, kn_ref, j, nblk)
    q = _conv3(q_ref[0].astype(F32), qp, qn, w_ref[:, :w], b_ref[:, :w])
    k = _conv3(k_ref[0].astype(F32), kp, kn, w_ref[:, w:], b_ref[:, w:])
    q_o[0] = q * jax.nn.sigmoid(q)
    k_o[0] = (k * jax.nn.sigmoid(k)) * ML_DIM ** -0.5
    x = g_ref[0].astype(F32) + gb_ref[...]
    lane = lax.broadcasted_iota(jnp.int32, x.shape, 1)
    log_sig = jnp.minimum(x, 0.0) - jnp.log(1.0 + jnp.exp(-jnp.abs(x)))
    g_o[0] = jnp.where((lane % 8) >= ML_HEADS, log_sig, x)


def ml_prep(p, conv_w, conv_b, gate_b):
    b, t, _ = p.shape
    tt = _tile(t, 512)
    w = ML_HEADS * ML_DIM
    blk = lambda c: pl.BlockSpec((1, tt, w), lambda i, j, c=c: (i, j, c // w))
    qp, qn = _halo_specs(tt, w, C_MLQ // w, t)
    kp, kn = _halo_specs(tt, w, C_MLK // w, t)
    gb = jnp.zeros((1, LANES), F32).at[0, :4 * ML_HEADS].set(gate_b.reshape(-1))
    return pl.pallas_call(
        _ml_prep_kernel,
        grid=(b, t // tt),
        in_specs=[blk(C_MLQ), qp, qn, blk(C_MLK), kp, kn,
                  pl.BlockSpec((1, tt, LANES), lambda i, j: (i, j, C_MLG // LANES)),
                  pl.BlockSpec((3, 2 * w), lambda i, j: (0, 0)),
                  pl.BlockSpec((1, 2 * w), lambda i, j: (0, 0)),
                  pl.BlockSpec((1, LANES), lambda i, j: (0, 0))],
        out_specs=[pl.BlockSpec((1, tt, w), lambda i, j: (i, j, 0)),
                   pl.BlockSpec((1, tt, w), lambda i, j: (i, j, 0)),
                   pl.BlockSpec((1, tt, LANES), lambda i, j: (i, j, 0))],
        out_shape=[jax.ShapeDtypeStruct((b, t, w), F32), jax.ShapeDtypeStruct((b, t, w), F32),
                   jax.ShapeDtypeStruct((b, t, LANES), F32)],
        compiler_params=_cparams(("parallel", "parallel")),
        name="ml_prep",
    )(p, p, p, p, p, p, p, conv_w, conv_b.reshape(1, 2 * w), gb)


def _ml_chunk_head(q, k, v, li_c, bc_c, li_r, bc_r, b_last, tri, c_ref, n_ref, m_ref):
    m = m_ref[:, 0:1]
    c_mat = c_ref[...]
    n_row = n_ref[...]
    dmat = jnp.where(tri, bc_c - bc_r + li_r, -jnp.inf)
    inter = bc_c + m
    m_t = jnp.maximum(inter, jnp.max(dmat, axis=1, keepdims=True))
    w_intra = jnp.exp(dmat - m_t)
    w_inter = jnp.exp(inter - m_t)
    qb = q.astype(BF16)
    s = lax.dot_general(qb, k.astype(BF16), (((1,), (1,)), ((), ())), preferred_element_type=F32) * w_intra
    num = (jnp.dot(s.astype(BF16), v.astype(BF16), preferred_element_type=F32)
           + w_inter * jnp.dot(qb, c_mat.astype(BF16), preferred_element_type=F32))
    den = jnp.sum(s, axis=1, keepdims=True) + w_inter * jnp.sum(q * n_row, axis=1, keepdims=True)
    h = num / jnp.maximum(jnp.abs(den), jnp.exp(-m_t))
    g = b_last - bc_c + li_c
    m_new = jnp.maximum(b_last + m, jnp.max(g, axis=0, keepdims=True))
    kw = k * jnp.exp(g - m_new)
    wc = jnp.exp(b_last + m - m_new)
    c_ref[...] = wc * c_mat + jnp.dot(kw.T.astype(BF16), v.astype(BF16), preferred_element_type=F32)
    n_ref[...] = wc * n_row + jnp.sum(kw, axis=0, keepdims=True)
    m_ref[...] = jnp.broadcast_to(m_new, m_ref.shape)
    return h


def _ml_scan_kernel(q_ref, k_ref, v_ref, g_ref, c0_ref, n0_ref, m0_ref, h_o, c1_o, n1_o, m1_o,
                    c_sc, n_sc, m_sc, *, direction):
    j = pl.program_id(1)

    @pl.when(j == 0)
    def _():
        c_sc[...] = c0_ref[0]
        n_sc[...] = n0_ref[0]
        m_sc[...] = m0_ref[0]

    r = lax.broadcasted_iota(jnp.int32, (ML_CHUNK, ML_CHUNK), 0)
    s = lax.broadcasted_iota(jnp.int32, (ML_CHUNK, ML_CHUNK), 1)
    tri = (s >= r) if direction else (s <= r)
    tri_f = tri.astype(F32)
    n_chunks = q_ref.shape[1] // ML_CHUNK
    order = range(n_chunks - 1, -1, -1) if direction else range(n_chunks)
    last = 0 if direction else ML_CHUNK - 1
    for c in order:
        rows = slice(c * ML_CHUNK, (c + 1) * ML_CHUNK)
        gch = g_ref[0, rows, :]
        bc = jnp.dot(tri_f, gch, preferred_element_type=F32, precision=lax.Precision.HIGHEST)
        gt = gch.T
        bct = bc.T
        for hd in range(ML_HEADS):
            ci = 2 * ML_HEADS * direction + hd
            cf = ci + ML_HEADS
            cols = slice(hd * ML_DIM, (hd + 1) * ML_DIM)
            h = _ml_chunk_head(q_ref[0, rows, cols], k_ref[0, rows, cols], v_ref[0, rows, cols],
                               gch[:, ci:ci + 1], bc[:, cf:cf + 1], gt[ci:ci + 1, :], bct[cf:cf + 1, :],
                               bc[last:last + 1, cf:cf + 1], tri, c_sc.at[hd], n_sc.at[hd], m_sc.at[hd])
            h_o[0, rows, cols] = h

    @pl.when(j == pl.num_programs(1) - 1)
    def _():
        c1_o[0] = c_sc[...]
        n1_o[0] = n_sc[...]
        m1_o[0] = m_sc[...]


def ml_scan(q, k, p, g, state, direction):
    b, t, w = q.shape
    ts = _tile(t, ML_STEP)
    nst = t // ts
    tok = (lambda i, j: (i, nst - 1 - j, 0)) if direction else (lambda i, j: (i, j, 0))
    tokv = (lambda i, j: (i, nst - 1 - j, C_MLV // w)) if direction else (lambda i, j: (i, j, C_MLV // w))
    st_specs = [pl.BlockSpec((1, ML_HEADS, ML_DIM, ML_DIM), lambda i, j: (i, 0, 0, 0)),
                pl.BlockSpec((1, ML_HEADS, 1, ML_DIM), lambda i, j: (i, 0, 0, 0)),
                pl.BlockSpec((1, ML_HEADS, 1, ML_DIM), lambda i, j: (i, 0, 0, 0))]
    st_shapes = [jax.ShapeDtypeStruct((b, ML_HEADS, ML_DIM, ML_DIM), F32),
                 jax.ShapeDtypeStruct((b, ML_HEADS, 1, ML_DIM), F32),
                 jax.ShapeDtypeStruct((b, ML_HEADS, 1, ML_DIM), F32)]
    h, c1, n1, m1 = pl.pallas_call(
        functools.partial(_ml_scan_kernel, direction=direction),
        grid=(b, nst),
        in_specs=[pl.BlockSpec((1, ts, w), tok), pl.BlockSpec((1, ts, w), tok), pl.BlockSpec((1, ts, w), tokv),
                  pl.BlockSpec((1, ts, LANES), tok)] + st_specs,
        out_specs=[pl.BlockSpec((1, ts, w), tok)] + st_specs,
        out_shape=[jax.ShapeDtypeStruct((b, t, w), F32)] + st_shapes,
        scratch_shapes=[pltpu.VMEM((ML_HEADS, ML_DIM, ML_DIM), F32), pltpu.VMEM((ML_HEADS, 1, ML_DIM), F32),
                        pltpu.VMEM((ML_HEADS, 1, ML_DIM), F32)],
        compiler_params=_cparams(("parallel", "arbitrary")),
        name="ml_scan",
    )(q, k, p, g, *state)
    return h, (c1, n1, m1)


def mlstm_branch(p_c, p_l, conv_w, conv_b, gate_b, need_ctx):
    b = p_l.shape[0]
    qc, kc, gc = ml_prep(p_c, conv_w, conv_b, gate_b)
    ql, kl, gl = ml_prep(p_l, conv_w, conv_b, gate_b)
    zero = (jnp.zeros((b, ML_HEADS, ML_DIM, ML_DIM), F32), jnp.zeros((b, ML_HEADS, 1, ML_DIM), F32),
            jnp.zeros((b, ML_HEADS, 1, ML_DIM), F32))
    hs_c, hs_l = [], []
    for direction in (0, 1):
        hc, st = ml_scan(qc, kc, p_c, gc, zero, direction)
        hl, _ = ml_scan(ql, kl, p_l, gl, st, direction)
        hs_c.append(hc)
        hs_l.append(hl)
    return (tuple(hs_c) if need_ctx else None), tuple(hs_l)


def _merge_kernel(ya_ref, hf_ref, hb_ref, yc_ref, yd_ref, og_ref, g_ref, mg_ref, wb_ref, wo_ref, x_ref, gate_ref,
                  o_ref):
    d = x_ref.shape[-1]
    mg = mg_ref[...]
    yb = []
    for hd in range(ML_HEADS):
        cols = slice(hd * ML_DIM, (hd + 1) * ML_DIM)
        h = hf_ref[0, :, cols] + hb_ref[0, :, cols]
        h = h * lax.rsqrt(jnp.mean(h * h, axis=-1, keepdims=True) + EPS) * mg
        yb.append((h * jax.nn.sigmoid(og_ref[0, :, cols].astype(F32))).astype(BF16))
    ys = (ya_ref[0].astype(BF16), jnp.concatenate(yb, axis=-1), yc_ref[0].astype(BF16), yd_ref[0].astype(BF16))
    acc = None
    for n, y in enumerate(ys):
        t = jnp.dot(y, wb_ref[n], preferred_element_type=F32)
        t = jax.nn.sigmoid(g_ref[0, :, n * d:(n + 1) * d].astype(F32)) * t
        acc = t if acc is None else acc + t
    z = jnp.dot(acc.astype(BF16), wo_ref[...], preferred_element_type=F32)
    o_ref[0] = x_ref[0] + gate_ref[0] * z


def merge(ya, hs, yc, yd, p, ml_norm_g, w_branch, w_out, x, gate):
    b, t, d = x.shape
    tm = _tile(t, 256)
    w = ML_HEADS * ML_DIM
    ysp = pl.BlockSpec((1, tm, w), lambda i, j: (i, j, 0))
    return pl.pallas_call(
        _merge_kernel,
        grid=(b, t // tm),
        in_specs=[ysp, ysp, ysp, ysp, ysp,
                  pl.BlockSpec((1, tm, w), lambda i, j: (i, j, C_MLO // w)),
                  pl.BlockSpec((1, tm, N_BRANCH * d), lambda i, j: (i, j, C_GATE // (N_BRANCH * d))),
                  pl.BlockSpec((1, ML_DIM), lambda i, j: (0, 0)),
                  pl.BlockSpec((N_BRANCH, w, d), lambda i, j: (0, 0, 0)),
                  pl.BlockSpec((d, d), lambda i, j: (0, 0)),
                  pl.BlockSpec((1, tm, d), lambda i, j: (i, j, 0)),
                  pl.BlockSpec((1, 1, d), lambda i, j: (i, 0, 0))],
        out_specs=pl.BlockSpec((1, tm, d), lambda i, j: (i, j, 0)),
        out_shape=jax.ShapeDtypeStruct((b, t, d), F32),
        compiler_params=_cparams(("parallel", "parallel")),
        name="merge",
    )(ya, hs[0], hs[1], yc, yd, p, p, ml_norm_g.reshape(1, ML_DIM), w_branch, w_out, x, gate.reshape(b, 1, d))


def _expert_kernel(be_ref, x_ref, wg_ref, wu_ref, wd_ref, o_ref, wg_sc, wu_sc, wd_sc):
    i = pl.program_id(0)

    @pl.when(jnp.logical_or(i == 0, be_ref[i] != be_ref[jnp.maximum(i - 1, 0)]))
    def _():
        wg_sc[...] = wg_ref[0].astype(BF16)
        wu_sc[...] = wu_ref[0].astype(BF16)
        wd_sc[...] = wd_ref[0].astype(BF16)

    x = x_ref[...]
    a = jnp.dot(x, wg_sc[...], preferred_element_type=F32)
    u = jnp.dot(x, wu_sc[...], preferred_element_type=F32)
    h = (a * jax.nn.sigmoid(a)) * u
    o_ref[...] = jnp.dot(h.astype(BF16), wd_sc[...], preferred_element_type=F32)


def expert_blocks(xb, blk_exp, w_gate, w_up, w_down):
    m, d = xb.shape
    hdim = w_gate.shape[-1]
    n_blocks = m // MOE_BLOCK
    grid_spec = pltpu.PrefetchScalarGridSpec(
        num_scalar_prefetch=1,
        grid=(n_blocks,),
        in_specs=[pl.BlockSpec((MOE_BLOCK, d), lambda i, be: (i, 0)),
                  pl.BlockSpec((1, d, hdim), lambda i, be: (be[i], 0, 0)),
                  pl.BlockSpec((1, d, hdim), lambda i, be: (be[i], 0, 0)),
                  pl.BlockSpec((1, hdim, d), lambda i, be: (be[i], 0, 0))],
        out_specs=pl.BlockSpec((MOE_BLOCK, d), lambda i, be: (i, 0)),
        scratch_shapes=[pltpu.VMEM((d, hdim), BF16), pltpu.VMEM((d, hdim), BF16), pltpu.VMEM((hdim, d), BF16)],
    )
    return pl.pallas_call(
        _expert_kernel,
        grid_spec=grid_spec,
        out_shape=jax.ShapeDtypeStruct((m, d), F32),
        compiler_params=_cparams(("arbitrary",)),
        name="moe_experts",
    )(blk_exp, xb, w_gate, w_up, w_down)


def _rope_tables(n_tok):
    rows = n_tok // GRID_W
    row = jnp.repeat(jnp.arange(rows, dtype=F32), GRID_W)
    col = jnp.broadcast_to(jnp.arange(GRID_W, dtype=F32), (rows, GRID_W)).reshape(-1)
    n_freq = DIFF_DIM // 4
    inv = ROPE_BASE ** (-jnp.arange(n_freq, dtype=F32) / n_freq)
    ar = row[:, None] * inv
    ac = col[:, None] * inv
    ang = jnp.concatenate([ar, ar, ac, ac], axis=-1)
    cos, sin = jnp.cos(ang), jnp.sin(ang)
    first = (jnp.arange(DIFF_DIM) % 32) < 16
    sa = jnp.where(first, -sin, 0.0)
    sb = jnp.where(first, 0.0, sin)
    return tuple(jnp.tile(a, (1, 2)) for a in (cos, sa, sb))


DFT_N2 = 256
DFT_J = 8
DFT_P = 4


def _hy_prep_kernel(*refs):
    ins, outs = refs[:9], refs[11:]
    w_ref, b_ref = refs[9], refs[10]
    j = pl.program_id(1)
    nblk = pl.num_programs(1)
    for n in range(HY_ORDER + 1):
        x_ref, p_ref, n_ref = ins[3 * n:3 * n + 3]
        cols = slice(n * HY_CH, (n + 1) * HY_CH)
        prev, nxt = _halo_rows(p_ref, n_ref, j, nblk)
        outs[n][0] = _conv3(x_ref[0].astype(F32), prev, nxt, w_ref[:, cols], b_ref[:, cols])


def hy_prep(p, conv_w, conv_b):
    b, t, _ = p.shape
    tt = _tile(t, 512)
    in_specs, args = [], []
    for n in range(HY_ORDER + 1):
        cb = C_HY // HY_CH + n
        prev, nxt = _halo_specs(tt, HY_CH, cb, t)
        in_specs += [pl.BlockSpec((1, tt, HY_CH), lambda i, j, cb=cb: (i, j, cb)), prev, nxt]
        args += [p, p, p]
    nch = (HY_ORDER + 1) * HY_CH
    in_specs += [pl.BlockSpec((3, nch), lambda i, j: (0, 0)), pl.BlockSpec((1, nch), lambda i, j: (0, 0))]
    osp = pl.BlockSpec((1, tt, HY_CH), lambda i, j: (i, j, 0))
    return pl.pallas_call(
        _hy_prep_kernel,
        grid=(b, t // tt),
        in_specs=in_specs,
        out_specs=[osp] * (HY_ORDER + 1),
        out_shape=[jax.ShapeDtypeStruct((b, t, HY_CH), F32)] * (HY_ORDER + 1),
        compiler_params=_cparams(("parallel", "parallel")),
        name="hy_prep",
    )(*args, conv_w, conv_b.reshape(1, nch))


def _hy_filter_kernel(emb_ref, w1_ref, b1_ref, w2_ref, b2_ref, w3_ref, fr_ref, al_ref, f_o, ss_o, *, length):
    j = pl.program_id(0)
    tt = emb_ref.shape[0]
    a = jnp.dot(emb_ref[...].astype(BF16), w1_ref[...].astype(BF16), preferred_element_type=F32) + b1_ref[...]
    a = jnp.sin(fr_ref[0:1, :] * a)
    a = jnp.dot(a.astype(BF16), w2_ref[...].astype(BF16), preferred_element_type=F32) + b2_ref[...]
    a = jnp.sin(fr_ref[1:2, :] * a)
    filt = jnp.dot(a.astype(BF16), w3_ref[...].astype(BF16), preferred_element_type=F32)
    row = lax.broadcasted_iota(jnp.int32, (tt, HY_CH), 0) + j * tt
    window = jnp.exp(-(row.astype(F32) / length) * al_ref[...]) + FILTER_SHIFT

    @pl.when(j == 0)
    def _():
        ss_o[...] = jnp.zeros(ss_o.shape, F32)

    for o in range(HY_ORDER):
        for d in range(2):
            idx = 2 * o + d
            f = filt[:, idx * HY_CH:(idx + 1) * HY_CH] * window
            if d == 1:
                f = jnp.where(row == 0, 0.0, f)
            f_o[idx] = f
            ss_o[o:o + 1, :] += jnp.sum(f * f, axis=0, keepdims=True)


def hy_filters(length, w1, b1, w2, b2, w3, freq):
    t = jnp.arange(length, dtype=F32) / length
    bands = jnp.arange(1, FILTER_BANDS + 1, dtype=F32)
    ang = 2.0 * math.pi * t[:, None] * bands
    emb = jnp.concatenate([t[:, None], jnp.cos(ang), jnp.sin(ang)], axis=-1)
    pad = LANES - emb.shape[1]
    emb = jnp.pad(emb, ((0, 0), (0, pad)))
    w1 = jnp.pad(w1, ((0, pad), (0, 0)))
    ne, nh = emb.shape[1], w1.shape[1]
    alpha = jnp.linspace(abs(math.log(DECAY_TARGET)) / SLOW_DECAY_PCT,
                         abs(math.log(DECAY_TARGET)) / FAST_DECAY_PCT, HY_CH).reshape(1, HY_CH)
    tt = _tile(length, 512)
    full = lambda shape: pl.BlockSpec(shape, lambda j: (0,) * len(shape))
    f, ss = pl.pallas_call(
        functools.partial(_hy_filter_kernel, length=length),
        grid=(length // tt,),
        in_specs=[pl.BlockSpec((tt, ne), lambda j: (j, 0)), full((ne, nh)), full((1, nh)), full((nh, nh)),
                  full((1, nh)), full((nh, 2 * HY_ORDER * HY_CH)), full((2, nh)), full((1, HY_CH))],
        out_specs=[pl.BlockSpec((2 * HY_ORDER, tt, HY_CH), lambda j: (0, j, 0)), full((HY_ORDER, HY_CH))],
        out_shape=[jax.ShapeDtypeStruct((2 * HY_ORDER, length, HY_CH), F32),
                   jax.ShapeDtypeStruct((HY_ORDER, HY_CH), F32)],
        compiler_params=_cparams(("arbitrary",)),
        name="hy_filter",
    )(emb, w1, b1.reshape(1, nh), w2, b2.reshape(1, nh), w3, freq, alpha)
    return f, lax.rsqrt(ss + EPS)


def _dft_tables(length):
    n = 2 * length
    n1 = n // DFT_N2
    half = n1 // 2
    k1 = jnp.arange(n1, dtype=jnp.int32)[None, :, None]
    tn = (DFT_N2 * jnp.arange(half, dtype=jnp.int32)[None, None, :]
          + jnp.arange(DFT_N2, dtype=jnp.int32)[:, None, None])
    th = (2.0 * math.pi / n) * ((k1 * tn) % n).astype(F32)
    ga = jnp.concatenate([jnp.cos(th), -jnp.sin(th)], axis=1).astype(BF16)
    thi = jnp.swapaxes(th, 1, 2)
    gi = jnp.concatenate([jnp.cos(thi), -jnp.sin(thi)], axis=2).astype(BF16)
    kk = jnp.arange(DFT_N2, dtype=jnp.int32)
    t2 = (2.0 * math.pi / DFT_N2) * ((kk[:, None] * kk[None, :]) % DFT_N2).astype(F32)
    c2, s2 = jnp.cos(t2), jnp.sin(t2)
    mf = jnp.block([[c2, s2], [-s2, c2]]).astype(BF16)
    mi = jnp.block([[c2, -s2], [s2, c2]]).astype(BF16)
    return ga, gi, mf, mi


def _dft_a_kernel(z_ref, g_ref, o_ref):
    c = HY_CH
    for j in range(DFT_J):
        slab = z_ref[0, :, j * c:(j + 1) * c].astype(BF16)
        r = jnp.dot(g_ref[j], slab, preferred_element_type=F32)
        o_ref[0, :, :, j * c:(j + 1) * c] = r.reshape(2, r.shape[0] // 2, c).astype(o_ref.dtype)


def dft_a(z, ga):
    bz, half, wid = z.shape
    n1 = ga.shape[1] // 2
    jc = DFT_J * HY_CH
    return pl.pallas_call(
        _dft_a_kernel,
        grid=(bz, DFT_N2 // DFT_J),
        in_specs=[pl.BlockSpec((1, half, jc), lambda i, j: (i, 0, j)),
                  pl.BlockSpec((DFT_J, 2 * n1, half), lambda i, j: (j, 0, 0))],
        out_specs=pl.BlockSpec((1, 2, n1, jc), lambda i, j: (i, 0, 0, j)),
        out_shape=jax.ShapeDtypeStruct((bz, 2, n1, wid), BF16),
        compiler_params=_cparams(("parallel", "parallel")),
        name="dft_a",
    )(z, ga)


def _stack_ri(ref, b, k):
    return jnp.concatenate([ref[b, 0, k], ref[b, 1, k]], axis=0)


def _spec_filter_kernel(f_ref, mf_ref, sc_ref, h_o):
    sc = sc_ref[0]
    for k in range(DFT_P):
        xf = jnp.dot(mf_ref[...], _stack_ri(f_ref, 0, k), preferred_element_type=F32)
        xb = jnp.dot(mf_ref[...], _stack_ri(f_ref, 1, k), preferred_element_type=F32)
        h_o[0, k, 0] = ((xf[:DFT_N2] + xb[:DFT_N2]) * sc).astype(h_o.dtype)
        h_o[0, k, 1] = ((xf[DFT_N2:] - xb[DFT_N2:]) * sc).astype(h_o.dtype)


def spec_filter(fa, mf, scale):
    nb, _, n1, _, c = fa.shape
    order = nb // 2
    return pl.pallas_call(
        _spec_filter_kernel,
        grid=(order, n1 // DFT_P),
        in_specs=[pl.BlockSpec((2, 2, DFT_P, DFT_N2, c), lambda o, k: (o, 0, k, 0, 0)),
                  pl.BlockSpec((2 * DFT_N2, 2 * DFT_N2), lambda o, k: (0, 0)),
                  pl.BlockSpec((1, 1, c), lambda o, k: (o, 0, 0))],
        out_specs=pl.BlockSpec((1, DFT_P, 2, DFT_N2, c), lambda o, k: (o, k, 0, 0, 0)),
        out_shape=jax.ShapeDtypeStruct((order, n1, 2, DFT_N2, c), BF16),
        compiler_params=_cparams(("parallel", "parallel")),
        name="spec_filter",
    )(fa, mf, scale.reshape(order, 1, c))


def _spec_conv_kernel(a_ref, h_ref, mf_ref, mi_ref, o_ref):
    for k in range(DFT_P):
        x = jnp.dot(mf_ref[...], _stack_ri(a_ref, 0, k), preferred_element_type=F32)
        xr, xi = x[:DFT_N2], x[DFT_N2:]
        hr, hi = h_ref[0, k, 0].astype(F32), h_ref[0, k, 1].astype(F32)
        y = jnp.concatenate([xr * hr - xi * hi, xr * hi + xi * hr], axis=0).astype(BF16)
        z = jnp.dot(mi_ref[...], y, preferred_element_type=F32)
        o_ref[0, 0, k] = z[:DFT_N2].astype(o_ref.dtype)
        o_ref[0, 1, k] = z[DFT_N2:].astype(o_ref.dtype)


def spec_conv(a, h, order, mf, mi):
    b, _, n1, _, c = a.shape
    blk = pl.BlockSpec((1, 2, DFT_P, DFT_N2, c), lambda i, k: (i, 0, k, 0, 0))
    mat = pl.BlockSpec((2 * DFT_N2, 2 * DFT_N2), lambda i, k: (0, 0))
    return pl.pallas_call(
        _spec_conv_kernel,
        grid=(b, n1 // DFT_P),
        in_specs=[blk, pl.BlockSpec((1, DFT_P, 2, DFT_N2, c), lambda i, k: (order, k, 0, 0, 0)), mat, mat],
        out_specs=blk,
        out_shape=jax.ShapeDtypeStruct(a.shape, BF16),
        compiler_params=_cparams(("parallel", "parallel")),
        name="spec_conv",
    )(a, h, mf, mi)


def _dft_ainv_kernel(z_ref, g_ref, xg_ref, zin_ref, bias_ref, o_ref):
    c = HY_CH
    n1 = z_ref.shape[2]
    for j in range(DFT_J):
        cols = slice(j * c, (j + 1) * c)
        zz = z_ref[0, :, :, cols].reshape(2 * n1, c)
        y = jnp.dot(g_ref[j], zz, preferred_element_type=F32)
        o_ref[0, :, cols] = xg_ref[0, :, cols] * (y + bias_ref[...] * zin_ref[0, :, cols])


def dft_ainv(z, gi, xg, zin, bias):
    b, _, n1, wid = z.shape
    half = gi.shape[1]
    jc = DFT_J * HY_CH
    tok = pl.BlockSpec((1, half, jc), lambda i, j: (i, 0, j))
    return pl.pallas_call(
        _dft_ainv_kernel,
        grid=(b, DFT_N2 // DFT_J),
        in_specs=[pl.BlockSpec((1, 2, n1, jc), lambda i, j: (i, 0, 0, j)),
                  pl.BlockSpec((DFT_J, half, 2 * n1), lambda i, j: (j, 0, 0)),
                  tok, tok, pl.BlockSpec((1, HY_CH), lambda i, j: (0, 0))],
        out_specs=tok,
        out_shape=jax.ShapeDtypeStruct((b, half, wid), F32),
        compiler_params=_cparams(("parallel", "parallel")),
        name="dft_ainv",
    )(z, gi, xg, zin, bias.reshape(1, HY_CH))


def _ctx_conv_kernel(z_ref, xg_ref, f_ref, mf_ref, mi_ref, sc_ref, bias_ref, o_ref):
    nf = mf_ref.shape[0] // 2
    mf = mf_ref[...]
    xf = jnp.dot(mf, f_ref[0].astype(BF16), preferred_element_type=F32)
    xb = jnp.dot(mf, f_ref[1].astype(BF16), preferred_element_type=F32)
    sc = sc_ref[0]
    hr = (xf[:nf] + xb[:nf]) * sc
    hi = (xf[nf:] - xb[nf:]) * sc
    z = z_ref[0]
    x = jnp.dot(mf, z.astype(BF16), preferred_element_type=F32)
    xr, xi = x[:nf], x[nf:]
    y = jnp.concatenate([xr * hr - xi * hi, xr * hi + xi * hr], axis=0).astype(BF16)
    o_ref[0] = xg_ref[0] * (jnp.dot(mi_ref[...], y, preferred_element_type=F32) + bias_ref[...] * z)


def ctx_conv(z, xg, f, order, scale, bias):
    b, length, c = z.shape
    n = 2 * length
    kk = jnp.arange(n, dtype=jnp.int32)[:, None]
    tn = jnp.arange(length, dtype=jnp.int32)[None, :]
    th = (2.0 * math.pi / n) * ((kk * tn) % n).astype(F32)
    mf = jnp.concatenate([jnp.cos(th), -jnp.sin(th)], axis=0).astype(BF16)
    mi = jnp.concatenate([jnp.cos(th.T), -jnp.sin(th.T)], axis=1).astype(BF16)
    tok = pl.BlockSpec((1, length, c), lambda i: (i, 0, 0))
    return pl.pallas_call(
        _ctx_conv_kernel,
        grid=(b,),
        in_specs=[tok, tok, pl.BlockSpec((2, length, c), lambda i: (order, 0, 0)),
                  pl.BlockSpec((2 * n, length), lambda i: (0, 0)), pl.BlockSpec((length, 2 * n), lambda i: (0, 0)),
                  pl.BlockSpec((1, 1, c), lambda i: (order, 0, 0)), pl.BlockSpec((1, c), lambda i: (0, 0))],
        out_specs=tok,
        out_shape=jax.ShapeDtypeStruct((b, length, c), F32),
        compiler_params=_cparams(("parallel",)),
        name="ctx_conv",
    )(z, xg, f, mf, mi, scale.reshape(-1, 1, c), bias.reshape(1, c))


def hyena(p, conv_w, conv_b, w1, b1, w2, b2, w3, freq, bias, tables):
    b, length, _ = p.shape
    parts = hy_prep(p, conv_w, conv_b)
    f, rnorm = hy_filters(length, w1, b1, w2, b2, w3, freq)
    scale = rnorm / (2 * length)
    z = parts[0]
    if tables is None:
        for o in range(HY_ORDER):
            z = ctx_conv(z, parts[o + 1], f, o, scale, bias[o])
        return z
    ga, gi, mf, mi = tables
    half = ga.shape[2]
    wid = DFT_N2 * HY_CH
    fa = dft_a(f.reshape(2 * HY_ORDER, half, wid), ga)
    n1 = fa.shape[2]
    h = spec_filter(fa.reshape(2 * HY_ORDER, 2, n1, DFT_N2, HY_CH), mf, scale)
    for o in range(HY_ORDER):
        a = dft_a(z.reshape(b, half, wid), ga).reshape(b, 2, n1, DFT_N2, HY_CH)
        zc = spec_conv(a, h, o, mf, mi).reshape(b, 2, n1, wid)
        z = dft_ainv(zc, gi, parts[o + 1].reshape(b, half, wid), z.reshape(b, half, wid), bias[o])
        z = z.reshape(b, length, HY_CH)
    return z


def _hier_moe(h, w_group, b_group, w_router, b_router, w_gate, w_up, w_down):
    n_tok, d = h.shape
    g_logits = (h @ w_group + b_group).astype(F32)
    grp = jnp.argmax(g_logits, axis=-1)
    p_grp = jnp.max(jax.nn.softmax(g_logits, axis=-1), axis=-1, keepdims=True)
    e_logits = (h @ w_router + b_router).astype(F32).reshape(n_tok, MOE_GROUPS, MOE_EPG)
    e_in = e_logits[jnp.arange(n_tok), grp]
    top_v, top_i = lax.top_k(e_in, MOE_TOP_K)
    gates = p_grp * jax.nn.softmax(top_v, axis=-1)
    e_flat = (grp[:, None] * MOE_EPG + top_i).reshape(-1).astype(jnp.int32)
    m_slots = n_tok * MOE_TOP_K
    onehot = (e_flat[:, None] == jnp.arange(MOE_EXPERTS, dtype=jnp.int32)[None, :]).astype(jnp.int32)
    csum = jnp.cumsum(onehot, axis=0)
    rank = jnp.sum(onehot * csum, axis=1) - 1
    counts = csum[-1]
    padded = (counts + MOE_BLOCK - 1) // MOE_BLOCK * MOE_BLOCK
    p_end = jnp.cumsum(padded)
    dest = (p_end - padded)[e_flat] + rank
    n_blocks = -(-(m_slots + MOE_EXPERTS * (MOE_BLOCK - 1)) // MOE_BLOCK)
    slot_tok = jnp.arange(m_slots, dtype=jnp.int32) // MOE_TOP_K
    buf_tok = jnp.zeros((n_blocks * MOE_BLOCK,), jnp.int32).at[dest].set(slot_tok)
    blk_exp = jnp.minimum(jnp.searchsorted(p_end, jnp.arange(n_blocks) * MOE_BLOCK, side='right'),
                          MOE_EXPERTS - 1).astype(jnp.int32)
    xb = h.astype(BF16)[buf_tok]
    yb = expert_blocks(xb, blk_exp, w_gate, w_up, w_down)
    return yb, dest.reshape(n_tok, MOE_TOP_K), gates


def _moe_combine_kernel(x_ref, y0_ref, y1_ref, g_ref, m_ref, o_ref):
    g = g_ref[...]
    f = g[:, 0:1] * y0_ref[...] + g[:, 1:2] * y1_ref[...]
    o_ref[0] = x_ref[0] + m_ref[0] * f


def moe_combine(x, yb, dest, gates, mod):
    b, t, d = x.shape
    y0 = yb[dest[:, 0]]
    y1 = yb[dest[:, 1]]
    tt = _tile(t, 512)
    nt = t // tt
    row = pl.BlockSpec((tt, d), lambda i, j: (i * nt + j, 0))
    return pl.pallas_call(
        _moe_combine_kernel,
        grid=(b, nt),
        in_specs=[pl.BlockSpec((1, tt, d), lambda i, j: (i, j, 0)), row, row,
                  pl.BlockSpec((tt, MOE_TOP_K), lambda i, j: (i * nt + j, 0)),
                  pl.BlockSpec((1, 1, d), lambda i, j: (i, 0, 0))],
        out_specs=pl.BlockSpec((1, tt, d), lambda i, j: (i, j, 0)),
        out_shape=jax.ShapeDtypeStruct((b, t, d), F32),
        compiler_params=_cparams(("parallel", "parallel")),
        name="moe_combine",
    )(x, y0, y1, gates, mod.reshape(b, 1, d))


def _permute_w_in(w):
    d = w.shape[0]
    sizes = (512, 512, 512, 1536, 512, 16, 512, 256, 1536, 4096)
    offs = np.cumsum((0,) + sizes)
    dq, dk, dv, mlqkv, mlo, mlg, gq, gkv, hy, gate = [w[:, offs[i]:offs[i + 1]] for i in range(10)]
    pad = jnp.zeros((d, N_P - C_MLG - 16), w.dtype)
    return jnp.concatenate([gate, dq, dk, dv, mlqkv, mlo, gq, hy, gkv, mlg, pad], axis=1)


def kernel(x, c, ctx, c_ctx, w_ada, b_ada, norm1_g, norm2_g, w_in, diff_lam, diff_norm_g, ml_conv_w, ml_conv_b, ml_gate_b, ml_norm_g, gqa_qnorm_g, gqa_knorm_g, hy_conv_w, hy_conv_b, hy_f_w1, hy_f_b1, hy_f_w2, hy_f_b2, hy_f_w3, hy_f_freq, hy_bias, w_branch, w_out, moe_w_group, moe_b_group, moe_w_router, moe_b_router, moe_w_gate, moe_w_up, moe_w_down, final_norm_g):
    b, n, d = x.shape
    n_ctx = ctx.shape[1]
    depth = w_in.shape[0]
    tk = _tile(n, 2048)
    tables = _rope_tables(n)
    dft_tables = _dft_tables(n)
    sc = jax.nn.silu(c)
    scx = jax.nn.silu(c_ctx)
    xs, cs = x, ctx
    for l in range(depth):
        need_ctx = l < depth - 1
        mod_l = jnp.split(sc @ w_ada[l] + b_ada[l], 6, axis=-1)
        mod_c = [jnp.broadcast_to(m, (b, d)) for m in jnp.split(scx @ w_ada[l] + b_ada[l], 6, axis=-1)]
        w_p = _permute_w_in(w_in[l]).astype(BF16)
        hl = norm_mod(xs, norm1_g[l], mod_l[0], mod_l[1], BF16)
        hc = norm_mod(cs, norm1_g[l], mod_c[0], mod_c[1], BF16)
        p_l = matmul(hl.reshape(b * n, d), w_p, BF16, tm=1024).reshape(b, n, N_P)
        p_c = matmul(hc.reshape(b * n_ctx, d), w_p, BF16, tm=1024).reshape(b, n_ctx, N_P)

        dq_l, dkt_l, dv_l, gq_l, gkt_l, gv_l = attn_prep(p_l, tables, gqa_qnorm_g[l], gqa_knorm_g[l], tk)
        dq_c, dkt_c, dv_c, gq_c, gkt_c, gv_c = attn_prep(p_c, None, gqa_qnorm_g[l], gqa_knorm_g[l], n_ctx)
        lam_init = 0.8 - 0.6 * math.exp(-0.3 * l)
        lp = diff_lam[l].astype(F32)
        lam = jnp.exp(jnp.sum(lp[0] * lp[1])) - jnp.exp(jnp.sum(lp[2] * lp[3])) + lam_init
        yl_a = diff_attention(dq_l, dkt_c, dv_c, dkt_l, dv_l, lam, diff_norm_g[l], 1.0 - lam_init, 1024)
        yl_c = gqa_attention(gq_l, gkt_c, gv_c, gkt_l, gv_l, 1024)
        if need_ctx:
            yc_a = diff_attention(dq_c, dkt_c, dv_c, None, None, lam, diff_norm_g[l], 1.0 - lam_init, 256)
            yc_c = gqa_attention(gq_c, gkt_c, gv_c, None, None, 256)

        hs_c, hs_l = mlstm_branch(p_c, p_l, ml_conv_w[l], ml_conv_b[l], ml_gate_b[l], need_ctx)
        hy_args = (hy_conv_w[l], hy_conv_b[l], hy_f_w1[l], hy_f_b1[l], hy_f_w2[l], hy_f_b2[l],
                   hy_f_w3[l], hy_f_freq[l], hy_bias[l])
        yl_d = hyena(p_l, *hy_args, dft_tables)

        wb = w_branch[l].astype(BF16)
        wo = w_out[l].astype(BF16)
        xs = merge(yl_a, hs_l, yl_c, yl_d, p_l, ml_norm_g[l], wb, wo, xs, mod_l[2])
        hl2 = norm_mod(xs, norm2_g[l], mod_l[3], mod_l[4], F32)
        moe_args = (moe_w_group[l], moe_b_group[l], moe_w_router[l], moe_b_router[l],
                    moe_w_gate[l], moe_w_up[l], moe_w_down[l])
        if need_ctx:
            yc_d = hyena(p_c, *hy_args, None)
            cs = merge(yc_a, hs_c, yc_c, yc_d, p_c, ml_norm_g[l], wb, wo, cs, mod_c[2])
            hc2 = norm_mod(cs, norm2_g[l], mod_c[3], mod_c[4], F32)
            yb, dest, gates = _hier_moe(jnp.concatenate([hc2.reshape(-1, d), hl2.reshape(-1, d)], axis=0), *moe_args)
            nc = b * n_ctx
            cs = moe_combine(cs, yb, dest[:nc], gates[:nc], mod_c[5])
            dest, gates = dest[nc:], gates[nc:]
        else:
            yb, dest, gates = _hier_moe(hl2.reshape(-1, d), *moe_args)
        xs = moe_combine(xs, yb, dest, gates, mod_l[5])
    zero = jnp.zeros((b, d), F32)
    return norm_mod(xs, final_norm_g, zero, zero, F32)
```

```python
import functools
import math

import jax
import jax.numpy as jnp
import numpy as np
from jax import lax
from jax.experimental import pallas as pl
from jax.experimental.pallas import tpu as pltpu

F32 = jnp.float32
BF16 = jnp.bfloat16

EPS = 1e-6
ROPE_BASE = 10000.0
GRID_W = 64

DIFF_HEADS = 4
DIFF_DIM = 64
DIFF_VDIM = 128
ML_HEADS = 4
ML_DIM = 128
ML_CHUNK = 64
GQA_HEADS = 8
GQA_KV = 2
GQA_DIM = 64
HY_CH = 512
HY_ORDER = 2
FILTER_BANDS = 16
FILTER_SHIFT = 0.05
DECAY_TARGET = 1e-2
FAST_DECAY_PCT = 0.3
SLOW_DECAY_PCT = 1.5
N_BRANCH = 4
MOE_GROUPS = 4
MOE_EPG = 8
MOE_EXPERTS = MOE_GROUPS * MOE_EPG
MOE_TOP_K = 2
MOE_BLOCK = 256

LANES = 128
VMEM_LIMIT = 48 * 1024 * 1024

C_GATE = 0
C_DQ = 4096
C_DK = 4608
C_DV = 5120
C_MLQ = 5632
C_MLK = 6144
C_MLV = 6656
C_MLO = 7168
C_GQ = 7680
C_HY = 8192
C_GK = 9728
C_GV = 9856
C_MLG = 9984
N_P = 10240

QSCALE = (DIFF_DIM ** -0.5) * math.log2(math.e)


def _cparams(sem):
    return pltpu.CompilerParams(dimension_semantics=sem, vmem_limit_bytes=VMEM_LIMIT)


def _tile(n, target):
    if n <= target:
        return n
    for t in range(target, 7, -1):
        if n % t == 0 and t % 8 == 0:
            return t
    return n


def _norm_mod_kernel(x_ref, g_ref, sh_ref, sc_ref, o_ref):
    x = x_ref[0]
    y = x * lax.rsqrt(jnp.mean(x * x, axis=-1, keepdims=True) + EPS)
    y = y * g_ref[...]
    o_ref[0] = (y * (1.0 + sc_ref[0]) + sh_ref[0]).astype(o_ref.dtype)


def norm_mod(x, g, shift, scale, out_dtype):
    b, t, d = x.shape
    tt = _tile(t, 512)
    return pl.pallas_call(
        _norm_mod_kernel,
        grid=(b, t // tt),
        in_specs=[pl.BlockSpec((1, tt, d), lambda i, j: (i, j, 0)),
                  pl.BlockSpec((1, d), lambda i, j: (0, 0)),
                  pl.BlockSpec((1, 1, d), lambda i, j: (i, 0, 0)),
                  pl.BlockSpec((1, 1, d), lambda i, j: (i, 0, 0))],
        out_specs=pl.BlockSpec((1, tt, d), lambda i, j: (i, j, 0)),
        out_shape=jax.ShapeDtypeStruct((b, t, d), out_dtype),
        compiler_params=_cparams(("parallel", "parallel")),
        name="norm_mod",
    )(x, g.reshape(1, d), shift.reshape(b, 1, d), scale.reshape(b, 1, d))


def _mm_kernel(a_ref, w_ref, o_ref):
    o_ref[...] = jnp.dot(a_ref[...], w_ref[...], preferred_element_type=F32).astype(o_ref.dtype)


def matmul(a, w, out_dtype, tm=512, tn=1024):
    m, k = a.shape
    n = w.shape[1]
    tm = _tile(m, tm)
    tn = _tile(n, tn)
    return pl.pallas_call(
        _mm_kernel,
        grid=(m // tm, n // tn),
        in_specs=[pl.BlockSpec((tm, k), lambda i, j: (i, 0)),
                  pl.BlockSpec((k, tn), lambda i, j: (0, j))],
        out_specs=pl.BlockSpec((tm, tn), lambda i, j: (i, j)),
        out_shape=jax.ShapeDtypeStruct((m, n), out_dtype),
        compiler_params=_cparams(("parallel", "parallel")),
        name="matmul",
    )(a, w)


def _rope(x, cos, sa, sb):
    xa = pltpu.roll(x, LANES - 16, axis=1)
    xb = pltpu.roll(x, 16, axis=1)
    return x * cos + xa * sa + xb * sb


def _seg_rmsnorm(x, g):
    lane = lax.broadcasted_iota(jnp.int32, x.shape, 1)
    lo = lane < GQA_DIM
    ss = x * x
    s_lo = jnp.sum(jnp.where(lo, ss, 0.0), axis=-1, keepdims=True)
    s_hi = jnp.sum(jnp.where(lo, 0.0, ss), axis=-1, keepdims=True)
    r = jnp.where(lo, lax.rsqrt(s_lo * (1.0 / GQA_DIM) + EPS), lax.rsqrt(s_hi * (1.0 / GQA_DIM) + EPS))
    return x * r * g


def _prep_kernel(dq_ref, dk_ref, dv_ref, gq_ref, gkv_ref, cos_ref, sa_ref, sb_ref, qg_ref, kg_ref,
                 dq_o, dkt_o, dv_o, gq_o, gkt_o, gv_o, *, rope):
    if rope:
        cos, sa, sb = cos_ref[...], sa_ref[...], sb_ref[...]
        rot = lambda x: _rope(x, cos, sa, sb)
    else:
        rot = lambda x: x
    qg = qg_ref[...]
    kg = kg_ref[...]
    for j in range(DIFF_HEADS):
        sl = slice(j * LANES, (j + 1) * LANES)
        dq_o[0, :, sl] = (rot(dq_ref[0, :, sl].astype(F32)) * QSCALE).astype(BF16)
        kt = rot(dk_ref[0, :, sl].astype(F32)).T
        dkt_o[0, j, 0, 0] = kt[:DIFF_DIM].astype(BF16)
        dkt_o[0, j, 1, 0] = kt[DIFF_DIM:].astype(BF16)
        gq_o[0, :, sl] = (rot(_seg_rmsnorm(gq_ref[0, :, sl].astype(F32), qg)) * QSCALE).astype(BF16)
    dv_o[0] = dv_ref[0].astype(BF16)
    kt = rot(_seg_rmsnorm(gkv_ref[0, :, :LANES].astype(F32), kg)).T
    gkt_o[0, 0, 0] = kt[:GQA_DIM].astype(BF16)
    gkt_o[0, 1, 0] = kt[GQA_DIM:].astype(BF16)
    v = gkv_ref[0, :, LANES:].astype(BF16)
    gv_o[0, 0] = v[:, :GQA_DIM]
    gv_o[0, 1] = v[:, GQA_DIM:]


def attn_prep(p, tables, q_g, k_g, tk):
    b, t, _ = p.shape
    rope = tables is not None
    if rope:
        cos, sa, sb = tables
    else:
        cos = sa = sb = jnp.zeros((t, LANES), F32)
    nck = t // tk
    tp = _tile(tk, 512)
    sub = tk // tp
    w512 = lambda c: pl.BlockSpec((1, tp, 512), lambda i, j, c=c: (i, j, c // 512))
    tab = pl.BlockSpec((tp, LANES), lambda i, j: (j, 0))
    vec = pl.BlockSpec((1, LANES), lambda i, j: (0, 0))
    outs = pl.pallas_call(
        functools.partial(_prep_kernel, rope=rope),
        grid=(b, t // tp),
        in_specs=[w512(C_DQ), w512(C_DK), w512(C_DV), w512(C_GQ),
                  pl.BlockSpec((1, tp, 256), lambda i, j: (i, j, C_GK // 256)),
                  tab, tab, tab, vec, vec],
        out_specs=[pl.BlockSpec((1, tp, 512), lambda i, j: (i, j, 0)),
                   pl.BlockSpec((1, DIFF_HEADS, 2, 1, DIFF_DIM, tp), lambda i, j: (i, 0, 0, j // sub, 0, j % sub)),
                   pl.BlockSpec((1, tp, 512), lambda i, j: (i, j, 0)),
                   pl.BlockSpec((1, tp, 512), lambda i, j: (i, j, 0)),
                   pl.BlockSpec((1, GQA_KV, 1, GQA_DIM, tp), lambda i, j: (i, 0, j // sub, 0, j % sub)),
                   pl.BlockSpec((1, GQA_KV, tp, GQA_DIM), lambda i, j: (i, 0, j, 0))],
        out_shape=[jax.ShapeDtypeStruct((b, t, 512), BF16),
                   jax.ShapeDtypeStruct((b, DIFF_HEADS, 2, nck, DIFF_DIM, tk), BF16),
                   jax.ShapeDtypeStruct((b, t, 512), BF16),
                   jax.ShapeDtypeStruct((b, t, 512), BF16),
                   jax.ShapeDtypeStruct((b, GQA_KV, nck, GQA_DIM, tk), BF16),
                   jax.ShapeDtypeStruct((b, GQA_KV, t, GQA_DIM), BF16)],
        compiler_params=_cparams(("parallel", "parallel")),
        name="attn_prep",
    )(p, p, p, p, p, cos, sa, sb,
      jnp.tile(q_g, 2).reshape(1, LANES), jnp.tile(k_g, 2).reshape(1, LANES))
    return outs


def _flash_step(q, kt, v, m_ref, l_ref, acc_ref):
    s = jnp.dot(q, kt, preferred_element_type=F32)
    tk = s.shape[1]
    m_prev = m_ref[...]
    m_next = jnp.maximum(m_prev, jnp.max(s, axis=1, keepdims=True))
    alpha = jnp.exp2(m_prev - m_next)
    p = jnp.exp2(s - jnp.concatenate([m_next] * (tk // LANES), axis=1))
    l_ref[...] = alpha * l_ref[...] + jnp.sum(p, axis=1, keepdims=True)
    dv = acc_ref.shape[-1]
    acc_ref[...] = acc_ref[...] * alpha[:, :dv] + jnp.dot(p.astype(BF16), v, preferred_element_type=F32)
    m_ref[...] = m_next


def _attn_body(q_ref, ktc, vc, ktl, vl, m_sc, l_sc, acc_sc, *, n_lat, tk):
    m_sc[...] = jnp.full(m_sc.shape, -jnp.inf, F32)
    l_sc[...] = jnp.zeros(l_sc.shape, F32)
    acc_sc[...] = jnp.zeros(acc_sc.shape, F32)
    q = q_ref[0]
    qs = (q[:, :DIFF_DIM], q[:, DIFF_DIM:])
    for c in range(2):
        _flash_step(qs[c], ktc(c), vc(c), m_sc.at[c], l_sc.at[c], acc_sc.at[c])
    if n_lat:
        def body(i, carry):
            for c in range(2):
                _flash_step(qs[c], ktl(c, i), vl(c, i), m_sc.at[c], l_sc.at[c], acc_sc.at[c])
            return carry
        lax.fori_loop(0, n_lat, body, 0)


def _diff_attn_kernel(*refs, n_lat, tk, out_scale):
    if n_lat:
        q_ref, ktc_ref, vc_ref, ktl_ref, vl_ref, lam_ref, g_ref, o_ref, m_sc, l_sc, acc_sc = refs
        ktl = lambda c, i: ktl_ref[0, 0, c, i]
        vl = lambda c, i: vl_ref[0, pl.ds(pl.multiple_of(i * tk, tk), tk), :]
    else:
        q_ref, ktc_ref, vc_ref, lam_ref, g_ref, o_ref, m_sc, l_sc, acc_sc = refs
        ktl = vl = None
    _attn_body(q_ref, lambda c: ktc_ref[0, 0, c, 0], lambda c: vc_ref[0], ktl, vl,
               m_sc, l_sc, acc_sc, n_lat=n_lat, tk=tk)
    o0 = acc_sc[0] / l_sc[0]
    o1 = acc_sc[1] / l_sc[1]
    o = o0 - lam_ref[...] * o1
    o = o * lax.rsqrt(jnp.mean(o * o, axis=-1, keepdims=True) + EPS)
    o_ref[0] = (o * g_ref[...] * out_scale).astype(o_ref.dtype)


def diff_attention(q, ktc, vc, ktl, vl, lam, norm_g, out_scale, tq):
    b, t, _ = q.shape
    sc = vc.shape[1]
    tq = _tile(t, tq)
    n_lat, tk = (ktl.shape[3], ktl.shape[5]) if ktl is not None else (0, 0)
    in_specs = [pl.BlockSpec((1, tq, LANES), lambda i, h, j: (i, j, h)),
                pl.BlockSpec((1, 1, 2, 1, DIFF_DIM, sc), lambda i, h, j: (i, h, 0, 0, 0, 0)),
                pl.BlockSpec((1, sc, LANES), lambda i, h, j: (i, 0, h))]
    args = [q, ktc, vc]
    if n_lat:
        in_specs += [pl.BlockSpec((1, 1, 2, n_lat, DIFF_DIM, tk), lambda i, h, j: (i, h, 0, 0, 0, 0)),
                     pl.BlockSpec((1, n_lat * tk, LANES), lambda i, h, j: (i, 0, h))]
        args += [ktl, vl]
    vec = pl.BlockSpec((1, LANES), lambda i, h, j: (0, 0))
    in_specs += [vec, vec]
    args += [jnp.full((1, LANES), lam, F32), norm_g.reshape(1, LANES)]
    return pl.pallas_call(
        functools.partial(_diff_attn_kernel, n_lat=n_lat, tk=tk, out_scale=out_scale),
        grid=(b, DIFF_HEADS, t // tq),
        in_specs=in_specs,
        out_specs=pl.BlockSpec((1, tq, LANES), lambda i, h, j: (i, j, h)),
        out_shape=jax.ShapeDtypeStruct((b, t, DIFF_HEADS * DIFF_VDIM), BF16),
        scratch_shapes=[pltpu.VMEM((2, tq, LANES), F32), pltpu.VMEM((2, tq, LANES), F32),
                        pltpu.VMEM((2, tq, DIFF_VDIM), F32)],
        compiler_params=_cparams(("parallel", "parallel", "parallel")),
        name="diff_attn",
    )(*args)


def _gqa_attn_kernel(*refs, n_lat, tk):
    if n_lat:
        q_ref, ktc_ref, vc_ref, ktl_ref, vl_ref, o_ref, m_sc, l_sc, acc_sc = refs
        ktl = lambda c, i: ktl_ref[0, 0, i]
        vl = lambda c, i: vl_ref[0, 0, pl.ds(pl.multiple_of(i * tk, tk), tk), :]
    else:
        q_ref, ktc_ref, vc_ref, o_ref, m_sc, l_sc, acc_sc = refs
        ktl = vl = None
    _attn_body(q_ref, lambda c: ktc_ref[0, 0, 0], lambda c: vc_ref[0, 0], ktl, vl,
               m_sc, l_sc, acc_sc, n_lat=n_lat, tk=tk)
    o0 = acc_sc[0] / l_sc[0][:, :GQA_DIM]
    o1 = acc_sc[1] / l_sc[1][:, :GQA_DIM]
    o_ref[0] = jnp.concatenate([o0, o1], axis=-1).astype(o_ref.dtype)


def gqa_attention(q, ktc, vc, ktl, vl, tq):
    b, t, _ = q.shape
    sc = vc.shape[2]
    tq = _tile(t, tq)
    n_lat, tk = (ktl.shape[2], ktl.shape[4]) if ktl is not None else (0, 0)
    pairs = GQA_HEADS // 2
    grp = lambda h: h // (pairs // GQA_KV)
    in_specs = [pl.BlockSpec((1, tq, LANES), lambda i, h, j: (i, j, h)),
                pl.BlockSpec((1, 1, 1, GQA_DIM, sc), lambda i, h, j: (i, grp(h), 0, 0, 0)),
                pl.BlockSpec((1, 1, sc, GQA_DIM), lambda i, h, j: (i, grp(h), 0, 0))]
    args = [q, ktc, vc]
    if n_lat:
        in_specs += [pl.BlockSpec((1, 1, n_lat, GQA_DIM, tk), lambda i, h, j: (i, grp(h), 0, 0, 0)),
                     pl.BlockSpec((1, 1, n_lat * tk, GQA_DIM), lambda i, h, j: (i, grp(h), 0, 0))]
        args += [ktl, vl]
    return pl.pallas_call(
        functools.partial(_gqa_attn_kernel, n_lat=n_lat, tk=tk),
        grid=(b, pairs, t // tq),
        in_specs=in_specs,
        out_specs=pl.BlockSpec((1, tq, LANES), lambda i, h, j: (i, j, h)),
        out_shape=jax.ShapeDtypeStruct((b, t, GQA_HEADS * GQA_DIM), BF16),
        scratch_shapes=[pltpu.VMEM((2, tq, LANES), F32), pltpu.VMEM((2, tq, LANES), F32),
                        pltpu.VMEM((2, tq, GQA_DIM), F32)],
        compiler_params=_cparams(("parallel", "parallel", "parallel")),
        name="gqa_attn",
    )(*args)


ML_STEP = 256
HALO = 16


def _halo_rows(prev_ref, next_ref, j, nblk):
    prev = jnp.where(j > 0, prev_ref[0, HALO - 1:HALO, :].astype(F32), 0.0)
    nxt = jnp.where(j < nblk - 1, next_ref[0, 0:1, :].astype(F32), 0.0)
    return prev, nxt


def _conv3(x, prev, nxt, w, bias):
    tt = x.shape[0]
    row = lax.broadcasted_iota(jnp.int32, x.shape, 0)
    xm = jnp.where(row == 0, prev, pltpu.roll(x, 1, axis=0))
    xp = jnp.where(row == tt - 1, nxt, pltpu.roll(x, tt - 1, axis=0))
    return xm * w[0:1] + x * w[1:2] + xp * w[2:3] + bias


def _halo_specs(tt, width, col_block, t):
    nb = t // HALO
    prev = pl.BlockSpec((1, HALO, width), lambda i, j: (i, jnp.maximum(j * (tt // HALO) - 1, 0), col_block))
    nxt = pl.BlockSpec((1, HALO, width), lambda i, j: (i, jnp.minimum((j + 1) * (tt // HALO), nb - 1), col_block))
    return prev, nxt


def _ml_prep_kernel(q_ref, qp_ref, qn_ref, k_ref, kp_ref, kn_ref, g_ref, w_ref, b_ref, gb_ref, q_o, k_o, g_o):
    j = pl.program_id(1)
    nblk = pl.num_programs(1)
    w = ML_HEADS * ML_DIM
    qp, qn = _halo_rows(qp_ref, qn_ref, j, nblk)
    kp, kn = _halo_rows(kp_ref, kn_ref, j, nblk)
    q = _conv3(q_ref[0].astype(F32), qp, qn, w_ref[:, :w], b_ref[:, :w])
    k = _conv3(k_ref[0].astype(F32), kp, kn, w_ref[:, w:], b_ref[:, w:])
    q_o[0] = q * jax.nn.sigmoid(q)
    k_o[0] = (k * jax.nn.sigmoid(k)) * ML_DIM ** -0.5
    x = g_ref[0].astype(F32) + gb_ref[...]
    lane = lax.broadcasted_iota(jnp.int32, x.shape, 1)
    log_sig = jnp.minimum(x, 0.0) - jnp.log(1.0 + jnp.exp(-jnp.abs(x)))
    g_o[0] = jnp.where((lane % 8) >= ML_HEADS, log_sig, x)


def ml_prep(p, conv_w, conv_b, gate_b):
    b, t, _ = p.shape
    tt = _tile(t, 512)
    w = ML_HEADS * ML_DIM
    blk = lambda c: pl.BlockSpec((1, tt, w), lambda i, j, c=c: (i, j, c // w))
    qp, qn = _halo_specs(tt, w, C_MLQ // w, t)
    kp, kn = _halo_specs(tt, w, C_MLK // w, t)
    gb = jnp.zeros((1, LANES), F32).at[0, :4 * ML_HEADS].set(gate_b.reshape(-1))
    return pl.pallas_call(
        _ml_prep_kernel,
        grid=(b, t // tt),
        in_specs=[blk(C_MLQ), qp, qn, blk(C_MLK), kp, kn,
                  pl.BlockSpec((1, tt, LANES), lambda i, j: (i, j, C_MLG // LANES)),
                  pl.BlockSpec((3, 2 * w), lambda i, j: (0, 0)),
                  pl.BlockSpec((1, 2 * w), lambda i, j: (0, 0)),
                  pl.BlockSpec((1, LANES), lambda i, j: (0, 0))],
        out_specs=[pl.BlockSpec((1, tt, w), lambda i, j: (i, j, 0)),
                   pl.BlockSpec((1, tt, w), lambda i, j: (i, j, 0)),
                   pl.BlockSpec((1, tt, LANES), lambda i, j: (i, j, 0))],
        out_shape=[jax.ShapeDtypeStruct((b, t, w), F32), jax.ShapeDtypeStruct((b, t, w), F32),
                   jax.ShapeDtypeStruct((b, t, LANES), F32)],
        compiler_params=_cparams(("parallel", "parallel")),
        name="ml_prep",
    )(p, p, p, p, p, p, p, conv_w, conv_b.reshape(1, 2 * w), gb)


def _ml_chunk_head(q, k, v, li_c, bc_c, li_r, bc_r, b_last, tri, c_ref, n_ref, m_ref):
    m = m_ref[:, 0:1]
    c_mat = c_ref[...]
    n_row = n_ref[...]
    dmat = jnp.where(tri, bc_c - bc_r + li_r, -jnp.inf)
    inter = bc_c + m
    m_t = jnp.maximum(inter, jnp.max(dmat, axis=1, keepdims=True))
    w_intra = jnp.exp(dmat - m_t)
    w_inter = jnp.exp(inter - m_t)
    qb = q.astype(BF16)
    s = lax.dot_general(qb, k.astype(BF16), (((1,), (1,)), ((), ())), preferred_element_type=F32) * w_intra
    num = (jnp.dot(s.astype(BF16), v.astype(BF16), preferred_element_type=F32)
           + w_inter * jnp.dot(qb, c_mat.astype(BF16), preferred_element_type=F32))
    den = jnp.sum(s, axis=1, keepdims=True) + w_inter * jnp.sum(q * n_row, axis=1, keepdims=True)
    h = num / jnp.maximum(jnp.abs(den), jnp.exp(-m_t))
    g = b_last - bc_c + li_c
    m_new = jnp.maximum(b_last + m, jnp.max(g, axis=0, keepdims=True))
    kw = k * jnp.exp(g - m_new)
    wc = jnp.exp(b_last + m - m_new)
    c_ref[...] = wc * c_mat + jnp.dot(kw.T.astype(BF16), v.astype(BF16), preferred_element_type=F32)
    n_ref[...] = wc * n_row + jnp.sum(kw, axis=0, keepdims=True)
    m_ref[...] = jnp.broadcast_to(m_new, m_ref.shape)
    return h


def _ml_scan_kernel(q_ref, k_ref, v_ref, g_ref, c0_ref, n0_ref, m0_ref, h_o, c1_o, n1_o, m1_o,
                    c_sc, n_sc, m_sc, *, direction):
    j = pl.program_id(0)

    @pl.when(j == 0)
    def _():
        c_sc[...] = c0_ref[...]
        n_sc[...] = n0_ref[...]
        m_sc[...] = m0_ref[...]

    r = lax.broadcasted_iota(jnp.int32, (ML_CHUNK, ML_CHUNK), 0)
    s = lax.broadcasted_iota(jnp.int32, (ML_CHUNK, ML_CHUNK), 1)
    tri = (s >= r) if direction else (s <= r)
    tri_f = tri.astype(F32)
    n_chunks = q_ref.shape[1] // ML_CHUNK
    order = range(n_chunks - 1, -1, -1) if direction else range(n_chunks)
    last = 0 if direction else ML_CHUNK - 1
    for c in order:
        rows = slice(c * ML_CHUNK, (c + 1) * ML_CHUNK)
        for bi in range(q_ref.shape[0]):
            gch = g_ref[bi, rows, :]
            bc = jnp.dot(tri_f, gch, preferred_element_type=F32, precision=lax.Precision.HIGHEST)
            gt = gch.T
            bct = bc.T
            for hd in range(ML_HEADS):
                ci = 2 * ML_HEADS * direction + hd
                cf = ci + ML_HEADS
                cols = slice(hd * ML_DIM, (hd + 1) * ML_DIM)
                h = _ml_chunk_head(q_ref[bi, rows, cols], k_ref[bi, rows, cols], v_ref[bi, rows, cols],
                                   gch[:, ci:ci + 1], bc[:, cf:cf + 1], gt[ci:ci + 1, :], bct[cf:cf + 1, :],
                                   bc[last:last + 1, cf:cf + 1], tri,
                                   c_sc.at[bi, hd], n_sc.at[bi, hd], m_sc.at[bi, hd])
                h_o[bi, rows, cols] = h

    @pl.when(j == pl.num_programs(0) - 1)
    def _():
        c1_o[...] = c_sc[...]
        n1_o[...] = n_sc[...]
        m1_o[...] = m_sc[...]


def ml_scan(q, k, p, g, state, direction):
    b, t, w = q.shape
    ts = _tile(t, ML_STEP)
    nst = t // ts
    tok = (lambda j: (0, nst - 1 - j, 0)) if direction else (lambda j: (0, j, 0))
    tokv = (lambda j: (0, nst - 1 - j, C_MLV // w)) if direction else (lambda j: (0, j, C_MLV // w))
    st_shapes = [(b, ML_HEADS, ML_DIM, ML_DIM), (b, ML_HEADS, 1, ML_DIM), (b, ML_HEADS, 1, ML_DIM)]
    st_specs = [pl.BlockSpec(s, lambda j: (0, 0, 0, 0)) for s in st_shapes]
    h, c1, n1, m1 = pl.pallas_call(
        functools.partial(_ml_scan_kernel, direction=direction),
        grid=(nst,),
        in_specs=[pl.BlockSpec((b, ts, w), tok), pl.BlockSpec((b, ts, w), tok), pl.BlockSpec((b, ts, w), tokv),
                  pl.BlockSpec((b, ts, LANES), tok)] + st_specs,
        out_specs=[pl.BlockSpec((b, ts, w), tok)] + st_specs,
        out_shape=[jax.ShapeDtypeStruct((b, t, w), F32)] + [jax.ShapeDtypeStruct(s, F32) for s in st_shapes],
        scratch_shapes=[pltpu.VMEM(s, F32) for s in st_shapes],
        compiler_params=_cparams(("arbitrary",)),
        name="ml_scan",
    )(q, k, p, g, *state)
    return h, (c1, n1, m1)


def mlstm_branch(p_c, p_l, conv_w, conv_b, gate_b, need_ctx):
    b = p_l.shape[0]
    qc, kc, gc = ml_prep(p_c, conv_w, conv_b, gate_b)
    ql, kl, gl = ml_prep(p_l, conv_w, conv_b, gate_b)
    zero = (jnp.zeros((b, ML_HEADS, ML_DIM, ML_DIM), F32), jnp.zeros((b, ML_HEADS, 1, ML_DIM), F32),
            jnp.zeros((b, ML_HEADS, 1, ML_DIM), F32))
    hs_c, hs_l = [], []
    for direction in (0, 1):
        hc, st = ml_scan(qc, kc, p_c, gc, zero, direction)
        hl, _ = ml_scan(ql, kl, p_l, gl, st, direction)
        hs_c.append(hc)
        hs_l.append(hl)
    return (tuple(hs_c) if need_ctx else None), tuple(hs_l)


def _merge_kernel(ya_ref, hf_ref, hb_ref, yc_ref, yd_ref, og_ref, g_ref, mg_ref, wb_ref, wo_ref, x_ref, gate_ref,
                  o_ref):
    d = x_ref.shape[-1]
    mg = mg_ref[...]
    yb = []
    for hd in range(ML_HEADS):
        cols = slice(hd * ML_DIM, (hd + 1) * ML_DIM)
        h = hf_ref[0, :, cols] + hb_ref[0, :, cols]
        h = h * lax.rsqrt(jnp.mean(h * h, axis=-1, keepdims=True) + EPS) * mg
        yb.append((h * jax.nn.sigmoid(og_ref[0, :, cols].astype(F32))).astype(BF16))
    ys = (ya_ref[0].astype(BF16), jnp.concatenate(yb, axis=-1), yc_ref[0].astype(BF16), yd_ref[0].astype(BF16))
    acc = None
    for n, y in enumerate(ys):
        t = jnp.dot(y, wb_ref[n], preferred_element_type=F32)
        t = jax.nn.sigmoid(g_ref[0, :, n * d:(n + 1) * d].astype(F32)) * t
        acc = t if acc is None else acc + t
    z = jnp.dot(acc.astype(BF16), wo_ref[...], preferred_element_type=F32)
    o_ref[0] = x_ref[0] + gate_ref[0] * z


def merge(ya, hs, yc, yd, p, ml_norm_g, w_branch, w_out, x, gate):
    b, t, d = x.shape
    tm = _tile(t, 256)
    w = ML_HEADS * ML_DIM
    ysp = pl.BlockSpec((1, tm, w), lambda i, j: (i, j, 0))
    return pl.pallas_call(
        _merge_kernel,
        grid=(b, t // tm),
        in_specs=[ysp, ysp, ysp, ysp, ysp,
                  pl.BlockSpec((1, tm, w), lambda i, j: (i, j, C_MLO // w)),
                  pl.BlockSpec((1, tm, N_BRANCH * d), lambda i, j: (i, j, C_GATE // (N_BRANCH * d))),
                  pl.BlockSpec((1, ML_DIM), lambda i, j: (0, 0)),
                  pl.BlockSpec((N_BRANCH, w, d), lambda i, j: (0, 0, 0)),
                  pl.BlockSpec((d, d), lambda i, j: (0, 0)),
                  pl.BlockSpec((1, tm, d), lambda i, j: (i, j, 0)),
                  pl.BlockSpec((1, 1, d), lambda i, j: (i, 0, 0))],
        out_specs=pl.BlockSpec((1, tm, d), lambda i, j: (i, j, 0)),
        out_shape=jax.ShapeDtypeStruct((b, t, d), F32),
        compiler_params=_cparams(("parallel", "parallel")),
        name="merge",
    )(ya, hs[0], hs[1], yc, yd, p, p, ml_norm_g.reshape(1, ML_DIM), w_branch, w_out, x, gate.reshape(b, 1, d))


def _expert_kernel(be_ref, x_ref, wg_ref, wu_ref, wd_ref, o_ref, wg_sc, wu_sc, wd_sc):
    i = pl.program_id(0)

    @pl.when(jnp.logical_or(i == 0, be_ref[i] != be_ref[jnp.maximum(i - 1, 0)]))
    def _():
        wg_sc[...] = wg_ref[0].astype(BF16)
        wu_sc[...] = wu_ref[0].astype(BF16)
        wd_sc[...] = wd_ref[0].astype(BF16)

    x = x_ref[...]
    a = jnp.dot(x, wg_sc[...], preferred_element_type=F32)
    u = jnp.dot(x, wu_sc[...], preferred_element_type=F32)
    h = (a * jax.nn.sigmoid(a)) * u
    o_ref[...] = jnp.dot(h.astype(BF16), wd_sc[...], preferred_element_type=F32)


def expert_blocks(xb, blk_exp, w_gate, w_up, w_down):
    m, d = xb.shape
    hdim = w_gate.shape[-1]
    n_blocks = m // MOE_BLOCK
    grid_spec = pltpu.PrefetchScalarGridSpec(
        num_scalar_prefetch=1,
        grid=(n_blocks,),
        in_specs=[pl.BlockSpec((MOE_BLOCK, d), lambda i, be: (i, 0)),
                  pl.BlockSpec((1, d, hdim), lambda i, be: (be[i], 0, 0)),
                  pl.BlockSpec((1, d, hdim), lambda i, be: (be[i], 0, 0)),
                  pl.BlockSpec((1, hdim, d), lambda i, be: (be[i], 0, 0))],
        out_specs=pl.BlockSpec((MOE_BLOCK, d), lambda i, be: (i, 0)),
        scratch_shapes=[pltpu.VMEM((d, hdim), BF16), pltpu.VMEM((d, hdim), BF16), pltpu.VMEM((hdim, d), BF16)],
    )
    return pl.pallas_call(
        _expert_kernel,
        grid_spec=grid_spec,
        out_shape=jax.ShapeDtypeStruct((m, d), F32),
        compiler_params=_cparams(("arbitrary",)),
        name="moe_experts",
    )(blk_exp, xb, w_gate, w_up, w_down)


def _rope_tables(n_tok):
    rows = n_tok // GRID_W
    row = jnp.repeat(jnp.arange(rows, dtype=F32), GRID_W)
    col = jnp.broadcast_to(jnp.arange(GRID_W, dtype=F32), (rows, GRID_W)).reshape(-1)
    n_freq = DIFF_DIM // 4
    inv = ROPE_BASE ** (-jnp.arange(n_freq, dtype=F32) / n_freq)
    ar = row[:, None] * inv
    ac = col[:, None] * inv
    ang = jnp.concatenate([ar, ar, ac, ac], axis=-1)
    cos, sin = jnp.cos(ang), jnp.sin(ang)
    first = (jnp.arange(DIFF_DIM) % 32) < 16
    sa = jnp.where(first, -sin, 0.0)
    sb = jnp.where(first, 0.0, sin)
    return tuple(jnp.tile(a, (1, 2)) for a in (cos, sa, sb))


DFT_N2 = 256
DFT_J = 8
DFT_P = 4


def _hy_prep_kernel(*refs):
    ins, outs = refs[:9], refs[11:]
    w_ref, b_ref = refs[9], refs[10]
    j = pl.program_id(1)
    nblk = pl.num_programs(1)
    for n in range(HY_ORDER + 1):
        x_ref, p_ref, n_ref = ins[3 * n:3 * n + 3]
        cols = slice(n * HY_CH, (n + 1) * HY_CH)
        prev, nxt = _halo_rows(p_ref, n_ref, j, nblk)
        outs[n][0] = _conv3(x_ref[0].astype(F32), prev, nxt, w_ref[:, cols], b_ref[:, cols])


def hy_prep(p, conv_w, conv_b):
    b, t, _ = p.shape
    tt = _tile(t, 512)
    in_specs, args = [], []
    for n in range(HY_ORDER + 1):
        cb = C_HY // HY_CH + n
        prev, nxt = _halo_specs(tt, HY_CH, cb, t)
        in_specs += [pl.BlockSpec((1, tt, HY_CH), lambda i, j, cb=cb: (i, j, cb)), prev, nxt]
        args += [p, p, p]
    nch = (HY_ORDER + 1) * HY_CH
    in_specs += [pl.BlockSpec((3, nch), lambda i, j: (0, 0)), pl.BlockSpec((1, nch), lambda i, j: (0, 0))]
    osp = pl.BlockSpec((1, tt, HY_CH), lambda i, j: (i, j, 0))
    return pl.pallas_call(
        _hy_prep_kernel,
        grid=(b, t // tt),
        in_specs=in_specs,
        out_specs=[osp] * (HY_ORDER + 1),
        out_shape=[jax.ShapeDtypeStruct((b, t, HY_CH), F32)] * (HY_ORDER + 1),
        compiler_params=_cparams(("parallel", "parallel")),
        name="hy_prep",
    )(*args, conv_w, conv_b.reshape(1, nch))


def _hy_filter_kernel(emb_ref, w1_ref, b1_ref, w2_ref, b2_ref, w3_ref, fr_ref, al_ref, f_o, ss_o, *, length):
    j = pl.program_id(0)
    tt = emb_ref.shape[0]
    a = jnp.dot(emb_ref[...].astype(BF16), w1_ref[...].astype(BF16), preferred_element_type=F32) + b1_ref[...]
    a = jnp.sin(fr_ref[0:1, :] * a)
    a = jnp.dot(a.astype(BF16), w2_ref[...].astype(BF16), preferred_element_type=F32) + b2_ref[...]
    a = jnp.sin(fr_ref[1:2, :] * a)
    filt = jnp.dot(a.astype(BF16), w3_ref[...].astype(BF16), preferred_element_type=F32)
    row = lax.broadcasted_iota(jnp.int32, (tt, HY_CH), 0) + j * tt
    window = jnp.exp(-(row.astype(F32) / length) * al_ref[...]) + FILTER_SHIFT

    @pl.when(j == 0)
    def _():
        ss_o[...] = jnp.zeros(ss_o.shape, F32)

    for o in range(HY_ORDER):
        for d in range(2):
            idx = 2 * o + d
            f = filt[:, idx * HY_CH:(idx + 1) * HY_CH] * window
            if d == 1:
                f = jnp.where(row == 0, 0.0, f)
            f_o[idx] = f
            ss_o[o:o + 1, :] += jnp.sum(f * f, axis=0, keepdims=True)


def hy_filters(length, w1, b1, w2, b2, w3, freq):
    t = jnp.arange(length, dtype=F32) / length
    bands = jnp.arange(1, FILTER_BANDS + 1, dtype=F32)
    ang = 2.0 * math.pi * t[:, None] * bands
    emb = jnp.concatenate([t[:, None], jnp.cos(ang), jnp.sin(ang)], axis=-1)
    pad = LANES - emb.shape[1]
    emb = jnp.pad(emb, ((0, 0), (0, pad)))
    w1 = jnp.pad(w1, ((0, pad), (0, 0)))
    ne, nh = emb.shape[1], w1.shape[1]
    alpha = jnp.linspace(abs(math.log(DECAY_TARGET)) / SLOW_DECAY_PCT,
                         abs(math.log(DECAY_TARGET)) / FAST_DECAY_PCT, HY_CH).reshape(1, HY_CH)
    tt = _tile(length, 512)
    full = lambda shape: pl.BlockSpec(shape, lambda j: (0,) * len(shape))
    f, ss = pl.pallas_call(
        functools.partial(_hy_filter_kernel, length=length),
        grid=(length // tt,),
        in_specs=[pl.BlockSpec((tt, ne), lambda j: (j, 0)), full((ne, nh)), full((1, nh)), full((nh, nh)),
                  full((1, nh)), full((nh, 2 * HY_ORDER * HY_CH)), full((2, nh)), full((1, HY_CH))],
        out_specs=[pl.BlockSpec((2 * HY_ORDER, tt, HY_CH), lambda j: (0, j, 0)), full((HY_ORDER, HY_CH))],
        out_shape=[jax.ShapeDtypeStruct((2 * HY_ORDER, length, HY_CH), F32),
                   jax.ShapeDtypeStruct((HY_ORDER, HY_CH), F32)],
        compiler_params=_cparams(("arbitrary",)),
        name="hy_filter",
    )(emb, w1, b1.reshape(1, nh), w2, b2.reshape(1, nh), w3, freq, alpha)
    return f, lax.rsqrt(ss + EPS)


def _dft_tables(length):
    n = 2 * length
    n1 = n // DFT_N2
    half = n1 // 2
    n1h = -(-(half + 1) // 16) * 16
    kv = jnp.arange(n1h, dtype=jnp.int32)
    valid = (kv <= half).astype(F32)[None, :, None]
    pair = jnp.where((kv == 0) | (kv == half), 1.0, 2.0)[None, :, None] * valid
    k1 = kv[None, :, None]
    tn = (DFT_N2 * jnp.arange(half, dtype=jnp.int32)[None, None, :]
          + jnp.arange(DFT_N2, dtype=jnp.int32)[:, None, None])
    th = (2.0 * math.pi / n) * ((k1 * tn) % n).astype(F32)
    ga = jnp.concatenate([jnp.cos(th) * valid, -jnp.sin(th) * valid], axis=1).astype(BF16)
    gi = jnp.swapaxes(jnp.concatenate([jnp.cos(th) * pair, -jnp.sin(th) * pair], axis=1), 1, 2).astype(BF16)
    kk = jnp.arange(DFT_N2, dtype=jnp.int32)
    t2 = (2.0 * math.pi / DFT_N2) * ((kk[:, None] * kk[None, :]) % DFT_N2).astype(F32)
    c2, s2 = jnp.cos(t2), jnp.sin(t2)
    mf = jnp.block([[c2, s2], [-s2, c2]]).astype(BF16)
    mi = jnp.block([[c2, -s2], [s2, c2]]).astype(BF16)
    return ga, gi, mf, mi


def _dft_a_kernel(z_ref, g_ref, o_ref):
    c = HY_CH
    for j in range(DFT_J):
        slab = z_ref[0, :, j * c:(j + 1) * c].astype(BF16)
        r = jnp.dot(g_ref[j], slab, preferred_element_type=F32)
        o_ref[0, :, :, j * c:(j + 1) * c] = r.reshape(2, r.shape[0] // 2, c).astype(o_ref.dtype)


def dft_a(z, ga):
    bz, half, wid = z.shape
    n1 = ga.shape[1] // 2
    jc = DFT_J * HY_CH
    return pl.pallas_call(
        _dft_a_kernel,
        grid=(bz, DFT_N2 // DFT_J),
        in_specs=[pl.BlockSpec((1, half, jc), lambda i, j: (i, 0, j)),
                  pl.BlockSpec((DFT_J, 2 * n1, half), lambda i, j: (j, 0, 0))],
        out_specs=pl.BlockSpec((1, 2, n1, jc), lambda i, j: (i, 0, 0, j)),
        out_shape=jax.ShapeDtypeStruct((bz, 2, n1, wid), BF16),
        compiler_params=_cparams(("parallel", "parallel")),
        name="dft_a",
    )(z, ga)


def _stack_ri(ref, b, k):
    return jnp.concatenate([ref[b, 0, k], ref[b, 1, k]], axis=0)


def _spec_filter_kernel(f_ref, mf_ref, sc_ref, h_o):
    sc = sc_ref[0]
    for k in range(DFT_P):
        xf = jnp.dot(mf_ref[...], _stack_ri(f_ref, 0, k), preferred_element_type=F32)
        xb = jnp.dot(mf_ref[...], _stack_ri(f_ref, 1, k), preferred_element_type=F32)
        h_o[0, k, 0] = ((xf[:DFT_N2] + xb[:DFT_N2]) * sc).astype(h_o.dtype)
        h_o[0, k, 1] = ((xf[DFT_N2:] - xb[DFT_N2:]) * sc).astype(h_o.dtype)


def spec_filter(fa, mf, scale):
    nb, _, n1, _, c = fa.shape
    order = nb // 2
    return pl.pallas_call(
        _spec_filter_kernel,
        grid=(order, n1 // DFT_P),
        in_specs=[pl.BlockSpec((2, 2, DFT_P, DFT_N2, c), lambda o, k: (o, 0, k, 0, 0)),
                  pl.BlockSpec((2 * DFT_N2, 2 * DFT_N2), lambda o, k: (0, 0)),
                  pl.BlockSpec((1, 1, c), lambda o, k: (o, 0, 0))],
        out_specs=pl.BlockSpec((1, DFT_P, 2, DFT_N2, c), lambda o, k: (o, k, 0, 0, 0)),
        out_shape=jax.ShapeDtypeStruct((order, n1, 2, DFT_N2, c), BF16),
        compiler_params=_cparams(("parallel", "parallel")),
        name="spec_filter",
    )(fa, mf, scale.reshape(order, 1, c))


def _spec_conv_kernel(a_ref, h_ref, mf_ref, mi_ref, o_ref):
    for k in range(DFT_P):
        x = jnp.dot(mf_ref[...], _stack_ri(a_ref, 0, k), preferred_element_type=F32)
        xr, xi = x[:DFT_N2], x[DFT_N2:]
        hr, hi = h_ref[0, k, 0].astype(F32), h_ref[0, k, 1].astype(F32)
        y = jnp.concatenate([xr * hr - xi * hi, xr * hi + xi * hr], axis=0).astype(BF16)
        z = jnp.dot(mi_ref[...], y, preferred_element_type=F32)
        o_ref[0, 0, k] = z[:DFT_N2].astype(o_ref.dtype)
        o_ref[0, 1, k] = z[DFT_N2:].astype(o_ref.dtype)


def spec_conv(a, h, order, mf, mi):
    b, _, n1, _, c = a.shape
    blk = pl.BlockSpec((1, 2, DFT_P, DFT_N2, c), lambda i, k: (i, 0, k, 0, 0))
    mat = pl.BlockSpec((2 * DFT_N2, 2 * DFT_N2), lambda i, k: (0, 0))
    return pl.pallas_call(
        _spec_conv_kernel,
        grid=(b, n1 // DFT_P),
        in_specs=[blk, pl.BlockSpec((1, DFT_P, 2, DFT_N2, c), lambda i, k: (order, k, 0, 0, 0)), mat, mat],
        out_specs=blk,
        out_shape=jax.ShapeDtypeStruct(a.shape, BF16),
        compiler_params=_cparams(("parallel", "parallel")),
        name="spec_conv",
    )(a, h, mf, mi)


def _dft_ainv_kernel(z_ref, g_ref, xg_ref, zin_ref, bias_ref, o_ref):
    c = HY_CH
    n1 = z_ref.shape[2]
    for j in range(DFT_J):
        cols = slice(j * c, (j + 1) * c)
        zz = z_ref[0, :, :, cols].reshape(2 * n1, c)
        y = jnp.dot(g_ref[j], zz, preferred_element_type=F32)
        o_ref[0, :, cols] = xg_ref[0, :, cols] * (y + bias_ref[...] * zin_ref[0, :, cols])


def dft_ainv(z, gi, xg, zin, bias):
    b, _, n1, wid = z.shape
    half = gi.shape[1]
    jc = DFT_J * HY_CH
    tok = pl.BlockSpec((1, half, jc), lambda i, j: (i, 0, j))
    return pl.pallas_call(
        _dft_ainv_kernel,
        grid=(b, DFT_N2 // DFT_J),
        in_specs=[pl.BlockSpec((1, 2, n1, jc), lambda i, j: (i, 0, 0, j)),
                  pl.BlockSpec((DFT_J, half, 2 * n1), lambda i, j: (j, 0, 0)),
                  tok, tok, pl.BlockSpec((1, HY_CH), lambda i, j: (0, 0))],
        out_specs=tok,
        out_shape=jax.ShapeDtypeStruct((b, half, wid), F32),
        compiler_params=_cparams(("parallel", "parallel")),
        name="dft_ainv",
    )(z, gi, xg, zin, bias.reshape(1, HY_CH))


def _ctx_conv_kernel(z_ref, xg_ref, f_ref, mf_ref, mi_ref, sc_ref, bias_ref, o_ref):
    nf = mf_ref.shape[0] // 2
    mf = mf_ref[...]
    xf = jnp.dot(mf, f_ref[0].astype(BF16), preferred_element_type=F32)
    xb = jnp.dot(mf, f_ref[1].astype(BF16), preferred_element_type=F32)
    sc = sc_ref[0]
    hr = (xf[:nf] + xb[:nf]) * sc
    hi = (xf[nf:] - xb[nf:]) * sc
    z = z_ref[0]
    x = jnp.dot(mf, z.astype(BF16), preferred_element_type=F32)
    xr, xi = x[:nf], x[nf:]
    y = jnp.concatenate([xr * hr - xi * hi, xr * hi + xi * hr], axis=0).astype(BF16)
    o_ref[0] = xg_ref[0] * (jnp.dot(mi_ref[...], y, preferred_element_type=F32) + bias_ref[...] * z)


def ctx_conv(z, xg, f, order, scale, bias):
    b, length, c = z.shape
    n = 2 * length
    kk = jnp.arange(n, dtype=jnp.int32)[:, None]
    tn = jnp.arange(length, dtype=jnp.int32)[None, :]
    th = (2.0 * math.pi / n) * ((kk * tn) % n).astype(F32)
    mf = jnp.concatenate([jnp.cos(th), -jnp.sin(th)], axis=0).astype(BF16)
    mi = jnp.concatenate([jnp.cos(th.T), -jnp.sin(th.T)], axis=1).astype(BF16)
    tok = pl.BlockSpec((1, length, c), lambda i: (i, 0, 0))
    return pl.pallas_call(
        _ctx_conv_kernel,
        grid=(b,),
        in_specs=[tok, tok, pl.BlockSpec((2, length, c), lambda i: (order, 0, 0)),
                  pl.BlockSpec((2 * n, length), lambda i: (0, 0)), pl.BlockSpec((length, 2 * n), lambda i: (0, 0)),
                  pl.BlockSpec((1, 1, c), lambda i: (order, 0, 0)), pl.BlockSpec((1, c), lambda i: (0, 0))],
        out_specs=tok,
        out_shape=jax.ShapeDtypeStruct((b, length, c), F32),
        compiler_params=_cparams(("parallel",)),
        name="ctx_conv",
    )(z, xg, f, mf, mi, scale.reshape(-1, 1, c), bias.reshape(1, c))


def hyena(p, conv_w, conv_b, w1, b1, w2, b2, w3, freq, bias, tables):
    b, length, _ = p.shape
    parts = hy_prep(p, conv_w, conv_b)
    f, rnorm = hy_filters(length, w1, b1, w2, b2, w3, freq)
    scale = rnorm / (2 * length)
    z = parts[0]
    if tables is None:
        for o in range(HY_ORDER):
            z = ctx_conv(z, parts[o + 1], f, o, scale, bias[o])
        return z
    ga, gi, mf, mi = tables
    half = ga.shape[2]
    wid = DFT_N2 * HY_CH
    fa = dft_a(f.reshape(2 * HY_ORDER, half, wid), ga)
    n1 = fa.shape[2]
    h = spec_filter(fa.reshape(2 * HY_ORDER, 2, n1, DFT_N2, HY_CH), mf, scale)
    for o in range(HY_ORDER):
        a = dft_a(z.reshape(b, half, wid), ga).reshape(b, 2, n1, DFT_N2, HY_CH)
        zc = spec_conv(a, h, o, mf, mi).reshape(b, 2, n1, wid)
        z = dft_ainv(zc, gi, parts[o + 1].reshape(b, half, wid), z.reshape(b, half, wid), bias[o])
        z = z.reshape(b, length, HY_CH)
    return z


def _router_kernel(x_ref, g_ref, sh_ref, sc_ref, w_ref, b_ref, h_o, id_o, gate_o):
    x = x_ref[0]
    y = x * lax.rsqrt(jnp.mean(x * x, axis=-1, keepdims=True) + EPS)
    y = ((y * g_ref[...]) * (1.0 + sc_ref[0]) + sh_ref[0]).astype(BF16)
    h_o[0] = y
    logits = jnp.dot(y, w_ref[...], preferred_element_type=F32) + b_ref[...]
    lane = lax.broadcasted_iota(jnp.int32, logits.shape, 1)
    lane_f = lane.astype(F32)
    none = float(LANES)

    def top(vals):
        v = jnp.max(vals, axis=1, keepdims=True)
        return v, jnp.min(jnp.where(vals == v, lane_f, none), axis=1, keepdims=True)

    is_grp = lane < MOE_GROUPS
    mg, grp = top(jnp.where(is_grp, logits, -jnp.inf))
    p_grp = 1.0 / jnp.sum(jnp.where(is_grp, jnp.exp(logits - mg), 0.0), axis=1, keepdims=True)
    lo = MOE_GROUPS + MOE_EPG * grp
    el = jnp.where((lane_f >= lo) & (lane_f < lo + MOE_EPG), logits, -jnp.inf)
    v1, i1 = top(el)
    v2, i2 = top(jnp.where(lane_f == i1, -jnp.inf, el))
    t = jnp.exp(v2 - v1)
    g1 = p_grp / (1.0 + t)
    id_o[0] = jnp.where(lane == 0, i1 - MOE_GROUPS, jnp.where(lane == 1, i2 - MOE_GROUPS, 0.0)).astype(jnp.int32)
    gate_o[0] = jnp.where(lane == 0, g1, jnp.where(lane == 1, g1 * t, 0.0))


def moe_router(x, g, shift, scale, w_group, b_group, w_router, b_router):
    b, t, d = x.shape
    tt = _tile(t, 512)
    npad = LANES - MOE_GROUPS - MOE_EXPERTS
    w = jnp.concatenate([w_group, w_router, jnp.zeros((d, npad), F32)], axis=1).astype(BF16)
    bias = jnp.concatenate([b_group, b_router, jnp.zeros((npad,), F32)]).reshape(1, LANES)
    tok = lambda width: pl.BlockSpec((1, tt, width), lambda i, j: (i, j, 0))
    mod = pl.BlockSpec((1, 1, d), lambda i, j: (i, 0, 0))
    return pl.pallas_call(
        _router_kernel,
        grid=(b, t // tt),
        in_specs=[tok(d), pl.BlockSpec((1, d), lambda i, j: (0, 0)), mod, mod,
                  pl.BlockSpec((d, LANES), lambda i, j: (0, 0)), pl.BlockSpec((1, LANES), lambda i, j: (0, 0))],
        out_specs=[tok(d), tok(LANES), tok(LANES)],
        out_shape=[jax.ShapeDtypeStruct((b, t, d), BF16), jax.ShapeDtypeStruct((b, t, LANES), jnp.int32),
                   jax.ShapeDtypeStruct((b, t, LANES), F32)],
        compiler_params=_cparams(("parallel", "parallel")),
        name="moe_router",
    )(x, g.reshape(1, d), shift.reshape(b, 1, d), scale.reshape(b, 1, d), w, bias)


def _hier_moe(h, ids, gates, w_gate, w_up, w_down):
    n_tok, d = h.shape
    e_flat = ids.reshape(-1)
    m_slots = n_tok * MOE_TOP_K
    onehot = (e_flat[:, None] == jnp.arange(MOE_EXPERTS, dtype=jnp.int32)[None, :]).astype(jnp.int32)
    csum = jnp.cumsum(onehot, axis=0)
    rank = jnp.sum(onehot * csum, axis=1) - 1
    counts = csum[-1]
    padded = (counts + MOE_BLOCK - 1) // MOE_BLOCK * MOE_BLOCK
    p_end = jnp.cumsum(padded)
    dest = (p_end - padded)[e_flat] + rank
    n_blocks = -(-(m_slots + MOE_EXPERTS * (MOE_BLOCK - 1)) // MOE_BLOCK)
    slot_tok = jnp.arange(m_slots, dtype=jnp.int32) // MOE_TOP_K
    buf_tok = jnp.zeros((n_blocks * MOE_BLOCK,), jnp.int32).at[dest].set(slot_tok)
    blk_exp = jnp.minimum(jnp.searchsorted(p_end, jnp.arange(n_blocks) * MOE_BLOCK, side='right'),
                          MOE_EXPERTS - 1).astype(jnp.int32)
    yb = expert_blocks(h[buf_tok], blk_exp, w_gate, w_up, w_down)
    return yb, dest.reshape(n_tok, MOE_TOP_K), gates


def _moe_combine_kernel(x_ref, y0_ref, y1_ref, g_ref, m_ref, o_ref):
    g = g_ref[...]
    f = g[:, 0:1] * y0_ref[...] + g[:, 1:2] * y1_ref[...]
    o_ref[0] = x_ref[0] + m_ref[0] * f


def moe_combine(x, yb, dest, gates, mod):
    b, t, d = x.shape
    y0 = yb[dest[:, 0]]
    y1 = yb[dest[:, 1]]
    tt = _tile(t, 512)
    nt = t // tt
    row = pl.BlockSpec((tt, d), lambda i, j: (i * nt + j, 0))
    return pl.pallas_call(
        _moe_combine_kernel,
        grid=(b, nt),
        in_specs=[pl.BlockSpec((1, tt, d), lambda i, j: (i, j, 0)), row, row,
                  pl.BlockSpec((tt, MOE_TOP_K), lambda i, j: (i * nt + j, 0)),
                  pl.BlockSpec((1, 1, d), lambda i, j: (i, 0, 0))],
        out_specs=pl.BlockSpec((1, tt, d), lambda i, j: (i, j, 0)),
        out_shape=jax.ShapeDtypeStruct((b, t, d), F32),
        compiler_params=_cparams(("parallel", "parallel")),
        name="moe_combine",
    )(x, y0, y1, gates, mod.reshape(b, 1, d))


def _permute_w_in(w):
    d = w.shape[0]
    sizes = (512, 512, 512, 1536, 512, 16, 512, 256, 1536, 4096)
    offs = np.cumsum((0,) + sizes)
    dq, dk, dv, mlqkv, mlo, mlg, gq, gkv, hy, gate = [w[:, offs[i]:offs[i + 1]] for i in range(10)]
    pad = jnp.zeros((d, N_P - C_MLG - 16), w.dtype)
    return jnp.concatenate([gate, dq, dk, dv, mlqkv, mlo, gq, hy, gkv, mlg, pad], axis=1)


def kernel(x, c, ctx, c_ctx, w_ada, b_ada, norm1_g, norm2_g, w_in, diff_lam, diff_norm_g, ml_conv_w, ml_conv_b, ml_gate_b, ml_norm_g, gqa_qnorm_g, gqa_knorm_g, hy_conv_w, hy_conv_b, hy_f_w1, hy_f_b1, hy_f_w2, hy_f_b2, hy_f_w3, hy_f_freq, hy_bias, w_branch, w_out, moe_w_group, moe_b_group, moe_w_router, moe_b_router, moe_w_gate, moe_w_up, moe_w_down, final_norm_g):
    b, n, d = x.shape
    n_ctx = ctx.shape[1]
    depth = w_in.shape[0]
    tk = _tile(n, 2048)
    tables = _rope_tables(n)
    dft_tables = _dft_tables(n)
    sc = jax.nn.silu(c)
    scx = jax.nn.silu(c_ctx)
    xs, cs = x, ctx
    for l in range(depth):
        need_ctx = l < depth - 1
        mod_l = jnp.split(sc @ w_ada[l] + b_ada[l], 6, axis=-1)
        mod_c = [jnp.broadcast_to(m, (b, d)) for m in jnp.split(scx @ w_ada[l] + b_ada[l], 6, axis=-1)]
        w_p = _permute_w_in(w_in[l]).astype(BF16)
        hl = norm_mod(xs, norm1_g[l], mod_l[0], mod_l[1], BF16)
        hc = norm_mod(cs, norm1_g[l], mod_c[0], mod_c[1], BF16)
        p_l = matmul(hl.reshape(b * n, d), w_p, BF16, tm=1024).reshape(b, n, N_P)
        p_c = matmul(hc.reshape(b * n_ctx, d), w_p, BF16, tm=1024).reshape(b, n_ctx, N_P)

        dq_l, dkt_l, dv_l, gq_l, gkt_l, gv_l = attn_prep(p_l, tables, gqa_qnorm_g[l], gqa_knorm_g[l], tk)
        dq_c, dkt_c, dv_c, gq_c, gkt_c, gv_c = attn_prep(p_c, None, gqa_qnorm_g[l], gqa_knorm_g[l], n_ctx)
        lam_init = 0.8 - 0.6 * math.exp(-0.3 * l)
        lp = diff_lam[l].astype(F32)
        lam = jnp.exp(jnp.sum(lp[0] * lp[1])) - jnp.exp(jnp.sum(lp[2] * lp[3])) + lam_init
        yl_a = diff_attention(dq_l, dkt_c, dv_c, dkt_l, dv_l, lam, diff_norm_g[l], 1.0 - lam_init, 1024)
        yl_c = gqa_attention(gq_l, gkt_c, gv_c, gkt_l, gv_l, 1024)
        if need_ctx:
            yc_a = diff_attention(dq_c, dkt_c, dv_c, None, None, lam, diff_norm_g[l], 1.0 - lam_init, 256)
            yc_c = gqa_attention(gq_c, gkt_c, gv_c, None, None, 256)

        hs_c, hs_l = mlstm_branch(p_c, p_l, ml_conv_w[l], ml_conv_b[l], ml_gate_b[l], need_ctx)
        hy_args = (hy_conv_w[l], hy_conv_b[l], hy_f_w1[l], hy_f_b1[l], hy_f_w2[l], hy_f_b2[l],
                   hy_f_w3[l], hy_f_freq[l], hy_bias[l])
        yl_d = hyena(p_l, *hy_args, dft_tables)

        wb = w_branch[l].astype(BF16)
        wo = w_out[l].astype(BF16)
        xs = merge(yl_a, hs_l, yl_c, yl_d, p_l, ml_norm_g[l], wb, wo, xs, mod_l[2])
        route = (moe_w_group[l], moe_b_group[l], moe_w_router[l], moe_b_router[l])
        experts = (moe_w_gate[l], moe_w_up[l], moe_w_down[l])
        flat = lambda a: a.reshape(-1, a.shape[-1])
        routed = [moe_router(xs, norm2_g[l], mod_l[3], mod_l[4], *route)]
        if need_ctx:
            yc_d = hyena(p_c, *hy_args, None)
            cs = merge(yc_a, hs_c, yc_c, yc_d, p_c, ml_norm_g[l], wb, wo, cs, mod_c[2])
            routed.insert(0, moe_router(cs, norm2_g[l], mod_c[3], mod_c[4], *route))
        h2, ids, gates = [jnp.concatenate([flat(r[i]) for r in routed], axis=0) for i in range(3)]
        yb, dest, gates = _hier_moe(h2, ids[:, :MOE_TOP_K], gates[:, :MOE_TOP_K], *experts)
        if need_ctx:
            nc = b * n_ctx
            cs = moe_combine(cs, yb, dest[:nc], gates[:nc], mod_c[5])
            dest, gates = dest[nc:], gates[nc:]
        xs = moe_combine(xs, yb, dest, gates, mod_l[5])
    zero = jnp.zeros((b, d), F32)
    return norm_mod(xs, final_norm_g, zero, zero, F32)
```

```python
import functools
import math

import jax
import jax.numpy as jnp
import numpy as np
from jax import lax
from jax.experimental import pallas as pl
from jax.experimental.pallas import tpu as pltpu

F32 = jnp.float32
BF16 = jnp.bfloat16

EPS = 1e-6
ROPE_BASE = 10000.0
GRID_W = 64

DIFF_HEADS = 4
DIFF_DIM = 64
DIFF_VDIM = 128
ML_HEADS = 4
ML_DIM = 128
ML_CHUNK = 64
GQA_HEADS = 8
GQA_KV = 2
GQA_DIM = 64
HY_CH = 512
HY_ORDER = 2
FILTER_BANDS = 16
FILTER_SHIFT = 0.05
DECAY_TARGET = 1e-2
FAST_DECAY_PCT = 0.3
SLOW_DECAY_PCT = 1.5
N_BRANCH = 4
MOE_GROUPS = 4
MOE_EPG = 8
MOE_EXPERTS = MOE_GROUPS * MOE_EPG
MOE_TOP_K = 2
MOE_BLOCK = 256

LANES = 128
VMEM_LIMIT = 48 * 1024 * 1024

C_GATE = 0
C_DQ = 4096
C_DK = 4608
C_DV = 5120
C_MLQ = 5632
C_MLK = 6144
C_MLV = 6656
C_MLO = 7168
C_GQ = 7680
C_HY = 8192
C_GK = 9728
C_GV = 9856
C_MLG = 9984
N_P = 10240

QSCALE = (DIFF_DIM ** -0.5) * math.log2(math.e)


def _cparams(sem):
    return pltpu.CompilerParams(dimension_semantics=sem, vmem_limit_bytes=VMEM_LIMIT)


def _tile(n, target):
    if n <= target:
        return n
    for t in range(target, 7, -1):
        if n % t == 0 and t % 8 == 0:
            return t
    return n


def _norm_mod_kernel(x_ref, g_ref, sh_ref, sc_ref, o_ref):
    x = x_ref[0]
    y = x * lax.rsqrt(jnp.mean(x * x, axis=-1, keepdims=True) + EPS)
    y = y * g_ref[...]
    o_ref[0] = (y * (1.0 + sc_ref[0]) + sh_ref[0]).astype(o_ref.dtype)


def norm_mod(x, g, shift, scale, out_dtype):
    b, t, d = x.shape
    tt = _tile(t, 512)
    return pl.pallas_call(
        _norm_mod_kernel,
        grid=(b, t // tt),
        in_specs=[pl.BlockSpec((1, tt, d), lambda i, j: (i, j, 0)),
                  pl.BlockSpec((1, d), lambda i, j: (0, 0)),
                  pl.BlockSpec((1, 1, d), lambda i, j: (i, 0, 0)),
                  pl.BlockSpec((1, 1, d), lambda i, j: (i, 0, 0))],
        out_specs=pl.BlockSpec((1, tt, d), lambda i, j: (i, j, 0)),
        out_shape=jax.ShapeDtypeStruct((b, t, d), out_dtype),
        compiler_params=_cparams(("parallel", "parallel")),
        name="norm_mod",
    )(x, g.reshape(1, d), shift.reshape(b, 1, d), scale.reshape(b, 1, d))


def _mm_kernel(a_ref, w_ref, o_ref):
    o_ref[...] = jnp.dot(a_ref[...], w_ref[...], preferred_element_type=F32).astype(o_ref.dtype)


def matmul(a, w, out_dtype, tm=512, tn=1024):
    m, k = a.shape
    n = w.shape[1]
    tm = _tile(m, tm)
    tn = _tile(n, tn)
    return pl.pallas_call(
        _mm_kernel,
        grid=(m // tm, n // tn),
        in_specs=[pl.BlockSpec((tm, k), lambda i, j: (i, 0)),
                  pl.BlockSpec((k, tn), lambda i, j: (0, j))],
        out_specs=pl.BlockSpec((tm, tn), lambda i, j: (i, j)),
        out_shape=jax.ShapeDtypeStruct((m, n), out_dtype),
        compiler_params=_cparams(("parallel", "parallel")),
        name="matmul",
    )(a, w)


def _rope(x, cos, sa, sb):
    xa = pltpu.roll(x, LANES - 16, axis=1)
    xb = pltpu.roll(x, 16, axis=1)
    return x * cos + xa * sa + xb * sb


def _seg_rmsnorm(x, g):
    lane = lax.broadcasted_iota(jnp.int32, x.shape, 1)
    lo = lane < GQA_DIM
    ss = x * x
    s_lo = jnp.sum(jnp.where(lo, ss, 0.0), axis=-1, keepdims=True)
    s_hi = jnp.sum(jnp.where(lo, 0.0, ss), axis=-1, keepdims=True)
    r = jnp.where(lo, lax.rsqrt(s_lo * (1.0 / GQA_DIM) + EPS), lax.rsqrt(s_hi * (1.0 / GQA_DIM) + EPS))
    return x * r * g


def _prep_kernel(dq_ref, dk_ref, dv_ref, gq_ref, gkv_ref, cos_ref, sa_ref, sb_ref, qg_ref, kg_ref,
                 dq_o, dkt_o, dv_o, gq_o, gkt_o, gv_o, *, rope):
    if rope:
        cos, sa, sb = cos_ref[...], sa_ref[...], sb_ref[...]
        rot = lambda x: _rope(x, cos, sa, sb)
    else:
        rot = lambda x: x
    qg = qg_ref[...]
    kg = kg_ref[...]
    for j in range(DIFF_HEADS):
        sl = slice(j * LANES, (j + 1) * LANES)
        dq_o[0, :, sl] = (rot(dq_ref[0, :, sl].astype(F32)) * QSCALE).astype(BF16)
        kt = rot(dk_ref[0, :, sl].astype(F32)).T
        dkt_o[0, j, 0, 0] = kt[:DIFF_DIM].astype(BF16)
        dkt_o[0, j, 1, 0] = kt[DIFF_DIM:].astype(BF16)
        gq_o[0, :, sl] = (rot(_seg_rmsnorm(gq_ref[0, :, sl].astype(F32), qg)) * QSCALE).astype(BF16)
    dv_o[0] = dv_ref[0].astype(BF16)
    kt = rot(_seg_rmsnorm(gkv_ref[0, :, :LANES].astype(F32), kg)).T
    gkt_o[0, 0, 0] = kt[:GQA_DIM].astype(BF16)
    gkt_o[0, 1, 0] = kt[GQA_DIM:].astype(BF16)
    v = gkv_ref[0, :, LANES:].astype(BF16)
    gv_o[0, 0] = v[:, :GQA_DIM]
    gv_o[0, 1] = v[:, GQA_DIM:]


def attn_prep(p, tables, q_g, k_g, tk):
    b, t, _ = p.shape
    rope = tables is not None
    if rope:
        cos, sa, sb = tables
    else:
        cos = sa = sb = jnp.zeros((t, LANES), F32)
    nck = t // tk
    tp = _tile(tk, 512)
    sub = tk // tp
    w512 = lambda c: pl.BlockSpec((1, tp, 512), lambda i, j, c=c: (i, j, c // 512))
    tab = pl.BlockSpec((tp, LANES), lambda i, j: (j, 0))
    vec = pl.BlockSpec((1, LANES), lambda i, j: (0, 0))
    outs = pl.pallas_call(
        functools.partial(_prep_kernel, rope=rope),
        grid=(b, t // tp),
        in_specs=[w512(C_DQ), w512(C_DK), w512(C_DV), w512(C_GQ),
                  pl.BlockSpec((1, tp, 256), lambda i, j: (i, j, C_GK // 256)),
                  tab, tab, tab, vec, vec],
        out_specs=[pl.BlockSpec((1, tp, 512), lambda i, j: (i, j, 0)),
                   pl.BlockSpec((1, DIFF_HEADS, 2, 1, DIFF_DIM, tp), lambda i, j: (i, 0, 0, j // sub, 0, j % sub)),
                   pl.BlockSpec((1, tp, 512), lambda i, j: (i, j, 0)),
                   pl.BlockSpec((1, tp, 512), lambda i, j: (i, j, 0)),
                   pl.BlockSpec((1, GQA_KV, 1, GQA_DIM, tp), lambda i, j: (i, 0, j // sub, 0, j % sub)),
                   pl.BlockSpec((1, GQA_KV, tp, GQA_DIM), lambda i, j: (i, 0, j, 0))],
        out_shape=[jax.ShapeDtypeStruct((b, t, 512), BF16),
                   jax.ShapeDtypeStruct((b, DIFF_HEADS, 2, nck, DIFF_DIM, tk), BF16),
                   jax.ShapeDtypeStruct((b, t, 512), BF16),
                   jax.ShapeDtypeStruct((b, t, 512), BF16),
                   jax.ShapeDtypeStruct((b, GQA_KV, nck, GQA_DIM, tk), BF16),
                   jax.ShapeDtypeStruct((b, GQA_KV, t, GQA_DIM), BF16)],
        compiler_params=_cparams(("parallel", "parallel")),
        name="attn_prep",
    )(p, p, p, p, p, cos, sa, sb,
      jnp.tile(q_g, 2).reshape(1, LANES), jnp.tile(k_g, 2).reshape(1, LANES))
    return outs


def _flash_step(q, kt, v, m_ref, l_ref, acc_ref):
    s = jnp.dot(q, kt, preferred_element_type=F32)
    tk = s.shape[1]
    m_prev = m_ref[...]
    m_next = jnp.maximum(m_prev, jnp.max(s, axis=1, keepdims=True))
    alpha = jnp.exp2(m_prev - m_next)
    p = jnp.exp2(s - jnp.concatenate([m_next] * (tk // LANES), axis=1))
    l_ref[...] = alpha * l_ref[...] + jnp.sum(p, axis=1, keepdims=True)
    dv = acc_ref.shape[-1]
    acc_ref[...] = acc_ref[...] * alpha[:, :dv] + jnp.dot(p.astype(BF16), v, preferred_element_type=F32)
    m_ref[...] = m_next


def _attn_body(q_ref, ktc, vc, ktl, vl, m_sc, l_sc, acc_sc, *, n_lat, tk):
    m_sc[...] = jnp.full(m_sc.shape, -jnp.inf, F32)
    l_sc[...] = jnp.zeros(l_sc.shape, F32)
    acc_sc[...] = jnp.zeros(acc_sc.shape, F32)
    q = q_ref[0]
    qs = (q[:, :DIFF_DIM], q[:, DIFF_DIM:])
    for c in range(2):
        _flash_step(qs[c], ktc(c), vc(c), m_sc.at[c], l_sc.at[c], acc_sc.at[c])
    if n_lat:
        def body(i, carry):
            for c in range(2):
                _flash_step(qs[c], ktl(c, i), vl(c, i), m_sc.at[c], l_sc.at[c], acc_sc.at[c])
            return carry
        lax.fori_loop(0, n_lat, body, 0)


def _diff_attn_kernel(*refs, n_lat, tk, out_scale):
    if n_lat:
        q_ref, ktc_ref, vc_ref, ktl_ref, vl_ref, lam_ref, g_ref, o_ref, m_sc, l_sc, acc_sc = refs
        ktl = lambda c, i: ktl_ref[0, 0, c, i]
        vl = lambda c, i: vl_ref[0, pl.ds(pl.multiple_of(i * tk, tk), tk), :]
    else:
        q_ref, ktc_ref, vc_ref, lam_ref, g_ref, o_ref, m_sc, l_sc, acc_sc = refs
        ktl = vl = None
    _attn_body(q_ref, lambda c: ktc_ref[0, 0, c, 0], lambda c: vc_ref[0], ktl, vl,
               m_sc, l_sc, acc_sc, n_lat=n_lat, tk=tk)
    o0 = acc_sc[0] / l_sc[0]
    o1 = acc_sc[1] / l_sc[1]
    o = o0 - lam_ref[...] * o1
    o = o * lax.rsqrt(jnp.mean(o * o, axis=-1, keepdims=True) + EPS)
    o_ref[0] = (o * g_ref[...] * out_scale).astype(o_ref.dtype)


def diff_attention(q, ktc, vc, ktl, vl, lam, norm_g, out_scale, tq):
    b, t, _ = q.shape
    sc = vc.shape[1]
    tq = _tile(t, tq)
    n_lat, tk = (ktl.shape[3], ktl.shape[5]) if ktl is not None else (0, 0)
    in_specs = [pl.BlockSpec((1, tq, LANES), lambda i, h, j: (i, j, h)),
                pl.BlockSpec((1, 1, 2, 1, DIFF_DIM, sc), lambda i, h, j: (i, h, 0, 0, 0, 0)),
                pl.BlockSpec((1, sc, LANES), lambda i, h, j: (i, 0, h))]
    args = [q, ktc, vc]
    if n_lat:
        in_specs += [pl.BlockSpec((1, 1, 2, n_lat, DIFF_DIM, tk), lambda i, h, j: (i, h, 0, 0, 0, 0)),
                     pl.BlockSpec((1, n_lat * tk, LANES), lambda i, h, j: (i, 0, h))]
        args += [ktl, vl]
    vec = pl.BlockSpec((1, LANES), lambda i, h, j: (0, 0))
    in_specs += [vec, vec]
    args += [jnp.full((1, LANES), lam, F32), norm_g.reshape(1, LANES)]
    return pl.pallas_call(
        functools.partial(_diff_attn_kernel, n_lat=n_lat, tk=tk, out_scale=out_scale),
        grid=(b, DIFF_HEADS, t // tq),
        in_specs=in_specs,
        out_specs=pl.BlockSpec((1, tq, LANES), lambda i, h, j: (i, j, h)),
        out_shape=jax.ShapeDtypeStruct((b, t, DIFF_HEADS * DIFF_VDIM), BF16),
        scratch_shapes=[pltpu.VMEM((2, tq, LANES), F32), pltpu.VMEM((2, tq, LANES), F32),
                        pltpu.VMEM((2, tq, DIFF_VDIM), F32)],
        compiler_params=_cparams(("parallel", "parallel", "parallel")),
        name="diff_attn",
    )(*args)


def _gqa_attn_kernel(*refs, n_lat, tk):
    if n_lat:
        q_ref, ktc_ref, vc_ref, ktl_ref, vl_ref, o_ref, m_sc, l_sc, acc_sc = refs
        ktl = lambda c, i: ktl_ref[0, 0, i]
        vl = lambda c, i: vl_ref[0, 0, pl.ds(pl.multiple_of(i * tk, tk), tk), :]
    else:
        q_ref, ktc_ref, vc_ref, o_ref, m_sc, l_sc, acc_sc = refs
        ktl = vl = None
    _attn_body(q_ref, lambda c: ktc_ref[0, 0, 0], lambda c: vc_ref[0, 0], ktl, vl,
               m_sc, l_sc, acc_sc, n_lat=n_lat, tk=tk)
    o0 = acc_sc[0] / l_sc[0][:, :GQA_DIM]
    o1 = acc_sc[1] / l_sc[1][:, :GQA_DIM]
    o_ref[0] = jnp.concatenate([o0, o1], axis=-1).astype(o_ref.dtype)


def gqa_attention(q, ktc, vc, ktl, vl, tq):
    b, t, _ = q.shape
    sc = vc.shape[2]
    tq = _tile(t, tq)
    n_lat, tk = (ktl.shape[2], ktl.shape[4]) if ktl is not None else (0, 0)
    pairs = GQA_HEADS // 2
    grp = lambda h: h // (pairs // GQA_KV)
    in_specs = [pl.BlockSpec((1, tq, LANES), lambda i, h, j: (i, j, h)),
                pl.BlockSpec((1, 1, 1, GQA_DIM, sc), lambda i, h, j: (i, grp(h), 0, 0, 0)),
                pl.BlockSpec((1, 1, sc, GQA_DIM), lambda i, h, j: (i, grp(h), 0, 0))]
    args = [q, ktc, vc]
    if n_lat:
        in_specs += [pl.BlockSpec((1, 1, n_lat, GQA_DIM, tk), lambda i, h, j: (i, grp(h), 0, 0, 0)),
                     pl.BlockSpec((1, 1, n_lat * tk, GQA_DIM), lambda i, h, j: (i, grp(h), 0, 0))]
        args += [ktl, vl]
    return pl.pallas_call(
        functools.partial(_gqa_attn_kernel, n_lat=n_lat, tk=tk),
        grid=(b, pairs, t // tq),
        in_specs=in_specs,
        out_specs=pl.BlockSpec((1, tq, LANES), lambda i, h, j: (i, j, h)),
        out_shape=jax.ShapeDtypeStruct((b, t, GQA_HEADS * GQA_DIM), BF16),
        scratch_shapes=[pltpu.VMEM((2, tq, LANES), F32), pltpu.VMEM((2, tq, LANES), F32),
                        pltpu.VMEM((2, tq, GQA_DIM), F32)],
        compiler_params=_cparams(("parallel", "parallel", "parallel")),
        name="gqa_attn",
    )(*args)


ML_STEP = 256
HALO = 16


def _halo_rows(prev_ref, next_ref, j, nblk):
    prev = jnp.where(j > 0, prev_ref[0, HALO - 1:HALO, :].astype(F32), 0.0)
    nxt = jnp.where(j < nblk - 1, next_ref[0, 0:1, :].astype(F32), 0.0)
    return prev, nxt


def _conv3(x, prev, nxt, w, bias):
    tt = x.shape[0]
    row = lax.broadcasted_iota(jnp.int32, x.shape, 0)
    xm = jnp.where(row == 0, prev, pltpu.roll(x, 1, axis=0))
    xp = jnp.where(row == tt - 1, nxt, pltpu.roll(x, tt - 1, axis=0))
    return xm * w[0:1] + x * w[1:2] + xp * w[2:3] + bias


def _halo_specs(tt, width, col_block, t):
    nb = t // HALO
    prev = pl.BlockSpec((1, HALO, width), lambda i, j: (i, jnp.maximum(j * (tt // HALO) - 1, 0), col_block))
    nxt = pl.BlockSpec((1, HALO, width), lambda i, j: (i, jnp.minimum((j + 1) * (tt // HALO), nb - 1), col_block))
    return prev, nxt


def _ml_prep_kernel(q_ref, qp_ref, qn_ref, k_ref, kp_ref, kn_ref, g_ref, w_ref, b_ref, gb_ref, q_o, k_o, g_o):
    j = pl.program_id(1)
    nblk = pl.num_programs(1)
    w = ML_HEADS * ML_DIM
    qp, qn = _halo_rows(qp_ref, qn_ref, j, nblk)
    kp, kn = _halo_rows(kp_ref, kn_ref, j, nblk)
    q = _conv3(q_ref[0].astype(F32), qp, qn, w_ref[:, :w], b_ref[:, :w])
    k = _conv3(k_ref[0].astype(F32), kp, kn, w_ref[:, w:], b_ref[:, w:])
    q_o[0] = q * jax.nn.sigmoid(q)
    k_o[0] = (k * jax.nn.sigmoid(k)) * ML_DIM ** -0.5
    x = g_ref[0].astype(F32) + gb_ref[...]
    lane = lax.broadcasted_iota(jnp.int32, x.shape, 1)
    log_sig = jnp.minimum(x, 0.0) - jnp.log(1.0 + jnp.exp(-jnp.abs(x)))
    g_o[0] = jnp.where((lane % 8) >= ML_HEADS, log_sig, x)


def ml_prep(p, conv_w, conv_b, gate_b):
    b, t, _ = p.shape
    tt = _tile(t, 512)
    w = ML_HEADS * ML_DIM
    blk = lambda c: pl.BlockSpec((1, tt, w), lambda i, j, c=c: (i, j, c // w))
    qp, qn = _halo_specs(tt, w, C_MLQ // w, t)
    kp, kn = _halo_specs(tt, w, C_MLK // w, t)
    gb = jnp.zeros((1, LANES), F32).at[0, :4 * ML_HEADS].set(gate_b.reshape(-1))
    return pl.pallas_call(
        _ml_prep_kernel,
        grid=(b, t // tt),
        in_specs=[blk(C_MLQ), qp, qn, blk(C_MLK), kp, kn,
                  pl.BlockSpec((1, tt, LANES), lambda i, j: (i, j, C_MLG // LANES)),
                  pl.BlockSpec((3, 2 * w), lambda i, j: (0, 0)),
                  pl.BlockSpec((1, 2 * w), lambda i, j: (0, 0)),
                  pl.BlockSpec((1, LANES), lambda i, j: (0, 0))],
        out_specs=[pl.BlockSpec((1, tt, w), lambda i, j: (i, j, 0)),
                   pl.BlockSpec((1, tt, w), lambda i, j: (i, j, 0)),
                   pl.BlockSpec((1, tt, LANES), lambda i, j: (i, j, 0))],
        out_shape=[jax.ShapeDtypeStruct((b, t, w), F32), jax.ShapeDtypeStruct((b, t, w), F32),
                   jax.ShapeDtypeStruct((b, t, LANES), F32)],
        compiler_params=_cparams(("parallel", "parallel")),
        name="ml_prep",
    )(p, p, p, p, p, p, p, conv_w, conv_b.reshape(1, 2 * w), gb)


def _ml_chunk_head(q, k, v, li_c, bc_c, li_r, bc_r, b_last, tri, c_ref, n_ref, m_ref):
    m = m_ref[:, 0:1]
    c_mat = c_ref[...]
    n_row = n_ref[...]
    dmat = jnp.where(tri, bc_c - bc_r + li_r, -jnp.inf)
    inter = bc_c + m
    m_t = jnp.maximum(inter, jnp.max(dmat, axis=1, keepdims=True))
    w_intra = jnp.exp(dmat - m_t)
    w_inter = jnp.exp(inter - m_t)
    qb = q.astype(BF16)
    s = lax.dot_general(qb, k.astype(BF16), (((1,), (1,)), ((), ())), preferred_element_type=F32) * w_intra
    num = (jnp.dot(s.astype(BF16), v.astype(BF16), preferred_element_type=F32)
           + w_inter * jnp.dot(qb, c_mat.astype(BF16), preferred_element_type=F32))
    den = jnp.sum(s, axis=1, keepdims=True) + w_inter * jnp.sum(q * n_row, axis=1, keepdims=True)
    h = num / jnp.maximum(jnp.abs(den), jnp.exp(-m_t))
    g = b_last - bc_c + li_c
    m_new = jnp.maximum(b_last + m, jnp.max(g, axis=0, keepdims=True))
    kw = k * jnp.exp(g - m_new)
    wc = jnp.exp(b_last + m - m_new)
    c_ref[...] = wc * c_mat + jnp.dot(kw.T.astype(BF16), v.astype(BF16), preferred_element_type=F32)
    n_ref[...] = wc * n_row + jnp.sum(kw, axis=0, keepdims=True)
    m_ref[...] = jnp.broadcast_to(m_new, m_ref.shape)
    return h


def _ml_scan_kernel(q_ref, k_ref, v_ref, g_ref, c0_ref, n0_ref, m0_ref, h_o, c1_o, n1_o, m1_o,
                    c_sc, n_sc, m_sc, *, direction):
    j = pl.program_id(0)

    @pl.when(j == 0)
    def _():
        c_sc[...] = c0_ref[...]
        n_sc[...] = n0_ref[...]
        m_sc[...] = m0_ref[...]

    r = lax.broadcasted_iota(jnp.int32, (ML_CHUNK, ML_CHUNK), 0)
    s = lax.broadcasted_iota(jnp.int32, (ML_CHUNK, ML_CHUNK), 1)
    tri = (s >= r) if direction else (s <= r)
    tri_f = tri.astype(F32)
    n_chunks = q_ref.shape[1] // ML_CHUNK
    order = range(n_chunks - 1, -1, -1) if direction else range(n_chunks)
    last = 0 if direction else ML_CHUNK - 1
    for c in order:
        rows = slice(c * ML_CHUNK, (c + 1) * ML_CHUNK)
        for bi in range(q_ref.shape[0]):
            gch = g_ref[bi, rows, :]
            bc = jnp.dot(tri_f, gch, preferred_element_type=F32, precision=lax.Precision.HIGHEST)
            gt = gch.T
            bct = bc.T
            for hd in range(ML_HEADS):
                ci = 2 * ML_HEADS * direction + hd
                cf = ci + ML_HEADS
                cols = slice(hd * ML_DIM, (hd + 1) * ML_DIM)
                h = _ml_chunk_head(q_ref[bi, rows, cols], k_ref[bi, rows, cols], v_ref[bi, rows, cols],
                                   gch[:, ci:ci + 1], bc[:, cf:cf + 1], gt[ci:ci + 1, :], bct[cf:cf + 1, :],
                                   bc[last:last + 1, cf:cf + 1], tri,
                                   c_sc.at[bi, hd], n_sc.at[bi, hd], m_sc.at[bi, hd])
                h_o[bi, rows, cols] = h

    @pl.when(j == pl.num_programs(0) - 1)
    def _():
        c1_o[...] = c_sc[...]
        n1_o[...] = n_sc[...]
        m1_o[...] = m_sc[...]


def ml_scan(q, k, p, g, state, direction):
    b, t, w = q.shape
    ts = _tile(t, ML_STEP)
    nst = t // ts
    tok = (lambda j: (0, nst - 1 - j, 0)) if direction else (lambda j: (0, j, 0))
    tokv = (lambda j: (0, nst - 1 - j, C_MLV // w)) if direction else (lambda j: (0, j, C_MLV // w))
    st_shapes = [(b, ML_HEADS, ML_DIM, ML_DIM), (b, ML_HEADS, 1, ML_DIM), (b, ML_HEADS, 1, ML_DIM)]
    st_specs = [pl.BlockSpec(s, lambda j: (0, 0, 0, 0)) for s in st_shapes]
    h, c1, n1, m1 = pl.pallas_call(
        functools.partial(_ml_scan_kernel, direction=direction),
        grid=(nst,),
        in_specs=[pl.BlockSpec((b, ts, w), tok), pl.BlockSpec((b, ts, w), tok), pl.BlockSpec((b, ts, w), tokv),
                  pl.BlockSpec((b, ts, LANES), tok)] + st_specs,
        out_specs=[pl.BlockSpec((b, ts, w), tok)] + st_specs,
        out_shape=[jax.ShapeDtypeStruct((b, t, w), F32)] + [jax.ShapeDtypeStruct(s, F32) for s in st_shapes],
        scratch_shapes=[pltpu.VMEM(s, F32) for s in st_shapes],
        compiler_params=_cparams(("arbitrary",)),
        name="ml_scan",
    )(q, k, p, g, *state)
    return h, (c1, n1, m1)


def mlstm_branch(p_c, p_l, conv_w, conv_b, gate_b, need_ctx):
    b = p_l.shape[0]
    qc, kc, gc = ml_prep(p_c, conv_w, conv_b, gate_b)
    ql, kl, gl = ml_prep(p_l, conv_w, conv_b, gate_b)
    zero = (jnp.zeros((b, ML_HEADS, ML_DIM, ML_DIM), F32), jnp.zeros((b, ML_HEADS, 1, ML_DIM), F32),
            jnp.zeros((b, ML_HEADS, 1, ML_DIM), F32))
    hs_c, hs_l = [], []
    for direction in (0, 1):
        hc, st = ml_scan(qc, kc, p_c, gc, zero, direction)
        hl, _ = ml_scan(ql, kl, p_l, gl, st, direction)
        hs_c.append(hc)
        hs_l.append(hl)
    return (tuple(hs_c) if need_ctx else None), tuple(hs_l)


def _merge_kernel(ya_ref, hf_ref, hb_ref, yc_ref, yd_ref, og_ref, g_ref, mg_ref, wb_ref, wo_ref, x_ref, gate_ref,
                  o_ref):
    d = x_ref.shape[-1]
    mg = mg_ref[...]
    yb = []
    for hd in range(ML_HEADS):
        cols = slice(hd * ML_DIM, (hd + 1) * ML_DIM)
        h = hf_ref[0, :, cols] + hb_ref[0, :, cols]
        h = h * lax.rsqrt(jnp.mean(h * h, axis=-1, keepdims=True) + EPS) * mg
        yb.append((h * jax.nn.sigmoid(og_ref[0, :, cols].astype(F32))).astype(BF16))
    ys = (ya_ref[0].astype(BF16), jnp.concatenate(yb, axis=-1), yc_ref[0].astype(BF16), yd_ref[0].astype(BF16))
    acc = None
    for n, y in enumerate(ys):
        t = jnp.dot(y, wb_ref[n], preferred_element_type=F32)
        t = jax.nn.sigmoid(g_ref[0, :, n * d:(n + 1) * d].astype(F32)) * t
        acc = t if acc is None else acc + t
    z = jnp.dot(acc.astype(BF16), wo_ref[...], preferred_element_type=F32)
    o_ref[0] = x_ref[0] + gate_ref[0] * z


def merge(ya, hs, yc, yd, p, ml_norm_g, w_branch, w_out, x, gate):
    b, t, d = x.shape
    tm = _tile(t, 512)
    w = ML_HEADS * ML_DIM
    ysp = pl.BlockSpec((1, tm, w), lambda i, j: (i, j, 0))
    return pl.pallas_call(
        _merge_kernel,
        grid=(b, t // tm),
        in_specs=[ysp, ysp, ysp, ysp, ysp,
                  pl.BlockSpec((1, tm, w), lambda i, j: (i, j, C_MLO // w)),
                  pl.BlockSpec((1, tm, N_BRANCH * d), lambda i, j: (i, j, C_GATE // (N_BRANCH * d))),
                  pl.BlockSpec((1, ML_DIM), lambda i, j: (0, 0)),
                  pl.BlockSpec((N_BRANCH, w, d), lambda i, j: (0, 0, 0)),
                  pl.BlockSpec((d, d), lambda i, j: (0, 0)),
                  pl.BlockSpec((1, tm, d), lambda i, j: (i, j, 0)),
                  pl.BlockSpec((1, 1, d), lambda i, j: (i, 0, 0))],
        out_specs=pl.BlockSpec((1, tm, d), lambda i, j: (i, j, 0)),
        out_shape=jax.ShapeDtypeStruct((b, t, d), F32),
        compiler_params=_cparams(("parallel", "parallel")),
        name="merge",
    )(ya, hs[0], hs[1], yc, yd, p, p, ml_norm_g.reshape(1, ML_DIM), w_branch, w_out, x, gate.reshape(b, 1, d))


def _expert_kernel(be_ref, x_ref, wg_ref, wu_ref, wd_ref, o_ref, wg_sc, wu_sc, wd_sc):
    i = pl.program_id(0)

    @pl.when(jnp.logical_or(i == 0, be_ref[i] != be_ref[jnp.maximum(i - 1, 0)]))
    def _():
        wg_sc[...] = wg_ref[0].astype(BF16)
        wu_sc[...] = wu_ref[0].astype(BF16)
        wd_sc[...] = wd_ref[0].astype(BF16)

    x = x_ref[...]
    a = jnp.dot(x, wg_sc[...], preferred_element_type=F32)
    u = jnp.dot(x, wu_sc[...], preferred_element_type=F32)
    h = (a * jax.nn.sigmoid(a)) * u
    o_ref[...] = jnp.dot(h.astype(BF16), wd_sc[...], preferred_element_type=F32).astype(o_ref.dtype)


def expert_blocks(xb, blk_exp, w_gate, w_up, w_down):
    m, d = xb.shape
    hdim = w_gate.shape[-1]
    n_blocks = m // MOE_BLOCK
    grid_spec = pltpu.PrefetchScalarGridSpec(
        num_scalar_prefetch=1,
        grid=(n_blocks,),
        in_specs=[pl.BlockSpec((MOE_BLOCK, d), lambda i, be: (i, 0)),
                  pl.BlockSpec((1, d, hdim), lambda i, be: (be[i], 0, 0)),
                  pl.BlockSpec((1, d, hdim), lambda i, be: (be[i], 0, 0)),
                  pl.BlockSpec((1, hdim, d), lambda i, be: (be[i], 0, 0))],
        out_specs=pl.BlockSpec((MOE_BLOCK, d), lambda i, be: (i, 0)),
        scratch_shapes=[pltpu.VMEM((d, hdim), BF16), pltpu.VMEM((d, hdim), BF16), pltpu.VMEM((hdim, d), BF16)],
    )
    return pl.pallas_call(
        _expert_kernel,
        grid_spec=grid_spec,
        out_shape=jax.ShapeDtypeStruct((m, d), BF16),
        compiler_params=_cparams(("arbitrary",)),
        name="moe_experts",
    )(blk_exp, xb, w_gate, w_up, w_down)


def _rope_tables(n_tok):
    rows = n_tok // GRID_W
    row = jnp.repeat(jnp.arange(rows, dtype=F32), GRID_W)
    col = jnp.broadcast_to(jnp.arange(GRID_W, dtype=F32), (rows, GRID_W)).reshape(-1)
    n_freq = DIFF_DIM // 4
    inv = ROPE_BASE ** (-jnp.arange(n_freq, dtype=F32) / n_freq)
    ar = row[:, None] * inv
    ac = col[:, None] * inv
    ang = jnp.concatenate([ar, ar, ac, ac], axis=-1)
    cos, sin = jnp.cos(ang), jnp.sin(ang)
    first = (jnp.arange(DIFF_DIM) % 32) < 16
    sa = jnp.where(first, -sin, 0.0)
    sb = jnp.where(first, 0.0, sin)
    return tuple(jnp.tile(a, (1, 2)) for a in (cos, sa, sb))


DFT_N2 = 256
DFT_J = 8
DFT_P = 4


def _hy_prep_kernel(*refs):
    ins, outs = refs[:9], refs[11:]
    w_ref, b_ref = refs[9], refs[10]
    j = pl.program_id(1)
    nblk = pl.num_programs(1)
    for n in range(HY_ORDER + 1):
        x_ref, p_ref, n_ref = ins[3 * n:3 * n + 3]
        cols = slice(n * HY_CH, (n + 1) * HY_CH)
        prev, nxt = _halo_rows(p_ref, n_ref, j, nblk)
        outs[n][0] = _conv3(x_ref[0].astype(F32), prev, nxt, w_ref[:, cols], b_ref[:, cols])


def hy_prep(p, conv_w, conv_b):
    b, t, _ = p.shape
    tt = _tile(t, 512)
    in_specs, args = [], []
    for n in range(HY_ORDER + 1):
        cb = C_HY // HY_CH + n
        prev, nxt = _halo_specs(tt, HY_CH, cb, t)
        in_specs += [pl.BlockSpec((1, tt, HY_CH), lambda i, j, cb=cb: (i, j, cb)), prev, nxt]
        args += [p, p, p]
    nch = (HY_ORDER + 1) * HY_CH
    in_specs += [pl.BlockSpec((3, nch), lambda i, j: (0, 0)), pl.BlockSpec((1, nch), lambda i, j: (0, 0))]
    osp = pl.BlockSpec((1, tt, HY_CH), lambda i, j: (i, j, 0))
    return pl.pallas_call(
        _hy_prep_kernel,
        grid=(b, t // tt),
        in_specs=in_specs,
        out_specs=[osp] * (HY_ORDER + 1),
        out_shape=[jax.ShapeDtypeStruct((b, t, HY_CH), F32)] * (HY_ORDER + 1),
        compiler_params=_cparams(("parallel", "parallel")),
        name="hy_prep",
    )(*args, conv_w, conv_b.reshape(1, nch))


def _hy_filter_kernel(emb_ref, w1_ref, b1_ref, w2_ref, b2_ref, w3_ref, fr_ref, al_ref, f_o, ss_o, *, length, half):
    j = pl.program_id(0)
    tt = emb_ref.shape[0]
    a = jnp.dot(emb_ref[...].astype(BF16), w1_ref[...].astype(BF16), preferred_element_type=F32) + b1_ref[...]
    a = jnp.sin(fr_ref[0:1, :] * a)
    a = jnp.dot(a.astype(BF16), w2_ref[...].astype(BF16), preferred_element_type=F32) + b2_ref[...]
    a = jnp.sin(fr_ref[1:2, :] * a)
    filt = jnp.dot(a.astype(BF16), w3_ref[...].astype(BF16), preferred_element_type=F32)
    r = lax.broadcasted_iota(jnp.int32, (tt, HY_CH), 0)
    if half:
        row = DFT_N2 * (r & (half - 1)) + j * (tt // half) + (r >> (half.bit_length() - 1))
    else:
        row = r + j * tt
    window = jnp.exp(-(row.astype(F32) / length) * al_ref[...]) + FILTER_SHIFT

    @pl.when(j == 0)
    def _():
        ss_o[...] = jnp.zeros(ss_o.shape, F32)

    for o in range(HY_ORDER):
        for d in range(2):
            idx = 2 * o + d
            f = filt[:, idx * HY_CH:(idx + 1) * HY_CH] * window
            if d == 1:
                f = jnp.where(row == 0, 0.0, f)
            if half:
                for jj in range(tt // half):
                    f_o[idx, :, jj * HY_CH:(jj + 1) * HY_CH] = f[jj * half:(jj + 1) * half]
            else:
                f_o[idx] = f
            ss_o[o:o + 1, :] += jnp.sum(f * f, axis=0, keepdims=True)


def hy_filters(length, w1, b1, w2, b2, w3, freq, wide):
    t = jnp.arange(length, dtype=F32) / length
    bands = jnp.arange(1, FILTER_BANDS + 1, dtype=F32)
    ang = 2.0 * math.pi * t[:, None] * bands
    emb = jnp.concatenate([t[:, None], jnp.cos(ang), jnp.sin(ang)], axis=-1)
    pad = LANES - emb.shape[1]
    emb = jnp.pad(emb, ((0, 0), (0, pad)))
    w1 = jnp.pad(w1, ((0, pad), (0, 0)))
    ne, nh = emb.shape[1], w1.shape[1]
    alpha = jnp.linspace(abs(math.log(DECAY_TARGET)) / SLOW_DECAY_PCT,
                         abs(math.log(DECAY_TARGET)) / FAST_DECAY_PCT, HY_CH).reshape(1, HY_CH)
    tt = _tile(length, 512)
    half = length // DFT_N2
    wide = wide and half > 0 and half & (half - 1) == 0 and tt % half == 0
    full = lambda shape: pl.BlockSpec(shape, lambda j: (0,) * len(shape))
    if wide:
        emb = jnp.swapaxes(emb.reshape(half, DFT_N2, ne), 0, 1).reshape(length, ne)
        f_spec = pl.BlockSpec((2 * HY_ORDER, half, (tt // half) * HY_CH), lambda j: (0, 0, j))
        f_shape = (2 * HY_ORDER, half, DFT_N2 * HY_CH)
    else:
        f_spec = pl.BlockSpec((2 * HY_ORDER, tt, HY_CH), lambda j: (0, j, 0))
        f_shape = (2 * HY_ORDER, length, HY_CH)
    f, ss = pl.pallas_call(
        functools.partial(_hy_filter_kernel, length=length, half=half if wide else 0),
        grid=(length // tt,),
        in_specs=[pl.BlockSpec((tt, ne), lambda j: (j, 0)), full((ne, nh)), full((1, nh)), full((nh, nh)),
                  full((1, nh)), full((nh, 2 * HY_ORDER * HY_CH)), full((2, nh)), full((1, HY_CH))],
        out_specs=[f_spec, full((HY_ORDER, HY_CH))],
        out_shape=[jax.ShapeDtypeStruct(f_shape, F32), jax.ShapeDtypeStruct((HY_ORDER, HY_CH), F32)],
        compiler_params=_cparams(("arbitrary",)),
        name="hy_filter",
    )(emb, w1, b1.reshape(1, nh), w2, b2.reshape(1, nh), w3, freq, alpha)
    return f, lax.rsqrt(ss + EPS)


def _dft_tables(length):
    n = 2 * length
    n1 = n // DFT_N2
    half = n1 // 2
    n1h = -(-(half + 1) // 16) * 16
    kv = jnp.arange(n1h, dtype=jnp.int32)
    valid = (kv <= half).astype(F32)[None, :, None]
    pair = jnp.where((kv == 0) | (kv == half), 1.0, 2.0)[None, :, None] * valid
    k1 = kv[None, :, None]
    tn = (DFT_N2 * jnp.arange(half, dtype=jnp.int32)[None, None, :]
          + jnp.arange(DFT_N2, dtype=jnp.int32)[:, None, None])
    th = (2.0 * math.pi / n) * ((k1 * tn) % n).astype(F32)
    ga = jnp.concatenate([jnp.cos(th) * valid, -jnp.sin(th) * valid], axis=1).astype(BF16)
    gi = jnp.swapaxes(jnp.concatenate([jnp.cos(th) * pair, -jnp.sin(th) * pair], axis=1), 1, 2).astype(BF16)
    kk = jnp.arange(DFT_N2, dtype=jnp.int32)
    t2 = (2.0 * math.pi / DFT_N2) * ((kk[:, None] * kk[None, :]) % DFT_N2).astype(F32)
    c2, s2 = jnp.cos(t2), jnp.sin(t2)
    mf = jnp.block([[c2, s2], [-s2, c2]]).astype(BF16)
    mi = jnp.block([[c2, -s2], [s2, c2]]).astype(BF16)
    return ga, gi, mf, mi


def _dft_a_kernel(z_ref, g_ref, o_ref):
    c = HY_CH
    for j in range(DFT_J):
        slab = z_ref[0, :, j * c:(j + 1) * c].astype(BF16)
        r = jnp.dot(g_ref[j], slab, preferred_element_type=F32)
        o_ref[0, :, :, j * c:(j + 1) * c] = r.reshape(2, r.shape[0] // 2, c).astype(o_ref.dtype)


def dft_a(z, ga):
    bz, half, wid = z.shape
    n1 = ga.shape[1] // 2
    jc = DFT_J * HY_CH
    return pl.pallas_call(
        _dft_a_kernel,
        grid=(bz, DFT_N2 // DFT_J),
        in_specs=[pl.BlockSpec((1, half, jc), lambda i, j: (i, 0, j)),
                  pl.BlockSpec((DFT_J, 2 * n1, half), lambda i, j: (j, 0, 0))],
        out_specs=pl.BlockSpec((1, 2, n1, jc), lambda i, j: (i, 0, 0, j)),
        out_shape=jax.ShapeDtypeStruct((bz, 2, n1, wid), BF16),
        compiler_params=_cparams(("parallel", "parallel")),
        name="dft_a",
    )(z, ga)


def _stack_ri(ref, b, k):
    return jnp.concatenate([ref[b, 0, k], ref[b, 1, k]], axis=0)


def _spec_filter_kernel(f_ref, mf_ref, sc_ref, h_o):
    sc = sc_ref[0]
    for k in range(DFT_P):
        xf = jnp.dot(mf_ref[...], _stack_ri(f_ref, 0, k), preferred_element_type=F32)
        xb = jnp.dot(mf_ref[...], _stack_ri(f_ref, 1, k), preferred_element_type=F32)
        h_o[0, k, 0] = ((xf[:DFT_N2] + xb[:DFT_N2]) * sc).astype(h_o.dtype)
        h_o[0, k, 1] = ((xf[DFT_N2:] - xb[DFT_N2:]) * sc).astype(h_o.dtype)


def spec_filter(fa, mf, scale):
    nb, _, n1, _, c = fa.shape
    order = nb // 2
    return pl.pallas_call(
        _spec_filter_kernel,
        grid=(order, n1 // DFT_P),
        in_specs=[pl.BlockSpec((2, 2, DFT_P, DFT_N2, c), lambda o, k: (o, 0, k, 0, 0)),
                  pl.BlockSpec((2 * DFT_N2, 2 * DFT_N2), lambda o, k: (0, 0)),
                  pl.BlockSpec((1, 1, c), lambda o, k: (o, 0, 0))],
        out_specs=pl.BlockSpec((1, DFT_P, 2, DFT_N2, c), lambda o, k: (o, k, 0, 0, 0)),
        out_shape=jax.ShapeDtypeStruct((order, n1, 2, DFT_N2, c), BF16),
        compiler_params=_cparams(("parallel", "parallel")),
        name="spec_filter",
    )(fa, mf, scale.reshape(order, 1, c))


def _spec_conv_kernel(a_ref, h_ref, mf_ref, mi_ref, o_ref):
    for k in range(DFT_P):
        x = jnp.dot(mf_ref[...], _stack_ri(a_ref, 0, k), preferred_element_type=F32)
        xr, xi = x[:DFT_N2], x[DFT_N2:]
        hr, hi = h_ref[0, k, 0].astype(F32), h_ref[0, k, 1].astype(F32)
        y = jnp.concatenate([xr * hr - xi * hi, xr * hi + xi * hr], axis=0).astype(BF16)
        z = jnp.dot(mi_ref[...], y, preferred_element_type=F32)
        o_ref[0, 0, k] = z[:DFT_N2].astype(o_ref.dtype)
        o_ref[0, 1, k] = z[DFT_N2:].astype(o_ref.dtype)


def spec_conv(a, h, order, mf, mi):
    b, _, n1, _, c = a.shape
    blk = pl.BlockSpec((1, 2, DFT_P, DFT_N2, c), lambda i, k: (i, 0, k, 0, 0))
    mat = pl.BlockSpec((2 * DFT_N2, 2 * DFT_N2), lambda i, k: (0, 0))
    return pl.pallas_call(
        _spec_conv_kernel,
        grid=(b, n1 // DFT_P),
        in_specs=[blk, pl.BlockSpec((1, DFT_P, 2, DFT_N2, c), lambda i, k: (order, k, 0, 0, 0)), mat, mat],
        out_specs=blk,
        out_shape=jax.ShapeDtypeStruct(a.shape, BF16),
        compiler_params=_cparams(("parallel", "parallel")),
        name="spec_conv",
    )(a, h, mf, mi)


def _dft_ainv_kernel(z_ref, g_ref, xg_ref, zin_ref, bias_ref, o_ref):
    c = HY_CH
    n1 = z_ref.shape[2]
    for j in range(DFT_J):
        cols = slice(j * c, (j + 1) * c)
        zz = z_ref[0, :, :, cols].reshape(2 * n1, c)
        y = jnp.dot(g_ref[j], zz, preferred_element_type=F32)
        o_ref[0, :, cols] = xg_ref[0, :, cols] * (y + bias_ref[...] * zin_ref[0, :, cols])


def dft_ainv(z, gi, xg, zin, bias):
    b, _, n1, wid = z.shape
    half = gi.shape[1]
    jc = DFT_J * HY_CH
    tok = pl.BlockSpec((1, half, jc), lambda i, j: (i, 0, j))
    return pl.pallas_call(
        _dft_ainv_kernel,
        grid=(b, DFT_N2 // DFT_J),
        in_specs=[pl.BlockSpec((1, 2, n1, jc), lambda i, j: (i, 0, 0, j)),
                  pl.BlockSpec((DFT_J, half, 2 * n1), lambda i, j: (j, 0, 0)),
                  tok, tok, pl.BlockSpec((1, HY_CH), lambda i, j: (0, 0))],
        out_specs=tok,
        out_shape=jax.ShapeDtypeStruct((b, half, wid), F32),
        compiler_params=_cparams(("parallel", "parallel")),
        name="dft_ainv",
    )(z, gi, xg, zin, bias.reshape(1, HY_CH))


def _ctx_conv_kernel(z_ref, xg_ref, f_ref, mf_ref, mi_ref, sc_ref, bias_ref, o_ref):
    nf = mf_ref.shape[0] // 2
    mf = mf_ref[...]
    xf = jnp.dot(mf, f_ref[0].astype(BF16), preferred_element_type=F32)
    xb = jnp.dot(mf, f_ref[1].astype(BF16), preferred_element_type=F32)
    sc = sc_ref[0]
    hr = (xf[:nf] + xb[:nf]) * sc
    hi = (xf[nf:] - xb[nf:]) * sc
    z = z_ref[0]
    x = jnp.dot(mf, z.astype(BF16), preferred_element_type=F32)
    xr, xi = x[:nf], x[nf:]
    y = jnp.concatenate([xr * hr - xi * hi, xr * hi + xi * hr], axis=0).astype(BF16)
    o_ref[0] = xg_ref[0] * (jnp.dot(mi_ref[...], y, preferred_element_type=F32) + bias_ref[...] * z)


def ctx_conv(z, xg, f, order, scale, bias):
    b, length, c = z.shape
    n = 2 * length
    kk = jnp.arange(n, dtype=jnp.int32)[:, None]
    tn = jnp.arange(length, dtype=jnp.int32)[None, :]
    th = (2.0 * math.pi / n) * ((kk * tn) % n).astype(F32)
    mf = jnp.concatenate([jnp.cos(th), -jnp.sin(th)], axis=0).astype(BF16)
    mi = jnp.concatenate([jnp.cos(th.T), -jnp.sin(th.T)], axis=1).astype(BF16)
    tok = pl.BlockSpec((1, length, c), lambda i: (i, 0, 0))
    return pl.pallas_call(
        _ctx_conv_kernel,
        grid=(b,),
        in_specs=[tok, tok, pl.BlockSpec((2, length, c), lambda i: (order, 0, 0)),
                  pl.BlockSpec((2 * n, length), lambda i: (0, 0)), pl.BlockSpec((length, 2 * n), lambda i: (0, 0)),
                  pl.BlockSpec((1, 1, c), lambda i: (order, 0, 0)), pl.BlockSpec((1, c), lambda i: (0, 0))],
        out_specs=tok,
        out_shape=jax.ShapeDtypeStruct((b, length, c), F32),
        compiler_params=_cparams(("parallel",)),
        name="ctx_conv",
    )(z, xg, f, mf, mi, scale.reshape(-1, 1, c), bias.reshape(1, c))


def hyena(p, conv_w, conv_b, w1, b1, w2, b2, w3, freq, bias, tables):
    b, length, _ = p.shape
    parts = hy_prep(p, conv_w, conv_b)
    f, rnorm = hy_filters(length, w1, b1, w2, b2, w3, freq, wide=tables is not None)
    scale = rnorm / (2 * length)
    z = parts[0]
    if tables is None:
        for o in range(HY_ORDER):
            z = ctx_conv(z, parts[o + 1], f, o, scale, bias[o])
        return z
    ga, gi, mf, mi = tables
    half = ga.shape[2]
    wid = DFT_N2 * HY_CH
    fa = dft_a(f.reshape(2 * HY_ORDER, half, wid), ga)
    n1 = fa.shape[2]
    h = spec_filter(fa.reshape(2 * HY_ORDER, 2, n1, DFT_N2, HY_CH), mf, scale)
    for o in range(HY_ORDER):
        a = dft_a(z.reshape(b, half, wid), ga).reshape(b, 2, n1, DFT_N2, HY_CH)
        zc = spec_conv(a, h, o, mf, mi).reshape(b, 2, n1, wid)
        z = dft_ainv(zc, gi, parts[o + 1].reshape(b, half, wid), z.reshape(b, half, wid), bias[o])
        z = z.reshape(b, length, HY_CH)
    return z


def _router_kernel(x_ref, g_ref, sh_ref, sc_ref, w_ref, b_ref, h_o, id_o, gate_o):
    x = x_ref[0]
    y = x * lax.rsqrt(jnp.mean(x * x, axis=-1, keepdims=True) + EPS)
    y = ((y * g_ref[...]) * (1.0 + sc_ref[0]) + sh_ref[0]).astype(BF16)
    h_o[0] = y
    logits = jnp.dot(y, w_ref[...], preferred_element_type=F32) + b_ref[...]
    lane = lax.broadcasted_iota(jnp.int32, logits.shape, 1)
    lane_f = lane.astype(F32)
    none = float(LANES)

    def top(vals):
        v = jnp.max(vals, axis=1, keepdims=True)
        return v, jnp.min(jnp.where(vals == v, lane_f, none), axis=1, keepdims=True)

    is_grp = lane < MOE_GROUPS
    mg, grp = top(jnp.where(is_grp, logits, -jnp.inf))
    p_grp = 1.0 / jnp.sum(jnp.where(is_grp, jnp.exp(logits - mg), 0.0), axis=1, keepdims=True)
    lo = MOE_GROUPS + MOE_EPG * grp
    el = jnp.where((lane_f >= lo) & (lane_f < lo + MOE_EPG), logits, -jnp.inf)
    v1, i1 = top(el)
    v2, i2 = top(jnp.where(lane_f == i1, -jnp.inf, el))
    t = jnp.exp(v2 - v1)
    g1 = p_grp / (1.0 + t)
    id_o[0] = jnp.where(lane == 0, i1 - MOE_GROUPS, jnp.where(lane == 1, i2 - MOE_GROUPS, 0.0)).astype(jnp.int32)
    gate_o[0] = jnp.where(lane == 0, g1, jnp.where(lane == 1, g1 * t, 0.0))


def moe_router(x, g, shift, scale, w_group, b_group, w_router, b_router):
    b, t, d = x.shape
    tt = _tile(t, 512)
    npad = LANES - MOE_GROUPS - MOE_EXPERTS
    w = jnp.concatenate([w_group, w_router, jnp.zeros((d, npad), F32)], axis=1).astype(BF16)
    bias = jnp.concatenate([b_group, b_router, jnp.zeros((npad,), F32)]).reshape(1, LANES)
    tok = lambda width: pl.BlockSpec((1, tt, width), lambda i, j: (i, j, 0))
    mod = pl.BlockSpec((1, 1, d), lambda i, j: (i, 0, 0))
    return pl.pallas_call(
        _router_kernel,
        grid=(b, t // tt),
        in_specs=[tok(d), pl.BlockSpec((1, d), lambda i, j: (0, 0)), mod, mod,
                  pl.BlockSpec((d, LANES), lambda i, j: (0, 0)), pl.BlockSpec((1, LANES), lambda i, j: (0, 0))],
        out_specs=[tok(d), tok(LANES), tok(LANES)],
        out_shape=[jax.ShapeDtypeStruct((b, t, d), BF16), jax.ShapeDtypeStruct((b, t, LANES), jnp.int32),
                   jax.ShapeDtypeStruct((b, t, LANES), F32)],
        compiler_params=_cparams(("parallel", "parallel")),
        name="moe_router",
    )(x, g.reshape(1, d), shift.reshape(b, 1, d), scale.reshape(b, 1, d), w, bias)


def _blocked_cumsum(onehot, blk=256):
    m, e = onehot.shape
    if m % blk:
        return jnp.cumsum(onehot, axis=0)
    oh = onehot.astype(BF16).reshape(m // blk, blk, e)
    tril = jnp.tril(jnp.ones((blk, blk), BF16))
    within = jnp.einsum('ts,bse->bte', tril, oh, preferred_element_type=F32)
    tot = within[:, -1, :]
    off = jnp.cumsum(tot, axis=0) - tot
    return (within + off[:, None, :]).reshape(m, e).astype(jnp.int32)


def _hier_moe(h, ids, gates, w_gate, w_up, w_down):
    n_tok, d = h.shape
    e_flat = ids.reshape(-1)
    m_slots = n_tok * MOE_TOP_K
    onehot = (e_flat[:, None] == jnp.arange(MOE_EXPERTS, dtype=jnp.int32)[None, :]).astype(jnp.int32)
    csum = _blocked_cumsum(onehot)
    rank = jnp.sum(onehot * csum, axis=1) - 1
    counts = csum[-1]
    padded = (counts + MOE_BLOCK - 1) // MOE_BLOCK * MOE_BLOCK
    p_end = jnp.cumsum(padded)
    dest = (p_end - padded)[e_flat] + rank
    n_blocks = -(-(m_slots + MOE_EXPERTS * (MOE_BLOCK - 1)) // MOE_BLOCK)
    slot_tok = jnp.arange(m_slots, dtype=jnp.int32) // MOE_TOP_K
    buf_tok = jnp.zeros((n_blocks * MOE_BLOCK,), jnp.int32).at[dest].set(slot_tok)
    blk_exp = jnp.minimum(jnp.searchsorted(p_end, jnp.arange(n_blocks) * MOE_BLOCK, side='right'),
                          MOE_EXPERTS - 1).astype(jnp.int32)
    yb = expert_blocks(h[buf_tok], blk_exp, w_gate, w_up, w_down)
    return yb, dest.reshape(n_tok, MOE_TOP_K), gates


def _moe_combine_kernel(x_ref, y0_ref, y1_ref, g_ref, m_ref, o_ref):
    g = g_ref[...]
    f = g[:, 0:1] * y0_ref[...].astype(F32) + g[:, 1:2] * y1_ref[...].astype(F32)
    o_ref[0] = x_ref[0] + m_ref[0] * f


def moe_combine(x, yb, dest, gates, mod):
    b, t, d = x.shape
    y0 = yb[dest[:, 0]]
    y1 = yb[dest[:, 1]]
    tt = _tile(t, 512)
    nt = t // tt
    row = pl.BlockSpec((tt, d), lambda i, j: (i * nt + j, 0))
    return pl.pallas_call(
        _moe_combine_kernel,
        grid=(b, nt),
        in_specs=[pl.BlockSpec((1, tt, d), lambda i, j: (i, j, 0)), row, row,
                  pl.BlockSpec((tt, MOE_TOP_K), lambda i, j: (i * nt + j, 0)),
                  pl.BlockSpec((1, 1, d), lambda i, j: (i, 0, 0))],
        out_specs=pl.BlockSpec((1, tt, d), lambda i, j: (i, j, 0)),
        out_shape=jax.ShapeDtypeStruct((b, t, d), F32),
        compiler_params=_cparams(("parallel", "parallel")),
        name="moe_combine",
    )(x, y0, y1, gates, mod.reshape(b, 1, d))


def _permute_w_in(w):
    d = w.shape[0]
    sizes = (512, 512, 512, 1536, 512, 16, 512, 256, 1536, 4096)
    offs = np.cumsum((0,) + sizes)
    dq, dk, dv, mlqkv, mlo, mlg, gq, gkv, hy, gate = [w[:, offs[i]:offs[i + 1]] for i in range(10)]
    pad = jnp.zeros((d, N_P - C_MLG - 16), w.dtype)
    return jnp.concatenate([gate, dq, dk, dv, mlqkv, mlo, gq, hy, gkv, mlg, pad], axis=1)


def kernel(x, c, ctx, c_ctx, w_ada, b_ada, norm1_g, norm2_g, w_in, diff_lam, diff_norm_g, ml_conv_w, ml_conv_b, ml_gate_b, ml_norm_g, gqa_qnorm_g, gqa_knorm_g, hy_conv_w, hy_conv_b, hy_f_w1, hy_f_b1, hy_f_w2, hy_f_b2, hy_f_w3, hy_f_freq, hy_bias, w_branch, w_out, moe_w_group, moe_b_group, moe_w_router, moe_b_router, moe_w_gate, moe_w_up, moe_w_down, final_norm_g):
    b, n, d = x.shape
    n_ctx = ctx.shape[1]
    depth = w_in.shape[0]
    tk = _tile(n, 2048)
    tables = _rope_tables(n)
    dft_tables = _dft_tables(n)
    sc = jax.nn.silu(c)
    scx = jax.nn.silu(c_ctx)
    xs, cs = x, ctx
    for l in range(depth):
        need_ctx = l < depth - 1
        mod_l = jnp.split(sc @ w_ada[l] + b_ada[l], 6, axis=-1)
        mod_c = [jnp.broadcast_to(m, (b, d)) for m in jnp.split(scx @ w_ada[l] + b_ada[l], 6, axis=-1)]
        w_p = _permute_w_in(w_in[l]).astype(BF16)
        hl = norm_mod(xs, norm1_g[l], mod_l[0], mod_l[1], BF16)
        hc = norm_mod(cs, norm1_g[l], mod_c[0], mod_c[1], BF16)
        p_l = matmul(hl.reshape(b * n, d), w_p, BF16, tm=1024).reshape(b, n, N_P)
        p_c = matmul(hc.reshape(b * n_ctx, d), w_p, BF16, tm=1024).reshape(b, n_ctx, N_P)

        dq_l, dkt_l, dv_l, gq_l, gkt_l, gv_l = attn_prep(p_l, tables, gqa_qnorm_g[l], gqa_knorm_g[l], tk)
        dq_c, dkt_c, dv_c, gq_c, gkt_c, gv_c = attn_prep(p_c, None, gqa_qnorm_g[l], gqa_knorm_g[l], n_ctx)
        lam_init = 0.8 - 0.6 * math.exp(-0.3 * l)
        lp = diff_lam[l].astype(F32)
        lam = jnp.exp(jnp.sum(lp[0] * lp[1])) - jnp.exp(jnp.sum(lp[2] * lp[3])) + lam_init
        yl_a = diff_attention(dq_l, dkt_c, dv_c, dkt_l, dv_l, lam, diff_norm_g[l], 1.0 - lam_init, 1024)
        yl_c = gqa_attention(gq_l, gkt_c, gv_c, gkt_l, gv_l, 1024)
        if need_ctx:
            yc_a = diff_attention(dq_c, dkt_c, dv_c, None, None, lam, diff_norm_g[l], 1.0 - lam_init, 256)
            yc_c = gqa_attention(gq_c, gkt_c, gv_c, None, None, 256)

        hs_c, hs_l = mlstm_branch(p_c, p_l, ml_conv_w[l], ml_conv_b[l], ml_gate_b[l], need_ctx)
        hy_args = (hy_conv_w[l], hy_conv_b[l], hy_f_w1[l], hy_f_b1[l], hy_f_w2[l], hy_f_b2[l],
                   hy_f_w3[l], hy_f_freq[l], hy_bias[l])
        yl_d = hyena(p_l, *hy_args, dft_tables)

        wb = w_branch[l].astype(BF16)
        wo = w_out[l].astype(BF16)
        xs = merge(yl_a, hs_l, yl_c, yl_d, p_l, ml_norm_g[l], wb, wo, xs, mod_l[2])
        route = (moe_w_group[l], moe_b_group[l], moe_w_router[l], moe_b_router[l])
        experts = (moe_w_gate[l], moe_w_up[l], moe_w_down[l])
        flat = lambda a: a.reshape(-1, a.shape[-1])
        routed = [moe_router(xs, norm2_g[l], mod_l[3], mod_l[4], *route)]
        if need_ctx:
            yc_d = hyena(p_c, *hy_args, None)
            cs = merge(yc_a, hs_c, yc_c, yc_d, p_c, ml_norm_g[l], wb, wo, cs, mod_c[2])
            routed.insert(0, moe_router(cs, norm2_g[l], mod_c[3], mod_c[4], *route))
        h2, ids, gates = [jnp.concatenate([flat(r[i]) for r in routed], axis=0) for i in range(3)]
        yb, dest, gates = _hier_moe(h2, ids[:, :MOE_TOP_K], gates[:, :MOE_TOP_K], *experts)
        if need_ctx:
            nc = b * n_ctx
            cs = moe_combine(cs, yb, dest[:nc], gates[:nc], mod_c[5])
            dest, gates = dest[nc:], gates[nc:]
        xs = moe_combine(xs, yb, dest, gates, mod_l[5])
    zero = jnp.zeros((b, d), F32)
    return norm_mod(xs, final_norm_g, zero, zero, F32)
```

```python
import functools
import math

import jax
import jax.numpy as jnp
import numpy as np
from jax import lax
from jax.experimental import pallas as pl
from jax.experimental.pallas import tpu as pltpu

F32 = jnp.float32
BF16 = jnp.bfloat16

EPS = 1e-6
ROPE_BASE = 10000.0
GRID_W = 64

DIFF_HEADS = 4
DIFF_DIM = 64
DIFF_VDIM = 128
ML_HEADS = 4
ML_DIM = 128
ML_CHUNK = 64
GQA_HEADS = 8
GQA_KV = 2
GQA_DIM = 64
HY_CH = 512
HY_ORDER = 2
FILTER_BANDS = 16
FILTER_SHIFT = 0.05
DECAY_TARGET = 1e-2
FAST_DECAY_PCT = 0.3
SLOW_DECAY_PCT = 1.5
N_BRANCH = 4
MOE_GROUPS = 4
MOE_EPG = 8
MOE_EXPERTS = MOE_GROUPS * MOE_EPG
MOE_TOP_K = 2
MOE_BLOCK = 256

LANES = 128
VMEM_LIMIT = 48 * 1024 * 1024

C_GATE = 0
C_DQ = 4096
C_DK = 4608
C_DV = 5120
C_MLQ = 5632
C_MLK = 6144
C_MLV = 6656
C_MLO = 7168
C_GQ = 7680
C_HY = 8192
C_GK = 9728
C_GV = 9856
C_MLG = 9984
N_P = 10240

QSCALE = (DIFF_DIM ** -0.5) * math.log2(math.e)


def _cparams(sem):
    return pltpu.CompilerParams(dimension_semantics=sem, vmem_limit_bytes=VMEM_LIMIT)


def _tile(n, target):
    if n <= target:
        return n
    for t in range(target, 7, -1):
        if n % t == 0 and t % 8 == 0:
            return t
    return n


def _norm_mod_kernel(x_ref, g_ref, sh_ref, sc_ref, o_ref):
    x = x_ref[0]
    y = x * lax.rsqrt(jnp.mean(x * x, axis=-1, keepdims=True) + EPS)
    y = y * g_ref[...]
    o_ref[0] = (y * (1.0 + sc_ref[0]) + sh_ref[0]).astype(o_ref.dtype)


def norm_mod(x, g, shift, scale, out_dtype):
    b, t, d = x.shape
    tt = _tile(t, 512)
    return pl.pallas_call(
        _norm_mod_kernel,
        grid=(b, t // tt),
        in_specs=[pl.BlockSpec((1, tt, d), lambda i, j: (i, j, 0)),
                  pl.BlockSpec((1, d), lambda i, j: (0, 0)),
                  pl.BlockSpec((1, 1, d), lambda i, j: (i, 0, 0)),
                  pl.BlockSpec((1, 1, d), lambda i, j: (i, 0, 0))],
        out_specs=pl.BlockSpec((1, tt, d), lambda i, j: (i, j, 0)),
        out_shape=jax.ShapeDtypeStruct((b, t, d), out_dtype),
        compiler_params=_cparams(("parallel", "parallel")),
        name="norm_mod",
    )(x, g.reshape(1, d), shift.reshape(b, 1, d), scale.reshape(b, 1, d))


def _mm_kernel(a_ref, w_ref, o_ref):
    o_ref[...] = jnp.dot(a_ref[...], w_ref[...], preferred_element_type=F32).astype(o_ref.dtype)


def matmul(a, w, out_dtype, tm=512, tn=1024):
    m, k = a.shape
    n = w.shape[1]
    tm = _tile(m, tm)
    tn = _tile(n, tn)
    return pl.pallas_call(
        _mm_kernel,
        grid=(m // tm, n // tn),
        in_specs=[pl.BlockSpec((tm, k), lambda i, j: (i, 0)),
                  pl.BlockSpec((k, tn), lambda i, j: (0, j))],
        out_specs=pl.BlockSpec((tm, tn), lambda i, j: (i, j)),
        out_shape=jax.ShapeDtypeStruct((m, n), out_dtype),
        compiler_params=_cparams(("parallel", "parallel")),
        name="matmul",
    )(a, w)


def _rope(x, cos, sa, sb):
    xa = pltpu.roll(x, LANES - 16, axis=1)
    xb = pltpu.roll(x, 16, axis=1)
    return x * cos + xa * sa + xb * sb


def _seg_rmsnorm(x, g):
    lane = lax.broadcasted_iota(jnp.int32, x.shape, 1)
    lo = lane < GQA_DIM
    ss = x * x
    s_lo = jnp.sum(jnp.where(lo, ss, 0.0), axis=-1, keepdims=True)
    s_hi = jnp.sum(jnp.where(lo, 0.0, ss), axis=-1, keepdims=True)
    r = jnp.where(lo, lax.rsqrt(s_lo * (1.0 / GQA_DIM) + EPS), lax.rsqrt(s_hi * (1.0 / GQA_DIM) + EPS))
    return x * r * g


def _prep_kernel(dq_ref, dk_ref, dv_ref, gq_ref, gkv_ref, cos_ref, sa_ref, sb_ref, qg_ref, kg_ref,
                 dq_o, dkt_o, dv_o, gq_o, gkt_o, gv_o, *, rope):
    if rope:
        cos, sa, sb = cos_ref[...], sa_ref[...], sb_ref[...]
        rot = lambda x: _rope(x, cos, sa, sb)
    else:
        rot = lambda x: x
    qg = qg_ref[...]
    kg = kg_ref[...]
    for j in range(DIFF_HEADS):
        sl = slice(j * LANES, (j + 1) * LANES)
        dq_o[0, :, sl] = (rot(dq_ref[0, :, sl].astype(F32)) * QSCALE).astype(BF16)
        kt = rot(dk_ref[0, :, sl].astype(F32)).T
        dkt_o[0, j, 0, 0] = kt[:DIFF_DIM].astype(BF16)
        dkt_o[0, j, 1, 0] = kt[DIFF_DIM:].astype(BF16)
        gq_o[0, :, sl] = (rot(_seg_rmsnorm(gq_ref[0, :, sl].astype(F32), qg)) * QSCALE).astype(BF16)
    dv_o[0] = dv_ref[0].astype(BF16)
    kt = rot(_seg_rmsnorm(gkv_ref[0, :, :LANES].astype(F32), kg)).T
    gkt_o[0, 0, 0] = kt[:GQA_DIM].astype(BF16)
    gkt_o[0, 1, 0] = kt[GQA_DIM:].astype(BF16)
    v = gkv_ref[0, :, LANES:].astype(BF16)
    gv_o[0, 0] = v[:, :GQA_DIM]
    gv_o[0, 1] = v[:, GQA_DIM:]


def attn_prep(p, tables, q_g, k_g, tk):
    b, t, _ = p.shape
    rope = tables is not None
    if rope:
        cos, sa, sb = tables
    else:
        cos = sa = sb = jnp.zeros((t, LANES), F32)
    nck = t // tk
    tp = _tile(tk, 512)
    sub = tk // tp
    w512 = lambda c: pl.BlockSpec((1, tp, 512), lambda i, j, c=c: (i, j, c // 512))
    tab = pl.BlockSpec((tp, LANES), lambda i, j: (j, 0))
    vec = pl.BlockSpec((1, LANES), lambda i, j: (0, 0))
    outs = pl.pallas_call(
        functools.partial(_prep_kernel, rope=rope),
        grid=(b, t // tp),
        in_specs=[w512(C_DQ), w512(C_DK), w512(C_DV), w512(C_GQ),
                  pl.BlockSpec((1, tp, 256), lambda i, j: (i, j, C_GK // 256)),
                  tab, tab, tab, vec, vec],
        out_specs=[pl.BlockSpec((1, tp, 512), lambda i, j: (i, j, 0)),
                   pl.BlockSpec((1, DIFF_HEADS, 2, 1, DIFF_DIM, tp), lambda i, j: (i, 0, 0, j // sub, 0, j % sub)),
                   pl.BlockSpec((1, tp, 512), lambda i, j: (i, j, 0)),
                   pl.BlockSpec((1, tp, 512), lambda i, j: (i, j, 0)),
                   pl.BlockSpec((1, GQA_KV, 1, GQA_DIM, tp), lambda i, j: (i, 0, j // sub, 0, j % sub)),
                   pl.BlockSpec((1, GQA_KV, tp, GQA_DIM), lambda i, j: (i, 0, j, 0))],
        out_shape=[jax.ShapeDtypeStruct((b, t, 512), BF16),
                   jax.ShapeDtypeStruct((b, DIFF_HEADS, 2, nck, DIFF_DIM, tk), BF16),
                   jax.ShapeDtypeStruct((b, t, 512), BF16),
                   jax.ShapeDtypeStruct((b, t, 512), BF16),
                   jax.ShapeDtypeStruct((b, GQA_KV, nck, GQA_DIM, tk), BF16),
                   jax.ShapeDtypeStruct((b, GQA_KV, t, GQA_DIM), BF16)],
        compiler_params=_cparams(("parallel", "parallel")),
        name="attn_prep",
    )(p, p, p, p, p, cos, sa, sb,
      jnp.tile(q_g, 2).reshape(1, LANES), jnp.tile(k_g, 2).reshape(1, LANES))
    return outs


def _flash_step(q, kt, v, m_ref, l_ref, acc_ref):
    s = jnp.dot(q, kt, preferred_element_type=F32)
    tk = s.shape[1]
    m_prev = m_ref[...]
    m_next = jnp.maximum(m_prev, jnp.max(s, axis=1, keepdims=True))
    alpha = jnp.exp2(m_prev - m_next)
    p = jnp.exp2(s - jnp.concatenate([m_next] * (tk // LANES), axis=1))
    l_ref[...] = alpha * l_ref[...] + jnp.sum(p, axis=1, keepdims=True)
    dv = acc_ref.shape[-1]
    acc_ref[...] = acc_ref[...] * alpha[:, :dv] + jnp.dot(p.astype(BF16), v, preferred_element_type=F32)
    m_ref[...] = m_next


def _attn_body(q_ref, ktc, vc, ktl, vl, m_sc, l_sc, acc_sc, *, n_lat, tk):
    m_sc[...] = jnp.full(m_sc.shape, -jnp.inf, F32)
    l_sc[...] = jnp.zeros(l_sc.shape, F32)
    acc_sc[...] = jnp.zeros(acc_sc.shape, F32)
    q = q_ref[0]
    qs = (q[:, :DIFF_DIM], q[:, DIFF_DIM:])
    for c in range(2):
        _flash_step(qs[c], ktc(c), vc(c), m_sc.at[c], l_sc.at[c], acc_sc.at[c])
    if n_lat:
        def body(i, carry):
            for c in range(2):
                _flash_step(qs[c], ktl(c, i), vl(c, i), m_sc.at[c], l_sc.at[c], acc_sc.at[c])
            return carry
        lax.fori_loop(0, n_lat, body, 0)


def _diff_attn_kernel(*refs, n_lat, tk, out_scale):
    if n_lat:
        q_ref, ktc_ref, vc_ref, ktl_ref, vl_ref, lam_ref, g_ref, o_ref, m_sc, l_sc, acc_sc = refs
        ktl = lambda c, i: ktl_ref[0, 0, c, i]
        vl = lambda c, i: vl_ref[0, pl.ds(pl.multiple_of(i * tk, tk), tk), :]
    else:
        q_ref, ktc_ref, vc_ref, lam_ref, g_ref, o_ref, m_sc, l_sc, acc_sc = refs
        ktl = vl = None
    _attn_body(q_ref, lambda c: ktc_ref[0, 0, c, 0], lambda c: vc_ref[0], ktl, vl,
               m_sc, l_sc, acc_sc, n_lat=n_lat, tk=tk)
    o0 = acc_sc[0] / l_sc[0]
    o1 = acc_sc[1] / l_sc[1]
    o = o0 - lam_ref[...] * o1
    o = o * lax.rsqrt(jnp.mean(o * o, axis=-1, keepdims=True) + EPS)
    o_ref[0] = (o * g_ref[...] * out_scale).astype(o_ref.dtype)


def diff_attention(q, ktc, vc, ktl, vl, lam, norm_g, out_scale, tq):
    b, t, _ = q.shape
    sc = vc.shape[1]
    tq = _tile(t, tq)
    n_lat, tk = (ktl.shape[3], ktl.shape[5]) if ktl is not None else (0, 0)
    in_specs = [pl.BlockSpec((1, tq, LANES), lambda i, h, j: (i, j, h)),
                pl.BlockSpec((1, 1, 2, 1, DIFF_DIM, sc), lambda i, h, j: (i, h, 0, 0, 0, 0)),
                pl.BlockSpec((1, sc, LANES), lambda i, h, j: (i, 0, h))]
    args = [q, ktc, vc]
    if n_lat:
        in_specs += [pl.BlockSpec((1, 1, 2, n_lat, DIFF_DIM, tk), lambda i, h, j: (i, h, 0, 0, 0, 0)),
                     pl.BlockSpec((1, n_lat * tk, LANES), lambda i, h, j: (i, 0, h))]
        args += [ktl, vl]
    vec = pl.BlockSpec((1, LANES), lambda i, h, j: (0, 0))
    in_specs += [vec, vec]
    args += [jnp.full((1, LANES), lam, F32), norm_g.reshape(1, LANES)]
    return pl.pallas_call(
        functools.partial(_diff_attn_kernel, n_lat=n_lat, tk=tk, out_scale=out_scale),
        grid=(b, DIFF_HEADS, t // tq),
        in_specs=in_specs,
        out_specs=pl.BlockSpec((1, tq, LANES), lambda i, h, j: (i, j, h)),
        out_shape=jax.ShapeDtypeStruct((b, t, DIFF_HEADS * DIFF_VDIM), BF16),
        scratch_shapes=[pltpu.VMEM((2, tq, LANES), F32), pltpu.VMEM((2, tq, LANES), F32),
                        pltpu.VMEM((2, tq, DIFF_VDIM), F32)],
        compiler_params=_cparams(("parallel", "parallel", "parallel")),
        name="diff_attn",
    )(*args)


def _gqa_attn_kernel(*refs, n_lat, tk):
    if n_lat:
        q_ref, ktc_ref, vc_ref, ktl_ref, vl_ref, o_ref, m_sc, l_sc, acc_sc = refs
        ktl = lambda c, i: ktl_ref[0, 0, i]
        vl = lambda c, i: vl_ref[0, 0, pl.ds(pl.multiple_of(i * tk, tk), tk), :]
    else:
        q_ref, ktc_ref, vc_ref, o_ref, m_sc, l_sc, acc_sc = refs
        ktl = vl = None
    _attn_body(q_ref, lambda c: ktc_ref[0, 0, 0], lambda c: vc_ref[0, 0], ktl, vl,
               m_sc, l_sc, acc_sc, n_lat=n_lat, tk=tk)
    o0 = acc_sc[0] / l_sc[0][:, :GQA_DIM]
    o1 = acc_sc[1] / l_sc[1][:, :GQA_DIM]
    o_ref[0] = jnp.concatenate([o0, o1], axis=-1).astype(o_ref.dtype)


def gqa_attention(q, ktc, vc, ktl, vl, tq):
    b, t, _ = q.shape
    sc = vc.shape[2]
    tq = _tile(t, tq)
    n_lat, tk = (ktl.shape[2], ktl.shape[4]) if ktl is not None else (0, 0)
    pairs = GQA_HEADS // 2
    grp = lambda h: h // (pairs // GQA_KV)
    in_specs = [pl.BlockSpec((1, tq, LANES), lambda i, h, j: (i, j, h)),
                pl.BlockSpec((1, 1, 1, GQA_DIM, sc), lambda i, h, j: (i, grp(h), 0, 0, 0)),
                pl.BlockSpec((1, 1, sc, GQA_DIM), lambda i, h, j: (i, grp(h), 0, 0))]
    args = [q, ktc, vc]
    if n_lat:
        in_specs += [pl.BlockSpec((1, 1, n_lat, GQA_DIM, tk), lambda i, h, j: (i, grp(h), 0, 0, 0)),
                     pl.BlockSpec((1, 1, n_lat * tk, GQA_DIM), lambda i, h, j: (i, grp(h), 0, 0))]
        args += [ktl, vl]
    return pl.pallas_call(
        functools.partial(_gqa_attn_kernel, n_lat=n_lat, tk=tk),
        grid=(b, pairs, t // tq),
        in_specs=in_specs,
        out_specs=pl.BlockSpec((1, tq, LANES), lambda i, h, j: (i, j, h)),
        out_shape=jax.ShapeDtypeStruct((b, t, GQA_HEADS * GQA_DIM), BF16),
        scratch_shapes=[pltpu.VMEM((2, tq, LANES), F32), pltpu.VMEM((2, tq, LANES), F32),
                        pltpu.VMEM((2, tq, GQA_DIM), F32)],
        compiler_params=_cparams(("parallel", "parallel", "parallel")),
        name="gqa_attn",
    )(*args)


ML_STEP = 256
HALO = 16


def _halo_rows(prev_ref, next_ref, j, nblk):
    prev = jnp.where(j > 0, prev_ref[0, HALO - 1:HALO, :].astype(F32), 0.0)
    nxt = jnp.where(j < nblk - 1, next_ref[0, 0:1, :].astype(F32), 0.0)
    return prev, nxt


def _conv3(x, prev, nxt, w, bias):
    tt = x.shape[0]
    row = lax.broadcasted_iota(jnp.int32, x.shape, 0)
    xm = jnp.where(row == 0, prev, pltpu.roll(x, 1, axis=0))
    xp = jnp.where(row == tt - 1, nxt, pltpu.roll(x, tt - 1, axis=0))
    return xm * w[0:1] + x * w[1:2] + xp * w[2:3] + bias


def _halo_specs(tt, width, col_block, t):
    nb = t // HALO
    prev = pl.BlockSpec((1, HALO, width), lambda i, j: (i, jnp.maximum(j * (tt // HALO) - 1, 0), col_block))
    nxt = pl.BlockSpec((1, HALO, width), lambda i, j: (i, jnp.minimum((j + 1) * (tt // HALO), nb - 1), col_block))
    return prev, nxt


def _ml_prep_kernel(q_ref, qp_ref, qn_ref, k_ref, kp_ref, kn_ref, g_ref, w_ref, b_ref, gb_ref, q_o, k_o, g_o):
    j = pl.program_id(1)
    nblk = pl.num_programs(1)
    w = ML_HEADS * ML_DIM
    qp, qn = _halo_rows(qp_ref, qn_ref, j, nblk)
    kp, kn = _halo_rows(kp_ref, kn_ref, j, nblk)
    q = _conv3(q_ref[0].astype(F32), qp, qn, w_ref[:, :w], b_ref[:, :w])
    k = _conv3(k_ref[0].astype(F32), kp, kn, w_ref[:, w:], b_ref[:, w:])
    q_o[0] = q * jax.nn.sigmoid(q)
    k_o[0] = (k * jax.nn.sigmoid(k)) * ML_DIM ** -0.5
    x = g_ref[0].astype(F32) + gb_ref[...]
    lane = lax.broadcasted_iota(jnp.int32, x.shape, 1)
    log_sig = jnp.minimum(x, 0.0) - jnp.log(1.0 + jnp.exp(-jnp.abs(x)))
    g_o[0] = jnp.where((lane % 8) >= ML_HEADS, log_sig, x)


def ml_prep(p, conv_w, conv_b, gate_b):
    b, t, _ = p.shape
    tt = _tile(t, 512)
    w = ML_HEADS * ML_DIM
    blk = lambda c: pl.BlockSpec((1, tt, w), lambda i, j, c=c: (i, j, c // w))
    qp, qn = _halo_specs(tt, w, C_MLQ // w, t)
    kp, kn = _halo_specs(tt, w, C_MLK // w, t)
    gb = jnp.zeros((1, LANES), F32).at[0, :4 * ML_HEADS].set(gate_b.reshape(-1))
    return pl.pallas_call(
        _ml_prep_kernel,
        grid=(b, t // tt),
        in_specs=[blk(C_MLQ), qp, qn, blk(C_MLK), kp, kn,
                  pl.BlockSpec((1, tt, LANES), lambda i, j: (i, j, C_MLG // LANES)),
                  pl.BlockSpec((3, 2 * w), lambda i, j: (0, 0)),
                  pl.BlockSpec((1, 2 * w), lambda i, j: (0, 0)),
                  pl.BlockSpec((1, LANES), lambda i, j: (0, 0))],
        out_specs=[pl.BlockSpec((1, tt, w), lambda i, j: (i, j, 0)),
                   pl.BlockSpec((1, tt, w), lambda i, j: (i, j, 0)),
                   pl.BlockSpec((1, tt, LANES), lambda i, j: (i, j, 0))],
        out_shape=[jax.ShapeDtypeStruct((b, t, w), F32), jax.ShapeDtypeStruct((b, t, w), F32),
                   jax.ShapeDtypeStruct((b, t, LANES), F32)],
        compiler_params=_cparams(("parallel", "parallel")),
        name="ml_prep",
    )(p, p, p, p, p, p, p, conv_w, conv_b.reshape(1, 2 * w), gb)


_NT = (((1,), (1,)), ((), ()))


def _ml_chunk_head(q, k, v, li_c, bc_c, bc_r, b_last, seen, ct_ref, n_ref, m_ref):
    m = m_ref[:, 0:1]
    ct = ct_ref[...]
    n_rows = n_ref[...]
    qb, kb, vb = q.astype(BF16), k.astype(BF16), v.astype(BF16)
    d_t = jnp.where(seen, bc_r + (li_c - bc_c), -jnp.inf)
    inter = bc_r + m
    m_t = jnp.maximum(inter, jnp.max(d_t, axis=0, keepdims=True))
    w_inter = jnp.exp(inter - m_t)
    s_t = lax.dot_general(kb, qb, _NT, preferred_element_type=F32) * jnp.exp(d_t - m_t)
    num_t = (jnp.dot(vb.T, s_t.astype(BF16), preferred_element_type=F32)
             + w_inter * lax.dot_general(ct.astype(BF16), qb, _NT, preferred_element_type=F32))
    qn = lax.dot_general(n_rows.astype(BF16), qb, _NT, preferred_element_type=F32)[0:1]
    den = jnp.sum(s_t, axis=0, keepdims=True) + w_inter * qn
    h_t = num_t / jnp.maximum(jnp.abs(den), jnp.exp(-m_t))
    g = b_last - bc_c + li_c
    m_new = jnp.maximum(b_last + m, jnp.max(g, axis=0, keepdims=True))
    kw = k * jnp.exp(g - m_new)
    wc = jnp.exp(b_last + m - m_new)
    ct_ref[...] = wc * ct + jnp.dot(vb.T, kw.astype(BF16), preferred_element_type=F32)
    n_ref[...] = wc * n_rows + jnp.sum(kw, axis=0, keepdims=True)
    m_ref[...] = jnp.broadcast_to(m_new, m_ref.shape)
    return h_t.T


def _ml_scan_kernel(q_ref, k_ref, v_ref, g_ref, c0_ref, n0_ref, m0_ref, h_o, c1_o, n1_o, m1_o,
                    c_sc, n_sc, m_sc, *, direction):
    j = pl.program_id(0)

    @pl.when(j == 0)
    def _():
        c_sc[...] = c0_ref[...]
        n_sc[...] = n0_ref[...]
        m_sc[...] = m0_ref[...]

    r = lax.broadcasted_iota(jnp.int32, (ML_CHUNK, ML_CHUNK), 0)
    s = lax.broadcasted_iota(jnp.int32, (ML_CHUNK, ML_CHUNK), 1)
    tri_f = ((s >= r) if direction else (s <= r)).astype(F32)
    seen = (r >= s) if direction else (r <= s)
    n_chunks = q_ref.shape[1] // ML_CHUNK
    order = range(n_chunks - 1, -1, -1) if direction else range(n_chunks)
    last = 0 if direction else ML_CHUNK - 1
    for c in order:
        rows = slice(c * ML_CHUNK, (c + 1) * ML_CHUNK)
        for bi in range(q_ref.shape[0]):
            gch = g_ref[bi, rows, :]
            bc = jnp.dot(tri_f, gch, preferred_element_type=F32, precision=lax.Precision.HIGHEST)
            bct = bc.T
            for hd in range(ML_HEADS):
                ci = 2 * ML_HEADS * direction + hd
                cf = ci + ML_HEADS
                cols = slice(hd * ML_DIM, (hd + 1) * ML_DIM)
                h = _ml_chunk_head(q_ref[bi, rows, cols], k_ref[bi, rows, cols], v_ref[bi, rows, cols],
                                   gch[:, ci:ci + 1], bc[:, cf:cf + 1], bct[cf:cf + 1, :],
                                   bc[last:last + 1, cf:cf + 1], seen,
                                   c_sc.at[bi, hd], n_sc.at[bi, hd], m_sc.at[bi, hd])
                h_o[bi, rows, cols] = h

    @pl.when(j == pl.num_programs(0) - 1)
    def _():
        c1_o[...] = c_sc[...]
        n1_o[...] = n_sc[...]
        m1_o[...] = m_sc[...]


def _ml_state_shapes(b):
    return [(b, ML_HEADS, ML_DIM, ML_DIM), (b, ML_HEADS, 8, ML_DIM), (b, ML_HEADS, 1, ML_DIM)]


def ml_scan(q, k, p, g, state, direction):
    b, t, w = q.shape
    ts = _tile(t, ML_STEP)
    nst = t // ts
    tok = (lambda j: (0, nst - 1 - j, 0)) if direction else (lambda j: (0, j, 0))
    tokv = (lambda j: (0, nst - 1 - j, C_MLV // w)) if direction else (lambda j: (0, j, C_MLV // w))
    st_shapes = _ml_state_shapes(b)
    st_specs = [pl.BlockSpec(s, lambda j: (0, 0, 0, 0)) for s in st_shapes]
    h, c1, n1, m1 = pl.pallas_call(
        functools.partial(_ml_scan_kernel, direction=direction),
        grid=(nst,),
        in_specs=[pl.BlockSpec((b, ts, w), tok), pl.BlockSpec((b, ts, w), tok), pl.BlockSpec((b, ts, w), tokv),
                  pl.BlockSpec((b, ts, LANES), tok)] + st_specs,
        out_specs=[pl.BlockSpec((b, ts, w), tok)] + st_specs,
        out_shape=[jax.ShapeDtypeStruct((b, t, w), F32)] + [jax.ShapeDtypeStruct(s, F32) for s in st_shapes],
        scratch_shapes=[pltpu.VMEM(s, F32) for s in st_shapes],
        compiler_params=_cparams(("arbitrary",)),
        name="ml_scan",
    )(q, k, p, g, *state)
    return h, (c1, n1, m1)


def mlstm_branch(p_c, p_l, conv_w, conv_b, gate_b, need_ctx):
    b = p_l.shape[0]
    qc, kc, gc = ml_prep(p_c, conv_w, conv_b, gate_b)
    ql, kl, gl = ml_prep(p_l, conv_w, conv_b, gate_b)
    zero = tuple(jnp.zeros(s, F32) for s in _ml_state_shapes(b))
    hs_c, hs_l = [], []
    for direction in (0, 1):
        hc, st = ml_scan(qc, kc, p_c, gc, zero, direction)
        hl, _ = ml_scan(ql, kl, p_l, gl, st, direction)
        hs_c.append(hc)
        hs_l.append(hl)
    return (tuple(hs_c) if need_ctx else None), tuple(hs_l)


def _merge_kernel(ya_ref, hf_ref, hb_ref, yc_ref, yd_ref, og_ref, g_ref, mg_ref, wb_ref, wo_ref, x_ref, gate_ref,
                  o_ref):
    d = x_ref.shape[-1]
    mg = mg_ref[...]
    yb = []
    for hd in range(ML_HEADS):
        cols = slice(hd * ML_DIM, (hd + 1) * ML_DIM)
        h = hf_ref[0, :, cols] + hb_ref[0, :, cols]
        h = h * lax.rsqrt(jnp.mean(h * h, axis=-1, keepdims=True) + EPS) * mg
        yb.append((h * jax.nn.sigmoid(og_ref[0, :, cols].astype(F32))).astype(BF16))
    ys = (ya_ref[0].astype(BF16), jnp.concatenate(yb, axis=-1), yc_ref[0].astype(BF16), yd_ref[0].astype(BF16))
    acc = None
    for n, y in enumerate(ys):
        t = jnp.dot(y, wb_ref[n], preferred_element_type=F32)
        t = jax.nn.sigmoid(g_ref[0, :, n * d:(n + 1) * d].astype(F32)) * t
        acc = t if acc is None else acc + t
    z = jnp.dot(acc.astype(BF16), wo_ref[...], preferred_element_type=F32)
    o_ref[0] = x_ref[0] + gate_ref[0] * z


def merge(ya, hs, yc, yd, p, ml_norm_g, w_branch, w_out, x, gate):
    b, t, d = x.shape
    tm = _tile(t, 512)
    w = ML_HEADS * ML_DIM
    ysp = pl.BlockSpec((1, tm, w), lambda i, j: (i, j, 0))
    return pl.pallas_call(
        _merge_kernel,
        grid=(b, t // tm),
        in_specs=[ysp, ysp, ysp, ysp, ysp,
                  pl.BlockSpec((1, tm, w), lambda i, j: (i, j, C_MLO // w)),
                  pl.BlockSpec((1, tm, N_BRANCH * d), lambda i, j: (i, j, C_GATE // (N_BRANCH * d))),
                  pl.BlockSpec((1, ML_DIM), lambda i, j: (0, 0)),
                  pl.BlockSpec((N_BRANCH, w, d), lambda i, j: (0, 0, 0)),
                  pl.BlockSpec((d, d), lambda i, j: (0, 0)),
                  pl.BlockSpec((1, tm, d), lambda i, j: (i, j, 0)),
                  pl.BlockSpec((1, 1, d), lambda i, j: (i, 0, 0))],
        out_specs=pl.BlockSpec((1, tm, d), lambda i, j: (i, j, 0)),
        out_shape=jax.ShapeDtypeStruct((b, t, d), F32),
        compiler_params=_cparams(("parallel", "parallel")),
        name="merge",
    )(ya, hs[0], hs[1], yc, yd, p, p, ml_norm_g.reshape(1, ML_DIM), w_branch, w_out, x, gate.reshape(b, 1, d))


def _expert_kernel(be_ref, x_ref, wg_ref, wu_ref, wd_ref, o_ref, wg_sc, wu_sc, wd_sc):
    i = pl.program_id(0)

    @pl.when(jnp.logical_or(i == 0, be_ref[i] != be_ref[jnp.maximum(i - 1, 0)]))
    def _():
        wg_sc[...] = wg_ref[0].astype(BF16)
        wu_sc[...] = wu_ref[0].astype(BF16)
        wd_sc[...] = wd_ref[0].astype(BF16)

    x = x_ref[...]
    a = jnp.dot(x, wg_sc[...], preferred_element_type=F32)
    u = jnp.dot(x, wu_sc[...], preferred_element_type=F32)
    h = (a * jax.nn.sigmoid(a)) * u
    o_ref[...] = jnp.dot(h.astype(BF16), wd_sc[...], preferred_element_type=F32).astype(o_ref.dtype)


def expert_blocks(xb, blk_exp, w_gate, w_up, w_down):
    m, d = xb.shape
    hdim = w_gate.shape[-1]
    n_blocks = m // MOE_BLOCK
    grid_spec = pltpu.PrefetchScalarGridSpec(
        num_scalar_prefetch=1,
        grid=(n_blocks,),
        in_specs=[pl.BlockSpec((MOE_BLOCK, d), lambda i, be: (i, 0)),
                  pl.BlockSpec((1, d, hdim), lambda i, be: (be[i], 0, 0)),
                  pl.BlockSpec((1, d, hdim), lambda i, be: (be[i], 0, 0)),
                  pl.BlockSpec((1, hdim, d), lambda i, be: (be[i], 0, 0))],
        out_specs=pl.BlockSpec((MOE_BLOCK, d), lambda i, be: (i, 0)),
        scratch_shapes=[pltpu.VMEM((d, hdim), BF16), pltpu.VMEM((d, hdim), BF16), pltpu.VMEM((hdim, d), BF16)],
    )
    return pl.pallas_call(
        _expert_kernel,
        grid_spec=grid_spec,
        out_shape=jax.ShapeDtypeStruct((m, d), BF16),
        compiler_params=_cparams(("arbitrary",)),
        name="moe_experts",
    )(blk_exp, xb, w_gate, w_up, w_down)


def _rope_tables(n_tok):
    rows = n_tok // GRID_W
    row = jnp.repeat(jnp.arange(rows, dtype=F32), GRID_W)
    col = jnp.broadcast_to(jnp.arange(GRID_W, dtype=F32), (rows, GRID_W)).reshape(-1)
    n_freq = DIFF_DIM // 4
    inv = ROPE_BASE ** (-jnp.arange(n_freq, dtype=F32) / n_freq)
    ar = row[:, None] * inv
    ac = col[:, None] * inv
    ang = jnp.concatenate([ar, ar, ac, ac], axis=-1)
    cos, sin = jnp.cos(ang), jnp.sin(ang)
    first = (jnp.arange(DIFF_DIM) % 32) < 16
    sa = jnp.where(first, -sin, 0.0)
    sb = jnp.where(first, 0.0, sin)
    return tuple(jnp.tile(a, (1, 2)) for a in (cos, sa, sb))


DFT_N2 = 256
DFT_J = 8
DFT_P = 4


def _hy_prep_kernel(*refs):
    ins, outs = refs[:9], refs[11:]
    w_ref, b_ref = refs[9], refs[10]
    j = pl.program_id(1)
    nblk = pl.num_programs(1)
    for n in range(HY_ORDER + 1):
        x_ref, p_ref, n_ref = ins[3 * n:3 * n + 3]
        cols = slice(n * HY_CH, (n + 1) * HY_CH)
        prev, nxt = _halo_rows(p_ref, n_ref, j, nblk)
        outs[n][0] = _conv3(x_ref[0].astype(F32), prev, nxt, w_ref[:, cols], b_ref[:, cols])


def hy_prep(p, conv_w, conv_b):
    b, t, _ = p.shape
    tt = _tile(t, 512)
    in_specs, args = [], []
    for n in range(HY_ORDER + 1):
        cb = C_HY // HY_CH + n
        prev, nxt = _halo_specs(tt, HY_CH, cb, t)
        in_specs += [pl.BlockSpec((1, tt, HY_CH), lambda i, j, cb=cb: (i, j, cb)), prev, nxt]
        args += [p, p, p]
    nch = (HY_ORDER + 1) * HY_CH
    in_specs += [pl.BlockSpec((3, nch), lambda i, j: (0, 0)), pl.BlockSpec((1, nch), lambda i, j: (0, 0))]
    osp = pl.BlockSpec((1, tt, HY_CH), lambda i, j: (i, j, 0))
    return pl.pallas_call(
        _hy_prep_kernel,
        grid=(b, t // tt),
        in_specs=in_specs,
        out_specs=[osp] * (HY_ORDER + 1),
        out_shape=[jax.ShapeDtypeStruct((b, t, HY_CH), F32)] * (HY_ORDER + 1),
        compiler_params=_cparams(("parallel", "parallel")),
        name="hy_prep",
    )(*args, conv_w, conv_b.reshape(1, nch))


def _hy_filter_kernel(emb_ref, w1_ref, b1_ref, w2_ref, b2_ref, w3_ref, fr_ref, al_ref, f_o, ss_o, *, length, half):
    j = pl.program_id(0)
    tt = emb_ref.shape[0]
    a = jnp.dot(emb_ref[...].astype(BF16), w1_ref[...].astype(BF16), preferred_element_type=F32) + b1_ref[...]
    a = jnp.sin(fr_ref[0:1, :] * a)
    a = jnp.dot(a.astype(BF16), w2_ref[...].astype(BF16), preferred_element_type=F32) + b2_ref[...]
    a = jnp.sin(fr_ref[1:2, :] * a)
    filt = jnp.dot(a.astype(BF16), w3_ref[...].astype(BF16), preferred_element_type=F32)
    r = lax.broadcasted_iota(jnp.int32, (tt, HY_CH), 0)
    if half:
        row = DFT_N2 * (r & (half - 1)) + j * (tt // half) + (r >> (half.bit_length() - 1))
    else:
        row = r + j * tt
    window = jnp.exp(-(row.astype(F32) / length) * al_ref[...]) + FILTER_SHIFT

    @pl.when(j == 0)
    def _():
        ss_o[...] = jnp.zeros(ss_o.shape, F32)

    for o in range(HY_ORDER):
        for d in range(2):
            idx = 2 * o + d
            f = filt[:, idx * HY_CH:(idx + 1) * HY_CH] * window
            if d == 1:
                f = jnp.where(row == 0, 0.0, f)
            if half:
                for jj in range(tt // half):
                    f_o[idx, :, jj * HY_CH:(jj + 1) * HY_CH] = f[jj * half:(jj + 1) * half]
            else:
                f_o[idx] = f
            ss_o[o:o + 1, :] += jnp.sum(f * f, axis=0, keepdims=True)


def hy_filters(length, w1, b1, w2, b2, w3, freq, wide):
    t = jnp.arange(length, dtype=F32) / length
    bands = jnp.arange(1, FILTER_BANDS + 1, dtype=F32)
    ang = 2.0 * math.pi * t[:, None] * bands
    emb = jnp.concatenate([t[:, None], jnp.cos(ang), jnp.sin(ang)], axis=-1)
    pad = LANES - emb.shape[1]
    emb = jnp.pad(emb, ((0, 0), (0, pad)))
    w1 = jnp.pad(w1, ((0, pad), (0, 0)))
    ne, nh = emb.shape[1], w1.shape[1]
    alpha = jnp.linspace(abs(math.log(DECAY_TARGET)) / SLOW_DECAY_PCT,
                         abs(math.log(DECAY_TARGET)) / FAST_DECAY_PCT, HY_CH).reshape(1, HY_CH)
    tt = _tile(length, 512)
    half = length // DFT_N2
    wide = wide and half > 0 and half & (half - 1) == 0 and tt % half == 0
    full = lambda shape: pl.BlockSpec(shape, lambda j: (0,) * len(shape))
    if wide:
        emb = jnp.swapaxes(emb.reshape(half, DFT_N2, ne), 0, 1).reshape(length, ne)
        f_spec = pl.BlockSpec((2 * HY_ORDER, half, (tt // half) * HY_CH), lambda j: (0, 0, j))
        f_shape = (2 * HY_ORDER, half, DFT_N2 * HY_CH)
    else:
        f_spec = pl.BlockSpec((2 * HY_ORDER, tt, HY_CH), lambda j: (0, j, 0))
        f_shape = (2 * HY_ORDER, length, HY_CH)
    f, ss = pl.pallas_call(
        functools.partial(_hy_filter_kernel, length=length, half=half if wide else 0),
        grid=(length // tt,),
        in_specs=[pl.BlockSpec((tt, ne), lambda j: (j, 0)), full((ne, nh)), full((1, nh)), full((nh, nh)),
                  full((1, nh)), full((nh, 2 * HY_ORDER * HY_CH)), full((2, nh)), full((1, HY_CH))],
        out_specs=[f_spec, full((HY_ORDER, HY_CH))],
        out_shape=[jax.ShapeDtypeStruct(f_shape, F32), jax.ShapeDtypeStruct((HY_ORDER, HY_CH), F32)],
        compiler_params=_cparams(("arbitrary",)),
        name="hy_filter",
    )(emb, w1, b1.reshape(1, nh), w2, b2.reshape(1, nh), w3, freq, alpha)
    return f, lax.rsqrt(ss + EPS)


def _dft_tables(length):
    n = 2 * length
    n1 = n // DFT_N2
    half = n1 // 2
    n1h = -(-(half + 1) // 16) * 16
    kv = jnp.arange(n1h, dtype=jnp.int32)
    valid = (kv <= half).astype(F32)[None, :, None]
    pair = jnp.where((kv == 0) | (kv == half), 1.0, 2.0)[None, :, None] * valid
    k1 = kv[None, :, None]
    tn = (DFT_N2 * jnp.arange(half, dtype=jnp.int32)[None, None, :]
          + jnp.arange(DFT_N2, dtype=jnp.int32)[:, None, None])
    th = (2.0 * math.pi / n) * ((k1 * tn) % n).astype(F32)
    ga = jnp.concatenate([jnp.cos(th) * valid, -jnp.sin(th) * valid], axis=1).astype(BF16)
    gi = jnp.swapaxes(jnp.concatenate([jnp.cos(th) * pair, -jnp.sin(th) * pair], axis=1), 1, 2).astype(BF16)
    kk = jnp.arange(DFT_N2, dtype=jnp.int32)
    t2 = (2.0 * math.pi / DFT_N2) * ((kk[:, None] * kk[None, :]) % DFT_N2).astype(F32)
    c2, s2 = jnp.cos(t2), jnp.sin(t2)
    mf = jnp.block([[c2, s2], [-s2, c2]]).astype(BF16)
    mi = jnp.block([[c2, -s2], [s2, c2]]).astype(BF16)
    return ga, gi, mf, mi


def _dft_a_kernel(z_ref, g_ref, o_ref):
    c = HY_CH
    for j in range(DFT_J):
        slab = z_ref[0, :, j * c:(j + 1) * c].astype(BF16)
        r = jnp.dot(g_ref[j], slab, preferred_element_type=F32)
        o_ref[0, :, :, j * c:(j + 1) * c] = r.reshape(2, r.shape[0] // 2, c).astype(o_ref.dtype)


def dft_a(z, ga):
    bz, half, wid = z.shape
    n1 = ga.shape[1] // 2
    jc = DFT_J * HY_CH
    return pl.pallas_call(
        _dft_a_kernel,
        grid=(bz, DFT_N2 // DFT_J),
        in_specs=[pl.BlockSpec((1, half, jc), lambda i, j: (i, 0, j)),
                  pl.BlockSpec((DFT_J, 2 * n1, half), lambda i, j: (j, 0, 0))],
        out_specs=pl.BlockSpec((1, 2, n1, jc), lambda i, j: (i, 0, 0, j)),
        out_shape=jax.ShapeDtypeStruct((bz, 2, n1, wid), BF16),
        compiler_params=_cparams(("parallel", "parallel")),
        name="dft_a",
    )(z, ga)


def _stack_ri(ref, b, k):
    return jnp.concatenate([ref[b, 0, k], ref[b, 1, k]], axis=0)


def _spec_filter_kernel(f_ref, mf_ref, sc_ref, h_o):
    sc = sc_ref[0]
    for k in range(DFT_P):
        xf = jnp.dot(mf_ref[...], _stack_ri(f_ref, 0, k), preferred_element_type=F32)
        xb = jnp.dot(mf_ref[...], _stack_ri(f_ref, 1, k), preferred_element_type=F32)
        h_o[0, k, 0] = ((xf[:DFT_N2] + xb[:DFT_N2]) * sc).astype(h_o.dtype)
        h_o[0, k, 1] = ((xf[DFT_N2:] - xb[DFT_N2:]) * sc).astype(h_o.dtype)


def spec_filter(fa, mf, scale):
    nb, _, n1, _, c = fa.shape
    order = nb // 2
    return pl.pallas_call(
        _spec_filter_kernel,
        grid=(order, n1 // DFT_P),
        in_specs=[pl.BlockSpec((2, 2, DFT_P, DFT_N2, c), lambda o, k: (o, 0, k, 0, 0)),
                  pl.BlockSpec((2 * DFT_N2, 2 * DFT_N2), lambda o, k: (0, 0)),
                  pl.BlockSpec((1, 1, c), lambda o, k: (o, 0, 0))],
        out_specs=pl.BlockSpec((1, DFT_P, 2, DFT_N2, c), lambda o, k: (o, k, 0, 0, 0)),
        out_shape=jax.ShapeDtypeStruct((order, n1, 2, DFT_N2, c), BF16),
        compiler_params=_cparams(("parallel", "parallel")),
        name="spec_filter",
    )(fa, mf, scale.reshape(order, 1, c))


def _spec_conv_kernel(a_ref, h_ref, mf_ref, mi_ref, o_ref):
    for k in range(DFT_P):
        x = jnp.dot(mf_ref[...], _stack_ri(a_ref, 0, k), preferred_element_type=F32)
        xr, xi = x[:DFT_N2], x[DFT_N2:]
        hr, hi = h_ref[0, k, 0].astype(F32), h_ref[0, k, 1].astype(F32)
        y = jnp.concatenate([xr * hr - xi * hi, xr * hi + xi * hr], axis=0).astype(BF16)
        z = jnp.dot(mi_ref[...], y, preferred_element_type=F32)
        o_ref[0, 0, k] = z[:DFT_N2].astype(o_ref.dtype)
        o_ref[0, 1, k] = z[DFT_N2:].astype(o_ref.dtype)


def spec_conv(a, h, order, mf, mi):
    b, _, n1, _, c = a.shape
    blk = pl.BlockSpec((1, 2, DFT_P, DFT_N2, c), lambda i, k: (i, 0, k, 0, 0))
    mat = pl.BlockSpec((2 * DFT_N2, 2 * DFT_N2), lambda i, k: (0, 0))
    return pl.pallas_call(
        _spec_conv_kernel,
        grid=(b, n1 // DFT_P),
        in_specs=[blk, pl.BlockSpec((1, DFT_P, 2, DFT_N2, c), lambda i, k: (order, k, 0, 0, 0)), mat, mat],
        out_specs=blk,
        out_shape=jax.ShapeDtypeStruct(a.shape, BF16),
        compiler_params=_cparams(("parallel", "parallel")),
        name="spec_conv",
    )(a, h, mf, mi)


def _dft_ainv_kernel(z_ref, g_ref, xg_ref, zin_ref, bias_ref, o_ref):
    c = HY_CH
    n1 = z_ref.shape[2]
    for j in range(DFT_J):
        cols = slice(j * c, (j + 1) * c)
        zz = z_ref[0, :, :, cols].reshape(2 * n1, c)
        y = jnp.dot(g_ref[j], zz, preferred_element_type=F32)
        o_ref[0, :, cols] = xg_ref[0, :, cols] * (y + bias_ref[...] * zin_ref[0, :, cols])


def dft_ainv(z, gi, xg, zin, bias):
    b, _, n1, wid = z.shape
    half = gi.shape[1]
    jc = DFT_J * HY_CH
    tok = pl.BlockSpec((1, half, jc), lambda i, j: (i, 0, j))
    return pl.pallas_call(
        _dft_ainv_kernel,
        grid=(b, DFT_N2 // DFT_J),
        in_specs=[pl.BlockSpec((1, 2, n1, jc), lambda i, j: (i, 0, 0, j)),
                  pl.BlockSpec((DFT_J, half, 2 * n1), lambda i, j: (j, 0, 0)),
                  tok, tok, pl.BlockSpec((1, HY_CH), lambda i, j: (0, 0))],
        out_specs=tok,
        out_shape=jax.ShapeDtypeStruct((b, half, wid), F32),
        compiler_params=_cparams(("parallel", "parallel")),
        name="dft_ainv",
    )(z, gi, xg, zin, bias.reshape(1, HY_CH))


def _ctx_conv_kernel(z_ref, xg_ref, f_ref, mf_ref, mi_ref, sc_ref, bias_ref, o_ref):
    nf = mf_ref.shape[0] // 2
    mf = mf_ref[...]
    xf = jnp.dot(mf, f_ref[0].astype(BF16), preferred_element_type=F32)
    xb = jnp.dot(mf, f_ref[1].astype(BF16), preferred_element_type=F32)
    sc = sc_ref[0]
    hr = (xf[:nf] + xb[:nf]) * sc
    hi = (xf[nf:] - xb[nf:]) * sc
    z = z_ref[0]
    x = jnp.dot(mf, z.astype(BF16), preferred_element_type=F32)
    xr, xi = x[:nf], x[nf:]
    y = jnp.concatenate([xr * hr - xi * hi, xr * hi + xi * hr], axis=0).astype(BF16)
    o_ref[0] = xg_ref[0] * (jnp.dot(mi_ref[...], y, preferred_element_type=F32) + bias_ref[...] * z)


def ctx_conv(z, xg, f, order, scale, bias):
    b, length, c = z.shape
    n = 2 * length
    kk = jnp.arange(n, dtype=jnp.int32)[:, None]
    tn = jnp.arange(length, dtype=jnp.int32)[None, :]
    th = (2.0 * math.pi / n) * ((kk * tn) % n).astype(F32)
    mf = jnp.concatenate([jnp.cos(th), -jnp.sin(th)], axis=0).astype(BF16)
    mi = jnp.concatenate([jnp.cos(th.T), -jnp.sin(th.T)], axis=1).astype(BF16)
    tok = pl.BlockSpec((1, length, c), lambda i: (i, 0, 0))
    return pl.pallas_call(
        _ctx_conv_kernel,
        grid=(b,),
        in_specs=[tok, tok, pl.BlockSpec((2, length, c), lambda i: (order, 0, 0)),
                  pl.BlockSpec((2 * n, length), lambda i: (0, 0)), pl.BlockSpec((length, 2 * n), lambda i: (0, 0)),
                  pl.BlockSpec((1, 1, c), lambda i: (order, 0, 0)), pl.BlockSpec((1, c), lambda i: (0, 0))],
        out_specs=tok,
        out_shape=jax.ShapeDtypeStruct((b, length, c), F32),
        compiler_params=_cparams(("parallel",)),
        name="ctx_conv",
    )(z, xg, f, mf, mi, scale.reshape(-1, 1, c), bias.reshape(1, c))


def hyena(p, conv_w, conv_b, w1, b1, w2, b2, w3, freq, bias, tables):
    b, length, _ = p.shape
    parts = hy_prep(p, conv_w, conv_b)
    f, rnorm = hy_filters(length, w1, b1, w2, b2, w3, freq, wide=tables is not None)
    scale = rnorm / (2 * length)
    z = parts[0]
    if tables is None:
        for o in range(HY_ORDER):
            z = ctx_conv(z, parts[o + 1], f, o, scale, bias[o])
        return z
    ga, gi, mf, mi = tables
    half = ga.shape[2]
    wid = DFT_N2 * HY_CH
    fa = dft_a(f.reshape(2 * HY_ORDER, half, wid), ga)
    n1 = fa.shape[2]
    h = spec_filter(fa.reshape(2 * HY_ORDER, 2, n1, DFT_N2, HY_CH), mf, scale)
    for o in range(HY_ORDER):
        a = dft_a(z.reshape(b, half, wid), ga).reshape(b, 2, n1, DFT_N2, HY_CH)
        zc = spec_conv(a, h, o, mf, mi).reshape(b, 2, n1, wid)
        z = dft_ainv(zc, gi, parts[o + 1].reshape(b, half, wid), z.reshape(b, half, wid), bias[o])
        z = z.reshape(b, length, HY_CH)
    return z


def _router_kernel(x_ref, g_ref, sh_ref, sc_ref, w_ref, b_ref, h_o, id_o, gate_o):
    x = x_ref[0]
    y = x * lax.rsqrt(jnp.mean(x * x, axis=-1, keepdims=True) + EPS)
    y = ((y * g_ref[...]) * (1.0 + sc_ref[0]) + sh_ref[0]).astype(BF16)
    h_o[0] = y
    logits = jnp.dot(y, w_ref[...], preferred_element_type=F32) + b_ref[...]
    lane = lax.broadcasted_iota(jnp.int32, logits.shape, 1)
    lane_f = lane.astype(F32)
    none = float(LANES)

    def top(vals):
        v = jnp.max(vals, axis=1, keepdims=True)
        return v, jnp.min(jnp.where(vals == v, lane_f, none), axis=1, keepdims=True)

    is_grp = lane < MOE_GROUPS
    mg, grp = top(jnp.where(is_grp, logits, -jnp.inf))
    p_grp = 1.0 / jnp.sum(jnp.where(is_grp, jnp.exp(logits - mg), 0.0), axis=1, keepdims=True)
    lo = MOE_GROUPS + MOE_EPG * grp
    el = jnp.where((lane_f >= lo) & (lane_f < lo + MOE_EPG), logits, -jnp.inf)
    v1, i1 = top(el)
    v2, i2 = top(jnp.where(lane_f == i1, -jnp.inf, el))
    t = jnp.exp(v2 - v1)
    g1 = p_grp / (1.0 + t)
    ids = jnp.where(lane == 0, i1 - MOE_GROUPS, jnp.where(lane == 1, i2 - MOE_GROUPS, 0.0))
    id_o[0, 0] = ids.T[:ROUTE_ROWS].astype(jnp.int32)
    gate_o[0, 0] = jnp.where(lane == 0, g1, jnp.where(lane == 1, g1 * t, 0.0)).T[:ROUTE_ROWS]


ROUTE_ROWS = 8


def moe_router(x, g, shift, scale, w_group, b_group, w_router, b_router):
    b, t, d = x.shape
    tt = _tile(t, 512)
    rows = pl.BlockSpec((1, 1, ROUTE_ROWS, tt), lambda i, j: (i, j, 0, 0))
    npad = LANES - MOE_GROUPS - MOE_EXPERTS
    w = jnp.concatenate([w_group, w_router, jnp.zeros((d, npad), F32)], axis=1).astype(BF16)
    bias = jnp.concatenate([b_group, b_router, jnp.zeros((npad,), F32)]).reshape(1, LANES)
    tok = lambda width: pl.BlockSpec((1, tt, width), lambda i, j: (i, j, 0))
    mod = pl.BlockSpec((1, 1, d), lambda i, j: (i, 0, 0))
    return pl.pallas_call(
        _router_kernel,
        grid=(b, t // tt),
        in_specs=[tok(d), pl.BlockSpec((1, d), lambda i, j: (0, 0)), mod, mod,
                  pl.BlockSpec((d, LANES), lambda i, j: (0, 0)), pl.BlockSpec((1, LANES), lambda i, j: (0, 0))],
        out_specs=[tok(d), rows, rows],
        out_shape=[jax.ShapeDtypeStruct((b, t, d), BF16),
                   jax.ShapeDtypeStruct((b, t // tt, ROUTE_ROWS, tt), jnp.int32),
                   jax.ShapeDtypeStruct((b, t // tt, ROUTE_ROWS, tt), F32)],
        compiler_params=_cparams(("parallel", "parallel")),
        name="moe_router",
    )(x, g.reshape(1, d), shift.reshape(b, 1, d), scale.reshape(b, 1, d), w, bias)


def _blocked_cumsum(onehot, blk=256):
    m, e = onehot.shape
    if m % blk:
        return jnp.cumsum(onehot, axis=0)
    oh = onehot.astype(BF16).reshape(m // blk, blk, e)
    tril = jnp.tril(jnp.ones((blk, blk), BF16))
    within = jnp.einsum('ts,bse->bte', tril, oh, preferred_element_type=F32)
    tot = within[:, -1, :]
    off = jnp.cumsum(tot, axis=0) - tot
    return (within + off[:, None, :]).reshape(m, e).astype(jnp.int32)


def _hier_moe(h, ids, w_gate, w_up, w_down):
    n_tok, d = h.shape
    e_flat = ids.reshape(-1)
    m_slots = n_tok * MOE_TOP_K
    onehot = (e_flat[:, None] == jnp.arange(MOE_EXPERTS, dtype=jnp.int32)[None, :]).astype(jnp.int32)
    csum = _blocked_cumsum(onehot)
    rank = jnp.sum(onehot * csum, axis=1) - 1
    counts = csum[-1]
    padded = (counts + MOE_BLOCK - 1) // MOE_BLOCK * MOE_BLOCK
    p_end = jnp.cumsum(padded)
    dest = (p_end - padded)[e_flat] + rank
    n_blocks = -(-(m_slots + MOE_EXPERTS * (MOE_BLOCK - 1)) // MOE_BLOCK)
    slot_tok = jnp.arange(m_slots, dtype=jnp.int32) // MOE_TOP_K
    buf_tok = jnp.zeros((n_blocks * MOE_BLOCK,), jnp.int32).at[dest].set(slot_tok)
    blk_exp = jnp.minimum(jnp.searchsorted(p_end, jnp.arange(n_blocks) * MOE_BLOCK, side='right'),
                          MOE_EXPERTS - 1).astype(jnp.int32)
    yb = expert_blocks(h[buf_tok], blk_exp, w_gate, w_up, w_down)
    return yb, dest.reshape(n_tok, MOE_TOP_K)


def _moe_combine_kernel(x_ref, y0_ref, y1_ref, g_ref, m_ref, o_ref):
    g = g_ref[0, 0].T
    f = g[:, 0:1] * y0_ref[...].astype(F32) + g[:, 1:2] * y1_ref[...].astype(F32)
    o_ref[0] = x_ref[0] + m_ref[0] * f


def moe_combine(x, yb, dest, gates, mod):
    b, t, d = x.shape
    y0 = yb[dest[:, 0]]
    y1 = yb[dest[:, 1]]
    nt, tt = gates.shape[1], gates.shape[3]
    row = pl.BlockSpec((tt, d), lambda i, j: (i * nt + j, 0))
    return pl.pallas_call(
        _moe_combine_kernel,
        grid=(b, nt),
        in_specs=[pl.BlockSpec((1, tt, d), lambda i, j: (i, j, 0)), row, row,
                  pl.BlockSpec((1, 1, ROUTE_ROWS, tt), lambda i, j: (i, j, 0, 0)),
                  pl.BlockSpec((1, 1, d), lambda i, j: (i, 0, 0))],
        out_specs=pl.BlockSpec((1, tt, d), lambda i, j: (i, j, 0)),
        out_shape=jax.ShapeDtypeStruct((b, t, d), F32),
        compiler_params=_cparams(("parallel", "parallel")),
        name="moe_combine",
    )(x, y0, y1, gates, mod.reshape(b, 1, d))


def _permute_w_in(w):
    d = w.shape[0]
    sizes = (512, 512, 512, 1536, 512, 16, 512, 256, 1536, 4096)
    offs = np.cumsum((0,) + sizes)
    dq, dk, dv, mlqkv, mlo, mlg, gq, gkv, hy, gate = [w[:, offs[i]:offs[i + 1]] for i in range(10)]
    pad = jnp.zeros((d, N_P - C_MLG - 16), w.dtype)
    return jnp.concatenate([gate, dq, dk, dv, mlqkv, mlo, gq, hy, gkv, mlg, pad], axis=1)


def kernel(x, c, ctx, c_ctx, w_ada, b_ada, norm1_g, norm2_g, w_in, diff_lam, diff_norm_g, ml_conv_w, ml_conv_b, ml_gate_b, ml_norm_g, gqa_qnorm_g, gqa_knorm_g, hy_conv_w, hy_conv_b, hy_f_w1, hy_f_b1, hy_f_w2, hy_f_b2, hy_f_w3, hy_f_freq, hy_bias, w_branch, w_out, moe_w_group, moe_b_group, moe_w_router, moe_b_router, moe_w_gate, moe_w_up, moe_w_down, final_norm_g):
    b, n, d = x.shape
    n_ctx = ctx.shape[1]
    depth = w_in.shape[0]
    tk = _tile(n, 2048)
    tables = _rope_tables(n)
    dft_tables = _dft_tables(n)
    sc = jax.nn.silu(c)
    scx = jax.nn.silu(c_ctx)
    xs, cs = x, ctx
    for l in range(depth):
        need_ctx = l < depth - 1
        mod_l = jnp.split(sc @ w_ada[l] + b_ada[l], 6, axis=-1)
        mod_c = [jnp.broadcast_to(m, (b, d)) for m in jnp.split(scx @ w_ada[l] + b_ada[l], 6, axis=-1)]
        w_p = _permute_w_in(w_in[l]).astype(BF16)
        hl = norm_mod(xs, norm1_g[l], mod_l[0], mod_l[1], BF16)
        hc = norm_mod(cs, norm1_g[l], mod_c[0], mod_c[1], BF16)
        p_l = matmul(hl.reshape(b * n, d), w_p, BF16, tm=1024).reshape(b, n, N_P)
        p_c = matmul(hc.reshape(b * n_ctx, d), w_p, BF16, tm=1024).reshape(b, n_ctx, N_P)

        dq_l, dkt_l, dv_l, gq_l, gkt_l, gv_l = attn_prep(p_l, tables, gqa_qnorm_g[l], gqa_knorm_g[l], tk)
        dq_c, dkt_c, dv_c, gq_c, gkt_c, gv_c = attn_prep(p_c, None, gqa_qnorm_g[l], gqa_knorm_g[l], n_ctx)
        lam_init = 0.8 - 0.6 * math.exp(-0.3 * l)
        lp = diff_lam[l].astype(F32)
        lam = jnp.exp(jnp.sum(lp[0] * lp[1])) - jnp.exp(jnp.sum(lp[2] * lp[3])) + lam_init
        yl_a = diff_attention(dq_l, dkt_c, dv_c, dkt_l, dv_l, lam, diff_norm_g[l], 1.0 - lam_init, 1024)
        yl_c = gqa_attention(gq_l, gkt_c, gv_c, gkt_l, gv_l, 1024)
        if need_ctx:
            yc_a = diff_attention(dq_c, dkt_c, dv_c, None, None, lam, diff_norm_g[l], 1.0 - lam_init, 256)
            yc_c = gqa_attention(gq_c, gkt_c, gv_c, None, None, 256)

        hs_c, hs_l = mlstm_branch(p_c, p_l, ml_conv_w[l], ml_conv_b[l], ml_gate_b[l], need_ctx)
        hy_args = (hy_conv_w[l], hy_conv_b[l], hy_f_w1[l], hy_f_b1[l], hy_f_w2[l], hy_f_b2[l],
                   hy_f_w3[l], hy_f_freq[l], hy_bias[l])
        yl_d = hyena(p_l, *hy_args, dft_tables)

        wb = w_branch[l].astype(BF16)
        wo = w_out[l].astype(BF16)
        xs = merge(yl_a, hs_l, yl_c, yl_d, p_l, ml_norm_g[l], wb, wo, xs, mod_l[2])
        route = (moe_w_group[l], moe_b_group[l], moe_w_router[l], moe_b_router[l])
        experts = (moe_w_gate[l], moe_w_up[l], moe_w_down[l])
        routed = [moe_router(xs, norm2_g[l], mod_l[3], mod_l[4], *route)]
        if need_ctx:
            yc_d = hyena(p_c, *hy_args, None)
            cs = merge(yc_a, hs_c, yc_c, yc_d, p_c, ml_norm_g[l], wb, wo, cs, mod_c[2])
            routed.insert(0, moe_router(cs, norm2_g[l], mod_c[3], mod_c[4], *route))
        h2 = jnp.concatenate([r[0].reshape(-1, d) for r in routed], axis=0)
        ids = jnp.concatenate([jnp.stack([r[1][:, :, k, :].reshape(-1) for k in range(MOE_TOP_K)], axis=1)
                               for r in routed], axis=0)
        yb, dest = _hier_moe(h2, ids, *experts)
        if need_ctx:
            nc = b * n_ctx
            cs = moe_combine(cs, yb, dest[:nc], routed[0][2], mod_c[5])
            dest = dest[nc:]
        xs = moe_combine(xs, yb, dest, routed[-1][2], mod_l[5])
    zero = jnp.zeros((b, d), F32)
    return norm_mod(xs, final_norm_g, zero, zero, F32)
```

```python
import functools
import math

import jax
import jax.numpy as jnp
import numpy as np
from jax import lax
from jax.experimental import pallas as pl
from jax.experimental.pallas import tpu as pltpu

F32 = jnp.float32
BF16 = jnp.bfloat16

EPS = 1e-6
ROPE_BASE = 10000.0
GRID_W = 64

DIFF_HEADS = 4
DIFF_DIM = 64
DIFF_VDIM = 128
ML_HEADS = 4
ML_DIM = 128
ML_CHUNK = 64
GQA_HEADS = 8
GQA_KV = 2
GQA_DIM = 64
HY_CH = 512
HY_ORDER = 2
FILTER_BANDS = 16
FILTER_SHIFT = 0.05
DECAY_TARGET = 1e-2
FAST_DECAY_PCT = 0.3
SLOW_DECAY_PCT = 1.5
N_BRANCH = 4
MOE_GROUPS = 4
MOE_EPG = 8
MOE_EXPERTS = MOE_GROUPS * MOE_EPG
MOE_TOP_K = 2
MOE_BLOCK = 256

LANES = 128
VMEM_LIMIT = 48 * 1024 * 1024

C_GATE = 0
C_DQ = 4096
C_DK = 4608
C_DV = 5120
C_MLQ = 5632
C_MLK = 6144
C_MLV = 6656
C_MLO = 7168
C_GQ = 7680
C_HY = 8192
C_GK = 9728
C_GV = 9856
C_MLG = 9984
N_P = 10240

QSCALE = (DIFF_DIM ** -0.5) * math.log2(math.e)


def _cparams(sem):
    return pltpu.CompilerParams(dimension_semantics=sem, vmem_limit_bytes=VMEM_LIMIT)


def _tile(n, target):
    if n <= target:
        return n
    for t in range(target, 7, -1):
        if n % t == 0 and t % 8 == 0:
            return t
    return n


def _norm_mod_kernel(x_ref, g_ref, sh_ref, sc_ref, o_ref):
    x = x_ref[0]
    y = x * lax.rsqrt(jnp.mean(x * x, axis=-1, keepdims=True) + EPS)
    y = y * g_ref[...]
    o_ref[0] = (y * (1.0 + sc_ref[0]) + sh_ref[0]).astype(o_ref.dtype)


def norm_mod(x, g, shift, scale, out_dtype):
    b, t, d = x.shape
    tt = _tile(t, 512)
    return pl.pallas_call(
        _norm_mod_kernel,
        grid=(b, t // tt),
        in_specs=[pl.BlockSpec((1, tt, d), lambda i, j: (i, j, 0)),
                  pl.BlockSpec((1, d), lambda i, j: (0, 0)),
                  pl.BlockSpec((1, 1, d), lambda i, j: (i, 0, 0)),
                  pl.BlockSpec((1, 1, d), lambda i, j: (i, 0, 0))],
        out_specs=pl.BlockSpec((1, tt, d), lambda i, j: (i, j, 0)),
        out_shape=jax.ShapeDtypeStruct((b, t, d), out_dtype),
        compiler_params=_cparams(("parallel", "parallel")),
        name="norm_mod",
    )(x, g.reshape(1, d), shift.reshape(b, 1, d), scale.reshape(b, 1, d))


def _mm_kernel(a_ref, w_ref, o_ref):
    o_ref[...] = jnp.dot(a_ref[...], w_ref[...], preferred_element_type=F32).astype(o_ref.dtype)


def matmul(a, w, out_dtype, tm=512, tn=1024):
    m, k = a.shape
    n = w.shape[1]
    tm = _tile(m, tm)
    tn = _tile(n, tn)
    return pl.pallas_call(
        _mm_kernel,
        grid=(m // tm, n // tn),
        in_specs=[pl.BlockSpec((tm, k), lambda i, j: (i, 0)),
                  pl.BlockSpec((k, tn), lambda i, j: (0, j))],
        out_specs=pl.BlockSpec((tm, tn), lambda i, j: (i, j)),
        out_shape=jax.ShapeDtypeStruct((m, n), out_dtype),
        compiler_params=_cparams(("parallel", "parallel")),
        name="matmul",
    )(a, w)


def _rope(x, cos, sa, sb):
    xa = pltpu.roll(x, LANES - 16, axis=1)
    xb = pltpu.roll(x, 16, axis=1)
    return x * cos + xa * sa + xb * sb


def _seg_rmsnorm(x, g):
    lane = lax.broadcasted_iota(jnp.int32, x.shape, 1)
    lo = lane < GQA_DIM
    ss = x * x
    s_lo = jnp.sum(jnp.where(lo, ss, 0.0), axis=-1, keepdims=True)
    s_hi = jnp.sum(jnp.where(lo, 0.0, ss), axis=-1, keepdims=True)
    r = jnp.where(lo, lax.rsqrt(s_lo * (1.0 / GQA_DIM) + EPS), lax.rsqrt(s_hi * (1.0 / GQA_DIM) + EPS))
    return x * r * g


def _prep_kernel(dq_ref, dk_ref, dv_ref, gq_ref, gkv_ref, cos_ref, sa_ref, sb_ref, qg_ref, kg_ref,
                 dq_o, dkt_o, dv_o, gq_o, gkt_o, gv_o, *, rope):
    if rope:
        cos, sa, sb = cos_ref[...], sa_ref[...], sb_ref[...]
        rot = lambda x: _rope(x, cos, sa, sb)
    else:
        rot = lambda x: x
    qg = qg_ref[...]
    kg = kg_ref[...]
    for j in range(DIFF_HEADS):
        sl = slice(j * LANES, (j + 1) * LANES)
        dq_o[0, :, sl] = (rot(dq_ref[0, :, sl].astype(F32)) * QSCALE).astype(BF16)
        kt = rot(dk_ref[0, :, sl].astype(F32)).T
        dkt_o[0, j, 0, 0] = kt[:DIFF_DIM].astype(BF16)
        dkt_o[0, j, 1, 0] = kt[DIFF_DIM:].astype(BF16)
        gq_o[0, :, sl] = (rot(_seg_rmsnorm(gq_ref[0, :, sl].astype(F32), qg)) * QSCALE).astype(BF16)
    dv_o[0] = dv_ref[0].astype(BF16)
    kt = rot(_seg_rmsnorm(gkv_ref[0, :, :LANES].astype(F32), kg)).T
    gkt_o[0, 0, 0] = kt[:GQA_DIM].astype(BF16)
    gkt_o[0, 1, 0] = kt[GQA_DIM:].astype(BF16)
    v = gkv_ref[0, :, LANES:].astype(BF16)
    gv_o[0, 0] = v[:, :GQA_DIM]
    gv_o[0, 1] = v[:, GQA_DIM:]


def attn_prep(p, tables, q_g, k_g, tk):
    b, t, _ = p.shape
    rope = tables is not None
    if rope:
        cos, sa, sb = tables
    else:
        cos = sa = sb = jnp.zeros((t, LANES), F32)
    nck = t // tk
    tp = _tile(tk, 512)
    sub = tk // tp
    w512 = lambda c: pl.BlockSpec((1, tp, 512), lambda i, j, c=c: (i, j, c // 512))
    tab = pl.BlockSpec((tp, LANES), lambda i, j: (j, 0))
    vec = pl.BlockSpec((1, LANES), lambda i, j: (0, 0))
    outs = pl.pallas_call(
        functools.partial(_prep_kernel, rope=rope),
        grid=(b, t // tp),
        in_specs=[w512(C_DQ), w512(C_DK), w512(C_DV), w512(C_GQ),
                  pl.BlockSpec((1, tp, 256), lambda i, j: (i, j, C_GK // 256)),
                  tab, tab, tab, vec, vec],
        out_specs=[pl.BlockSpec((1, tp, 512), lambda i, j: (i, j, 0)),
                   pl.BlockSpec((1, DIFF_HEADS, 2, 1, DIFF_DIM, tp), lambda i, j: (i, 0, 0, j // sub, 0, j % sub)),
                   pl.BlockSpec((1, tp, 512), lambda i, j: (i, j, 0)),
                   pl.BlockSpec((1, tp, 512), lambda i, j: (i, j, 0)),
                   pl.BlockSpec((1, GQA_KV, 1, GQA_DIM, tp), lambda i, j: (i, 0, j // sub, 0, j % sub)),
                   pl.BlockSpec((1, GQA_KV, tp, GQA_DIM), lambda i, j: (i, 0, j, 0))],
        out_shape=[jax.ShapeDtypeStruct((b, t, 512), BF16),
                   jax.ShapeDtypeStruct((b, DIFF_HEADS, 2, nck, DIFF_DIM, tk), BF16),
                   jax.ShapeDtypeStruct((b, t, 512), BF16),
                   jax.ShapeDtypeStruct((b, t, 512), BF16),
                   jax.ShapeDtypeStruct((b, GQA_KV, nck, GQA_DIM, tk), BF16),
                   jax.ShapeDtypeStruct((b, GQA_KV, t, GQA_DIM), BF16)],
        compiler_params=_cparams(("parallel", "parallel")),
        name="attn_prep",
    )(p, p, p, p, p, cos, sa, sb,
      jnp.tile(q_g, 2).reshape(1, LANES), jnp.tile(k_g, 2).reshape(1, LANES))
    return outs


def _flash_step(q, kt, v, m_ref, l_ref, acc_ref):
    s = jnp.dot(q, kt, preferred_element_type=F32)
    tk = s.shape[1]
    m_prev = m_ref[...]
    m_next = jnp.maximum(m_prev, jnp.max(s, axis=1, keepdims=True))
    alpha = jnp.exp2(m_prev - m_next)
    p = jnp.exp2(s - jnp.concatenate([m_next] * (tk // LANES), axis=1))
    l_ref[...] = alpha * l_ref[...] + jnp.sum(p, axis=1, keepdims=True)
    dv = acc_ref.shape[-1]
    acc_ref[...] = acc_ref[...] * alpha[:, :dv] + jnp.dot(p.astype(BF16), v, preferred_element_type=F32)
    m_ref[...] = m_next


def _attn_body(q_ref, ktc, vc, ktl, vl, m_sc, l_sc, acc_sc, *, n_lat, tk):
    m_sc[...] = jnp.full(m_sc.shape, -jnp.inf, F32)
    l_sc[...] = jnp.zeros(l_sc.shape, F32)
    acc_sc[...] = jnp.zeros(acc_sc.shape, F32)
    q = q_ref[0]
    qs = (q[:, :DIFF_DIM], q[:, DIFF_DIM:])
    for c in range(2):
        _flash_step(qs[c], ktc(c), vc(c), m_sc.at[c], l_sc.at[c], acc_sc.at[c])
    if n_lat:
        def body(i, carry):
            for c in range(2):
                _flash_step(qs[c], ktl(c, i), vl(c, i), m_sc.at[c], l_sc.at[c], acc_sc.at[c])
            return carry
        lax.fori_loop(0, n_lat, body, 0)


def _diff_attn_kernel(*refs, n_lat, tk, out_scale):
    if n_lat:
        q_ref, ktc_ref, vc_ref, ktl_ref, vl_ref, lam_ref, g_ref, o_ref, m_sc, l_sc, acc_sc = refs
        ktl = lambda c, i: ktl_ref[0, 0, c, i]
        vl = lambda c, i: vl_ref[0, pl.ds(pl.multiple_of(i * tk, tk), tk), :]
    else:
        q_ref, ktc_ref, vc_ref, lam_ref, g_ref, o_ref, m_sc, l_sc, acc_sc = refs
        ktl = vl = None
    _attn_body(q_ref, lambda c: ktc_ref[0, 0, c, 0], lambda c: vc_ref[0], ktl, vl,
               m_sc, l_sc, acc_sc, n_lat=n_lat, tk=tk)
    o0 = acc_sc[0] / l_sc[0]
    o1 = acc_sc[1] / l_sc[1]
    o = o0 - lam_ref[...] * o1
    o = o * lax.rsqrt(jnp.mean(o * o, axis=-1, keepdims=True) + EPS)
    o_ref[0] = (o * g_ref[...] * out_scale).astype(o_ref.dtype)


def diff_attention(q, ktc, vc, ktl, vl, lam, norm_g, out_scale, tq):
    b, t, _ = q.shape
    sc = vc.shape[1]
    tq = _tile(t, tq)
    n_lat, tk = (ktl.shape[3], ktl.shape[5]) if ktl is not None else (0, 0)
    in_specs = [pl.BlockSpec((1, tq, LANES), lambda i, h, j: (i, j, h)),
                pl.BlockSpec((1, 1, 2, 1, DIFF_DIM, sc), lambda i, h, j: (i, h, 0, 0, 0, 0)),
                pl.BlockSpec((1, sc, LANES), lambda i, h, j: (i, 0, h))]
    args = [q, ktc, vc]
    if n_lat:
        in_specs += [pl.BlockSpec((1, 1, 2, n_lat, DIFF_DIM, tk), lambda i, h, j: (i, h, 0, 0, 0, 0)),
                     pl.BlockSpec((1, n_lat * tk, LANES), lambda i, h, j: (i, 0, h))]
        args += [ktl, vl]
    vec = pl.BlockSpec((1, LANES), lambda i, h, j: (0, 0))
    in_specs += [vec, vec]
    args += [jnp.full((1, LANES), lam, F32), norm_g.reshape(1, LANES)]
    return pl.pallas_call(
        functools.partial(_diff_attn_kernel, n_lat=n_lat, tk=tk, out_scale=out_scale),
        grid=(b, DIFF_HEADS, t // tq),
        in_specs=in_specs,
        out_specs=pl.BlockSpec((1, tq, LANES), lambda i, h, j: (i, j, h)),
        out_shape=jax.ShapeDtypeStruct((b, t, DIFF_HEADS * DIFF_VDIM), BF16),
        scratch_shapes=[pltpu.VMEM((2, tq, LANES), F32), pltpu.VMEM((2, tq, LANES), F32),
                        pltpu.VMEM((2, tq, DIFF_VDIM), F32)],
        compiler_params=_cparams(("parallel", "parallel", "parallel")),
        name="diff_attn",
    )(*args)


def _gqa_attn_kernel(*refs, n_lat, tk):
    if n_lat:
        q_ref, ktc_ref, vc_ref, ktl_ref, vl_ref, o_ref, m_sc, l_sc, acc_sc = refs
        ktl = lambda c, i: ktl_ref[0, 0, i]
        vl = lambda c, i: vl_ref[0, 0, pl.ds(pl.multiple_of(i * tk, tk), tk), :]
    else:
        q_ref, ktc_ref, vc_ref, o_ref, m_sc, l_sc, acc_sc = refs
        ktl = vl = None
    _attn_body(q_ref, lambda c: ktc_ref[0, 0, 0], lambda c: vc_ref[0, 0], ktl, vl,
               m_sc, l_sc, acc_sc, n_lat=n_lat, tk=tk)
    o0 = acc_sc[0] / l_sc[0][:, :GQA_DIM]
    o1 = acc_sc[1] / l_sc[1][:, :GQA_DIM]
    o_ref[0] = jnp.concatenate([o0, o1], axis=-1).astype(o_ref.dtype)


def gqa_attention(q, ktc, vc, ktl, vl, tq):
    b, t, _ = q.shape
    sc = vc.shape[2]
    tq = _tile(t, tq)
    n_lat, tk = (ktl.shape[2], ktl.shape[4]) if ktl is not None else (0, 0)
    pairs = GQA_HEADS // 2
    grp = lambda h: h // (pairs // GQA_KV)
    in_specs = [pl.BlockSpec((1, tq, LANES), lambda i, h, j: (i, j, h)),
                pl.BlockSpec((1, 1, 1, GQA_DIM, sc), lambda i, h, j: (i, grp(h), 0, 0, 0)),
                pl.BlockSpec((1, 1, sc, GQA_DIM), lambda i, h, j: (i, grp(h), 0, 0))]
    args = [q, ktc, vc]
    if n_lat:
        in_specs += [pl.BlockSpec((1, 1, n_lat, GQA_DIM, tk), lambda i, h, j: (i, grp(h), 0, 0, 0)),
                     pl.BlockSpec((1, 1, n_lat * tk, GQA_DIM), lambda i, h, j: (i, grp(h), 0, 0))]
        args += [ktl, vl]
    return pl.pallas_call(
        functools.partial(_gqa_attn_kernel, n_lat=n_lat, tk=tk),
        grid=(b, pairs, t // tq),
        in_specs=in_specs,
        out_specs=pl.BlockSpec((1, tq, LANES), lambda i, h, j: (i, j, h)),
        out_shape=jax.ShapeDtypeStruct((b, t, GQA_HEADS * GQA_DIM), BF16),
        scratch_shapes=[pltpu.VMEM((2, tq, LANES), F32), pltpu.VMEM((2, tq, LANES), F32),
                        pltpu.VMEM((2, tq, GQA_DIM), F32)],
        compiler_params=_cparams(("parallel", "parallel", "parallel")),
        name="gqa_attn",
    )(*args)


ML_STEP = 256
HALO = 16


def _halo_rows(prev_ref, next_ref, j, nblk):
    prev = jnp.where(j > 0, prev_ref[0, HALO - 1:HALO, :].astype(F32), 0.0)
    nxt = jnp.where(j < nblk - 1, next_ref[0, 0:1, :].astype(F32), 0.0)
    return prev, nxt


def _conv3(x, prev, nxt, w, bias):
    tt = x.shape[0]
    row = lax.broadcasted_iota(jnp.int32, x.shape, 0)
    xm = jnp.where(row == 0, prev, pltpu.roll(x, 1, axis=0))
    xp = jnp.where(row == tt - 1, nxt, pltpu.roll(x, tt - 1, axis=0))
    return xm * w[0:1] + x * w[1:2] + xp * w[2:3] + bias


def _halo_specs(tt, width, col_block, t):
    nb = t // HALO
    prev = pl.BlockSpec((1, HALO, width), lambda i, j: (i, jnp.maximum(j * (tt // HALO) - 1, 0), col_block))
    nxt = pl.BlockSpec((1, HALO, width), lambda i, j: (i, jnp.minimum((j + 1) * (tt // HALO), nb - 1), col_block))
    return prev, nxt


def _ml_prep_kernel(q_ref, qp_ref, qn_ref, k_ref, kp_ref, kn_ref, g_ref, w_ref, b_ref, gb_ref, q_o, k_o, g_o):
    j = pl.program_id(1)
    nblk = pl.num_programs(1)
    w = ML_HEADS * ML_DIM
    qp, qn = _halo_rows(qp_ref, qn_ref, j, nblk)
    kp, kn = _halo_rows(kp_ref, kn_ref, j, nblk)
    q = _conv3(q_ref[0].astype(F32), qp, qn, w_ref[:, :w], b_ref[:, :w])
    k = _conv3(k_ref[0].astype(F32), kp, kn, w_ref[:, w:], b_ref[:, w:])
    q_o[0] = q * jax.nn.sigmoid(q)
    k_o[0] = (k * jax.nn.sigmoid(k)) * ML_DIM ** -0.5
    x = g_ref[0].astype(F32) + gb_ref[...]
    lane = lax.broadcasted_iota(jnp.int32, x.shape, 1)
    log_sig = jnp.minimum(x, 0.0) - jnp.log(1.0 + jnp.exp(-jnp.abs(x)))
    g_o[0] = jnp.where((lane % 8) >= ML_HEADS, log_sig, x)


def ml_prep(p, conv_w, conv_b, gate_b):
    b, t, _ = p.shape
    tt = _tile(t, 512)
    w = ML_HEADS * ML_DIM
    blk = lambda c: pl.BlockSpec((1, tt, w), lambda i, j, c=c: (i, j, c // w))
    qp, qn = _halo_specs(tt, w, C_MLQ // w, t)
    kp, kn = _halo_specs(tt, w, C_MLK // w, t)
    gb = jnp.zeros((1, LANES), F32).at[0, :4 * ML_HEADS].set(gate_b.reshape(-1))
    return pl.pallas_call(
        _ml_prep_kernel,
        grid=(b, t // tt),
        in_specs=[blk(C_MLQ), qp, qn, blk(C_MLK), kp, kn,
                  pl.BlockSpec((1, tt, LANES), lambda i, j: (i, j, C_MLG // LANES)),
                  pl.BlockSpec((3, 2 * w), lambda i, j: (0, 0)),
                  pl.BlockSpec((1, 2 * w), lambda i, j: (0, 0)),
                  pl.BlockSpec((1, LANES), lambda i, j: (0, 0))],
        out_specs=[pl.BlockSpec((1, tt, w), lambda i, j: (i, j, 0)),
                   pl.BlockSpec((1, tt, w), lambda i, j: (i, j, 0)),
                   pl.BlockSpec((1, tt, LANES), lambda i, j: (i, j, 0))],
        out_shape=[jax.ShapeDtypeStruct((b, t, w), F32), jax.ShapeDtypeStruct((b, t, w), F32),
                   jax.ShapeDtypeStruct((b, t, LANES), F32)],
        compiler_params=_cparams(("parallel", "parallel")),
        name="ml_prep",
    )(p, p, p, p, p, p, p, conv_w, conv_b.reshape(1, 2 * w), gb)


_NT = (((1,), (1,)), ((), ()))


def _ml_chunk_head(q, k, v, li_c, bc_c, bc_r, b_last, seen, ct_ref, n_ref, m_ref):
    m = m_ref[:, 0:1]
    ct = ct_ref[...]
    n_rows = n_ref[...]
    qb, kb, vb = q.astype(BF16), k.astype(BF16), v.astype(BF16)
    d_t = jnp.where(seen, bc_r + (li_c - bc_c), -jnp.inf)
    inter = bc_r + m
    m_t = jnp.maximum(inter, jnp.max(d_t, axis=0, keepdims=True))
    w_inter = jnp.exp(inter - m_t)
    s_t = lax.dot_general(kb, qb, _NT, preferred_element_type=F32) * jnp.exp(d_t - m_t)
    num_t = (jnp.dot(vb.T, s_t.astype(BF16), preferred_element_type=F32)
             + w_inter * lax.dot_general(ct.astype(BF16), qb, _NT, preferred_element_type=F32))
    qn = lax.dot_general(n_rows.astype(BF16), qb, _NT, preferred_element_type=F32)[0:1]
    den = jnp.sum(s_t, axis=0, keepdims=True) + w_inter * qn
    h_t = num_t / jnp.maximum(jnp.abs(den), jnp.exp(-m_t))
    g = b_last - bc_c + li_c
    m_new = jnp.maximum(b_last + m, jnp.max(g, axis=0, keepdims=True))
    kw = k * jnp.exp(g - m_new)
    wc = jnp.exp(b_last + m - m_new)
    ct_ref[...] = wc * ct + jnp.dot(vb.T, kw.astype(BF16), preferred_element_type=F32)
    n_ref[...] = wc * n_rows + jnp.sum(kw, axis=0, keepdims=True)
    m_ref[...] = jnp.broadcast_to(m_new, m_ref.shape)
    return h_t.T


def _ml_scan_kernel(q_ref, k_ref, v_ref, g_ref, c0_ref, n0_ref, m0_ref, h_o, c1_o, n1_o, m1_o,
                    c_sc, n_sc, m_sc, *, direction):
    j = pl.program_id(0)

    @pl.when(j == 0)
    def _():
        c_sc[...] = c0_ref[...]
        n_sc[...] = n0_ref[...]
        m_sc[...] = m0_ref[...]

    r = lax.broadcasted_iota(jnp.int32, (ML_CHUNK, ML_CHUNK), 0)
    s = lax.broadcasted_iota(jnp.int32, (ML_CHUNK, ML_CHUNK), 1)
    tri_f = ((s >= r) if direction else (s <= r)).astype(F32)
    seen = (r >= s) if direction else (r <= s)
    n_chunks = q_ref.shape[1] // ML_CHUNK
    order = range(n_chunks - 1, -1, -1) if direction else range(n_chunks)
    last = 0 if direction else ML_CHUNK - 1
    for c in order:
        rows = slice(c * ML_CHUNK, (c + 1) * ML_CHUNK)
        for bi in range(q_ref.shape[0]):
            gch = g_ref[bi, rows, :]
            bc = jnp.dot(tri_f, gch, preferred_element_type=F32, precision=lax.Precision.HIGHEST)
            bct = bc.T
            for hd in range(ML_HEADS):
                ci = 2 * ML_HEADS * direction + hd
                cf = ci + ML_HEADS
                cols = slice(hd * ML_DIM, (hd + 1) * ML_DIM)
                h = _ml_chunk_head(q_ref[bi, rows, cols], k_ref[bi, rows, cols], v_ref[bi, rows, cols],
                                   gch[:, ci:ci + 1], bc[:, cf:cf + 1], bct[cf:cf + 1, :],
                                   bc[last:last + 1, cf:cf + 1], seen,
                                   c_sc.at[bi, hd], n_sc.at[bi, hd], m_sc.at[bi, hd])
                h_o[bi, rows, cols] = h

    @pl.when(j == pl.num_programs(0) - 1)
    def _():
        c1_o[...] = c_sc[...]
        n1_o[...] = n_sc[...]
        m1_o[...] = m_sc[...]


def _ml_state_shapes(b):
    return [(b, ML_HEADS, ML_DIM, ML_DIM), (b, ML_HEADS, 8, ML_DIM), (b, ML_HEADS, 1, ML_DIM)]


def ml_scan(q, k, p, g, state, direction):
    b, t, w = q.shape
    ts = _tile(t, ML_STEP)
    nst = t // ts
    tok = (lambda j: (0, nst - 1 - j, 0)) if direction else (lambda j: (0, j, 0))
    tokv = (lambda j: (0, nst - 1 - j, C_MLV // w)) if direction else (lambda j: (0, j, C_MLV // w))
    st_shapes = _ml_state_shapes(b)
    st_specs = [pl.BlockSpec(s, lambda j: (0, 0, 0, 0)) for s in st_shapes]
    h, c1, n1, m1 = pl.pallas_call(
        functools.partial(_ml_scan_kernel, direction=direction),
        grid=(nst,),
        in_specs=[pl.BlockSpec((b, ts, w), tok), pl.BlockSpec((b, ts, w), tok), pl.BlockSpec((b, ts, w), tokv),
                  pl.BlockSpec((b, ts, LANES), tok)] + st_specs,
        out_specs=[pl.BlockSpec((b, ts, w), tok)] + st_specs,
        out_shape=[jax.ShapeDtypeStruct((b, t, w), F32)] + [jax.ShapeDtypeStruct(s, F32) for s in st_shapes],
        scratch_shapes=[pltpu.VMEM(s, F32) for s in st_shapes],
        compiler_params=_cparams(("arbitrary",)),
        name="ml_scan",
    )(q, k, p, g, *state)
    return h, (c1, n1, m1)


def mlstm_branch(p_c, p_l, conv_w, conv_b, gate_b, need_ctx):
    b = p_l.shape[0]
    qc, kc, gc = ml_prep(p_c, conv_w, conv_b, gate_b)
    ql, kl, gl = ml_prep(p_l, conv_w, conv_b, gate_b)
    zero = tuple(jnp.zeros(s, F32) for s in _ml_state_shapes(b))
    hs_c, hs_l = [], []
    for direction in (0, 1):
        hc, st = ml_scan(qc, kc, p_c, gc, zero, direction)
        hl, _ = ml_scan(ql, kl, p_l, gl, st, direction)
        hs_c.append(hc)
        hs_l.append(hl)
    return (tuple(hs_c) if need_ctx else None), tuple(hs_l)


def _merge_kernel(ya_ref, hf_ref, hb_ref, yc_ref, yd_ref, og_ref, g_ref, mg_ref, wb_ref, wo_ref, x_ref, gate_ref,
                  o_ref):
    d = x_ref.shape[-1]
    mg = mg_ref[...]
    yb = []
    for hd in range(ML_HEADS):
        cols = slice(hd * ML_DIM, (hd + 1) * ML_DIM)
        h = hf_ref[0, :, cols] + hb_ref[0, :, cols]
        h = h * lax.rsqrt(jnp.mean(h * h, axis=-1, keepdims=True) + EPS) * mg
        yb.append((h * jax.nn.sigmoid(og_ref[0, :, cols].astype(F32))).astype(BF16))
    ys = (ya_ref[0].astype(BF16), jnp.concatenate(yb, axis=-1), yc_ref[0].astype(BF16), yd_ref[0].astype(BF16))
    acc = None
    for n, y in enumerate(ys):
        t = jnp.dot(y, wb_ref[n], preferred_element_type=F32)
        t = jax.nn.sigmoid(g_ref[0, :, n * d:(n + 1) * d].astype(F32)) * t
        acc = t if acc is None else acc + t
    z = jnp.dot(acc.astype(BF16), wo_ref[...], preferred_element_type=F32)
    o_ref[0] = x_ref[0] + gate_ref[0] * z


def merge(ya, hs, yc, yd, p, ml_norm_g, w_branch, w_out, x, gate):
    b, t, d = x.shape
    tm = _tile(t, 512)
    w = ML_HEADS * ML_DIM
    ysp = pl.BlockSpec((1, tm, w), lambda i, j: (i, j, 0))
    return pl.pallas_call(
        _merge_kernel,
        grid=(b, t // tm),
        in_specs=[ysp, ysp, ysp, ysp, ysp,
                  pl.BlockSpec((1, tm, w), lambda i, j: (i, j, C_MLO // w)),
                  pl.BlockSpec((1, tm, N_BRANCH * d), lambda i, j: (i, j, C_GATE // (N_BRANCH * d))),
                  pl.BlockSpec((1, ML_DIM), lambda i, j: (0, 0)),
                  pl.BlockSpec((N_BRANCH, w, d), lambda i, j: (0, 0, 0)),
                  pl.BlockSpec((d, d), lambda i, j: (0, 0)),
                  pl.BlockSpec((1, tm, d), lambda i, j: (i, j, 0)),
                  pl.BlockSpec((1, 1, d), lambda i, j: (i, 0, 0))],
        out_specs=pl.BlockSpec((1, tm, d), lambda i, j: (i, j, 0)),
        out_shape=jax.ShapeDtypeStruct((b, t, d), F32),
        compiler_params=_cparams(("parallel", "parallel")),
        name="merge",
    )(ya, hs[0], hs[1], yc, yd, p, p, ml_norm_g.reshape(1, ML_DIM), w_branch, w_out, x, gate.reshape(b, 1, d))


def _expert_kernel(be_ref, x_ref, wg_ref, wu_ref, wd_ref, o_ref, wg_sc, wu_sc, wd_sc):
    i = pl.program_id(0)

    @pl.when(jnp.logical_or(i == 0, be_ref[i] != be_ref[jnp.maximum(i - 1, 0)]))
    def _():
        wg_sc[...] = wg_ref[0, 0].astype(BF16)
        wu_sc[...] = wu_ref[0, 0].astype(BF16)
        wd_sc[...] = wd_ref[0, 0].astype(BF16)

    x = x_ref[...]
    a = jnp.dot(x, wg_sc[...], preferred_element_type=F32)
    u = jnp.dot(x, wu_sc[...], preferred_element_type=F32)
    h = (a * jax.nn.sigmoid(a)) * u
    o_ref[...] = jnp.dot(h.astype(BF16), wd_sc[...], preferred_element_type=F32).astype(o_ref.dtype)


def expert_blocks(xb, blk_exp, layer, w_gate, w_up, w_down):
    m, d = xb.shape
    hdim = w_gate.shape[-1]
    n_blocks = m // MOE_BLOCK
    grid_spec = pltpu.PrefetchScalarGridSpec(
        num_scalar_prefetch=1,
        grid=(n_blocks,),
        in_specs=[pl.BlockSpec((MOE_BLOCK, d), lambda i, be: (i, 0)),
                  pl.BlockSpec((1, 1, d, hdim), lambda i, be: (layer, be[i], 0, 0)),
                  pl.BlockSpec((1, 1, d, hdim), lambda i, be: (layer, be[i], 0, 0)),
                  pl.BlockSpec((1, 1, hdim, d), lambda i, be: (layer, be[i], 0, 0))],
        out_specs=pl.BlockSpec((MOE_BLOCK, d), lambda i, be: (i, 0)),
        scratch_shapes=[pltpu.VMEM((d, hdim), BF16), pltpu.VMEM((d, hdim), BF16), pltpu.VMEM((hdim, d), BF16)],
    )
    return pl.pallas_call(
        _expert_kernel,
        grid_spec=grid_spec,
        out_shape=jax.ShapeDtypeStruct((m, d), BF16),
        compiler_params=_cparams(("arbitrary",)),
        name="moe_experts",
    )(blk_exp, xb, w_gate, w_up, w_down)


def _rope_tables(n_tok):
    rows = n_tok // GRID_W
    row = jnp.repeat(jnp.arange(rows, dtype=F32), GRID_W)
    col = jnp.broadcast_to(jnp.arange(GRID_W, dtype=F32), (rows, GRID_W)).reshape(-1)
    n_freq = DIFF_DIM // 4
    inv = ROPE_BASE ** (-jnp.arange(n_freq, dtype=F32) / n_freq)
    ar = row[:, None] * inv
    ac = col[:, None] * inv
    ang = jnp.concatenate([ar, ar, ac, ac], axis=-1)
    cos, sin = jnp.cos(ang), jnp.sin(ang)
    first = (jnp.arange(DIFF_DIM) % 32) < 16
    sa = jnp.where(first, -sin, 0.0)
    sb = jnp.where(first, 0.0, sin)
    return tuple(jnp.tile(a, (1, 2)) for a in (cos, sa, sb))


DFT_N2 = 256
DFT_J = 8
DFT_P = 4


def _hy_prep_kernel(*refs):
    ins, outs = refs[:9], refs[11:]
    w_ref, b_ref = refs[9], refs[10]
    j = pl.program_id(1)
    nblk = pl.num_programs(1)
    for n in range(HY_ORDER + 1):
        x_ref, p_ref, n_ref = ins[3 * n:3 * n + 3]
        cols = slice(n * HY_CH, (n + 1) * HY_CH)
        prev, nxt = _halo_rows(p_ref, n_ref, j, nblk)
        outs[n][0] = _conv3(x_ref[0].astype(F32), prev, nxt, w_ref[:, cols], b_ref[:, cols])


def hy_prep(p, conv_w, conv_b):
    b, t, _ = p.shape
    tt = _tile(t, 512)
    in_specs, args = [], []
    for n in range(HY_ORDER + 1):
        cb = C_HY // HY_CH + n
        prev, nxt = _halo_specs(tt, HY_CH, cb, t)
        in_specs += [pl.BlockSpec((1, tt, HY_CH), lambda i, j, cb=cb: (i, j, cb)), prev, nxt]
        args += [p, p, p]
    nch = (HY_ORDER + 1) * HY_CH
    in_specs += [pl.BlockSpec((3, nch), lambda i, j: (0, 0)), pl.BlockSpec((1, nch), lambda i, j: (0, 0))]
    osp = pl.BlockSpec((1, tt, HY_CH), lambda i, j: (i, j, 0))
    return pl.pallas_call(
        _hy_prep_kernel,
        grid=(b, t // tt),
        in_specs=in_specs,
        out_specs=[osp] * (HY_ORDER + 1),
        out_shape=[jax.ShapeDtypeStruct((b, t, HY_CH), F32)] * (HY_ORDER + 1),
        compiler_params=_cparams(("parallel", "parallel")),
        name="hy_prep",
    )(*args, conv_w, conv_b.reshape(1, nch))


def _hy_filter_kernel(emb_ref, w1_ref, b1_ref, w2_ref, b2_ref, w3_ref, fr_ref, al_ref, f_o, ss_o, *, length, half):
    j = pl.program_id(0)
    tt = emb_ref.shape[0]
    a = jnp.dot(emb_ref[...].astype(BF16), w1_ref[...].astype(BF16), preferred_element_type=F32) + b1_ref[...]
    a = jnp.sin(fr_ref[0:1, :] * a)
    a = jnp.dot(a.astype(BF16), w2_ref[...].astype(BF16), preferred_element_type=F32) + b2_ref[...]
    a = jnp.sin(fr_ref[1:2, :] * a)
    filt = jnp.dot(a.astype(BF16), w3_ref[...].astype(BF16), preferred_element_type=F32)
    r = lax.broadcasted_iota(jnp.int32, (tt, HY_CH), 0)
    if half:
        row = DFT_N2 * (r & (half - 1)) + j * (tt // half) + (r >> (half.bit_length() - 1))
    else:
        row = r + j * tt
    window = jnp.exp(-(row.astype(F32) / length) * al_ref[...]) + FILTER_SHIFT

    @pl.when(j == 0)
    def _():
        ss_o[...] = jnp.zeros(ss_o.shape, F32)

    for o in range(HY_ORDER):
        for d in range(2):
            idx = 2 * o + d
            f = filt[:, idx * HY_CH:(idx + 1) * HY_CH] * window
            if d == 1:
                f = jnp.where(row == 0, 0.0, f)
            if half:
                for jj in range(tt // half):
                    f_o[idx, :, jj * HY_CH:(jj + 1) * HY_CH] = f[jj * half:(jj + 1) * half]
            else:
                f_o[idx] = f
            ss_o[o:o + 1, :] += jnp.sum(f * f, axis=0, keepdims=True)


def hy_filters(length, w1, b1, w2, b2, w3, freq, wide):
    t = jnp.arange(length, dtype=F32) / length
    bands = jnp.arange(1, FILTER_BANDS + 1, dtype=F32)
    ang = 2.0 * math.pi * t[:, None] * bands
    emb = jnp.concatenate([t[:, None], jnp.cos(ang), jnp.sin(ang)], axis=-1)
    pad = LANES - emb.shape[1]
    emb = jnp.pad(emb, ((0, 0), (0, pad)))
    w1 = jnp.pad(w1, ((0, pad), (0, 0)))
    ne, nh = emb.shape[1], w1.shape[1]
    alpha = jnp.linspace(abs(math.log(DECAY_TARGET)) / SLOW_DECAY_PCT,
                         abs(math.log(DECAY_TARGET)) / FAST_DECAY_PCT, HY_CH).reshape(1, HY_CH)
    tt = _tile(length, 512)
    half = length // DFT_N2
    wide = wide and half > 0 and half & (half - 1) == 0 and tt % half == 0
    full = lambda shape: pl.BlockSpec(shape, lambda j: (0,) * len(shape))
    if wide:
        emb = jnp.swapaxes(emb.reshape(half, DFT_N2, ne), 0, 1).reshape(length, ne)
        f_spec = pl.BlockSpec((2 * HY_ORDER, half, (tt // half) * HY_CH), lambda j: (0, 0, j))
        f_shape = (2 * HY_ORDER, half, DFT_N2 * HY_CH)
    else:
        f_spec = pl.BlockSpec((2 * HY_ORDER, tt, HY_CH), lambda j: (0, j, 0))
        f_shape = (2 * HY_ORDER, length, HY_CH)
    f, ss = pl.pallas_call(
        functools.partial(_hy_filter_kernel, length=length, half=half if wide else 0),
        grid=(length // tt,),
        in_specs=[pl.BlockSpec((tt, ne), lambda j: (j, 0)), full((ne, nh)), full((1, nh)), full((nh, nh)),
                  full((1, nh)), full((nh, 2 * HY_ORDER * HY_CH)), full((2, nh)), full((1, HY_CH))],
        out_specs=[f_spec, full((HY_ORDER, HY_CH))],
        out_shape=[jax.ShapeDtypeStruct(f_shape, F32), jax.ShapeDtypeStruct((HY_ORDER, HY_CH), F32)],
        compiler_params=_cparams(("arbitrary",)),
        name="hy_filter",
    )(emb, w1, b1.reshape(1, nh), w2, b2.reshape(1, nh), w3, freq, alpha)
    return f, lax.rsqrt(ss + EPS)


def _dft_tables(length):
    n = 2 * length
    n1 = n // DFT_N2
    half = n1 // 2
    n1h = -(-(half + 1) // 16) * 16
    kv = jnp.arange(n1h, dtype=jnp.int32)
    valid = (kv <= half).astype(F32)[None, :, None]
    pair = jnp.where((kv == 0) | (kv == half), 1.0, 2.0)[None, :, None] * valid
    k1 = kv[None, :, None]
    tn = (DFT_N2 * jnp.arange(half, dtype=jnp.int32)[None, None, :]
          + jnp.arange(DFT_N2, dtype=jnp.int32)[:, None, None])
    th = (2.0 * math.pi / n) * ((k1 * tn) % n).astype(F32)
    ga = jnp.concatenate([jnp.cos(th) * valid, -jnp.sin(th) * valid], axis=1).astype(BF16)
    gi = jnp.swapaxes(jnp.concatenate([jnp.cos(th) * pair, -jnp.sin(th) * pair], axis=1), 1, 2).astype(BF16)
    kk = jnp.arange(DFT_N2, dtype=jnp.int32)
    t2 = (2.0 * math.pi / DFT_N2) * ((kk[:, None] * kk[None, :]) % DFT_N2).astype(F32)
    c2, s2 = jnp.cos(t2), jnp.sin(t2)
    mf = jnp.block([[c2, s2], [-s2, c2]]).astype(BF16)
    mi = jnp.block([[c2, -s2], [s2, c2]]).astype(BF16)
    return ga, gi, mf, mi


def _dft_a_kernel(z_ref, g_ref, o_ref):
    c = HY_CH
    for j in range(DFT_J):
        slab = z_ref[0, :, j * c:(j + 1) * c].astype(BF16)
        r = jnp.dot(g_ref[j], slab, preferred_element_type=F32)
        o_ref[0, :, :, j * c:(j + 1) * c] = r.reshape(2, r.shape[0] // 2, c).astype(o_ref.dtype)


def dft_a(z, ga):
    bz, half, wid = z.shape
    n1 = ga.shape[1] // 2
    jc = DFT_J * HY_CH
    return pl.pallas_call(
        _dft_a_kernel,
        grid=(bz, DFT_N2 // DFT_J),
        in_specs=[pl.BlockSpec((1, half, jc), lambda i, j: (i, 0, j)),
                  pl.BlockSpec((DFT_J, 2 * n1, half), lambda i, j: (j, 0, 0))],
        out_specs=pl.BlockSpec((1, 2, n1, jc), lambda i, j: (i, 0, 0, j)),
        out_shape=jax.ShapeDtypeStruct((bz, 2, n1, wid), BF16),
        compiler_params=_cparams(("parallel", "parallel")),
        name="dft_a",
    )(z, ga)


def _stack_ri(ref, b, k):
    return jnp.concatenate([ref[b, 0, k], ref[b, 1, k]], axis=0)


def _spec_filter_kernel(f_ref, mf_ref, sc_ref, h_o):
    sc = sc_ref[0]
    for k in range(DFT_P):
        xf = jnp.dot(mf_ref[...], _stack_ri(f_ref, 0, k), preferred_element_type=F32)
        xb = jnp.dot(mf_ref[...], _stack_ri(f_ref, 1, k), preferred_element_type=F32)
        h_o[0, k, 0] = ((xf[:DFT_N2] + xb[:DFT_N2]) * sc).astype(h_o.dtype)
        h_o[0, k, 1] = ((xf[DFT_N2:] - xb[DFT_N2:]) * sc).astype(h_o.dtype)


def spec_filter(fa, mf, scale):
    nb, _, n1, _, c = fa.shape
    order = nb // 2
    return pl.pallas_call(
        _spec_filter_kernel,
        grid=(order, n1 // DFT_P),
        in_specs=[pl.BlockSpec((2, 2, DFT_P, DFT_N2, c), lambda o, k: (o, 0, k, 0, 0)),
                  pl.BlockSpec((2 * DFT_N2, 2 * DFT_N2), lambda o, k: (0, 0)),
                  pl.BlockSpec((1, 1, c), lambda o, k: (o, 0, 0))],
        out_specs=pl.BlockSpec((1, DFT_P, 2, DFT_N2, c), lambda o, k: (o, k, 0, 0, 0)),
        out_shape=jax.ShapeDtypeStruct((order, n1, 2, DFT_N2, c), BF16),
        compiler_params=_cparams(("parallel", "parallel")),
        name="spec_filter",
    )(fa, mf, scale.reshape(order, 1, c))


def _spec_conv_kernel(a_ref, h_ref, mf_ref, mi_ref, o_ref):
    for k in range(DFT_P):
        x = jnp.dot(mf_ref[...], _stack_ri(a_ref, 0, k), preferred_element_type=F32)
        xr, xi = x[:DFT_N2], x[DFT_N2:]
        hr, hi = h_ref[0, k, 0].astype(F32), h_ref[0, k, 1].astype(F32)
        y = jnp.concatenate([xr * hr - xi * hi, xr * hi + xi * hr], axis=0).astype(BF16)
        z = jnp.dot(mi_ref[...], y, preferred_element_type=F32)
        o_ref[0, 0, k] = z[:DFT_N2].astype(o_ref.dtype)
        o_ref[0, 1, k] = z[DFT_N2:].astype(o_ref.dtype)


def spec_conv(a, h, order, mf, mi):
    b, _, n1, _, c = a.shape
    blk = pl.BlockSpec((1, 2, DFT_P, DFT_N2, c), lambda i, k: (i, 0, k, 0, 0))
    mat = pl.BlockSpec((2 * DFT_N2, 2 * DFT_N2), lambda i, k: (0, 0))
    return pl.pallas_call(
        _spec_conv_kernel,
        grid=(b, n1 // DFT_P),
        in_specs=[blk, pl.BlockSpec((1, DFT_P, 2, DFT_N2, c), lambda i, k: (order, k, 0, 0, 0)), mat, mat],
        out_specs=blk,
        out_shape=jax.ShapeDtypeStruct(a.shape, BF16),
        compiler_params=_cparams(("parallel", "parallel")),
        name="spec_conv",
    )(a, h, mf, mi)


def _dft_ainv_kernel(z_ref, g_ref, xg_ref, zin_ref, bias_ref, o_ref):
    c = HY_CH
    n1 = z_ref.shape[2]
    for j in range(DFT_J):
        cols = slice(j * c, (j + 1) * c)
        zz = z_ref[0, :, :, cols].reshape(2 * n1, c)
        y = jnp.dot(g_ref[j], zz, preferred_element_type=F32)
        o_ref[0, :, cols] = xg_ref[0, :, cols] * (y + bias_ref[...] * zin_ref[0, :, cols])


def dft_ainv(z, gi, xg, zin, bias):
    b, _, n1, wid = z.shape
    half = gi.shape[1]
    jc = DFT_J * HY_CH
    tok = pl.BlockSpec((1, half, jc), lambda i, j: (i, 0, j))
    return pl.pallas_call(
        _dft_ainv_kernel,
        grid=(b, DFT_N2 // DFT_J),
        in_specs=[pl.BlockSpec((1, 2, n1, jc), lambda i, j: (i, 0, 0, j)),
                  pl.BlockSpec((DFT_J, half, 2 * n1), lambda i, j: (j, 0, 0)),
                  tok, tok, pl.BlockSpec((1, HY_CH), lambda i, j: (0, 0))],
        out_specs=tok,
        out_shape=jax.ShapeDtypeStruct((b, half, wid), F32),
        compiler_params=_cparams(("parallel", "parallel")),
        name="dft_ainv",
    )(z, gi, xg, zin, bias.reshape(1, HY_CH))


def _ctx_conv_kernel(z_ref, xg_ref, f_ref, mf_ref, mi_ref, sc_ref, bias_ref, o_ref):
    nf = mf_ref.shape[0] // 2
    mf = mf_ref[...]
    xf = jnp.dot(mf, f_ref[0].astype(BF16), preferred_element_type=F32)
    xb = jnp.dot(mf, f_ref[1].astype(BF16), preferred_element_type=F32)
    sc = sc_ref[0]
    hr = (xf[:nf] + xb[:nf]) * sc
    hi = (xf[nf:] - xb[nf:]) * sc
    z = z_ref[0]
    x = jnp.dot(mf, z.astype(BF16), preferred_element_type=F32)
    xr, xi = x[:nf], x[nf:]
    y = jnp.concatenate([xr * hr - xi * hi, xr * hi + xi * hr], axis=0).astype(BF16)
    o_ref[0] = xg_ref[0] * (jnp.dot(mi_ref[...], y, preferred_element_type=F32) + bias_ref[...] * z)


def ctx_conv(z, xg, f, order, scale, bias):
    b, length, c = z.shape
    n = 2 * length
    kk = jnp.arange(n, dtype=jnp.int32)[:, None]
    tn = jnp.arange(length, dtype=jnp.int32)[None, :]
    th = (2.0 * math.pi / n) * ((kk * tn) % n).astype(F32)
    mf = jnp.concatenate([jnp.cos(th), -jnp.sin(th)], axis=0).astype(BF16)
    mi = jnp.concatenate([jnp.cos(th.T), -jnp.sin(th.T)], axis=1).astype(BF16)
    tok = pl.BlockSpec((1, length, c), lambda i: (i, 0, 0))
    return pl.pallas_call(
        _ctx_conv_kernel,
        grid=(b,),
        in_specs=[tok, tok, pl.BlockSpec((2, length, c), lambda i: (order, 0, 0)),
                  pl.BlockSpec((2 * n, length), lambda i: (0, 0)), pl.BlockSpec((length, 2 * n), lambda i: (0, 0)),
                  pl.BlockSpec((1, 1, c), lambda i: (order, 0, 0)), pl.BlockSpec((1, c), lambda i: (0, 0))],
        out_specs=tok,
        out_shape=jax.ShapeDtypeStruct((b, length, c), F32),
        compiler_params=_cparams(("parallel",)),
        name="ctx_conv",
    )(z, xg, f, mf, mi, scale.reshape(-1, 1, c), bias.reshape(1, c))


def hyena(p, conv_w, conv_b, w1, b1, w2, b2, w3, freq, bias, tables):
    b, length, _ = p.shape
    parts = hy_prep(p, conv_w, conv_b)
    f, rnorm = hy_filters(length, w1, b1, w2, b2, w3, freq, wide=tables is not None)
    scale = rnorm / (2 * length)
    z = parts[0]
    if tables is None:
        for o in range(HY_ORDER):
            z = ctx_conv(z, parts[o + 1], f, o, scale, bias[o])
        return z
    ga, gi, mf, mi = tables
    half = ga.shape[2]
    wid = DFT_N2 * HY_CH
    fa = dft_a(f.reshape(2 * HY_ORDER, half, wid), ga)
    n1 = fa.shape[2]
    h = spec_filter(fa.reshape(2 * HY_ORDER, 2, n1, DFT_N2, HY_CH), mf, scale)
    for o in range(HY_ORDER):
        a = dft_a(z.reshape(b, half, wid), ga).reshape(b, 2, n1, DFT_N2, HY_CH)
        zc = spec_conv(a, h, o, mf, mi).reshape(b, 2, n1, wid)
        z = dft_ainv(zc, gi, parts[o + 1].reshape(b, half, wid), z.reshape(b, half, wid), bias[o])
        z = z.reshape(b, length, HY_CH)
    return z


ROUTE_ROWS = 8


def _router_kernel(x_ref, g_ref, sh_ref, sc_ref, w_ref, b_ref, h_o, id_o, gate_o):
    x = x_ref[0]
    y = x * lax.rsqrt(jnp.mean(x * x, axis=-1, keepdims=True) + EPS)
    y = ((y * g_ref[...]) * (1.0 + sc_ref[0]) + sh_ref[0]).astype(BF16)
    h_o[0] = y
    logits = jnp.dot(y, w_ref[...], preferred_element_type=F32) + b_ref[...]
    lane = lax.broadcasted_iota(jnp.int32, logits.shape, 1)
    lane_f = lane.astype(F32)
    none = float(LANES)

    def top(vals):
        v = jnp.max(vals, axis=1, keepdims=True)
        return v, jnp.min(jnp.where(vals == v, lane_f, none), axis=1, keepdims=True)

    is_grp = lane < MOE_GROUPS
    mg, grp = top(jnp.where(is_grp, logits, -jnp.inf))
    p_grp = 1.0 / jnp.sum(jnp.where(is_grp, jnp.exp(logits - mg), 0.0), axis=1, keepdims=True)
    lo = MOE_GROUPS + MOE_EPG * grp
    el = jnp.where((lane_f >= lo) & (lane_f < lo + MOE_EPG), logits, -jnp.inf)
    v1, i1 = top(el)
    v2, i2 = top(jnp.where(lane_f == i1, -jnp.inf, el))
    t = jnp.exp(v2 - v1)
    g1 = p_grp / (1.0 + t)
    ids = jnp.where(lane == 0, i1 - MOE_GROUPS, jnp.where(lane == 1, i2 - MOE_GROUPS, 0.0))
    id_o[0, 0] = ids.T[:ROUTE_ROWS].astype(jnp.int32)
    gate_o[0, 0] = jnp.where(lane == 0, g1, jnp.where(lane == 1, g1 * t, 0.0)).T[:ROUTE_ROWS]


def moe_router(x, g, shift, scale, w_group, b_group, w_router, b_router):
    b, t, d = x.shape
    tt = _tile(t, 512)
    rows = pl.BlockSpec((1, 1, ROUTE_ROWS, tt), lambda i, j: (i, j, 0, 0))
    npad = LANES - MOE_GROUPS - MOE_EXPERTS
    w = jnp.concatenate([w_group, w_router, jnp.zeros((d, npad), F32)], axis=1).astype(BF16)
    bias = jnp.concatenate([b_group, b_router, jnp.zeros((npad,), F32)]).reshape(1, LANES)
    tok = lambda width: pl.BlockSpec((1, tt, width), lambda i, j: (i, j, 0))
    mod = pl.BlockSpec((1, 1, d), lambda i, j: (i, 0, 0))
    return pl.pallas_call(
        _router_kernel,
        grid=(b, t // tt),
        in_specs=[tok(d), pl.BlockSpec((1, d), lambda i, j: (0, 0)), mod, mod,
                  pl.BlockSpec((d, LANES), lambda i, j: (0, 0)), pl.BlockSpec((1, LANES), lambda i, j: (0, 0))],
        out_specs=[tok(d), rows, rows],
        out_shape=[jax.ShapeDtypeStruct((b, t, d), BF16),
                   jax.ShapeDtypeStruct((b, t // tt, ROUTE_ROWS, tt), jnp.int32),
                   jax.ShapeDtypeStruct((b, t // tt, ROUTE_ROWS, tt), F32)],
        compiler_params=_cparams(("parallel", "parallel")),
        name="moe_router",
    )(x, g.reshape(1, d), shift.reshape(b, 1, d), scale.reshape(b, 1, d), w, bias)


def _blocked_cumsum(onehot, blk=256):
    m, e = onehot.shape
    if m % blk:
        return jnp.cumsum(onehot, axis=0)
    oh = onehot.astype(BF16).reshape(m // blk, blk, e)
    tril = jnp.tril(jnp.ones((blk, blk), BF16))
    within = jnp.einsum('ts,bse->bte', tril, oh, preferred_element_type=F32)
    tot = within[:, -1, :]
    off = jnp.cumsum(tot, axis=0) - tot
    return (within + off[:, None, :]).reshape(m, e).astype(jnp.int32)


def _hier_moe(h, ids, layer, w_gate, w_up, w_down):
    n_tok, d = h.shape
    e_flat = ids.reshape(-1)
    m_slots = n_tok * MOE_TOP_K
    onehot = (e_flat[:, None] == jnp.arange(MOE_EXPERTS, dtype=jnp.int32)[None, :]).astype(jnp.int32)
    csum = _blocked_cumsum(onehot)
    rank = jnp.sum(onehot * csum, axis=1) - 1
    counts = csum[-1]
    padded = (counts + MOE_BLOCK - 1) // MOE_BLOCK * MOE_BLOCK
    p_end = jnp.cumsum(padded)
    dest = (p_end - padded)[e_flat] + rank
    n_blocks = -(-(m_slots + MOE_EXPERTS * (MOE_BLOCK - 1)) // MOE_BLOCK)
    slot_tok = jnp.arange(m_slots, dtype=jnp.int32) // MOE_TOP_K
    buf_tok = jnp.zeros((n_blocks * MOE_BLOCK,), jnp.int32).at[dest].set(slot_tok)
    first_row = jnp.arange(n_blocks, dtype=jnp.int32)[:, None] * MOE_BLOCK
    blk_exp = jnp.minimum(jnp.sum((p_end[None, :] <= first_row).astype(jnp.int32), axis=1), MOE_EXPERTS - 1)
    yb = expert_blocks(h[buf_tok], blk_exp, layer, w_gate, w_up, w_down)
    return yb, dest.reshape(n_tok, MOE_TOP_K)


def _moe_combine_kernel(x_ref, y0_ref, y1_ref, g_ref, m_ref, o_ref):
    g = g_ref[0, 0].T
    f = g[:, 0:1] * y0_ref[...].astype(F32) + g[:, 1:2] * y1_ref[...].astype(F32)
    o_ref[0] = x_ref[0] + m_ref[0] * f


def moe_combine(x, yb, dest, gates, mod):
    b, t, d = x.shape
    y0 = yb[dest[:, 0]]
    y1 = yb[dest[:, 1]]
    nt, tt = gates.shape[1], gates.shape[3]
    row = pl.BlockSpec((tt, d), lambda i, j: (i * nt + j, 0))
    return pl.pallas_call(
        _moe_combine_kernel,
        grid=(b, nt),
        in_specs=[pl.BlockSpec((1, tt, d), lambda i, j: (i, j, 0)), row, row,
                  pl.BlockSpec((1, 1, ROUTE_ROWS, tt), lambda i, j: (i, j, 0, 0)),
                  pl.BlockSpec((1, 1, d), lambda i, j: (i, 0, 0))],
        out_specs=pl.BlockSpec((1, tt, d), lambda i, j: (i, j, 0)),
        out_shape=jax.ShapeDtypeStruct((b, t, d), F32),
        compiler_params=_cparams(("parallel", "parallel")),
        name="moe_combine",
    )(x, y0, y1, gates, mod.reshape(b, 1, d))


def _permute_w_in(w):
    d = w.shape[0]
    sizes = (512, 512, 512, 1536, 512, 16, 512, 256, 1536, 4096)
    offs = np.cumsum((0,) + sizes)
    dq, dk, dv, mlqkv, mlo, mlg, gq, gkv, hy, gate = [w[:, offs[i]:offs[i + 1]] for i in range(10)]
    pad = jnp.zeros((d, N_P - C_MLG - 16), w.dtype)
    return jnp.concatenate([gate, dq, dk, dv, mlqkv, mlo, gq, hy, gkv, mlg, pad], axis=1)


def kernel(x, c, ctx, c_ctx, w_ada, b_ada, norm1_g, norm2_g, w_in, diff_lam, diff_norm_g, ml_conv_w, ml_conv_b, ml_gate_b, ml_norm_g, gqa_qnorm_g, gqa_knorm_g, hy_conv_w, hy_conv_b, hy_f_w1, hy_f_b1, hy_f_w2, hy_f_b2, hy_f_w3, hy_f_freq, hy_bias, w_branch, w_out, moe_w_group, moe_b_group, moe_w_router, moe_b_router, moe_w_gate, moe_w_up, moe_w_down, final_norm_g):
    b, n, d = x.shape
    n_ctx = ctx.shape[1]
    depth = w_in.shape[0]
    tk = _tile(n, 2048)
    tables = _rope_tables(n)
    dft_tables = _dft_tables(n)
    sc = jax.nn.silu(c)
    scx = jax.nn.silu(c_ctx)
    xs, cs = x, ctx
    for l in range(depth):
        need_ctx = l < depth - 1
        mod_l = jnp.split(sc @ w_ada[l] + b_ada[l], 6, axis=-1)
        mod_c = [jnp.broadcast_to(m, (b, d)) for m in jnp.split(scx @ w_ada[l] + b_ada[l], 6, axis=-1)]
        w_p = _permute_w_in(w_in[l]).astype(BF16)
        hl = norm_mod(xs, norm1_g[l], mod_l[0], mod_l[1], BF16)
        hc = norm_mod(cs, norm1_g[l], mod_c[0], mod_c[1], BF16)
        p_l = matmul(hl.reshape(b * n, d), w_p, BF16, tm=1024).reshape(b, n, N_P)
        p_c = matmul(hc.reshape(b * n_ctx, d), w_p, BF16, tm=1024).reshape(b, n_ctx, N_P)

        dq_l, dkt_l, dv_l, gq_l, gkt_l, gv_l = attn_prep(p_l, tables, gqa_qnorm_g[l], gqa_knorm_g[l], tk)
        dq_c, dkt_c, dv_c, gq_c, gkt_c, gv_c = attn_prep(p_c, None, gqa_qnorm_g[l], gqa_knorm_g[l], n_ctx)
        lam_init = 0.8 - 0.6 * math.exp(-0.3 * l)
        lp = diff_lam[l].astype(F32)
        lam = jnp.exp(jnp.sum(lp[0] * lp[1])) - jnp.exp(jnp.sum(lp[2] * lp[3])) + lam_init
        yl_a = diff_attention(dq_l, dkt_c, dv_c, dkt_l, dv_l, lam, diff_norm_g[l], 1.0 - lam_init, 1024)
        yl_c = gqa_attention(gq_l, gkt_c, gv_c, gkt_l, gv_l, 1024)
        if need_ctx:
            yc_a = diff_attention(dq_c, dkt_c, dv_c, None, None, lam, diff_norm_g[l], 1.0 - lam_init, 256)
            yc_c = gqa_attention(gq_c, gkt_c, gv_c, None, None, 256)

        hs_c, hs_l = mlstm_branch(p_c, p_l, ml_conv_w[l], ml_conv_b[l], ml_gate_b[l], need_ctx)
        hy_args = (hy_conv_w[l], hy_conv_b[l], hy_f_w1[l], hy_f_b1[l], hy_f_w2[l], hy_f_b2[l],
                   hy_f_w3[l], hy_f_freq[l], hy_bias[l])
        yl_d = hyena(p_l, *hy_args, dft_tables)

        wb = w_branch[l].astype(BF16)
        wo = w_out[l].astype(BF16)
        xs = merge(yl_a, hs_l, yl_c, yl_d, p_l, ml_norm_g[l], wb, wo, xs, mod_l[2])
        route = (moe_w_group[l], moe_b_group[l], moe_w_router[l], moe_b_router[l])
        experts = (l, moe_w_gate, moe_w_up, moe_w_down)
        routed = [moe_router(xs, norm2_g[l], mod_l[3], mod_l[4], *route)]
        if need_ctx:
            yc_d = hyena(p_c, *hy_args, None)
            cs = merge(yc_a, hs_c, yc_c, yc_d, p_c, ml_norm_g[l], wb, wo, cs, mod_c[2])
            routed.insert(0, moe_router(cs, norm2_g[l], mod_c[3], mod_c[4], *route))
        h2 = jnp.concatenate([r[0].reshape(-1, d) for r in routed], axis=0)
        ids = jnp.concatenate([jnp.stack([r[1][:, :, k, :].reshape(-1) for k in range(MOE_TOP_K)], axis=1)
                               for r in routed], axis=0)
        yb, dest = _hier_moe(h2, ids, *experts)
        if need_ctx:
            nc = b * n_ctx
            cs = moe_combine(cs, yb, dest[:nc], routed[0][2], mod_c[5])
            dest = dest[nc:]
        xs = moe_combine(xs, yb, dest, routed[-1][2], mod_l[5])
    zero = jnp.zeros((b, d), F32)
    return norm_mod(xs, final_norm_g, zero, zero, F32)
```

```python
import functools
import math

import jax
import jax.numpy as jnp
import numpy as np
from jax import lax
from jax.experimental import pallas as pl
from jax.experimental.pallas import tpu as pltpu

F32 = jnp.float32
BF16 = jnp.bfloat16

EPS = 1e-6
ROPE_BASE = 10000.0
GRID_W = 64

DIFF_HEADS = 4
DIFF_DIM = 64
DIFF_VDIM = 128
ML_HEADS = 4
ML_DIM = 128
ML_CHUNK = 64
GQA_HEADS = 8
GQA_KV = 2
GQA_DIM = 64
HY_CH = 512
HY_ORDER = 2
FILTER_BANDS = 16
FILTER_SHIFT = 0.05
DECAY_TARGET = 1e-2
FAST_DECAY_PCT = 0.3
SLOW_DECAY_PCT = 1.5
N_BRANCH = 4
MOE_GROUPS = 4
MOE_EPG = 8
MOE_EXPERTS = MOE_GROUPS * MOE_EPG
MOE_TOP_K = 2
MOE_BLOCK = 256

LANES = 128
VMEM_LIMIT = 48 * 1024 * 1024
ROW_TILE = 512
GROUP_W = 512

C_GATE = 0
C_DQ = 4096
C_DK = 4608
C_DV = 5120
C_MLQ = 5632
C_MLK = 6144
C_MLV = 6656
C_MLO = 7168
C_GQ = 7680
C_HY = 8192
C_GK = 9728
C_GV = 9856
C_MLG = 9984
N_P = 10240

QSCALE = (DIFF_DIM ** -0.5) * math.log2(math.e)


def _cparams(sem):
    return pltpu.CompilerParams(dimension_semantics=sem, vmem_limit_bytes=VMEM_LIMIT)


def _tile(n, target):
    if n <= target:
        return n
    for t in range(target, 7, -1):
        if n % t == 0 and t % 8 == 0:
            return t
    return n


def _norm_mod_kernel(x_ref, g_ref, sh_ref, sc_ref, o_ref):
    x = x_ref[0]
    y = x * lax.rsqrt(jnp.mean(x * x, axis=-1, keepdims=True) + EPS)
    y = y * g_ref[...]
    o_ref[0] = (y * (1.0 + sc_ref[0]) + sh_ref[0]).astype(o_ref.dtype)


def norm_mod(x, g, shift, scale, out_dtype):
    b, t, d = x.shape
    tt = _tile(t, ROW_TILE)
    return pl.pallas_call(
        _norm_mod_kernel,
        grid=(b, t // tt),
        in_specs=[pl.BlockSpec((1, tt, d), lambda i, j: (i, j, 0)),
                  pl.BlockSpec((1, d), lambda i, j: (0, 0)),
                  pl.BlockSpec((1, 1, d), lambda i, j: (i, 0, 0)),
                  pl.BlockSpec((1, 1, d), lambda i, j: (i, 0, 0))],
        out_specs=pl.BlockSpec((1, tt, d), lambda i, j: (i, j, 0)),
        out_shape=jax.ShapeDtypeStruct((b, t, d), out_dtype),
        compiler_params=_cparams(("parallel", "parallel")),
        name="norm_mod",
    )(x, g.reshape(1, d), shift.reshape(b, 1, d), scale.reshape(b, 1, d))


def _mm_kernel(a_ref, w_ref, o_ref):
    o_ref[...] = jnp.dot(a_ref[...], w_ref[...], preferred_element_type=F32).astype(o_ref.dtype)


def matmul(a, w, out_dtype, tm=512, tn=1024):
    m, k = a.shape
    n = w.shape[1]
    tm = _tile(m, tm)
    tn = _tile(n, tn)
    return pl.pallas_call(
        _mm_kernel,
        grid=(m // tm, n // tn),
        in_specs=[pl.BlockSpec((tm, k), lambda i, j: (i, 0)),
                  pl.BlockSpec((k, tn), lambda i, j: (0, j))],
        out_specs=pl.BlockSpec((tm, tn), lambda i, j: (i, j)),
        out_shape=jax.ShapeDtypeStruct((m, n), out_dtype),
        compiler_params=_cparams(("parallel", "parallel")),
        name="matmul",
    )(a, w)


def _rope(x, cos, sa, sb):
    xa = pltpu.roll(x, LANES - 16, axis=1)
    xb = pltpu.roll(x, 16, axis=1)
    return x * cos + xa * sa + xb * sb


def _seg_rmsnorm(x, g):
    lane = lax.broadcasted_iota(jnp.int32, x.shape, 1)
    lo = lane < GQA_DIM
    ss = x * x
    s_lo = jnp.sum(jnp.where(lo, ss, 0.0), axis=-1, keepdims=True)
    s_hi = jnp.sum(jnp.where(lo, 0.0, ss), axis=-1, keepdims=True)
    r = jnp.where(lo, lax.rsqrt(s_lo * (1.0 / GQA_DIM) + EPS), lax.rsqrt(s_hi * (1.0 / GQA_DIM) + EPS))
    return x * r * g


def _prep_kernel(dq_ref, dk_ref, dv_ref, gq_ref, gkv_ref, cos_ref, sa_ref, sb_ref, qg_ref, kg_ref,
                 dq_o, dkt_o, dv_o, gq_o, gkt_o, gv_o, *, rope):
    if rope:
        cos, sa, sb = cos_ref[...], sa_ref[...], sb_ref[...]
        rot = lambda x: _rope(x, cos, sa, sb)
    else:
        rot = lambda x: x
    qg = qg_ref[...]
    kg = kg_ref[...]
    for j in range(DIFF_HEADS):
        sl = slice(j * LANES, (j + 1) * LANES)
        dq_o[0, :, sl] = (rot(dq_ref[0, :, sl].astype(F32)) * QSCALE).astype(BF16)
        kt = rot(dk_ref[0, :, sl].astype(F32)).T
        dkt_o[0, j, 0, 0] = kt[:DIFF_DIM].astype(BF16)
        dkt_o[0, j, 1, 0] = kt[DIFF_DIM:].astype(BF16)
        gq_o[0, :, sl] = (rot(_seg_rmsnorm(gq_ref[0, :, sl].astype(F32), qg)) * QSCALE).astype(BF16)
    dv_o[0] = dv_ref[0].astype(BF16)
    kt = rot(_seg_rmsnorm(gkv_ref[0, :, :LANES].astype(F32), kg)).T
    gkt_o[0, 0, 0] = kt[:GQA_DIM].astype(BF16)
    gkt_o[0, 1, 0] = kt[GQA_DIM:].astype(BF16)
    v = gkv_ref[0, :, LANES:].astype(BF16)
    gv_o[0, 0] = v[:, :GQA_DIM]
    gv_o[0, 1] = v[:, GQA_DIM:]


def attn_prep(p, tables, q_g, k_g, tk):
    b, t, _ = p.shape
    rope = tables is not None
    if rope:
        cos, sa, sb = tables
    else:
        cos = sa = sb = jnp.zeros((t, LANES), F32)
    nck = t // tk
    tp = _tile(tk, ROW_TILE)
    sub = tk // tp
    gw = GROUP_W
    kvw = 2 * GQA_KV * GQA_DIM
    grp = lambda c: pl.BlockSpec((1, tp, gw), lambda i, j, c=c: (i, j, c // gw))
    tab = pl.BlockSpec((tp, LANES), lambda i, j: (j, 0))
    vec = pl.BlockSpec((1, LANES), lambda i, j: (0, 0))
    outs = pl.pallas_call(
        functools.partial(_prep_kernel, rope=rope),
        grid=(b, t // tp),
        in_specs=[grp(C_DQ), grp(C_DK), grp(C_DV), grp(C_GQ),
                  pl.BlockSpec((1, tp, kvw), lambda i, j: (i, j, C_GK // kvw)),
                  tab, tab, tab, vec, vec],
        out_specs=[pl.BlockSpec((1, tp, gw), lambda i, j: (i, j, 0)),
                   pl.BlockSpec((1, DIFF_HEADS, 2, 1, DIFF_DIM, tp), lambda i, j: (i, 0, 0, j // sub, 0, j % sub)),
                   pl.BlockSpec((1, tp, gw), lambda i, j: (i, j, 0)),
                   pl.BlockSpec((1, tp, gw), lambda i, j: (i, j, 0)),
                   pl.BlockSpec((1, GQA_KV, 1, GQA_DIM, tp), lambda i, j: (i, 0, j // sub, 0, j % sub)),
                   pl.BlockSpec((1, GQA_KV, tp, GQA_DIM), lambda i, j: (i, 0, j, 0))],
        out_shape=[jax.ShapeDtypeStruct((b, t, gw), BF16),
                   jax.ShapeDtypeStruct((b, DIFF_HEADS, 2, nck, DIFF_DIM, tk), BF16),
                   jax.ShapeDtypeStruct((b, t, gw), BF16),
                   jax.ShapeDtypeStruct((b, t, gw), BF16),
                   jax.ShapeDtypeStruct((b, GQA_KV, nck, GQA_DIM, tk), BF16),
                   jax.ShapeDtypeStruct((b, GQA_KV, t, GQA_DIM), BF16)],
        compiler_params=_cparams(("parallel", "parallel")),
        name="attn_prep",
    )(p, p, p, p, p, cos, sa, sb,
      jnp.tile(q_g, 2).reshape(1, LANES), jnp.tile(k_g, 2).reshape(1, LANES))
    return outs


def _flash_step(q, kt, v, m_ref, l_ref, acc_ref):
    s = jnp.dot(q, kt, preferred_element_type=F32)
    tk = s.shape[1]
    m_prev = m_ref[...]
    m_next = jnp.maximum(m_prev, jnp.max(s, axis=1, keepdims=True))
    alpha = jnp.exp2(m_prev - m_next)
    p = jnp.exp2(s - jnp.concatenate([m_next] * (tk // LANES), axis=1))
    l_ref[...] = alpha * l_ref[...] + jnp.sum(p, axis=1, keepdims=True)
    dv = acc_ref.shape[-1]
    acc_ref[...] = acc_ref[...] * alpha[:, :dv] + jnp.dot(p.astype(BF16), v, preferred_element_type=F32)
    m_ref[...] = m_next


def _attn_body(q_ref, ktc, vc, ktl, vl, m_sc, l_sc, acc_sc, *, n_lat, tk):
    m_sc[...] = jnp.full(m_sc.shape, -jnp.inf, F32)
    l_sc[...] = jnp.zeros(l_sc.shape, F32)
    acc_sc[...] = jnp.zeros(acc_sc.shape, F32)
    q = q_ref[0]
    qs = (q[:, :DIFF_DIM], q[:, DIFF_DIM:])
    for c in range(2):
        _flash_step(qs[c], ktc(c), vc(c), m_sc.at[c], l_sc.at[c], acc_sc.at[c])
    if n_lat:
        def body(i, carry):
            for c in range(2):
                _flash_step(qs[c], ktl(c, i), vl(c, i), m_sc.at[c], l_sc.at[c], acc_sc.at[c])
            return carry
        lax.fori_loop(0, n_lat, body, 0)


def _diff_attn_kernel(*refs, n_lat, tk, out_scale):
    if n_lat:
        q_ref, ktc_ref, vc_ref, ktl_ref, vl_ref, lam_ref, g_ref, o_ref, m_sc, l_sc, acc_sc = refs
        ktl = lambda c, i: ktl_ref[0, 0, c, i]
        vl = lambda c, i: vl_ref[0, pl.ds(pl.multiple_of(i * tk, tk), tk), :]
    else:
        q_ref, ktc_ref, vc_ref, lam_ref, g_ref, o_ref, m_sc, l_sc, acc_sc = refs
        ktl = vl = None
    _attn_body(q_ref, lambda c: ktc_ref[0, 0, c, 0], lambda c: vc_ref[0], ktl, vl,
               m_sc, l_sc, acc_sc, n_lat=n_lat, tk=tk)
    o0 = acc_sc[0] / l_sc[0]
    o1 = acc_sc[1] / l_sc[1]
    o = o0 - lam_ref[...] * o1
    o = o * lax.rsqrt(jnp.mean(o * o, axis=-1, keepdims=True) + EPS)
    o_ref[0] = (o * g_ref[...] * out_scale).astype(o_ref.dtype)


def diff_attention(q, ktc, vc, ktl, vl, lam, norm_g, out_scale, tq):
    b, t, _ = q.shape
    sc = vc.shape[1]
    tq = _tile(t, tq)
    n_lat, tk = (ktl.shape[3], ktl.shape[5]) if ktl is not None else (0, 0)
    in_specs = [pl.BlockSpec((1, tq, LANES), lambda i, h, j: (i, j, h)),
                pl.BlockSpec((1, 1, 2, 1, DIFF_DIM, sc), lambda i, h, j: (i, h, 0, 0, 0, 0)),
                pl.BlockSpec((1, sc, LANES), lambda i, h, j: (i, 0, h))]
    args = [q, ktc, vc]
    if n_lat:
        in_specs += [pl.BlockSpec((1, 1, 2, n_lat, DIFF_DIM, tk), lambda i, h, j: (i, h, 0, 0, 0, 0)),
                     pl.BlockSpec((1, n_lat * tk, LANES), lambda i, h, j: (i, 0, h))]
        args += [ktl, vl]
    vec = pl.BlockSpec((1, LANES), lambda i, h, j: (0, 0))
    in_specs += [vec, vec]
    args += [jnp.full((1, LANES), lam, F32), norm_g.reshape(1, LANES)]
    return pl.pallas_call(
        functools.partial(_diff_attn_kernel, n_lat=n_lat, tk=tk, out_scale=out_scale),
        grid=(b, DIFF_HEADS, t // tq),
        in_specs=in_specs,
        out_specs=pl.BlockSpec((1, tq, LANES), lambda i, h, j: (i, j, h)),
        out_shape=jax.ShapeDtypeStruct((b, t, DIFF_HEADS * DIFF_VDIM), BF16),
        scratch_shapes=[pltpu.VMEM((2, tq, LANES), F32), pltpu.VMEM((2, tq, LANES), F32),
                        pltpu.VMEM((2, tq, DIFF_VDIM), F32)],
        compiler_params=_cparams(("parallel", "parallel", "parallel")),
        name="diff_attn",
    )(*args)


def _gqa_attn_kernel(*refs, n_lat, tk):
    if n_lat:
        q_ref, ktc_ref, vc_ref, ktl_ref, vl_ref, o_ref, m_sc, l_sc, acc_sc = refs
        ktl = lambda c, i: ktl_ref[0, 0, i]
        vl = lambda c, i: vl_ref[0, 0, pl.ds(pl.multiple_of(i * tk, tk), tk), :]
    else:
        q_ref, ktc_ref, vc_ref, o_ref, m_sc, l_sc, acc_sc = refs
        ktl = vl = None
    _attn_body(q_ref, lambda c: ktc_ref[0, 0, 0], lambda c: vc_ref[0, 0], ktl, vl,
               m_sc, l_sc, acc_sc, n_lat=n_lat, tk=tk)
    o0 = acc_sc[0] / l_sc[0][:, :GQA_DIM]
    o1 = acc_sc[1] / l_sc[1][:, :GQA_DIM]
    o_ref[0] = jnp.concatenate([o0, o1], axis=-1).astype(o_ref.dtype)


def gqa_attention(q, ktc, vc, ktl, vl, tq):
    b, t, _ = q.shape
    sc = vc.shape[2]
    tq = _tile(t, tq)
    n_lat, tk = (ktl.shape[2], ktl.shape[4]) if ktl is not None else (0, 0)
    pairs = GQA_HEADS // 2
    grp = lambda h: h // (pairs // GQA_KV)
    in_specs = [pl.BlockSpec((1, tq, LANES), lambda i, h, j: (i, j, h)),
                pl.BlockSpec((1, 1, 1, GQA_DIM, sc), lambda i, h, j: (i, grp(h), 0, 0, 0)),
                pl.BlockSpec((1, 1, sc, GQA_DIM), lambda i, h, j: (i, grp(h), 0, 0))]
    args = [q, ktc, vc]
    if n_lat:
        in_specs += [pl.BlockSpec((1, 1, n_lat, GQA_DIM, tk), lambda i, h, j: (i, grp(h), 0, 0, 0)),
                     pl.BlockSpec((1, 1, n_lat * tk, GQA_DIM), lambda i, h, j: (i, grp(h), 0, 0))]
        args += [ktl, vl]
    return pl.pallas_call(
        functools.partial(_gqa_attn_kernel, n_lat=n_lat, tk=tk),
        grid=(b, pairs, t // tq),
        in_specs=in_specs,
        out_specs=pl.BlockSpec((1, tq, LANES), lambda i, h, j: (i, j, h)),
        out_shape=jax.ShapeDtypeStruct((b, t, GQA_HEADS * GQA_DIM), BF16),
        scratch_shapes=[pltpu.VMEM((2, tq, LANES), F32), pltpu.VMEM((2, tq, LANES), F32),
                        pltpu.VMEM((2, tq, GQA_DIM), F32)],
        compiler_params=_cparams(("parallel", "parallel", "parallel")),
        name="gqa_attn",
    )(*args)


ML_STEP = 256
HALO = 16


def _halo_rows(prev_ref, next_ref, j, nblk):
    prev = jnp.where(j > 0, prev_ref[0, HALO - 1:HALO, :].astype(F32), 0.0)
    nxt = jnp.where(j < nblk - 1, next_ref[0, 0:1, :].astype(F32), 0.0)
    return prev, nxt


def _conv3(x, prev, nxt, w, bias):
    tt = x.shape[0]
    row = lax.broadcasted_iota(jnp.int32, x.shape, 0)
    xm = jnp.where(row == 0, prev, pltpu.roll(x, 1, axis=0))
    xp = jnp.where(row == tt - 1, nxt, pltpu.roll(x, tt - 1, axis=0))
    return xm * w[0:1] + x * w[1:2] + xp * w[2:3] + bias


def _halo_specs(tt, width, col_block, t):
    nb = t // HALO
    prev = pl.BlockSpec((1, HALO, width), lambda i, j: (i, jnp.maximum(j * (tt // HALO) - 1, 0), col_block))
    nxt = pl.BlockSpec((1, HALO, width), lambda i, j: (i, jnp.minimum((j + 1) * (tt // HALO), nb - 1), col_block))
    return prev, nxt


def _ml_prep_kernel(q_ref, qp_ref, qn_ref, k_ref, kp_ref, kn_ref, g_ref, w_ref, b_ref, gb_ref, q_o, k_o, g_o):
    j = pl.program_id(1)
    nblk = pl.num_programs(1)
    w = ML_HEADS * ML_DIM
    qp, qn = _halo_rows(qp_ref, qn_ref, j, nblk)
    kp, kn = _halo_rows(kp_ref, kn_ref, j, nblk)
    q = _conv3(q_ref[0].astype(F32), qp, qn, w_ref[:, :w], b_ref[:, :w])
    k = _conv3(k_ref[0].astype(F32), kp, kn, w_ref[:, w:], b_ref[:, w:])
    q_o[0] = q * jax.nn.sigmoid(q)
    k_o[0] = (k * jax.nn.sigmoid(k)) * ML_DIM ** -0.5
    x = g_ref[0].astype(F32) + gb_ref[...]
    lane = lax.broadcasted_iota(jnp.int32, x.shape, 1)
    log_sig = jnp.minimum(x, 0.0) - jnp.log(1.0 + jnp.exp(-jnp.abs(x)))
    g_o[0] = jnp.where((lane % 8) >= ML_HEADS, log_sig, x)


def ml_prep(p, conv_w, conv_b, gate_b):
    b, t, _ = p.shape
    tt = _tile(t, ROW_TILE)
    w = ML_HEADS * ML_DIM
    blk = lambda c: pl.BlockSpec((1, tt, w), lambda i, j, c=c: (i, j, c // w))
    qp, qn = _halo_specs(tt, w, C_MLQ // w, t)
    kp, kn = _halo_specs(tt, w, C_MLK // w, t)
    gb = jnp.zeros((1, LANES), F32).at[0, :4 * ML_HEADS].set(gate_b.reshape(-1))
    return pl.pallas_call(
        _ml_prep_kernel,
        grid=(b, t // tt),
        in_specs=[blk(C_MLQ), qp, qn, blk(C_MLK), kp, kn,
                  pl.BlockSpec((1, tt, LANES), lambda i, j: (i, j, C_MLG // LANES)),
                  pl.BlockSpec((3, 2 * w), lambda i, j: (0, 0)),
                  pl.BlockSpec((1, 2 * w), lambda i, j: (0, 0)),
                  pl.BlockSpec((1, LANES), lambda i, j: (0, 0))],
        out_specs=[pl.BlockSpec((1, tt, w), lambda i, j: (i, j, 0)),
                   pl.BlockSpec((1, tt, w), lambda i, j: (i, j, 0)),
                   pl.BlockSpec((1, tt, LANES), lambda i, j: (i, j, 0))],
        out_shape=[jax.ShapeDtypeStruct((b, t, w), F32), jax.ShapeDtypeStruct((b, t, w), F32),
                   jax.ShapeDtypeStruct((b, t, LANES), F32)],
        compiler_params=_cparams(("parallel", "parallel")),
        name="ml_prep",
    )(p, p, p, p, p, p, p, conv_w, conv_b.reshape(1, 2 * w), gb)


_NT = (((1,), (1,)), ((), ()))


def _ml_chunk_head(q, k, v, li_c, bc_c, bc_r, b_last, seen, ct_ref, n_ref, m_ref):
    m = m_ref[:, 0:1]
    ct = ct_ref[...]
    n_rows = n_ref[...]
    qb, kb, vb = q.astype(BF16), k.astype(BF16), v.astype(BF16)
    d_t = jnp.where(seen, bc_r + (li_c - bc_c), -jnp.inf)
    inter = bc_r + m
    m_t = jnp.maximum(inter, jnp.max(d_t, axis=0, keepdims=True))
    w_inter = jnp.exp(inter - m_t)
    s_t = lax.dot_general(kb, qb, _NT, preferred_element_type=F32) * jnp.exp(d_t - m_t)
    num_t = (jnp.dot(vb.T, s_t.astype(BF16), preferred_element_type=F32)
             + w_inter * lax.dot_general(ct.astype(BF16), qb, _NT, preferred_element_type=F32))
    qn = lax.dot_general(n_rows.astype(BF16), qb, _NT, preferred_element_type=F32)[0:1]
    den = jnp.sum(s_t, axis=0, keepdims=True) + w_inter * qn
    h_t = num_t / jnp.maximum(jnp.abs(den), jnp.exp(-m_t))
    g = b_last - bc_c + li_c
    m_new = jnp.maximum(b_last + m, jnp.max(g, axis=0, keepdims=True))
    kw = k * jnp.exp(g - m_new)
    wc = jnp.exp(b_last + m - m_new)
    ct_ref[...] = wc * ct + jnp.dot(vb.T, kw.astype(BF16), preferred_element_type=F32)
    n_ref[...] = wc * n_rows + jnp.sum(kw, axis=0, keepdims=True)
    m_ref[...] = jnp.broadcast_to(m_new, m_ref.shape)
    return h_t.T


def _ml_scan_kernel(q_ref, k_ref, v_ref, g_ref, c0_ref, n0_ref, m0_ref, h_o, c1_o, n1_o, m1_o,
                    c_sc, n_sc, m_sc, *, direction):
    j = pl.program_id(0)

    @pl.when(j == 0)
    def _():
        c_sc[...] = c0_ref[...]
        n_sc[...] = n0_ref[...]
        m_sc[...] = m0_ref[...]

    r = lax.broadcasted_iota(jnp.int32, (ML_CHUNK, ML_CHUNK), 0)
    s = lax.broadcasted_iota(jnp.int32, (ML_CHUNK, ML_CHUNK), 1)
    tri_f = ((s >= r) if direction else (s <= r)).astype(F32)
    seen = (r >= s) if direction else (r <= s)
    n_chunks = q_ref.shape[1] // ML_CHUNK
    order = range(n_chunks - 1, -1, -1) if direction else range(n_chunks)
    last = 0 if direction else ML_CHUNK - 1
    for c in order:
        rows = slice(c * ML_CHUNK, (c + 1) * ML_CHUNK)
        for bi in range(q_ref.shape[0]):
            gch = g_ref[bi, rows, :]
            bc = jnp.dot(tri_f, gch, preferred_element_type=F32, precision=lax.Precision.HIGHEST)
            bct = bc.T
            for hd in range(ML_HEADS):
                ci = 2 * ML_HEADS * direction + hd
                cf = ci + ML_HEADS
                cols = slice(hd * ML_DIM, (hd + 1) * ML_DIM)
                h = _ml_chunk_head(q_ref[bi, rows, cols], k_ref[bi, rows, cols], v_ref[bi, rows, cols],
                                   gch[:, ci:ci + 1], bc[:, cf:cf + 1], bct[cf:cf + 1, :],
                                   bc[last:last + 1, cf:cf + 1], seen,
                                   c_sc.at[bi, hd], n_sc.at[bi, hd], m_sc.at[bi, hd])
                h_o[bi, rows, cols] = h

    @pl.when(j == pl.num_programs(0) - 1)
    def _():
        c1_o[...] = c_sc[...]
        n1_o[...] = n_sc[...]
        m1_o[...] = m_sc[...]


def _ml_state_shapes(b):
    return [(b, ML_HEADS, ML_DIM, ML_DIM), (b, ML_HEADS, 8, ML_DIM), (b, ML_HEADS, 1, ML_DIM)]


def ml_scan(q, k, p, g, state, direction):
    b, t, w = q.shape
    ts = _tile(t, ML_STEP)
    nst = t // ts
    tok = (lambda j: (0, nst - 1 - j, 0)) if direction else (lambda j: (0, j, 0))
    tokv = (lambda j: (0, nst - 1 - j, C_MLV // w)) if direction else (lambda j: (0, j, C_MLV // w))
    st_shapes = _ml_state_shapes(b)
    st_specs = [pl.BlockSpec(s, lambda j: (0, 0, 0, 0)) for s in st_shapes]
    h, c1, n1, m1 = pl.pallas_call(
        functools.partial(_ml_scan_kernel, direction=direction),
        grid=(nst,),
        in_specs=[pl.BlockSpec((b, ts, w), tok), pl.BlockSpec((b, ts, w), tok), pl.BlockSpec((b, ts, w), tokv),
                  pl.BlockSpec((b, ts, LANES), tok)] + st_specs,
        out_specs=[pl.BlockSpec((b, ts, w), tok)] + st_specs,
        out_shape=[jax.ShapeDtypeStruct((b, t, w), F32)] + [jax.ShapeDtypeStruct(s, F32) for s in st_shapes],
        scratch_shapes=[pltpu.VMEM(s, F32) for s in st_shapes],
        compiler_params=_cparams(("arbitrary",)),
        name="ml_scan",
    )(q, k, p, g, *state)
    return h, (c1, n1, m1)


def mlstm_branch(p_c, p_l, conv_w, conv_b, gate_b, need_ctx):
    b = p_l.shape[0]
    qc, kc, gc = ml_prep(p_c, conv_w, conv_b, gate_b)
    ql, kl, gl = ml_prep(p_l, conv_w, conv_b, gate_b)
    zero = tuple(jnp.zeros(s, F32) for s in _ml_state_shapes(b))
    hs_c, hs_l = [], []
    for direction in (0, 1):
        hc, st = ml_scan(qc, kc, p_c, gc, zero, direction)
        hl, _ = ml_scan(ql, kl, p_l, gl, st, direction)
        hs_c.append(hc)
        hs_l.append(hl)
    return (tuple(hs_c) if need_ctx else None), tuple(hs_l)


def _merge_kernel(ya_ref, hf_ref, hb_ref, yc_ref, yd_ref, og_ref, g_ref, mg_ref, wb_ref, wo_ref, x_ref, gate_ref,
                  o_ref):
    d = x_ref.shape[-1]
    mg = mg_ref[...]
    yb = []
    for hd in range(ML_HEADS):
        cols = slice(hd * ML_DIM, (hd + 1) * ML_DIM)
        h = hf_ref[0, :, cols] + hb_ref[0, :, cols]
        h = h * lax.rsqrt(jnp.mean(h * h, axis=-1, keepdims=True) + EPS) * mg
        yb.append((h * jax.nn.sigmoid(og_ref[0, :, cols].astype(F32))).astype(BF16))
    ys = (ya_ref[0].astype(BF16), jnp.concatenate(yb, axis=-1), yc_ref[0].astype(BF16), yd_ref[0].astype(BF16))
    acc = None
    for n, y in enumerate(ys):
        t = jnp.dot(y, wb_ref[n], preferred_element_type=F32)
        t = jax.nn.sigmoid(g_ref[0, :, n * d:(n + 1) * d].astype(F32)) * t
        acc = t if acc is None else acc + t
    z = jnp.dot(acc.astype(BF16), wo_ref[...], preferred_element_type=F32)
    o_ref[0] = x_ref[0] + gate_ref[0] * z


def merge(ya, hs, yc, yd, p, ml_norm_g, w_branch, w_out, x, gate):
    b, t, d = x.shape
    tm = _tile(t, ROW_TILE)
    w = ML_HEADS * ML_DIM
    ysp = pl.BlockSpec((1, tm, w), lambda i, j: (i, j, 0))
    return pl.pallas_call(
        _merge_kernel,
        grid=(b, t // tm),
        in_specs=[ysp, ysp, ysp, ysp, ysp,
                  pl.BlockSpec((1, tm, w), lambda i, j: (i, j, C_MLO // w)),
                  pl.BlockSpec((1, tm, N_BRANCH * d), lambda i, j: (i, j, C_GATE // (N_BRANCH * d))),
                  pl.BlockSpec((1, ML_DIM), lambda i, j: (0, 0)),
                  pl.BlockSpec((N_BRANCH, w, d), lambda i, j: (0, 0, 0)),
                  pl.BlockSpec((d, d), lambda i, j: (0, 0)),
                  pl.BlockSpec((1, tm, d), lambda i, j: (i, j, 0)),
                  pl.BlockSpec((1, 1, d), lambda i, j: (i, 0, 0))],
        out_specs=pl.BlockSpec((1, tm, d), lambda i, j: (i, j, 0)),
        out_shape=jax.ShapeDtypeStruct((b, t, d), F32),
        compiler_params=_cparams(("parallel", "parallel")),
        name="merge",
    )(ya, hs[0], hs[1], yc, yd, p, p, ml_norm_g.reshape(1, ML_DIM), w_branch, w_out, x, gate.reshape(b, 1, d))


def _expert_kernel(be_ref, x_ref, wg_ref, wu_ref, wd_ref, o_ref, wg_sc, wu_sc, wd_sc):
    i = pl.program_id(0)

    @pl.when(jnp.logical_or(i == 0, be_ref[i] != be_ref[jnp.maximum(i - 1, 0)]))
    def _():
        wg_sc[...] = wg_ref[0, 0].astype(BF16)
        wu_sc[...] = wu_ref[0, 0].astype(BF16)
        wd_sc[...] = wd_ref[0, 0].astype(BF16)

    x = x_ref[...]
    a = jnp.dot(x, wg_sc[...], preferred_element_type=F32)
    u = jnp.dot(x, wu_sc[...], preferred_element_type=F32)
    h = (a * jax.nn.sigmoid(a)) * u
    o_ref[...] = jnp.dot(h.astype(BF16), wd_sc[...], preferred_element_type=F32).astype(o_ref.dtype)


def expert_blocks(xb, blk_exp, layer, w_gate, w_up, w_down):
    m, d = xb.shape
    hdim = w_gate.shape[-1]
    n_blocks = m // MOE_BLOCK
    grid_spec = pltpu.PrefetchScalarGridSpec(
        num_scalar_prefetch=1,
        grid=(n_blocks,),
        in_specs=[pl.BlockSpec((MOE_BLOCK, d), lambda i, be: (i, 0)),
                  pl.BlockSpec((1, 1, d, hdim), lambda i, be: (layer, be[i], 0, 0)),
                  pl.BlockSpec((1, 1, d, hdim), lambda i, be: (layer, be[i], 0, 0)),
                  pl.BlockSpec((1, 1, hdim, d), lambda i, be: (layer, be[i], 0, 0))],
        out_specs=pl.BlockSpec((MOE_BLOCK, d), lambda i, be: (i, 0)),
        scratch_shapes=[pltpu.VMEM((d, hdim), BF16), pltpu.VMEM((d, hdim), BF16), pltpu.VMEM((hdim, d), BF16)],
    )
    return pl.pallas_call(
        _expert_kernel,
        grid_spec=grid_spec,
        out_shape=jax.ShapeDtypeStruct((m, d), BF16),
        compiler_params=_cparams(("arbitrary",)),
        name="moe_experts",
    )(blk_exp, xb, w_gate, w_up, w_down)


def _rope_tables(n_tok):
    rows = n_tok // GRID_W
    row = jnp.repeat(jnp.arange(rows, dtype=F32), GRID_W)
    col = jnp.broadcast_to(jnp.arange(GRID_W, dtype=F32), (rows, GRID_W)).reshape(-1)
    n_freq = DIFF_DIM // 4
    inv = ROPE_BASE ** (-jnp.arange(n_freq, dtype=F32) / n_freq)
    ar = row[:, None] * inv
    ac = col[:, None] * inv
    ang = jnp.concatenate([ar, ar, ac, ac], axis=-1)
    cos, sin = jnp.cos(ang), jnp.sin(ang)
    first = (jnp.arange(DIFF_DIM) % 32) < 16
    sa = jnp.where(first, -sin, 0.0)
    sb = jnp.where(first, 0.0, sin)
    return tuple(jnp.tile(a, (1, 2)) for a in (cos, sa, sb))


DFT_N2 = 256
DFT_J = 8
DFT_P = 4


def _hy_prep_kernel(*refs):
    ins, outs = refs[:9], refs[11:]
    w_ref, b_ref = refs[9], refs[10]
    j = pl.program_id(1)
    nblk = pl.num_programs(1)
    for n in range(HY_ORDER + 1):
        x_ref, p_ref, n_ref = ins[3 * n:3 * n + 3]
        cols = slice(n * HY_CH, (n + 1) * HY_CH)
        prev, nxt = _halo_rows(p_ref, n_ref, j, nblk)
        outs[n][0] = _conv3(x_ref[0].astype(F32), prev, nxt, w_ref[:, cols], b_ref[:, cols])


def hy_prep(p, conv_w, conv_b):
    b, t, _ = p.shape
    tt = _tile(t, ROW_TILE)
    in_specs, args = [], []
    for n in range(HY_ORDER + 1):
        cb = C_HY // HY_CH + n
        prev, nxt = _halo_specs(tt, HY_CH, cb, t)
        in_specs += [pl.BlockSpec((1, tt, HY_CH), lambda i, j, cb=cb: (i, j, cb)), prev, nxt]
        args += [p, p, p]
    nch = (HY_ORDER + 1) * HY_CH
    in_specs += [pl.BlockSpec((3, nch), lambda i, j: (0, 0)), pl.BlockSpec((1, nch), lambda i, j: (0, 0))]
    osp = pl.BlockSpec((1, tt, HY_CH), lambda i, j: (i, j, 0))
    return pl.pallas_call(
        _hy_prep_kernel,
        grid=(b, t // tt),
        in_specs=in_specs,
        out_specs=[osp] * (HY_ORDER + 1),
        out_shape=[jax.ShapeDtypeStruct((b, t, HY_CH), F32)] * (HY_ORDER + 1),
        compiler_params=_cparams(("parallel", "parallel")),
        name="hy_prep",
    )(*args, conv_w, conv_b.reshape(1, nch))


def _hy_filter_kernel(emb_ref, w1_ref, b1_ref, w2_ref, b2_ref, w3_ref, fr_ref, al_ref, f_o, ss_o, *, length, half):
    j = pl.program_id(0)
    tt = emb_ref.shape[0]
    a = jnp.dot(emb_ref[...].astype(BF16), w1_ref[...].astype(BF16), preferred_element_type=F32) + b1_ref[...]
    a = jnp.sin(fr_ref[0:1, :] * a)
    a = jnp.dot(a.astype(BF16), w2_ref[...].astype(BF16), preferred_element_type=F32) + b2_ref[...]
    a = jnp.sin(fr_ref[1:2, :] * a)
    filt = jnp.dot(a.astype(BF16), w3_ref[...].astype(BF16), preferred_element_type=F32)
    r = lax.broadcasted_iota(jnp.int32, (tt, HY_CH), 0)
    if half:
        row = DFT_N2 * (r & (half - 1)) + j * (tt // half) + (r >> (half.bit_length() - 1))
    else:
        row = r + j * tt
    window = jnp.exp(-(row.astype(F32) / length) * al_ref[...]) + FILTER_SHIFT

    @pl.when(j == 0)
    def _():
        ss_o[...] = jnp.zeros(ss_o.shape, F32)

    for o in range(HY_ORDER):
        for d in range(2):
            idx = 2 * o + d
            f = filt[:, idx * HY_CH:(idx + 1) * HY_CH] * window
            if d == 1:
                f = jnp.where(row == 0, 0.0, f)
            if half:
                for jj in range(tt // half):
                    f_o[idx, :, jj * HY_CH:(jj + 1) * HY_CH] = f[jj * half:(jj + 1) * half]
            else:
                f_o[idx] = f
            ss_o[o:o + 1, :] += jnp.sum(f * f, axis=0, keepdims=True)


def hy_filters(length, w1, b1, w2, b2, w3, freq, wide):
    t = jnp.arange(length, dtype=F32) / length
    bands = jnp.arange(1, FILTER_BANDS + 1, dtype=F32)
    ang = 2.0 * math.pi * t[:, None] * bands
    emb = jnp.concatenate([t[:, None], jnp.cos(ang), jnp.sin(ang)], axis=-1)
    pad = LANES - emb.shape[1]
    emb = jnp.pad(emb, ((0, 0), (0, pad)))
    w1 = jnp.pad(w1, ((0, pad), (0, 0)))
    ne, nh = emb.shape[1], w1.shape[1]
    alpha = jnp.linspace(abs(math.log(DECAY_TARGET)) / SLOW_DECAY_PCT,
                         abs(math.log(DECAY_TARGET)) / FAST_DECAY_PCT, HY_CH).reshape(1, HY_CH)
    tt = _tile(length, ROW_TILE)
    half = length // DFT_N2
    wide = wide and half > 0 and half & (half - 1) == 0 and tt % half == 0
    full = lambda shape: pl.BlockSpec(shape, lambda j: (0,) * len(shape))
    if wide:
        emb = jnp.swapaxes(emb.reshape(half, DFT_N2, ne), 0, 1).reshape(length, ne)
        f_spec = pl.BlockSpec((2 * HY_ORDER, half, (tt // half) * HY_CH), lambda j: (0, 0, j))
        f_shape = (2 * HY_ORDER, half, DFT_N2 * HY_CH)
    else:
        f_spec = pl.BlockSpec((2 * HY_ORDER, tt, HY_CH), lambda j: (0, j, 0))
        f_shape = (2 * HY_ORDER, length, HY_CH)
    f, ss = pl.pallas_call(
        functools.partial(_hy_filter_kernel, length=length, half=half if wide else 0),
        grid=(length // tt,),
        in_specs=[pl.BlockSpec((tt, ne), lambda j: (j, 0)), full((ne, nh)), full((1, nh)), full((nh, nh)),
                  full((1, nh)), full((nh, 2 * HY_ORDER * HY_CH)), full((2, nh)), full((1, HY_CH))],
        out_specs=[f_spec, full((HY_ORDER, HY_CH))],
        out_shape=[jax.ShapeDtypeStruct(f_shape, F32), jax.ShapeDtypeStruct((HY_ORDER, HY_CH), F32)],
        compiler_params=_cparams(("arbitrary",)),
        name="hy_filter",
    )(emb, w1, b1.reshape(1, nh), w2, b2.reshape(1, nh), w3, freq, alpha)
    return f, lax.rsqrt(ss + EPS)


def _dft_tables(length):
    n = 2 * length
    n1 = n // DFT_N2
    half = n1 // 2
    n1h = -(-(half + 1) // 16) * 16
    kv = jnp.arange(n1h, dtype=jnp.int32)
    valid = (kv <= half).astype(F32)[None, :, None]
    pair = jnp.where((kv == 0) | (kv == half), 1.0, 2.0)[None, :, None] * valid
    k1 = kv[None, :, None]
    tn = (DFT_N2 * jnp.arange(half, dtype=jnp.int32)[None, None, :]
          + jnp.arange(DFT_N2, dtype=jnp.int32)[:, None, None])
    th = (2.0 * math.pi / n) * ((k1 * tn) % n).astype(F32)
    ga = jnp.concatenate([jnp.cos(th) * valid, -jnp.sin(th) * valid], axis=1).astype(BF16)
    gi = jnp.swapaxes(jnp.concatenate([jnp.cos(th) * pair, -jnp.sin(th) * pair], axis=1), 1, 2).astype(BF16)
    kk = jnp.arange(DFT_N2, dtype=jnp.int32)
    t2 = (2.0 * math.pi / DFT_N2) * ((kk[:, None] * kk[None, :]) % DFT_N2).astype(F32)
    c2, s2 = jnp.cos(t2), jnp.sin(t2)
    mf = jnp.block([[c2, s2], [-s2, c2]]).astype(BF16)
    mi = jnp.block([[c2, -s2], [s2, c2]]).astype(BF16)
    return ga, gi, mf, mi


def _dft_a_kernel(z_ref, g_ref, o_ref):
    c = HY_CH
    for j in range(DFT_J):
        slab = z_ref[0, :, j * c:(j + 1) * c].astype(BF16)
        r = jnp.dot(g_ref[j], slab, preferred_element_type=F32)
        o_ref[0, :, :, j * c:(j + 1) * c] = r.reshape(2, r.shape[0] // 2, c).astype(o_ref.dtype)


def dft_a(z, ga):
    bz, half, wid = z.shape
    n1 = ga.shape[1] // 2
    jc = DFT_J * HY_CH
    return pl.pallas_call(
        _dft_a_kernel,
        grid=(bz, DFT_N2 // DFT_J),
        in_specs=[pl.BlockSpec((1, half, jc), lambda i, j: (i, 0, j)),
                  pl.BlockSpec((DFT_J, 2 * n1, half), lambda i, j: (j, 0, 0))],
        out_specs=pl.BlockSpec((1, 2, n1, jc), lambda i, j: (i, 0, 0, j)),
        out_shape=jax.ShapeDtypeStruct((bz, 2, n1, wid), BF16),
        compiler_params=_cparams(("parallel", "parallel")),
        name="dft_a",
    )(z, ga)


def _stack_ri(ref, b, k):
    return jnp.concatenate([ref[b, 0, k], ref[b, 1, k]], axis=0)


def _spec_filter_kernel(f_ref, mf_ref, sc_ref, h_o):
    sc = sc_ref[0]
    for k in range(DFT_P):
        xf = jnp.dot(mf_ref[...], _stack_ri(f_ref, 0, k), preferred_element_type=F32)
        xb = jnp.dot(mf_ref[...], _stack_ri(f_ref, 1, k), preferred_element_type=F32)
        h_o[0, k, 0] = ((xf[:DFT_N2] + xb[:DFT_N2]) * sc).astype(h_o.dtype)
        h_o[0, k, 1] = ((xf[DFT_N2:] - xb[DFT_N2:]) * sc).astype(h_o.dtype)


def spec_filter(fa, mf, scale):
    nb, _, n1, _, c = fa.shape
    order = nb // 2
    return pl.pallas_call(
        _spec_filter_kernel,
        grid=(order, n1 // DFT_P),
        in_specs=[pl.BlockSpec((2, 2, DFT_P, DFT_N2, c), lambda o, k: (o, 0, k, 0, 0)),
                  pl.BlockSpec((2 * DFT_N2, 2 * DFT_N2), lambda o, k: (0, 0)),
                  pl.BlockSpec((1, 1, c), lambda o, k: (o, 0, 0))],
        out_specs=pl.BlockSpec((1, DFT_P, 2, DFT_N2, c), lambda o, k: (o, k, 0, 0, 0)),
        out_shape=jax.ShapeDtypeStruct((order, n1, 2, DFT_N2, c), BF16),
        compiler_params=_cparams(("parallel", "parallel")),
        name="spec_filter",
    )(fa, mf, scale.reshape(order, 1, c))


def _spec_conv_kernel(a_ref, h_ref, mf_ref, mi_ref, o_ref):
    for k in range(DFT_P):
        x = jnp.dot(mf_ref[...], _stack_ri(a_ref, 0, k), preferred_element_type=F32)
        xr, xi = x[:DFT_N2], x[DFT_N2:]
        hr, hi = h_ref[0, k, 0].astype(F32), h_ref[0, k, 1].astype(F32)
        y = jnp.concatenate([xr * hr - xi * hi, xr * hi + xi * hr], axis=0).astype(BF16)
        z = jnp.dot(mi_ref[...], y, preferred_element_type=F32)
        o_ref[0, 0, k] = z[:DFT_N2].astype(o_ref.dtype)
        o_ref[0, 1, k] = z[DFT_N2:].astype(o_ref.dtype)


def spec_conv(a, h, order, mf, mi):
    b, _, n1, _, c = a.shape
    blk = pl.BlockSpec((1, 2, DFT_P, DFT_N2, c), lambda i, k: (i, 0, k, 0, 0))
    mat = pl.BlockSpec((2 * DFT_N2, 2 * DFT_N2), lambda i, k: (0, 0))
    return pl.pallas_call(
        _spec_conv_kernel,
        grid=(b, n1 // DFT_P),
        in_specs=[blk, pl.BlockSpec((1, DFT_P, 2, DFT_N2, c), lambda i, k: (order, k, 0, 0, 0)), mat, mat],
        out_specs=blk,
        out_shape=jax.ShapeDtypeStruct(a.shape, BF16),
        compiler_params=_cparams(("parallel", "parallel")),
        name="spec_conv",
    )(a, h, mf, mi)


def _dft_ainv_kernel(z_ref, g_ref, xg_ref, zin_ref, bias_ref, o_ref):
    c = HY_CH
    n1 = z_ref.shape[2]
    for j in range(DFT_J):
        cols = slice(j * c, (j + 1) * c)
        zz = z_ref[0, :, :, cols].reshape(2 * n1, c)
        y = jnp.dot(g_ref[j], zz, preferred_element_type=F32)
        o_ref[0, :, cols] = xg_ref[0, :, cols] * (y + bias_ref[...] * zin_ref[0, :, cols])


def dft_ainv(z, gi, xg, zin, bias):
    b, _, n1, wid = z.shape
    half = gi.shape[1]
    jc = DFT_J * HY_CH
    tok = pl.BlockSpec((1, half, jc), lambda i, j: (i, 0, j))
    return pl.pallas_call(
        _dft_ainv_kernel,
        grid=(b, DFT_N2 // DFT_J),
        in_specs=[pl.BlockSpec((1, 2, n1, jc), lambda i, j: (i, 0, 0, j)),
                  pl.BlockSpec((DFT_J, half, 2 * n1), lambda i, j: (j, 0, 0)),
                  tok, tok, pl.BlockSpec((1, HY_CH), lambda i, j: (0, 0))],
        out_specs=tok,
        out_shape=jax.ShapeDtypeStruct((b, half, wid), F32),
        compiler_params=_cparams(("parallel", "parallel")),
        name="dft_ainv",
    )(z, gi, xg, zin, bias.reshape(1, HY_CH))


def _ctx_conv_kernel(z_ref, xg_ref, f_ref, mf_ref, mi_ref, sc_ref, bias_ref, o_ref):
    nf = mf_ref.shape[0] // 2
    mf = mf_ref[...]
    xf = jnp.dot(mf, f_ref[0].astype(BF16), preferred_element_type=F32)
    xb = jnp.dot(mf, f_ref[1].astype(BF16), preferred_element_type=F32)
    sc = sc_ref[0]
    hr = (xf[:nf] + xb[:nf]) * sc
    hi = (xf[nf:] - xb[nf:]) * sc
    z = z_ref[0]
    x = jnp.dot(mf, z.astype(BF16), preferred_element_type=F32)
    xr, xi = x[:nf], x[nf:]
    y = jnp.concatenate([xr * hr - xi * hi, xr * hi + xi * hr], axis=0).astype(BF16)
    o_ref[0] = xg_ref[0] * (jnp.dot(mi_ref[...], y, preferred_element_type=F32) + bias_ref[...] * z)


def ctx_conv(z, xg, f, order, scale, bias):
    b, length, c = z.shape
    n = 2 * length
    kk = jnp.arange(n, dtype=jnp.int32)[:, None]
    tn = jnp.arange(length, dtype=jnp.int32)[None, :]
    th = (2.0 * math.pi / n) * ((kk * tn) % n).astype(F32)
    mf = jnp.concatenate([jnp.cos(th), -jnp.sin(th)], axis=0).astype(BF16)
    mi = jnp.concatenate([jnp.cos(th.T), -jnp.sin(th.T)], axis=1).astype(BF16)
    tok = pl.BlockSpec((1, length, c), lambda i: (i, 0, 0))
    return pl.pallas_call(
        _ctx_conv_kernel,
        grid=(b,),
        in_specs=[tok, tok, pl.BlockSpec((2, length, c), lambda i: (order, 0, 0)),
                  pl.BlockSpec((2 * n, length), lambda i: (0, 0)), pl.BlockSpec((length, 2 * n), lambda i: (0, 0)),
                  pl.BlockSpec((1, 1, c), lambda i: (order, 0, 0)), pl.BlockSpec((1, c), lambda i: (0, 0))],
        out_specs=tok,
        out_shape=jax.ShapeDtypeStruct((b, length, c), F32),
        compiler_params=_cparams(("parallel",)),
        name="ctx_conv",
    )(z, xg, f, mf, mi, scale.reshape(-1, 1, c), bias.reshape(1, c))


def hyena(p, conv_w, conv_b, w1, b1, w2, b2, w3, freq, bias, tables):
    b, length, _ = p.shape
    parts = hy_prep(p, conv_w, conv_b)
    f, rnorm = hy_filters(length, w1, b1, w2, b2, w3, freq, wide=tables is not None)
    scale = rnorm / (2 * length)
    z = parts[0]
    if tables is None:
        for o in range(HY_ORDER):
            z = ctx_conv(z, parts[o + 1], f, o, scale, bias[o])
        return z
    ga, gi, mf, mi = tables
    half = ga.shape[2]
    wid = DFT_N2 * HY_CH
    fa = dft_a(f.reshape(2 * HY_ORDER, half, wid), ga)
    n1 = fa.shape[2]
    h = spec_filter(fa.reshape(2 * HY_ORDER, 2, n1, DFT_N2, HY_CH), mf, scale)
    for o in range(HY_ORDER):
        a = dft_a(z.reshape(b, half, wid), ga).reshape(b, 2, n1, DFT_N2, HY_CH)
        zc = spec_conv(a, h, o, mf, mi).reshape(b, 2, n1, wid)
        z = dft_ainv(zc, gi, parts[o + 1].reshape(b, half, wid), z.reshape(b, half, wid), bias[o])
        z = z.reshape(b, length, HY_CH)
    return z


ROUTE_ROWS = 8


def _router_kernel(x_ref, g_ref, sh_ref, sc_ref, w_ref, b_ref, h_o, id_o, gate_o):
    x = x_ref[0]
    y = x * lax.rsqrt(jnp.mean(x * x, axis=-1, keepdims=True) + EPS)
    y = ((y * g_ref[...]) * (1.0 + sc_ref[0]) + sh_ref[0]).astype(BF16)
    h_o[0] = y
    logits = jnp.dot(y, w_ref[...], preferred_element_type=F32) + b_ref[...]
    lane = lax.broadcasted_iota(jnp.int32, logits.shape, 1)
    lane_f = lane.astype(F32)
    none = float(LANES)

    def top(vals):
        v = jnp.max(vals, axis=1, keepdims=True)
        return v, jnp.min(jnp.where(vals == v, lane_f, none), axis=1, keepdims=True)

    is_grp = lane < MOE_GROUPS
    mg, grp = top(jnp.where(is_grp, logits, -jnp.inf))
    p_grp = 1.0 / jnp.sum(jnp.where(is_grp, jnp.exp(logits - mg), 0.0), axis=1, keepdims=True)
    lo = MOE_GROUPS + MOE_EPG * grp
    el = jnp.where((lane_f >= lo) & (lane_f < lo + MOE_EPG), logits, -jnp.inf)
    v1, i1 = top(el)
    v2, i2 = top(jnp.where(lane_f == i1, -jnp.inf, el))
    t = jnp.exp(v2 - v1)
    g1 = p_grp / (1.0 + t)
    ids = jnp.where(lane == 0, i1 - MOE_GROUPS, jnp.where(lane == 1, i2 - MOE_GROUPS, 0.0))
    id_o[0, 0] = ids.T[:ROUTE_ROWS].astype(jnp.int32)
    gate_o[0, 0] = jnp.where(lane == 0, g1, jnp.where(lane == 1, g1 * t, 0.0)).T[:ROUTE_ROWS]


def moe_router(x, g, shift, scale, w_group, b_group, w_router, b_router):
    b, t, d = x.shape
    tt = _tile(t, ROW_TILE)
    rows = pl.BlockSpec((1, 1, ROUTE_ROWS, tt), lambda i, j: (i, j, 0, 0))
    npad = LANES - MOE_GROUPS - MOE_EXPERTS
    w = jnp.concatenate([w_group, w_router, jnp.zeros((d, npad), F32)], axis=1).astype(BF16)
    bias = jnp.concatenate([b_group, b_router, jnp.zeros((npad,), F32)]).reshape(1, LANES)
    tok = lambda width: pl.BlockSpec((1, tt, width), lambda i, j: (i, j, 0))
    mod = pl.BlockSpec((1, 1, d), lambda i, j: (i, 0, 0))
    return pl.pallas_call(
        _router_kernel,
        grid=(b, t // tt),
        in_specs=[tok(d), pl.BlockSpec((1, d), lambda i, j: (0, 0)), mod, mod,
                  pl.BlockSpec((d, LANES), lambda i, j: (0, 0)), pl.BlockSpec((1, LANES), lambda i, j: (0, 0))],
        out_specs=[tok(d), rows, rows],
        out_shape=[jax.ShapeDtypeStruct((b, t, d), BF16),
                   jax.ShapeDtypeStruct((b, t // tt, ROUTE_ROWS, tt), jnp.int32),
                   jax.ShapeDtypeStruct((b, t // tt, ROUTE_ROWS, tt), F32)],
        compiler_params=_cparams(("parallel", "parallel")),
        name="moe_router",
    )(x, g.reshape(1, d), shift.reshape(b, 1, d), scale.reshape(b, 1, d), w, bias)


def _blocked_cumsum(onehot, blk=256):
    m, e = onehot.shape
    if m % blk:
        return jnp.cumsum(onehot, axis=0)
    oh = onehot.astype(BF16).reshape(m // blk, blk, e)
    tril = jnp.tril(jnp.ones((blk, blk), BF16))
    within = jnp.einsum('ts,bse->bte', tril, oh, preferred_element_type=F32)
    tot = within[:, -1, :]
    off = jnp.cumsum(tot, axis=0) - tot
    return (within + off[:, None, :]).reshape(m, e).astype(jnp.int32)


def _hier_moe(h, ids, layer, w_gate, w_up, w_down):
    n_tok, d = h.shape
    e_flat = ids.reshape(-1)
    m_slots = n_tok * MOE_TOP_K
    onehot = (e_flat[:, None] == jnp.arange(MOE_EXPERTS, dtype=jnp.int32)[None, :]).astype(jnp.int32)
    csum = _blocked_cumsum(onehot)
    rank = jnp.sum(onehot * csum, axis=1) - 1
    counts = csum[-1]
    padded = (counts + MOE_BLOCK - 1) // MOE_BLOCK * MOE_BLOCK
    p_end = jnp.cumsum(padded)
    dest = (p_end - padded)[e_flat] + rank
    n_blocks = -(-(m_slots + MOE_EXPERTS * (MOE_BLOCK - 1)) // MOE_BLOCK)
    slot_tok = jnp.arange(m_slots, dtype=jnp.int32) // MOE_TOP_K
    buf_tok = jnp.zeros((n_blocks * MOE_BLOCK,), jnp.int32).at[dest].set(slot_tok)
    first_row = jnp.arange(n_blocks, dtype=jnp.int32)[:, None] * MOE_BLOCK
    blk_exp = jnp.minimum(jnp.sum((p_end[None, :] <= first_row).astype(jnp.int32), axis=1), MOE_EXPERTS - 1)
    yb = expert_blocks(h[buf_tok], blk_exp, layer, w_gate, w_up, w_down)
    return yb, dest.reshape(n_tok, MOE_TOP_K)


def _moe_combine_kernel(x_ref, y0_ref, y1_ref, g_ref, m_ref, o_ref):
    g = g_ref[0, 0].T
    f = g[:, 0:1] * y0_ref[...].astype(F32) + g[:, 1:2] * y1_ref[...].astype(F32)
    o_ref[0] = x_ref[0] + m_ref[0] * f


def moe_combine(x, yb, dest, gates, mod):
    b, t, d = x.shape
    y0 = yb[dest[:, 0]]
    y1 = yb[dest[:, 1]]
    nt, tt = gates.shape[1], gates.shape[3]
    row = pl.BlockSpec((tt, d), lambda i, j: (i * nt + j, 0))
    return pl.pallas_call(
        _moe_combine_kernel,
        grid=(b, nt),
        in_specs=[pl.BlockSpec((1, tt, d), lambda i, j: (i, j, 0)), row, row,
                  pl.BlockSpec((1, 1, ROUTE_ROWS, tt), lambda i, j: (i, j, 0, 0)),
                  pl.BlockSpec((1, 1, d), lambda i, j: (i, 0, 0))],
        out_specs=pl.BlockSpec((1, tt, d), lambda i, j: (i, j, 0)),
        out_shape=jax.ShapeDtypeStruct((b, t, d), F32),
        compiler_params=_cparams(("parallel", "parallel")),
        name="moe_combine",
    )(x, y0, y1, gates, mod.reshape(b, 1, d))


def _permute_w_in(w):
    d = w.shape[0]
    sizes = (DIFF_HEADS * 2 * DIFF_DIM, DIFF_HEADS * 2 * DIFF_DIM, DIFF_HEADS * DIFF_VDIM, 3 * ML_HEADS * ML_DIM,
             ML_HEADS * ML_DIM, 4 * ML_HEADS, GQA_HEADS * GQA_DIM, 2 * GQA_KV * GQA_DIM, (HY_ORDER + 1) * HY_CH,
             N_BRANCH * d)
    offs = np.cumsum((0,) + sizes)
    dq, dk, dv, mlqkv, mlo, mlg, gq, gkv, hy, gate = [w[:, offs[i]:offs[i + 1]] for i in range(10)]
    pad = jnp.zeros((d, N_P - C_MLG - 4 * ML_HEADS), w.dtype)
    return jnp.concatenate([gate, dq, dk, dv, mlqkv, mlo, gq, hy, gkv, mlg, pad], axis=1)


def kernel(x, c, ctx, c_ctx, w_ada, b_ada, norm1_g, norm2_g, w_in, diff_lam, diff_norm_g, ml_conv_w, ml_conv_b, ml_gate_b, ml_norm_g, gqa_qnorm_g, gqa_knorm_g, hy_conv_w, hy_conv_b, hy_f_w1, hy_f_b1, hy_f_w2, hy_f_b2, hy_f_w3, hy_f_freq, hy_bias, w_branch, w_out, moe_w_group, moe_b_group, moe_w_router, moe_b_router, moe_w_gate, moe_w_up, moe_w_down, final_norm_g):
    b, n, d = x.shape
    n_ctx = ctx.shape[1]
    depth = w_in.shape[0]
    tk = _tile(n, 2048)
    tables = _rope_tables(n)
    dft_tables = _dft_tables(n)
    sc = jax.nn.silu(c)
    scx = jax.nn.silu(c_ctx)
    xs, cs = x, ctx
    for l in range(depth):
        need_ctx = l < depth - 1
        mod_l = jnp.split(sc @ w_ada[l] + b_ada[l], 6, axis=-1)
        mod_c = [jnp.broadcast_to(m, (b, d)) for m in jnp.split(scx @ w_ada[l] + b_ada[l], 6, axis=-1)]
        w_p = _permute_w_in(w_in[l]).astype(BF16)
        hl = norm_mod(xs, norm1_g[l], mod_l[0], mod_l[1], BF16)
        hc = norm_mod(cs, norm1_g[l], mod_c[0], mod_c[1], BF16)
        p_l = matmul(hl.reshape(b * n, d), w_p, BF16, tm=1024).reshape(b, n, N_P)
        p_c = matmul(hc.reshape(b * n_ctx, d), w_p, BF16, tm=1024).reshape(b, n_ctx, N_P)

        dq_l, dkt_l, dv_l, gq_l, gkt_l, gv_l = attn_prep(p_l, tables, gqa_qnorm_g[l], gqa_knorm_g[l], tk)
        dq_c, dkt_c, dv_c, gq_c, gkt_c, gv_c = attn_prep(p_c, None, gqa_qnorm_g[l], gqa_knorm_g[l], n_ctx)
        lam_init = 0.8 - 0.6 * math.exp(-0.3 * l)
        lp = diff_lam[l].astype(F32)
        lam = jnp.exp(jnp.sum(lp[0] * lp[1])) - jnp.exp(jnp.sum(lp[2] * lp[3])) + lam_init
        yl_a = diff_attention(dq_l, dkt_c, dv_c, dkt_l, dv_l, lam, diff_norm_g[l], 1.0 - lam_init, 1024)
        yl_c = gqa_attention(gq_l, gkt_c, gv_c, gkt_l, gv_l, 1024)
        if need_ctx:
            yc_a = diff_attention(dq_c, dkt_c, dv_c, None, None, lam, diff_norm_g[l], 1.0 - lam_init, 256)
            yc_c = gqa_attention(gq_c, gkt_c, gv_c, None, None, 256)

        hs_c, hs_l = mlstm_branch(p_c, p_l, ml_conv_w[l], ml_conv_b[l], ml_gate_b[l], need_ctx)
        hy_args = (hy_conv_w[l], hy_conv_b[l], hy_f_w1[l], hy_f_b1[l], hy_f_w2[l], hy_f_b2[l],
                   hy_f_w3[l], hy_f_freq[l], hy_bias[l])
        yl_d = hyena(p_l, *hy_args, dft_tables)

        wb = w_branch[l].astype(BF16)
        wo = w_out[l].astype(BF16)
        xs = merge(yl_a, hs_l, yl_c, yl_d, p_l, ml_norm_g[l], wb, wo, xs, mod_l[2])
        route = (moe_w_group[l], moe_b_group[l], moe_w_router[l], moe_b_router[l])
        experts = (l, moe_w_gate, moe_w_up, moe_w_down)
        routed = [moe_router(xs, norm2_g[l], mod_l[3], mod_l[4], *route)]
        if need_ctx:
            yc_d = hyena(p_c, *hy_args, None)
            cs = merge(yc_a, hs_c, yc_c, yc_d, p_c, ml_norm_g[l], wb, wo, cs, mod_c[2])
            routed.insert(0, moe_router(cs, norm2_g[l], mod_c[3], mod_c[4], *route))
        h2 = jnp.concatenate([r[0].reshape(-1, d) for r in routed], axis=0)
        ids = jnp.concatenate([jnp.stack([r[1][:, :, k, :].reshape(-1) for k in range(MOE_TOP_K)], axis=1)
                               for r in routed], axis=0)
        yb, dest = _hier_moe(h2, ids, *experts)
        if need_ctx:
            nc = b * n_ctx
            cs = moe_combine(cs, yb, dest[:nc], routed[0][2], mod_c[5])
            dest = dest[nc:]
        xs = moe_combine(xs, yb, dest, routed[-1][2], mod_l[5])
    zero = jnp.zeros((b, d), F32)
    return norm_mod(xs, final_norm_g, zero, zero, F32)
```

```python
import functools
import math

import jax
import jax.numpy as jnp
import numpy as np
from jax import lax
from jax.experimental import pallas as pl
from jax.experimental.pallas import tpu as pltpu

F32 = jnp.float32
BF16 = jnp.bfloat16

EPS = 1e-6
ROPE_BASE = 10000.0
GRID_W = 64

DIFF_HEADS = 4
DIFF_DIM = 64
DIFF_VDIM = 128
ML_HEADS = 4
ML_DIM = 128
ML_CHUNK = 64
GQA_HEADS = 8
GQA_KV = 2
GQA_DIM = 64
HY_CH = 512
HY_ORDER = 2
FILTER_BANDS = 16
FILTER_SHIFT = 0.05
DECAY_TARGET = 1e-2
FAST_DECAY_PCT = 0.3
SLOW_DECAY_PCT = 1.5
N_BRANCH = 4
MOE_GROUPS = 4
MOE_EPG = 8
MOE_EXPERTS = MOE_GROUPS * MOE_EPG
MOE_TOP_K = 2
MOE_BLOCK = 256
MOE_PIECES = 4

LANES = 128
VMEM_LIMIT = 48 * 1024 * 1024
ROW_TILE = 512
GROUP_W = 512

C_GATE = 0
C_DQ = 4096
C_DK = 4608
C_DV = 5120
C_MLQ = 5632
C_MLK = 6144
C_MLV = 6656
C_MLO = 7168
C_GQ = 7680
C_HY = 8192
C_GK = 9728
C_GV = 9856
C_MLG = 9984
N_P = 10240

QSCALE = (DIFF_DIM ** -0.5) * math.log2(math.e)


def _cparams(sem):
    return pltpu.CompilerParams(dimension_semantics=sem, vmem_limit_bytes=VMEM_LIMIT)


def _tile(n, target):
    if n <= target:
        return n
    for t in range(target, 7, -1):
        if n % t == 0 and t % 8 == 0:
            return t
    return n


def _norm_mod_kernel(x_ref, g_ref, sh_ref, sc_ref, o_ref):
    x = x_ref[0]
    y = x * lax.rsqrt(jnp.mean(x * x, axis=-1, keepdims=True) + EPS)
    y = y * g_ref[...]
    o_ref[0] = (y * (1.0 + sc_ref[0]) + sh_ref[0]).astype(o_ref.dtype)


def norm_mod(x, g, shift, scale, out_dtype):
    b, t, d = x.shape
    tt = _tile(t, ROW_TILE)
    return pl.pallas_call(
        _norm_mod_kernel,
        grid=(b, t // tt),
        in_specs=[pl.BlockSpec((1, tt, d), lambda i, j: (i, j, 0)),
                  pl.BlockSpec((1, d), lambda i, j: (0, 0)),
                  pl.BlockSpec((1, 1, d), lambda i, j: (i, 0, 0)),
                  pl.BlockSpec((1, 1, d), lambda i, j: (i, 0, 0))],
        out_specs=pl.BlockSpec((1, tt, d), lambda i, j: (i, j, 0)),
        out_shape=jax.ShapeDtypeStruct((b, t, d), out_dtype),
        compiler_params=_cparams(("parallel", "parallel")),
        name="norm_mod",
    )(x, g.reshape(1, d), shift.reshape(b, 1, d), scale.reshape(b, 1, d))


def _mm_kernel(a_ref, w_ref, o_ref):
    o_ref[...] = jnp.dot(a_ref[...], w_ref[...], preferred_element_type=F32).astype(o_ref.dtype)


def matmul(a, w, out_dtype, tm=512, tn=1024):
    m, k = a.shape
    n = w.shape[1]
    tm = _tile(m, tm)
    tn = _tile(n, tn)
    return pl.pallas_call(
        _mm_kernel,
        grid=(m // tm, n // tn),
        in_specs=[pl.BlockSpec((tm, k), lambda i, j: (i, 0)),
                  pl.BlockSpec((k, tn), lambda i, j: (0, j))],
        out_specs=pl.BlockSpec((tm, tn), lambda i, j: (i, j)),
        out_shape=jax.ShapeDtypeStruct((m, n), out_dtype),
        compiler_params=_cparams(("parallel", "parallel")),
        name="matmul",
    )(a, w)


def _rope(x, cos, sa, sb):
    xa = pltpu.roll(x, LANES - 16, axis=1)
    xb = pltpu.roll(x, 16, axis=1)
    return x * cos + xa * sa + xb * sb


def _seg_rmsnorm(x, g):
    lane = lax.broadcasted_iota(jnp.int32, x.shape, 1)
    lo = lane < GQA_DIM
    ss = x * x
    s_lo = jnp.sum(jnp.where(lo, ss, 0.0), axis=-1, keepdims=True)
    s_hi = jnp.sum(jnp.where(lo, 0.0, ss), axis=-1, keepdims=True)
    r = jnp.where(lo, lax.rsqrt(s_lo * (1.0 / GQA_DIM) + EPS), lax.rsqrt(s_hi * (1.0 / GQA_DIM) + EPS))
    return x * r * g


def _prep_kernel(dq_ref, dk_ref, dv_ref, gq_ref, gkv_ref, cos_ref, sa_ref, sb_ref, qg_ref, kg_ref,
                 dq_o, dkt_o, dv_o, gq_o, gkt_o, gv_o, *, rope):
    if rope:
        cos, sa, sb = cos_ref[...], sa_ref[...], sb_ref[...]
        rot = lambda x: _rope(x, cos, sa, sb)
    else:
        rot = lambda x: x
    qg = qg_ref[...]
    kg = kg_ref[...]
    for j in range(DIFF_HEADS):
        sl = slice(j * LANES, (j + 1) * LANES)
        dq_o[0, :, sl] = (rot(dq_ref[0, :, sl].astype(F32)) * QSCALE).astype(BF16)
        kt = rot(dk_ref[0, :, sl].astype(F32)).T
        dkt_o[0, j, 0, 0] = kt[:DIFF_DIM].astype(BF16)
        dkt_o[0, j, 1, 0] = kt[DIFF_DIM:].astype(BF16)
        gq_o[0, :, sl] = (rot(_seg_rmsnorm(gq_ref[0, :, sl].astype(F32), qg)) * QSCALE).astype(BF16)
    dv_o[0] = dv_ref[0].astype(BF16)
    kt = rot(_seg_rmsnorm(gkv_ref[0, :, :LANES].astype(F32), kg)).T
    gkt_o[0, 0, 0] = kt[:GQA_DIM].astype(BF16)
    gkt_o[0, 1, 0] = kt[GQA_DIM:].astype(BF16)
    v = gkv_ref[0, :, LANES:].astype(BF16)
    gv_o[0, 0] = v[:, :GQA_DIM]
    gv_o[0, 1] = v[:, GQA_DIM:]


def attn_prep(p, tables, q_g, k_g, tk):
    b, t, _ = p.shape
    rope = tables is not None
    if rope:
        cos, sa, sb = tables
    else:
        cos = sa = sb = jnp.zeros((t, LANES), F32)
    nck = t // tk
    tp = _tile(tk, ROW_TILE)
    sub = tk // tp
    gw = GROUP_W
    kvw = 2 * GQA_KV * GQA_DIM
    grp = lambda c: pl.BlockSpec((1, tp, gw), lambda i, j, c=c: (i, j, c // gw))
    tab = pl.BlockSpec((tp, LANES), lambda i, j: (j, 0))
    vec = pl.BlockSpec((1, LANES), lambda i, j: (0, 0))
    outs = pl.pallas_call(
        functools.partial(_prep_kernel, rope=rope),
        grid=(b, t // tp),
        in_specs=[grp(C_DQ), grp(C_DK), grp(C_DV), grp(C_GQ),
                  pl.BlockSpec((1, tp, kvw), lambda i, j: (i, j, C_GK // kvw)),
                  tab, tab, tab, vec, vec],
        out_specs=[pl.BlockSpec((1, tp, gw), lambda i, j: (i, j, 0)),
                   pl.BlockSpec((1, DIFF_HEADS, 2, 1, DIFF_DIM, tp), lambda i, j: (i, 0, 0, j // sub, 0, j % sub)),
                   pl.BlockSpec((1, tp, gw), lambda i, j: (i, j, 0)),
                   pl.BlockSpec((1, tp, gw), lambda i, j: (i, j, 0)),
                   pl.BlockSpec((1, GQA_KV, 1, GQA_DIM, tp), lambda i, j: (i, 0, j // sub, 0, j % sub)),
                   pl.BlockSpec((1, GQA_KV, tp, GQA_DIM), lambda i, j: (i, 0, j, 0))],
        out_shape=[jax.ShapeDtypeStruct((b, t, gw), BF16),
                   jax.ShapeDtypeStruct((b, DIFF_HEADS, 2, nck, DIFF_DIM, tk), BF16),
                   jax.ShapeDtypeStruct((b, t, gw), BF16),
                   jax.ShapeDtypeStruct((b, t, gw), BF16),
                   jax.ShapeDtypeStruct((b, GQA_KV, nck, GQA_DIM, tk), BF16),
                   jax.ShapeDtypeStruct((b, GQA_KV, t, GQA_DIM), BF16)],
        compiler_params=_cparams(("parallel", "parallel")),
        name="attn_prep",
    )(p, p, p, p, p, cos, sa, sb,
      jnp.tile(q_g, 2).reshape(1, LANES), jnp.tile(k_g, 2).reshape(1, LANES))
    return outs


def _flash_step(q, kt, v, m_ref, l_ref, acc_ref):
    s = jnp.dot(q, kt, preferred_element_type=F32)
    tk = s.shape[1]
    m_prev = m_ref[...]
    m_next = jnp.maximum(m_prev, jnp.max(s, axis=1, keepdims=True))
    alpha = jnp.exp2(m_prev - m_next)
    p = jnp.exp2(s - jnp.concatenate([m_next] * (tk // LANES), axis=1))
    l_ref[...] = alpha * l_ref[...] + jnp.sum(p, axis=1, keepdims=True)
    dv = acc_ref.shape[-1]
    acc_ref[...] = acc_ref[...] * alpha[:, :dv] + jnp.dot(p.astype(BF16), v, preferred_element_type=F32)
    m_ref[...] = m_next


def _attn_body(q_ref, ktc, vc, ktl, vl, m_sc, l_sc, acc_sc, *, n_lat, tk):
    m_sc[...] = jnp.full(m_sc.shape, -jnp.inf, F32)
    l_sc[...] = jnp.zeros(l_sc.shape, F32)
    acc_sc[...] = jnp.zeros(acc_sc.shape, F32)
    q = q_ref[0]
    qs = (q[:, :DIFF_DIM], q[:, DIFF_DIM:])
    for c in range(2):
        _flash_step(qs[c], ktc(c), vc(c), m_sc.at[c], l_sc.at[c], acc_sc.at[c])
    if n_lat:
        def body(i, carry):
            for c in range(2):
                _flash_step(qs[c], ktl(c, i), vl(c, i), m_sc.at[c], l_sc.at[c], acc_sc.at[c])
            return carry
        lax.fori_loop(0, n_lat, body, 0)


def _diff_attn_kernel(*refs, n_lat, tk, out_scale):
    if n_lat:
        q_ref, ktc_ref, vc_ref, ktl_ref, vl_ref, lam_ref, g_ref, o_ref, m_sc, l_sc, acc_sc = refs
        ktl = lambda c, i: ktl_ref[0, 0, c, i]
        vl = lambda c, i: vl_ref[0, pl.ds(pl.multiple_of(i * tk, tk), tk), :]
    else:
        q_ref, ktc_ref, vc_ref, lam_ref, g_ref, o_ref, m_sc, l_sc, acc_sc = refs
        ktl = vl = None
    _attn_body(q_ref, lambda c: ktc_ref[0, 0, c, 0], lambda c: vc_ref[0], ktl, vl,
               m_sc, l_sc, acc_sc, n_lat=n_lat, tk=tk)
    o0 = acc_sc[0] / l_sc[0]
    o1 = acc_sc[1] / l_sc[1]
    o = o0 - lam_ref[...] * o1
    o = o * lax.rsqrt(jnp.mean(o * o, axis=-1, keepdims=True) + EPS)
    o_ref[0] = (o * g_ref[...] * out_scale).astype(o_ref.dtype)


def diff_attention(q, ktc, vc, ktl, vl, lam, norm_g, out_scale, tq):
    b, t, _ = q.shape
    sc = vc.shape[1]
    tq = _tile(t, tq)
    n_lat, tk = (ktl.shape[3], ktl.shape[5]) if ktl is not None else (0, 0)
    in_specs = [pl.BlockSpec((1, tq, LANES), lambda i, h, j: (i, j, h)),
                pl.BlockSpec((1, 1, 2, 1, DIFF_DIM, sc), lambda i, h, j: (i, h, 0, 0, 0, 0)),
                pl.BlockSpec((1, sc, LANES), lambda i, h, j: (i, 0, h))]
    args = [q, ktc, vc]
    if n_lat:
        in_specs += [pl.BlockSpec((1, 1, 2, n_lat, DIFF_DIM, tk), lambda i, h, j: (i, h, 0, 0, 0, 0)),
                     pl.BlockSpec((1, n_lat * tk, LANES), lambda i, h, j: (i, 0, h))]
        args += [ktl, vl]
    vec = pl.BlockSpec((1, LANES), lambda i, h, j: (0, 0))
    in_specs += [vec, vec]
    args += [jnp.full((1, LANES), lam, F32), norm_g.reshape(1, LANES)]
    return pl.pallas_call(
        functools.partial(_diff_attn_kernel, n_lat=n_lat, tk=tk, out_scale=out_scale),
        grid=(b, DIFF_HEADS, t // tq),
        in_specs=in_specs,
        out_specs=pl.BlockSpec((1, tq, LANES), lambda i, h, j: (i, j, h)),
        out_shape=jax.ShapeDtypeStruct((b, t, DIFF_HEADS * DIFF_VDIM), BF16),
        scratch_shapes=[pltpu.VMEM((2, tq, LANES), F32), pltpu.VMEM((2, tq, LANES), F32),
                        pltpu.VMEM((2, tq, DIFF_VDIM), F32)],
        compiler_params=_cparams(("parallel", "parallel", "parallel")),
        name="diff_attn",
    )(*args)


def _gqa_attn_kernel(*refs, n_lat, tk):
    if n_lat:
        q_ref, ktc_ref, vc_ref, ktl_ref, vl_ref, o_ref, m_sc, l_sc, acc_sc = refs
        ktl = lambda c, i: ktl_ref[0, 0, i]
        vl = lambda c, i: vl_ref[0, 0, pl.ds(pl.multiple_of(i * tk, tk), tk), :]
    else:
        q_ref, ktc_ref, vc_ref, o_ref, m_sc, l_sc, acc_sc = refs
        ktl = vl = None
    _attn_body(q_ref, lambda c: ktc_ref[0, 0, 0], lambda c: vc_ref[0, 0], ktl, vl,
               m_sc, l_sc, acc_sc, n_lat=n_lat, tk=tk)
    o0 = acc_sc[0] / l_sc[0][:, :GQA_DIM]
    o1 = acc_sc[1] / l_sc[1][:, :GQA_DIM]
    o_ref[0] = jnp.concatenate([o0, o1], axis=-1).astype(o_ref.dtype)


def gqa_attention(q, ktc, vc, ktl, vl, tq):
    b, t, _ = q.shape
    sc = vc.shape[2]
    tq = _tile(t, tq)
    n_lat, tk = (ktl.shape[2], ktl.shape[4]) if ktl is not None else (0, 0)
    pairs = GQA_HEADS // 2
    grp = lambda h: h // (pairs // GQA_KV)
    in_specs = [pl.BlockSpec((1, tq, LANES), lambda i, h, j: (i, j, h)),
                pl.BlockSpec((1, 1, 1, GQA_DIM, sc), lambda i, h, j: (i, grp(h), 0, 0, 0)),
                pl.BlockSpec((1, 1, sc, GQA_DIM), lambda i, h, j: (i, grp(h), 0, 0))]
    args = [q, ktc, vc]
    if n_lat:
        in_specs += [pl.BlockSpec((1, 1, n_lat, GQA_DIM, tk), lambda i, h, j: (i, grp(h), 0, 0, 0)),
                     pl.BlockSpec((1, 1, n_lat * tk, GQA_DIM), lambda i, h, j: (i, grp(h), 0, 0))]
        args += [ktl, vl]
    return pl.pallas_call(
        functools.partial(_gqa_attn_kernel, n_lat=n_lat, tk=tk),
        grid=(b, pairs, t // tq),
        in_specs=in_specs,
        out_specs=pl.BlockSpec((1, tq, LANES), lambda i, h, j: (i, j, h)),
        out_shape=jax.ShapeDtypeStruct((b, t, GQA_HEADS * GQA_DIM), BF16),
        scratch_shapes=[pltpu.VMEM((2, tq, LANES), F32), pltpu.VMEM((2, tq, LANES), F32),
                        pltpu.VMEM((2, tq, GQA_DIM), F32)],
        compiler_params=_cparams(("parallel", "parallel", "parallel")),
        name="gqa_attn",
    )(*args)


ML_STEP = 256
HALO = 16


def _halo_rows(prev_ref, next_ref, j, nblk):
    prev = jnp.where(j > 0, prev_ref[0, HALO - 1:HALO, :].astype(F32), 0.0)
    nxt = jnp.where(j < nblk - 1, next_ref[0, 0:1, :].astype(F32), 0.0)
    return prev, nxt


def _conv3(x, prev, nxt, w, bias):
    tt = x.shape[0]
    row = lax.broadcasted_iota(jnp.int32, x.shape, 0)
    xm = jnp.where(row == 0, prev, pltpu.roll(x, 1, axis=0))
    xp = jnp.where(row == tt - 1, nxt, pltpu.roll(x, tt - 1, axis=0))
    return xm * w[0:1] + x * w[1:2] + xp * w[2:3] + bias


def _halo_specs(tt, width, col_block, t):
    nb = t // HALO
    prev = pl.BlockSpec((1, HALO, width), lambda i, j: (i, jnp.maximum(j * (tt // HALO) - 1, 0), col_block))
    nxt = pl.BlockSpec((1, HALO, width), lambda i, j: (i, jnp.minimum((j + 1) * (tt // HALO), nb - 1), col_block))
    return prev, nxt


def _ml_prep_kernel(q_ref, qp_ref, qn_ref, k_ref, kp_ref, kn_ref, g_ref, w_ref, b_ref, gb_ref, q_o, k_o, g_o):
    j = pl.program_id(1)
    nblk = pl.num_programs(1)
    w = ML_HEADS * ML_DIM
    qp, qn = _halo_rows(qp_ref, qn_ref, j, nblk)
    kp, kn = _halo_rows(kp_ref, kn_ref, j, nblk)
    q = _conv3(q_ref[0].astype(F32), qp, qn, w_ref[:, :w], b_ref[:, :w])
    k = _conv3(k_ref[0].astype(F32), kp, kn, w_ref[:, w:], b_ref[:, w:])
    q_o[0] = q * jax.nn.sigmoid(q)
    k_o[0] = (k * jax.nn.sigmoid(k)) * ML_DIM ** -0.5
    x = g_ref[0].astype(F32) + gb_ref[...]
    lane = lax.broadcasted_iota(jnp.int32, x.shape, 1)
    log_sig = jnp.minimum(x, 0.0) - jnp.log(1.0 + jnp.exp(-jnp.abs(x)))
    g_o[0] = jnp.where((lane % 8) >= ML_HEADS, log_sig, x)


def ml_prep(p, conv_w, conv_b, gate_b):
    b, t, _ = p.shape
    tt = _tile(t, ROW_TILE)
    w = ML_HEADS * ML_DIM
    blk = lambda c: pl.BlockSpec((1, tt, w), lambda i, j, c=c: (i, j, c // w))
    qp, qn = _halo_specs(tt, w, C_MLQ // w, t)
    kp, kn = _halo_specs(tt, w, C_MLK // w, t)
    gb = jnp.zeros((1, LANES), F32).at[0, :4 * ML_HEADS].set(gate_b.reshape(-1))
    return pl.pallas_call(
        _ml_prep_kernel,
        grid=(b, t // tt),
        in_specs=[blk(C_MLQ), qp, qn, blk(C_MLK), kp, kn,
                  pl.BlockSpec((1, tt, LANES), lambda i, j: (i, j, C_MLG // LANES)),
                  pl.BlockSpec((3, 2 * w), lambda i, j: (0, 0)),
                  pl.BlockSpec((1, 2 * w), lambda i, j: (0, 0)),
                  pl.BlockSpec((1, LANES), lambda i, j: (0, 0))],
        out_specs=[pl.BlockSpec((1, tt, w), lambda i, j: (i, j, 0)),
                   pl.BlockSpec((1, tt, w), lambda i, j: (i, j, 0)),
                   pl.BlockSpec((1, tt, LANES), lambda i, j: (i, j, 0))],
        out_shape=[jax.ShapeDtypeStruct((b, t, w), F32), jax.ShapeDtypeStruct((b, t, w), F32),
                   jax.ShapeDtypeStruct((b, t, LANES), F32)],
        compiler_params=_cparams(("parallel", "parallel")),
        name="ml_prep",
    )(p, p, p, p, p, p, p, conv_w, conv_b.reshape(1, 2 * w), gb)


_NT = (((1,), (1,)), ((), ()))


def _ml_chunk_head(q, k, v, li_c, bc_c, bc_r, b_last, seen, ct_ref, n_ref, m_ref):
    m = m_ref[:, 0:1]
    ct = ct_ref[...]
    n_rows = n_ref[...]
    qb, kb, vb = q.astype(BF16), k.astype(BF16), v.astype(BF16)
    d_t = jnp.where(seen, bc_r + (li_c - bc_c), -jnp.inf)
    inter = bc_r + m
    m_t = jnp.maximum(inter, jnp.max(d_t, axis=0, keepdims=True))
    w_inter = jnp.exp(inter - m_t)
    s_t = lax.dot_general(kb, qb, _NT, preferred_element_type=F32) * jnp.exp(d_t - m_t)
    num_t = (jnp.dot(vb.T, s_t.astype(BF16), preferred_element_type=F32)
             + w_inter * lax.dot_general(ct.astype(BF16), qb, _NT, preferred_element_type=F32))
    qn = lax.dot_general(n_rows.astype(BF16), qb, _NT, preferred_element_type=F32)[0:1]
    den = jnp.sum(s_t, axis=0, keepdims=True) + w_inter * qn
    h_t = num_t / jnp.maximum(jnp.abs(den), jnp.exp(-m_t))
    g = b_last - bc_c + li_c
    m_new = jnp.maximum(b_last + m, jnp.max(g, axis=0, keepdims=True))
    kw = k * jnp.exp(g - m_new)
    wc = jnp.exp(b_last + m - m_new)
    ct_ref[...] = wc * ct + jnp.dot(vb.T, kw.astype(BF16), preferred_element_type=F32)
    n_ref[...] = wc * n_rows + jnp.sum(kw, axis=0, keepdims=True)
    m_ref[...] = jnp.broadcast_to(m_new, m_ref.shape)
    return h_t.T


def _ml_scan_kernel(q_ref, k_ref, v_ref, g_ref, c0_ref, n0_ref, m0_ref, h_o, c1_o, n1_o, m1_o,
                    c_sc, n_sc, m_sc, *, direction):
    j = pl.program_id(0)

    @pl.when(j == 0)
    def _():
        c_sc[...] = c0_ref[...]
        n_sc[...] = n0_ref[...]
        m_sc[...] = m0_ref[...]

    r = lax.broadcasted_iota(jnp.int32, (ML_CHUNK, ML_CHUNK), 0)
    s = lax.broadcasted_iota(jnp.int32, (ML_CHUNK, ML_CHUNK), 1)
    tri_f = ((s >= r) if direction else (s <= r)).astype(F32)
    seen = (r >= s) if direction else (r <= s)
    n_chunks = q_ref.shape[1] // ML_CHUNK
    order = range(n_chunks - 1, -1, -1) if direction else range(n_chunks)
    last = 0 if direction else ML_CHUNK - 1
    for c in order:
        rows = slice(c * ML_CHUNK, (c + 1) * ML_CHUNK)
        for bi in range(q_ref.shape[0]):
            gch = g_ref[bi, rows, :]
            bc = jnp.dot(tri_f, gch, preferred_element_type=F32, precision=lax.Precision.HIGHEST)
            bct = bc.T
            for hd in range(ML_HEADS):
                ci = 2 * ML_HEADS * direction + hd
                cf = ci + ML_HEADS
                cols = slice(hd * ML_DIM, (hd + 1) * ML_DIM)
                h = _ml_chunk_head(q_ref[bi, rows, cols], k_ref[bi, rows, cols], v_ref[bi, rows, cols],
                                   gch[:, ci:ci + 1], bc[:, cf:cf + 1], bct[cf:cf + 1, :],
                                   bc[last:last + 1, cf:cf + 1], seen,
                                   c_sc.at[bi, hd], n_sc.at[bi, hd], m_sc.at[bi, hd])
                h_o[bi, rows, cols] = h

    @pl.when(j == pl.num_programs(0) - 1)
    def _():
        c1_o[...] = c_sc[...]
        n1_o[...] = n_sc[...]
        m1_o[...] = m_sc[...]


def _ml_state_shapes(b):
    return [(b, ML_HEADS, ML_DIM, ML_DIM), (b, ML_HEADS, 8, ML_DIM), (b, ML_HEADS, 1, ML_DIM)]


def ml_scan(q, k, p, g, state, direction):
    b, t, w = q.shape
    ts = _tile(t, ML_STEP)
    nst = t // ts
    tok = (lambda j: (0, nst - 1 - j, 0)) if direction else (lambda j: (0, j, 0))
    tokv = (lambda j: (0, nst - 1 - j, C_MLV // w)) if direction else (lambda j: (0, j, C_MLV // w))
    st_shapes = _ml_state_shapes(b)
    st_specs = [pl.BlockSpec(s, lambda j: (0, 0, 0, 0)) for s in st_shapes]
    h, c1, n1, m1 = pl.pallas_call(
        functools.partial(_ml_scan_kernel, direction=direction),
        grid=(nst,),
        in_specs=[pl.BlockSpec((b, ts, w), tok), pl.BlockSpec((b, ts, w), tok), pl.BlockSpec((b, ts, w), tokv),
                  pl.BlockSpec((b, ts, LANES), tok)] + st_specs,
        out_specs=[pl.BlockSpec((b, ts, w), tok)] + st_specs,
        out_shape=[jax.ShapeDtypeStruct((b, t, w), F32)] + [jax.ShapeDtypeStruct(s, F32) for s in st_shapes],
        scratch_shapes=[pltpu.VMEM(s, F32) for s in st_shapes],
        compiler_params=_cparams(("arbitrary",)),
        name="ml_scan",
    )(q, k, p, g, *state)
    return h, (c1, n1, m1)


def mlstm_branch(p_c, p_l, conv_w, conv_b, gate_b, need_ctx):
    b = p_l.shape[0]
    qc, kc, gc = ml_prep(p_c, conv_w, conv_b, gate_b)
    ql, kl, gl = ml_prep(p_l, conv_w, conv_b, gate_b)
    zero = tuple(jnp.zeros(s, F32) for s in _ml_state_shapes(b))
    hs_c, hs_l = [], []
    for direction in (0, 1):
        hc, st = ml_scan(qc, kc, p_c, gc, zero, direction)
        hl, _ = ml_scan(ql, kl, p_l, gl, st, direction)
        hs_c.append(hc)
        hs_l.append(hl)
    return (tuple(hs_c) if need_ctx else None), tuple(hs_l)


def _merge_kernel(ya_ref, hf_ref, hb_ref, yc_ref, yd_ref, og_ref, g_ref, mg_ref, wb_ref, wo_ref, x_ref, gate_ref,
                  o_ref):
    d = x_ref.shape[-1]
    mg = mg_ref[...]
    yb = []
    for hd in range(ML_HEADS):
        cols = slice(hd * ML_DIM, (hd + 1) * ML_DIM)
        h = hf_ref[0, :, cols] + hb_ref[0, :, cols]
        h = h * lax.rsqrt(jnp.mean(h * h, axis=-1, keepdims=True) + EPS) * mg
        yb.append((h * jax.nn.sigmoid(og_ref[0, :, cols].astype(F32))).astype(BF16))
    ys = (ya_ref[0].astype(BF16), jnp.concatenate(yb, axis=-1), yc_ref[0].astype(BF16), yd_ref[0].astype(BF16))
    acc = None
    for n, y in enumerate(ys):
        t = jnp.dot(y, wb_ref[n], preferred_element_type=F32)
        t = jax.nn.sigmoid(g_ref[0, :, n * d:(n + 1) * d].astype(F32)) * t
        acc = t if acc is None else acc + t
    z = jnp.dot(acc.astype(BF16), wo_ref[...], preferred_element_type=F32)
    o_ref[0] = x_ref[0] + gate_ref[0] * z


def merge(ya, hs, yc, yd, p, ml_norm_g, w_branch, w_out, x, gate):
    b, t, d = x.shape
    tm = _tile(t, ROW_TILE)
    w = ML_HEADS * ML_DIM
    ysp = pl.BlockSpec((1, tm, w), lambda i, j: (i, j, 0))
    return pl.pallas_call(
        _merge_kernel,
        grid=(b, t // tm),
        in_specs=[ysp, ysp, ysp, ysp, ysp,
                  pl.BlockSpec((1, tm, w), lambda i, j: (i, j, C_MLO // w)),
                  pl.BlockSpec((1, tm, N_BRANCH * d), lambda i, j: (i, j, C_GATE // (N_BRANCH * d))),
                  pl.BlockSpec((1, ML_DIM), lambda i, j: (0, 0)),
                  pl.BlockSpec((N_BRANCH, w, d), lambda i, j: (0, 0, 0)),
                  pl.BlockSpec((d, d), lambda i, j: (0, 0)),
                  pl.BlockSpec((1, tm, d), lambda i, j: (i, j, 0)),
                  pl.BlockSpec((1, 1, d), lambda i, j: (i, 0, 0))],
        out_specs=pl.BlockSpec((1, tm, d), lambda i, j: (i, j, 0)),
        out_shape=jax.ShapeDtypeStruct((b, t, d), F32),
        compiler_params=_cparams(("parallel", "parallel")),
        name="merge",
    )(ya, hs[0], hs[1], yc, yd, p, p, ml_norm_g.reshape(1, ML_DIM), w_branch, w_out, x, gate.reshape(b, 1, d))


def _expert_kernel(be_ref, x_ref, wg_ref, wu_ref, wd_ref, o_ref, wg_sc, wu_sc, wd_sc):
    i = pl.program_id(0)

    @pl.when(jnp.logical_or(i == 0, be_ref[i] != be_ref[jnp.maximum(i - 1, 0)]))
    def _():
        wg_sc[...] = wg_ref[0, 0].astype(BF16)
        wu_sc[...] = wu_ref[0, 0].astype(BF16)
        wd_sc[...] = wd_ref[0, 0].astype(BF16)

    x = x_ref[...]
    a = jnp.dot(x, wg_sc[...], preferred_element_type=F32)
    u = jnp.dot(x, wu_sc[...], preferred_element_type=F32)
    h = (a * jax.nn.sigmoid(a)) * u
    o_ref[...] = jnp.dot(h.astype(BF16), wd_sc[...], preferred_element_type=F32).astype(o_ref.dtype)


def expert_blocks(xb, blk_exp, layer, w_gate, w_up, w_down):
    m, d = xb.shape
    hdim = w_gate.shape[-1]
    n_blocks = m // MOE_BLOCK
    grid_spec = pltpu.PrefetchScalarGridSpec(
        num_scalar_prefetch=1,
        grid=(n_blocks,),
        in_specs=[pl.BlockSpec((MOE_BLOCK, d), lambda i, be: (i, 0)),
                  pl.BlockSpec((1, 1, d, hdim), lambda i, be: (layer, be[i], 0, 0)),
                  pl.BlockSpec((1, 1, d, hdim), lambda i, be: (layer, be[i], 0, 0)),
                  pl.BlockSpec((1, 1, hdim, d), lambda i, be: (layer, be[i], 0, 0))],
        out_specs=pl.BlockSpec((MOE_BLOCK, d), lambda i, be: (i, 0)),
        scratch_shapes=[pltpu.VMEM((d, hdim), BF16), pltpu.VMEM((d, hdim), BF16), pltpu.VMEM((hdim, d), BF16)],
    )
    return pl.pallas_call(
        _expert_kernel,
        grid_spec=grid_spec,
        out_shape=jax.ShapeDtypeStruct((m, d), BF16),
        compiler_params=_cparams(("arbitrary",)),
        name="moe_experts",
    )(blk_exp, xb, w_gate, w_up, w_down)


def _rope_tables(n_tok):
    rows = n_tok // GRID_W
    row = jnp.repeat(jnp.arange(rows, dtype=F32), GRID_W)
    col = jnp.broadcast_to(jnp.arange(GRID_W, dtype=F32), (rows, GRID_W)).reshape(-1)
    n_freq = DIFF_DIM // 4
    inv = ROPE_BASE ** (-jnp.arange(n_freq, dtype=F32) / n_freq)
    ar = row[:, None] * inv
    ac = col[:, None] * inv
    ang = jnp.concatenate([ar, ar, ac, ac], axis=-1)
    cos, sin = jnp.cos(ang), jnp.sin(ang)
    first = (jnp.arange(DIFF_DIM) % 32) < 16
    sa = jnp.where(first, -sin, 0.0)
    sb = jnp.where(first, 0.0, sin)
    return tuple(jnp.tile(a, (1, 2)) for a in (cos, sa, sb))


DFT_N2 = 256
DFT_J = 8
DFT_P = 4


def _hy_prep_kernel(*refs):
    ins, outs = refs[:9], refs[11:]
    w_ref, b_ref = refs[9], refs[10]
    j = pl.program_id(1)
    nblk = pl.num_programs(1)
    for n in range(HY_ORDER + 1):
        x_ref, p_ref, n_ref = ins[3 * n:3 * n + 3]
        cols = slice(n * HY_CH, (n + 1) * HY_CH)
        prev, nxt = _halo_rows(p_ref, n_ref, j, nblk)
        outs[n][0] = _conv3(x_ref[0].astype(F32), prev, nxt, w_ref[:, cols], b_ref[:, cols])


def hy_prep(p, conv_w, conv_b):
    b, t, _ = p.shape
    tt = _tile(t, ROW_TILE)
    in_specs, args = [], []
    for n in range(HY_ORDER + 1):
        cb = C_HY // HY_CH + n
        prev, nxt = _halo_specs(tt, HY_CH, cb, t)
        in_specs += [pl.BlockSpec((1, tt, HY_CH), lambda i, j, cb=cb: (i, j, cb)), prev, nxt]
        args += [p, p, p]
    nch = (HY_ORDER + 1) * HY_CH
    in_specs += [pl.BlockSpec((3, nch), lambda i, j: (0, 0)), pl.BlockSpec((1, nch), lambda i, j: (0, 0))]
    osp = pl.BlockSpec((1, tt, HY_CH), lambda i, j: (i, j, 0))
    return pl.pallas_call(
        _hy_prep_kernel,
        grid=(b, t // tt),
        in_specs=in_specs,
        out_specs=[osp] * (HY_ORDER + 1),
        out_shape=[jax.ShapeDtypeStruct((b, t, HY_CH), F32)] * (HY_ORDER + 1),
        compiler_params=_cparams(("parallel", "parallel")),
        name="hy_prep",
    )(*args, conv_w, conv_b.reshape(1, nch))


def _hy_filter_kernel(emb_ref, w1_ref, b1_ref, w2_ref, b2_ref, w3_ref, fr_ref, al_ref, f_o, ss_o, *, length, half):
    j = pl.program_id(0)
    tt = emb_ref.shape[0]
    a = jnp.dot(emb_ref[...].astype(BF16), w1_ref[...].astype(BF16), preferred_element_type=F32) + b1_ref[...]
    a = jnp.sin(fr_ref[0:1, :] * a)
    a = jnp.dot(a.astype(BF16), w2_ref[...].astype(BF16), preferred_element_type=F32) + b2_ref[...]
    a = jnp.sin(fr_ref[1:2, :] * a)
    filt = jnp.dot(a.astype(BF16), w3_ref[...].astype(BF16), preferred_element_type=F32)
    r = lax.broadcasted_iota(jnp.int32, (tt, HY_CH), 0)
    if half:
        row = DFT_N2 * (r & (half - 1)) + j * (tt // half) + (r >> (half.bit_length() - 1))
    else:
        row = r + j * tt
    window = jnp.exp(-(row.astype(F32) / length) * al_ref[...]) + FILTER_SHIFT

    @pl.when(j == 0)
    def _():
        ss_o[...] = jnp.zeros(ss_o.shape, F32)

    for o in range(HY_ORDER):
        for d in range(2):
            idx = 2 * o + d
            f = filt[:, idx * HY_CH:(idx + 1) * HY_CH] * window
            if d == 1:
                f = jnp.where(row == 0, 0.0, f)
            if half:
                for jj in range(tt // half):
                    f_o[idx, :, jj * HY_CH:(jj + 1) * HY_CH] = f[jj * half:(jj + 1) * half]
            else:
                f_o[idx] = f
            ss_o[o:o + 1, :] += jnp.sum(f * f, axis=0, keepdims=True)


def hy_filters(length, w1, b1, w2, b2, w3, freq, wide):
    t = jnp.arange(length, dtype=F32) / length
    bands = jnp.arange(1, FILTER_BANDS + 1, dtype=F32)
    ang = 2.0 * math.pi * t[:, None] * bands
    emb = jnp.concatenate([t[:, None], jnp.cos(ang), jnp.sin(ang)], axis=-1)
    pad = LANES - emb.shape[1]
    emb = jnp.pad(emb, ((0, 0), (0, pad)))
    w1 = jnp.pad(w1, ((0, pad), (0, 0)))
    ne, nh = emb.shape[1], w1.shape[1]
    alpha = jnp.linspace(abs(math.log(DECAY_TARGET)) / SLOW_DECAY_PCT,
                         abs(math.log(DECAY_TARGET)) / FAST_DECAY_PCT, HY_CH).reshape(1, HY_CH)
    tt = _tile(length, ROW_TILE)
    half = length // DFT_N2
    wide = wide and half > 0 and half & (half - 1) == 0 and tt % half == 0
    full = lambda shape: pl.BlockSpec(shape, lambda j: (0,) * len(shape))
    if wide:
        emb = jnp.swapaxes(emb.reshape(half, DFT_N2, ne), 0, 1).reshape(length, ne)
        f_spec = pl.BlockSpec((2 * HY_ORDER, half, (tt // half) * HY_CH), lambda j: (0, 0, j))
        f_shape = (2 * HY_ORDER, half, DFT_N2 * HY_CH)
    else:
        f_spec = pl.BlockSpec((2 * HY_ORDER, tt, HY_CH), lambda j: (0, j, 0))
        f_shape = (2 * HY_ORDER, length, HY_CH)
    f, ss = pl.pallas_call(
        functools.partial(_hy_filter_kernel, length=length, half=half if wide else 0),
        grid=(length // tt,),
        in_specs=[pl.BlockSpec((tt, ne), lambda j: (j, 0)), full((ne, nh)), full((1, nh)), full((nh, nh)),
                  full((1, nh)), full((nh, 2 * HY_ORDER * HY_CH)), full((2, nh)), full((1, HY_CH))],
        out_specs=[f_spec, full((HY_ORDER, HY_CH))],
        out_shape=[jax.ShapeDtypeStruct(f_shape, F32), jax.ShapeDtypeStruct((HY_ORDER, HY_CH), F32)],
        compiler_params=_cparams(("arbitrary",)),
        name="hy_filter",
    )(emb, w1, b1.reshape(1, nh), w2, b2.reshape(1, nh), w3, freq, alpha)
    return f, lax.rsqrt(ss + EPS)


def _dft_tables(length):
    n = 2 * length
    n1 = n // DFT_N2
    half = n1 // 2
    n1h = -(-(half + 1) // 16) * 16
    kv = jnp.arange(n1h, dtype=jnp.int32)
    valid = (kv <= half).astype(F32)[None, :, None]
    pair = jnp.where((kv == 0) | (kv == half), 1.0, 2.0)[None, :, None] * valid
    k1 = kv[None, :, None]
    tn = (DFT_N2 * jnp.arange(half, dtype=jnp.int32)[None, None, :]
          + jnp.arange(DFT_N2, dtype=jnp.int32)[:, None, None])
    th = (2.0 * math.pi / n) * ((k1 * tn) % n).astype(F32)
    ga = jnp.concatenate([jnp.cos(th) * valid, -jnp.sin(th) * valid], axis=1).astype(BF16)
    gi = jnp.swapaxes(jnp.concatenate([jnp.cos(th) * pair, -jnp.sin(th) * pair], axis=1), 1, 2).astype(BF16)
    kk = jnp.arange(DFT_N2, dtype=jnp.int32)
    t2 = (2.0 * math.pi / DFT_N2) * ((kk[:, None] * kk[None, :]) % DFT_N2).astype(F32)
    c2, s2 = jnp.cos(t2), jnp.sin(t2)
    mf = jnp.block([[c2, s2], [-s2, c2]]).astype(BF16)
    mi = jnp.block([[c2, -s2], [s2, c2]]).astype(BF16)
    return ga, gi, mf, mi


def _dft_a_kernel(z_ref, g_ref, o_ref):
    c = HY_CH
    for j in range(DFT_J):
        slab = z_ref[0, :, j * c:(j + 1) * c].astype(BF16)
        r = jnp.dot(g_ref[j], slab, preferred_element_type=F32)
        o_ref[0, :, :, j * c:(j + 1) * c] = r.reshape(2, r.shape[0] // 2, c).astype(o_ref.dtype)


def dft_a(z, ga):
    bz, half, wid = z.shape
    n1 = ga.shape[1] // 2
    jc = DFT_J * HY_CH
    return pl.pallas_call(
        _dft_a_kernel,
        grid=(bz, DFT_N2 // DFT_J),
        in_specs=[pl.BlockSpec((1, half, jc), lambda i, j: (i, 0, j)),
                  pl.BlockSpec((DFT_J, 2 * n1, half), lambda i, j: (j, 0, 0))],
        out_specs=pl.BlockSpec((1, 2, n1, jc), lambda i, j: (i, 0, 0, j)),
        out_shape=jax.ShapeDtypeStruct((bz, 2, n1, wid), BF16),
        compiler_params=_cparams(("parallel", "parallel")),
        name="dft_a",
    )(z, ga)


def _stack_ri(ref, b, k):
    return jnp.concatenate([ref[b, 0, k], ref[b, 1, k]], axis=0)


def _spec_filter_kernel(f_ref, mf_ref, sc_ref, h_o):
    sc = sc_ref[0]
    for k in range(DFT_P):
        xf = jnp.dot(mf_ref[...], _stack_ri(f_ref, 0, k), preferred_element_type=F32)
        xb = jnp.dot(mf_ref[...], _stack_ri(f_ref, 1, k), preferred_element_type=F32)
        h_o[0, k, 0] = ((xf[:DFT_N2] + xb[:DFT_N2]) * sc).astype(h_o.dtype)
        h_o[0, k, 1] = ((xf[DFT_N2:] - xb[DFT_N2:]) * sc).astype(h_o.dtype)


def spec_filter(fa, mf, scale):
    nb, _, n1, _, c = fa.shape
    order = nb // 2
    return pl.pallas_call(
        _spec_filter_kernel,
        grid=(order, n1 // DFT_P),
        in_specs=[pl.BlockSpec((2, 2, DFT_P, DFT_N2, c), lambda o, k: (o, 0, k, 0, 0)),
                  pl.BlockSpec((2 * DFT_N2, 2 * DFT_N2), lambda o, k: (0, 0)),
                  pl.BlockSpec((1, 1, c), lambda o, k: (o, 0, 0))],
        out_specs=pl.BlockSpec((1, DFT_P, 2, DFT_N2, c), lambda o, k: (o, k, 0, 0, 0)),
        out_shape=jax.ShapeDtypeStruct((order, n1, 2, DFT_N2, c), BF16),
        compiler_params=_cparams(("parallel", "parallel")),
        name="spec_filter",
    )(fa, mf, scale.reshape(order, 1, c))


def _spec_conv_kernel(a_ref, h_ref, mf_ref, mi_ref, o_ref):
    for k in range(DFT_P):
        x = jnp.dot(mf_ref[...], _stack_ri(a_ref, 0, k), preferred_element_type=F32)
        xr, xi = x[:DFT_N2], x[DFT_N2:]
        hr, hi = h_ref[0, k, 0].astype(F32), h_ref[0, k, 1].astype(F32)
        y = jnp.concatenate([xr * hr - xi * hi, xr * hi + xi * hr], axis=0).astype(BF16)
        z = jnp.dot(mi_ref[...], y, preferred_element_type=F32)
        o_ref[0, 0, k] = z[:DFT_N2].astype(o_ref.dtype)
        o_ref[0, 1, k] = z[DFT_N2:].astype(o_ref.dtype)


def spec_conv(a, h, order, mf, mi):
    b, _, n1, _, c = a.shape
    blk = pl.BlockSpec((1, 2, DFT_P, DFT_N2, c), lambda i, k: (i, 0, k, 0, 0))
    mat = pl.BlockSpec((2 * DFT_N2, 2 * DFT_N2), lambda i, k: (0, 0))
    return pl.pallas_call(
        _spec_conv_kernel,
        grid=(b, n1 // DFT_P),
        in_specs=[blk, pl.BlockSpec((1, DFT_P, 2, DFT_N2, c), lambda i, k: (order, k, 0, 0, 0)), mat, mat],
        out_specs=blk,
        out_shape=jax.ShapeDtypeStruct(a.shape, BF16),
        compiler_params=_cparams(("parallel", "parallel")),
        name="spec_conv",
    )(a, h, mf, mi)


def _dft_ainv_kernel(z_ref, g_ref, xg_ref, zin_ref, bias_ref, o_ref):
    c = HY_CH
    n1 = z_ref.shape[2]
    for j in range(DFT_J):
        cols = slice(j * c, (j + 1) * c)
        zz = z_ref[0, :, :, cols].reshape(2 * n1, c)
        y = jnp.dot(g_ref[j], zz, preferred_element_type=F32)
        o_ref[0, :, cols] = xg_ref[0, :, cols] * (y + bias_ref[...] * zin_ref[0, :, cols])


def dft_ainv(z, gi, xg, zin, bias):
    b, _, n1, wid = z.shape
    half = gi.shape[1]
    jc = DFT_J * HY_CH
    tok = pl.BlockSpec((1, half, jc), lambda i, j: (i, 0, j))
    return pl.pallas_call(
        _dft_ainv_kernel,
        grid=(b, DFT_N2 // DFT_J),
        in_specs=[pl.BlockSpec((1, 2, n1, jc), lambda i, j: (i, 0, 0, j)),
                  pl.BlockSpec((DFT_J, half, 2 * n1), lambda i, j: (j, 0, 0)),
                  tok, tok, pl.BlockSpec((1, HY_CH), lambda i, j: (0, 0))],
        out_specs=tok,
        out_shape=jax.ShapeDtypeStruct((b, half, wid), F32),
        compiler_params=_cparams(("parallel", "parallel")),
        name="dft_ainv",
    )(z, gi, xg, zin, bias.reshape(1, HY_CH))


def _ctx_conv_kernel(z_ref, xg_ref, f_ref, mf_ref, mi_ref, sc_ref, bias_ref, o_ref):
    nf = mf_ref.shape[0] // 2
    mf = mf_ref[...]
    xf = jnp.dot(mf, f_ref[0].astype(BF16), preferred_element_type=F32)
    xb = jnp.dot(mf, f_ref[1].astype(BF16), preferred_element_type=F32)
    sc = sc_ref[0]
    hr = (xf[:nf] + xb[:nf]) * sc
    hi = (xf[nf:] - xb[nf:]) * sc
    z = z_ref[0]
    x = jnp.dot(mf, z.astype(BF16), preferred_element_type=F32)
    xr, xi = x[:nf], x[nf:]
    y = jnp.concatenate([xr * hr - xi * hi, xr * hi + xi * hr], axis=0).astype(BF16)
    o_ref[0] = xg_ref[0] * (jnp.dot(mi_ref[...], y, preferred_element_type=F32) + bias_ref[...] * z)


def ctx_conv(z, xg, f, order, scale, bias):
    b, length, c = z.shape
    n = 2 * length
    kk = jnp.arange(n, dtype=jnp.int32)[:, None]
    tn = jnp.arange(length, dtype=jnp.int32)[None, :]
    th = (2.0 * math.pi / n) * ((kk * tn) % n).astype(F32)
    mf = jnp.concatenate([jnp.cos(th), -jnp.sin(th)], axis=0).astype(BF16)
    mi = jnp.concatenate([jnp.cos(th.T), -jnp.sin(th.T)], axis=1).astype(BF16)
    tok = pl.BlockSpec((1, length, c), lambda i: (i, 0, 0))
    return pl.pallas_call(
        _ctx_conv_kernel,
        grid=(b,),
        in_specs=[tok, tok, pl.BlockSpec((2, length, c), lambda i: (order, 0, 0)),
                  pl.BlockSpec((2 * n, length), lambda i: (0, 0)), pl.BlockSpec((length, 2 * n), lambda i: (0, 0)),
                  pl.BlockSpec((1, 1, c), lambda i: (order, 0, 0)), pl.BlockSpec((1, c), lambda i: (0, 0))],
        out_specs=tok,
        out_shape=jax.ShapeDtypeStruct((b, length, c), F32),
        compiler_params=_cparams(("parallel",)),
        name="ctx_conv",
    )(z, xg, f, mf, mi, scale.reshape(-1, 1, c), bias.reshape(1, c))


def hyena(p, conv_w, conv_b, w1, b1, w2, b2, w3, freq, bias, tables):
    b, length, _ = p.shape
    parts = hy_prep(p, conv_w, conv_b)
    f, rnorm = hy_filters(length, w1, b1, w2, b2, w3, freq, wide=tables is not None)
    scale = rnorm / (2 * length)
    z = parts[0]
    if tables is None:
        for o in range(HY_ORDER):
            z = ctx_conv(z, parts[o + 1], f, o, scale, bias[o])
        return z
    ga, gi, mf, mi = tables
    half = ga.shape[2]
    wid = DFT_N2 * HY_CH
    fa = dft_a(f.reshape(2 * HY_ORDER, half, wid), ga)
    n1 = fa.shape[2]
    h = spec_filter(fa.reshape(2 * HY_ORDER, 2, n1, DFT_N2, HY_CH), mf, scale)
    for o in range(HY_ORDER):
        a = dft_a(z.reshape(b, half, wid), ga).reshape(b, 2, n1, DFT_N2, HY_CH)
        zc = spec_conv(a, h, o, mf, mi).reshape(b, 2, n1, wid)
        z = dft_ainv(zc, gi, parts[o + 1].reshape(b, half, wid), z.reshape(b, half, wid), bias[o])
        z = z.reshape(b, length, HY_CH)
    return z


ROUTE_ROWS = 8


def _router_kernel(x_ref, g_ref, sh_ref, sc_ref, w_ref, b_ref, h_o, id_o, gate_o):
    x = x_ref[0]
    y = x * lax.rsqrt(jnp.mean(x * x, axis=-1, keepdims=True) + EPS)
    y = ((y * g_ref[...]) * (1.0 + sc_ref[0]) + sh_ref[0]).astype(BF16)
    h_o[0] = y
    logits = jnp.dot(y, w_ref[...], preferred_element_type=F32) + b_ref[...]
    lane = lax.broadcasted_iota(jnp.int32, logits.shape, 1)
    lane_f = lane.astype(F32)
    none = float(LANES)

    def top(vals):
        v = jnp.max(vals, axis=1, keepdims=True)
        return v, jnp.min(jnp.where(vals == v, lane_f, none), axis=1, keepdims=True)

    is_grp = lane < MOE_GROUPS
    mg, grp = top(jnp.where(is_grp, logits, -jnp.inf))
    p_grp = 1.0 / jnp.sum(jnp.where(is_grp, jnp.exp(logits - mg), 0.0), axis=1, keepdims=True)
    lo = MOE_GROUPS + MOE_EPG * grp
    el = jnp.where((lane_f >= lo) & (lane_f < lo + MOE_EPG), logits, -jnp.inf)
    v1, i1 = top(el)
    v2, i2 = top(jnp.where(lane_f == i1, -jnp.inf, el))
    t = jnp.exp(v2 - v1)
    g1 = p_grp / (1.0 + t)
    ids = jnp.where(lane == 0, i1 - MOE_GROUPS, jnp.where(lane == 1, i2 - MOE_GROUPS, 0.0))
    id_o[0, 0] = ids.T[:ROUTE_ROWS].astype(jnp.int32)
    gate_o[0, 0] = jnp.where(lane == 0, g1, jnp.where(lane == 1, g1 * t, 0.0)).T[:ROUTE_ROWS]


def moe_router(x, g, shift, scale, w_group, b_group, w_router, b_router):
    b, t, d = x.shape
    tt = _tile(t, ROW_TILE)
    rows = pl.BlockSpec((1, 1, ROUTE_ROWS, tt), lambda i, j: (i, j, 0, 0))
    npad = LANES - MOE_GROUPS - MOE_EXPERTS
    w = jnp.concatenate([w_group, w_router, jnp.zeros((d, npad), F32)], axis=1).astype(BF16)
    bias = jnp.concatenate([b_group, b_router, jnp.zeros((npad,), F32)]).reshape(1, LANES)
    tok = lambda width: pl.BlockSpec((1, tt, width), lambda i, j: (i, j, 0))
    mod = pl.BlockSpec((1, 1, d), lambda i, j: (i, 0, 0))
    return pl.pallas_call(
        _router_kernel,
        grid=(b, t // tt),
        in_specs=[tok(d), pl.BlockSpec((1, d), lambda i, j: (0, 0)), mod, mod,
                  pl.BlockSpec((d, LANES), lambda i, j: (0, 0)), pl.BlockSpec((1, LANES), lambda i, j: (0, 0))],
        out_specs=[tok(d), rows, rows],
        out_shape=[jax.ShapeDtypeStruct((b, t, d), BF16),
                   jax.ShapeDtypeStruct((b, t // tt, ROUTE_ROWS, tt), jnp.int32),
                   jax.ShapeDtypeStruct((b, t // tt, ROUTE_ROWS, tt), F32)],
        compiler_params=_cparams(("parallel", "parallel")),
        name="moe_router",
    )(x, g.reshape(1, d), shift.reshape(b, 1, d), scale.reshape(b, 1, d), w, bias)


def _blocked_cumsum(onehot, blk=256):
    m, e = onehot.shape
    if m % blk:
        return jnp.cumsum(onehot, axis=0)
    oh = onehot.astype(BF16).reshape(m // blk, blk, e)
    tril = jnp.tril(jnp.ones((blk, blk), BF16))
    within = jnp.einsum('ts,bse->bte', tril, oh, preferred_element_type=F32)
    tot = within[:, -1, :]
    off = jnp.cumsum(tot, axis=0) - tot
    return (within + off[:, None, :]).reshape(m, e).astype(jnp.int32)


def _hier_moe(h, ids, layer, w_gate, w_up, w_down):
    n_tok, d = h.shape
    e_flat = ids.reshape(-1)
    m_slots = n_tok * MOE_TOP_K
    onehot = (e_flat[:, None] == jnp.arange(MOE_EXPERTS, dtype=jnp.int32)[None, :]).astype(jnp.int32)
    csum = _blocked_cumsum(onehot)
    rank = jnp.sum(onehot * csum, axis=1) - 1
    counts = csum[-1]
    padded = (counts + MOE_BLOCK - 1) // MOE_BLOCK * MOE_BLOCK
    p_end = jnp.cumsum(padded)
    dest = (p_end - padded)[e_flat] + rank
    n_blocks = -(-(m_slots + MOE_EXPERTS * (MOE_BLOCK - 1)) // MOE_BLOCK)
    slot_tok = jnp.arange(m_slots, dtype=jnp.int32) // MOE_TOP_K
    buf_tok = jnp.zeros((n_blocks * MOE_BLOCK,), jnp.int32).at[dest].set(slot_tok)
    first_row = jnp.arange(n_blocks, dtype=jnp.int32)[:, None] * MOE_BLOCK
    blk_exp = jnp.minimum(jnp.sum((p_end[None, :] <= first_row).astype(jnp.int32), axis=1), MOE_EXPERTS - 1)
    cuts = [n_blocks * i // MOE_PIECES for i in range(MOE_PIECES + 1)]
    yb = jnp.concatenate([expert_blocks(h[buf_tok[a * MOE_BLOCK:e * MOE_BLOCK]], blk_exp[a:e], layer,
                                        w_gate, w_up, w_down) for a, e in zip(cuts[:-1], cuts[1:])], axis=0)
    return yb, dest.reshape(n_tok, MOE_TOP_K)


def _moe_combine_kernel(x_ref, y0_ref, y1_ref, g_ref, m_ref, o_ref):
    g = g_ref[0, 0].T
    f = g[:, 0:1] * y0_ref[...].astype(F32) + g[:, 1:2] * y1_ref[...].astype(F32)
    o_ref[0] = x_ref[0] + m_ref[0] * f


def moe_combine(x, yb, dest, gates, mod):
    b, t, d = x.shape
    y0 = yb[dest[:, 0]]
    y1 = yb[dest[:, 1]]
    nt, tt = gates.shape[1], gates.shape[3]
    row = pl.BlockSpec((tt, d), lambda i, j: (i * nt + j, 0))
    return pl.pallas_call(
        _moe_combine_kernel,
        grid=(b, nt),
        in_specs=[pl.BlockSpec((1, tt, d), lambda i, j: (i, j, 0)), row, row,
                  pl.BlockSpec((1, 1, ROUTE_ROWS, tt), lambda i, j: (i, j, 0, 0)),
                  pl.BlockSpec((1, 1, d), lambda i, j: (i, 0, 0))],
        out_specs=pl.BlockSpec((1, tt, d), lambda i, j: (i, j, 0)),
        out_shape=jax.ShapeDtypeStruct((b, t, d), F32),
        compiler_params=_cparams(("parallel", "parallel")),
        name="moe_combine",
    )(x, y0, y1, gates, mod.reshape(b, 1, d))


def _permute_w_in(w):
    d = w.shape[0]
    sizes = (DIFF_HEADS * 2 * DIFF_DIM, DIFF_HEADS * 2 * DIFF_DIM, DIFF_HEADS * DIFF_VDIM, 3 * ML_HEADS * ML_DIM,
             ML_HEADS * ML_DIM, 4 * ML_HEADS, GQA_HEADS * GQA_DIM, 2 * GQA_KV * GQA_DIM, (HY_ORDER + 1) * HY_CH,
             N_BRANCH * d)
    offs = np.cumsum((0,) + sizes)
    dq, dk, dv, mlqkv, mlo, mlg, gq, gkv, hy, gate = [w[:, offs[i]:offs[i + 1]] for i in range(10)]
    pad = jnp.zeros((d, N_P - C_MLG - 4 * ML_HEADS), w.dtype)
    return jnp.concatenate([gate, dq, dk, dv, mlqkv, mlo, gq, hy, gkv, mlg, pad], axis=1)


def kernel(x, c, ctx, c_ctx, w_ada, b_ada, norm1_g, norm2_g, w_in, diff_lam, diff_norm_g, ml_conv_w, ml_conv_b, ml_gate_b, ml_norm_g, gqa_qnorm_g, gqa_knorm_g, hy_conv_w, hy_conv_b, hy_f_w1, hy_f_b1, hy_f_w2, hy_f_b2, hy_f_w3, hy_f_freq, hy_bias, w_branch, w_out, moe_w_group, moe_b_group, moe_w_router, moe_b_router, moe_w_gate, moe_w_up, moe_w_down, final_norm_g):
    b, n, d = x.shape
    n_ctx = ctx.shape[1]
    depth = w_in.shape[0]
    tk = _tile(n, 2048)
    tables = _rope_tables(n)
    dft_tables = _dft_tables(n)
    sc = jax.nn.silu(c)
    scx = jax.nn.silu(c_ctx)
    xs, cs = x, ctx
    for l in range(depth):
        need_ctx = l < depth - 1
        mod_l = jnp.split(sc @ w_ada[l] + b_ada[l], 6, axis=-1)
        mod_c = [jnp.broadcast_to(m, (b, d)) for m in jnp.split(scx @ w_ada[l] + b_ada[l], 6, axis=-1)]
        w_p = _permute_w_in(w_in[l]).astype(BF16)
        hl = norm_mod(xs, norm1_g[l], mod_l[0], mod_l[1], BF16)
        hc = norm_mod(cs, norm1_g[l], mod_c[0], mod_c[1], BF16)
        p_l = matmul(hl.reshape(b * n, d), w_p, BF16, tm=1024).reshape(b, n, N_P)
        p_c = matmul(hc.reshape(b * n_ctx, d), w_p, BF16, tm=1024).reshape(b, n_ctx, N_P)

        dq_l, dkt_l, dv_l, gq_l, gkt_l, gv_l = attn_prep(p_l, tables, gqa_qnorm_g[l], gqa_knorm_g[l], tk)
        dq_c, dkt_c, dv_c, gq_c, gkt_c, gv_c = attn_prep(p_c, None, gqa_qnorm_g[l], gqa_knorm_g[l], n_ctx)
        lam_init = 0.8 - 0.6 * math.exp(-0.3 * l)
        lp = diff_lam[l].astype(F32)
        lam = jnp.exp(jnp.sum(lp[0] * lp[1])) - jnp.exp(jnp.sum(lp[2] * lp[3])) + lam_init
        yl_a = diff_attention(dq_l, dkt_c, dv_c, dkt_l, dv_l, lam, diff_norm_g[l], 1.0 - lam_init, 1024)
        yl_c = gqa_attention(gq_l, gkt_c, gv_c, gkt_l, gv_l, 1024)
        if need_ctx:
            yc_a = diff_attention(dq_c, dkt_c, dv_c, None, None, lam, diff_norm_g[l], 1.0 - lam_init, 256)
            yc_c = gqa_attention(gq_c, gkt_c, gv_c, None, None, 256)

        hs_c, hs_l = mlstm_branch(p_c, p_l, ml_conv_w[l], ml_conv_b[l], ml_gate_b[l], need_ctx)
        hy_args = (hy_conv_w[l], hy_conv_b[l], hy_f_w1[l], hy_f_b1[l], hy_f_w2[l], hy_f_b2[l],
                   hy_f_w3[l], hy_f_freq[l], hy_bias[l])
        yl_d = hyena(p_l, *hy_args, dft_tables)

        wb = w_branch[l].astype(BF16)
        wo = w_out[l].astype(BF16)
        xs = merge(yl_a, hs_l, yl_c, yl_d, p_l, ml_norm_g[l], wb, wo, xs, mod_l[2])
        route = (moe_w_group[l], moe_b_group[l], moe_w_router[l], moe_b_router[l])
        experts = (l, moe_w_gate, moe_w_up, moe_w_down)
        routed = [moe_router(xs, norm2_g[l], mod_l[3], mod_l[4], *route)]
        if need_ctx:
            yc_d = hyena(p_c, *hy_args, None)
            cs = merge(yc_a, hs_c, yc_c, yc_d, p_c, ml_norm_g[l], wb, wo, cs, mod_c[2])
            routed.insert(0, moe_router(cs, norm2_g[l], mod_c[3], mod_c[4], *route))
        h2 = jnp.concatenate([r[0].reshape(-1, d) for r in routed], axis=0)
        ids = jnp.concatenate([jnp.stack([r[1][:, :, k, :].reshape(-1) for k in range(MOE_TOP_K)], axis=1)
                               for r in routed], axis=0)
        yb, dest = _hier_moe(h2, ids, *experts)
        if need_ctx:
            nc = b * n_ctx
            cs = moe_combine(cs, yb, dest[:nc], routed[0][2], mod_c[5])
            dest = dest[nc:]
        xs = moe_combine(xs, yb, dest, routed[-1][2], mod_l[5])
    zero = jnp.zeros((b, d), F32)
    return norm_mod(xs, final_norm_g, zero, zero, F32)
```

```python
import functools
import math

import jax
import jax.numpy as jnp
import numpy as np
from jax import lax
from jax.experimental import pallas as pl
from jax.experimental.pallas import tpu as pltpu

F32 = jnp.float32
BF16 = jnp.bfloat16

EPS = 1e-6
ROPE_BASE = 10000.0
GRID_W = 64

DIFF_HEADS = 4
DIFF_DIM = 64
DIFF_VDIM = 128
ML_HEADS = 4
ML_DIM = 128
ML_CHUNK = 256
GQA_HEADS = 8
GQA_KV = 2
GQA_DIM = 64
HY_CH = 512
HY_ORDER = 2
FILTER_BANDS = 16
FILTER_SHIFT = 0.05
DECAY_TARGET = 1e-2
FAST_DECAY_PCT = 0.3
SLOW_DECAY_PCT = 1.5
N_BRANCH = 4
MOE_GROUPS = 4
MOE_EPG = 8
MOE_EXPERTS = MOE_GROUPS * MOE_EPG
MOE_TOP_K = 2
MOE_BLOCK = 256

LANES = 128
VMEM_LIMIT = 48 * 1024 * 1024
ROW_TILE = 512
GROUP_W = 512

C_GATE = 0
C_DQ = 4096
C_DK = 4608
C_DV = 5120
C_MLQ = 5632
C_MLK = 6144
C_MLV = 6656
C_MLO = 7168
C_GQ = 7680
C_HY = 8192
C_GK = 9728
C_GV = 9856
C_MLG = 9984
N_P = 10240

QSCALE = (DIFF_DIM ** -0.5) * math.log2(math.e)


def _cparams(sem):
    return pltpu.CompilerParams(dimension_semantics=sem, vmem_limit_bytes=VMEM_LIMIT)


def _tile(n, target):
    if n <= target:
        return n
    for t in range(target, 7, -1):
        if n % t == 0 and t % 8 == 0:
            return t
    return n


def _norm_mod_kernel(x_ref, g_ref, sh_ref, sc_ref, o_ref):
    x = x_ref[0]
    y = x * lax.rsqrt(jnp.mean(x * x, axis=-1, keepdims=True) + EPS)
    y = y * g_ref[...]
    o_ref[0] = (y * (1.0 + sc_ref[0]) + sh_ref[0]).astype(o_ref.dtype)


def norm_mod(x, g, shift, scale, out_dtype):
    b, t, d = x.shape
    tt = _tile(t, ROW_TILE)
    return pl.pallas_call(
        _norm_mod_kernel,
        grid=(b, t // tt),
        in_specs=[pl.BlockSpec((1, tt, d), lambda i, j: (i, j, 0)),
                  pl.BlockSpec((1, d), lambda i, j: (0, 0)),
                  pl.BlockSpec((1, 1, d), lambda i, j: (i, 0, 0)),
                  pl.BlockSpec((1, 1, d), lambda i, j: (i, 0, 0))],
        out_specs=pl.BlockSpec((1, tt, d), lambda i, j: (i, j, 0)),
        out_shape=jax.ShapeDtypeStruct((b, t, d), out_dtype),
        compiler_params=_cparams(("parallel", "parallel")),
        name="norm_mod",
    )(x, g.reshape(1, d), shift.reshape(b, 1, d), scale.reshape(b, 1, d))


def _mm_kernel(a_ref, w_ref, o_ref):
    o_ref[...] = jnp.dot(a_ref[...], w_ref[...], preferred_element_type=F32).astype(o_ref.dtype)


def matmul(a, w, out_dtype, tm=512, tn=1024):
    m, k = a.shape
    n = w.shape[1]
    tm = _tile(m, tm)
    tn = _tile(n, tn)
    return pl.pallas_call(
        _mm_kernel,
        grid=(m // tm, n // tn),
        in_specs=[pl.BlockSpec((tm, k), lambda i, j: (i, 0)),
                  pl.BlockSpec((k, tn), lambda i, j: (0, j))],
        out_specs=pl.BlockSpec((tm, tn), lambda i, j: (i, j)),
        out_shape=jax.ShapeDtypeStruct((m, n), out_dtype),
        compiler_params=_cparams(("parallel", "parallel")),
        name="matmul",
    )(a, w)


def _rope(x, cos, sa, sb):
    xa = pltpu.roll(x, LANES - 16, axis=1)
    xb = pltpu.roll(x, 16, axis=1)
    return x * cos + xa * sa + xb * sb


def _seg_rmsnorm(x, g):
    lane = lax.broadcasted_iota(jnp.int32, x.shape, 1)
    lo = lane < GQA_DIM
    ss = x * x
    s_lo = jnp.sum(jnp.where(lo, ss, 0.0), axis=-1, keepdims=True)
    s_hi = jnp.sum(jnp.where(lo, 0.0, ss), axis=-1, keepdims=True)
    r = jnp.where(lo, lax.rsqrt(s_lo * (1.0 / GQA_DIM) + EPS), lax.rsqrt(s_hi * (1.0 / GQA_DIM) + EPS))
    return x * r * g


def _prep_kernel(dq_ref, dk_ref, dv_ref, gq_ref, gkv_ref, cos_ref, sa_ref, sb_ref, qg_ref, kg_ref,
                 dq_o, dkt_o, dv_o, gq_o, gkt_o, gv_o, *, rope):
    if rope:
        cos, sa, sb = cos_ref[...], sa_ref[...], sb_ref[...]
        rot = lambda x: _rope(x, cos, sa, sb)
    else:
        rot = lambda x: x
    qg = qg_ref[...]
    kg = kg_ref[...]
    for j in range(DIFF_HEADS):
        sl = slice(j * LANES, (j + 1) * LANES)
        dq_o[0, :, sl] = (rot(dq_ref[0, :, sl].astype(F32)) * QSCALE).astype(BF16)
        kt = rot(dk_ref[0, :, sl].astype(F32)).T
        dkt_o[0, j, 0, 0] = kt[:DIFF_DIM].astype(BF16)
        dkt_o[0, j, 1, 0] = kt[DIFF_DIM:].astype(BF16)
        gq_o[0, :, sl] = (rot(_seg_rmsnorm(gq_ref[0, :, sl].astype(F32), qg)) * QSCALE).astype(BF16)
    dv_o[0] = dv_ref[0].astype(BF16)
    kt = rot(_seg_rmsnorm(gkv_ref[0, :, :LANES].astype(F32), kg)).T
    gkt_o[0, 0, 0] = kt[:GQA_DIM].astype(BF16)
    gkt_o[0, 1, 0] = kt[GQA_DIM:].astype(BF16)
    v = gkv_ref[0, :, LANES:].astype(BF16)
    gv_o[0, 0] = v[:, :GQA_DIM]
    gv_o[0, 1] = v[:, GQA_DIM:]


def attn_prep(p, tables, q_g, k_g, tk):
    b, t, _ = p.shape
    rope = tables is not None
    if rope:
        cos, sa, sb = tables
    else:
        cos = sa = sb = jnp.zeros((t, LANES), F32)
    nck = t // tk
    tp = _tile(tk, ROW_TILE)
    sub = tk // tp
    gw = GROUP_W
    kvw = 2 * GQA_KV * GQA_DIM
    grp = lambda c: pl.BlockSpec((1, tp, gw), lambda i, j, c=c: (i, j, c // gw))
    tab = pl.BlockSpec((tp, LANES), lambda i, j: (j, 0))
    vec = pl.BlockSpec((1, LANES), lambda i, j: (0, 0))
    outs = pl.pallas_call(
        functools.partial(_prep_kernel, rope=rope),
        grid=(b, t // tp),
        in_specs=[grp(C_DQ), grp(C_DK), grp(C_DV), grp(C_GQ),
                  pl.BlockSpec((1, tp, kvw), lambda i, j: (i, j, C_GK // kvw)),
                  tab, tab, tab, vec, vec],
        out_specs=[pl.BlockSpec((1, tp, gw), lambda i, j: (i, j, 0)),
                   pl.BlockSpec((1, DIFF_HEADS, 2, 1, DIFF_DIM, tp), lambda i, j: (i, 0, 0, j // sub, 0, j % sub)),
                   pl.BlockSpec((1, tp, gw), lambda i, j: (i, j, 0)),
                   pl.BlockSpec((1, tp, gw), lambda i, j: (i, j, 0)),
                   pl.BlockSpec((1, GQA_KV, 1, GQA_DIM, tp), lambda i, j: (i, 0, j // sub, 0, j % sub)),
                   pl.BlockSpec((1, GQA_KV, tp, GQA_DIM), lambda i, j: (i, 0, j, 0))],
        out_shape=[jax.ShapeDtypeStruct((b, t, gw), BF16),
                   jax.ShapeDtypeStruct((b, DIFF_HEADS, 2, nck, DIFF_DIM, tk), BF16),
                   jax.ShapeDtypeStruct((b, t, gw), BF16),
                   jax.ShapeDtypeStruct((b, t, gw), BF16),
                   jax.ShapeDtypeStruct((b, GQA_KV, nck, GQA_DIM, tk), BF16),
                   jax.ShapeDtypeStruct((b, GQA_KV, t, GQA_DIM), BF16)],
        compiler_params=_cparams(("parallel", "parallel")),
        name="attn_prep",
    )(p, p, p, p, p, cos, sa, sb,
      jnp.tile(q_g, 2).reshape(1, LANES), jnp.tile(k_g, 2).reshape(1, LANES))
    return outs


def _flash_step(q, kt, v, m_ref, l_ref, acc_ref):
    s = jnp.dot(q, kt, preferred_element_type=F32)
    tk = s.shape[1]
    m_prev = m_ref[...]
    m_next = jnp.maximum(m_prev, jnp.max(s, axis=1, keepdims=True))
    alpha = jnp.exp2(m_prev - m_next)
    p = jnp.exp2(s - jnp.concatenate([m_next] * (tk // LANES), axis=1))
    l_ref[...] = alpha * l_ref[...] + jnp.sum(p, axis=1, keepdims=True)
    dv = acc_ref.shape[-1]
    acc_ref[...] = acc_ref[...] * alpha[:, :dv] + jnp.dot(p.astype(BF16), v, preferred_element_type=F32)
    m_ref[...] = m_next


def _attn_body(q_ref, ktc, vc, ktl, vl, m_sc, l_sc, acc_sc, *, n_lat, tk):
    m_sc[...] = jnp.full(m_sc.shape, -jnp.inf, F32)
    l_sc[...] = jnp.zeros(l_sc.shape, F32)
    acc_sc[...] = jnp.zeros(acc_sc.shape, F32)
    q = q_ref[0]
    qs = (q[:, :DIFF_DIM], q[:, DIFF_DIM:])
    for c in range(2):
        _flash_step(qs[c], ktc(c), vc(c), m_sc.at[c], l_sc.at[c], acc_sc.at[c])
    if n_lat:
        def body(i, carry):
            for c in range(2):
                _flash_step(qs[c], ktl(c, i), vl(c, i), m_sc.at[c], l_sc.at[c], acc_sc.at[c])
            return carry
        lax.fori_loop(0, n_lat, body, 0)


def _diff_attn_kernel(*refs, n_lat, tk, out_scale):
    if n_lat:
        q_ref, ktc_ref, vc_ref, ktl_ref, vl_ref, lam_ref, g_ref, o_ref, m_sc, l_sc, acc_sc = refs
        ktl = lambda c, i: ktl_ref[0, 0, c, i]
        vl = lambda c, i: vl_ref[0, pl.ds(pl.multiple_of(i * tk, tk), tk), :]
    else:
        q_ref, ktc_ref, vc_ref, lam_ref, g_ref, o_ref, m_sc, l_sc, acc_sc = refs
        ktl = vl = None
    _attn_body(q_ref, lambda c: ktc_ref[0, 0, c, 0], lambda c: vc_ref[0], ktl, vl,
               m_sc, l_sc, acc_sc, n_lat=n_lat, tk=tk)
    o0 = acc_sc[0] / l_sc[0]
    o1 = acc_sc[1] / l_sc[1]
    o = o0 - lam_ref[...] * o1
    o = o * lax.rsqrt(jnp.mean(o * o, axis=-1, keepdims=True) + EPS)
    o_ref[0] = (o * g_ref[...] * out_scale).astype(o_ref.dtype)


def diff_attention(q, ktc, vc, ktl, vl, lam, norm_g, out_scale, tq):
    b, t, _ = q.shape
    sc = vc.shape[1]
    tq = _tile(t, tq)
    n_lat, tk = (ktl.shape[3], ktl.shape[5]) if ktl is not None else (0, 0)
    in_specs = [pl.BlockSpec((1, tq, LANES), lambda i, h, j: (i, j, h)),
                pl.BlockSpec((1, 1, 2, 1, DIFF_DIM, sc), lambda i, h, j: (i, h, 0, 0, 0, 0)),
                pl.BlockSpec((1, sc, LANES), lambda i, h, j: (i, 0, h))]
    args = [q, ktc, vc]
    if n_lat:
        in_specs += [pl.BlockSpec((1, 1, 2, n_lat, DIFF_DIM, tk), lambda i, h, j: (i, h, 0, 0, 0, 0)),
                     pl.BlockSpec((1, n_lat * tk, LANES), lambda i, h, j: (i, 0, h))]
        args += [ktl, vl]
    vec = pl.BlockSpec((1, LANES), lambda i, h, j: (0, 0))
    in_specs += [vec, vec]
    args += [jnp.full((1, LANES), lam, F32), norm_g.reshape(1, LANES)]
    return pl.pallas_call(
        functools.partial(_diff_attn_kernel, n_lat=n_lat, tk=tk, out_scale=out_scale),
        grid=(b, DIFF_HEADS, t // tq),
        in_specs=in_specs,
        out_specs=pl.BlockSpec((1, tq, LANES), lambda i, h, j: (i, j, h)),
        out_shape=jax.ShapeDtypeStruct((b, t, DIFF_HEADS * DIFF_VDIM), BF16),
        scratch_shapes=[pltpu.VMEM((2, tq, LANES), F32), pltpu.VMEM((2, tq, LANES), F32),
                        pltpu.VMEM((2, tq, DIFF_VDIM), F32)],
        compiler_params=_cparams(("parallel", "parallel", "parallel")),
        name="diff_attn",
    )(*args)


def _gqa_attn_kernel(*refs, n_lat, tk):
    if n_lat:
        q_ref, ktc_ref, vc_ref, ktl_ref, vl_ref, o_ref, m_sc, l_sc, acc_sc = refs
        ktl = lambda c, i: ktl_ref[0, 0, i]
        vl = lambda c, i: vl_ref[0, 0, pl.ds(pl.multiple_of(i * tk, tk), tk), :]
    else:
        q_ref, ktc_ref, vc_ref, o_ref, m_sc, l_sc, acc_sc = refs
        ktl = vl = None
    _attn_body(q_ref, lambda c: ktc_ref[0, 0, 0], lambda c: vc_ref[0, 0], ktl, vl,
               m_sc, l_sc, acc_sc, n_lat=n_lat, tk=tk)
    o0 = acc_sc[0] / l_sc[0][:, :GQA_DIM]
    o1 = acc_sc[1] / l_sc[1][:, :GQA_DIM]
    o_ref[0] = jnp.concatenate([o0, o1], axis=-1).astype(o_ref.dtype)


def gqa_attention(q, ktc, vc, ktl, vl, tq):
    b, t, _ = q.shape
    sc = vc.shape[2]
    tq = _tile(t, tq)
    n_lat, tk = (ktl.shape[2], ktl.shape[4]) if ktl is not None else (0, 0)
    pairs = GQA_HEADS // 2
    grp = lambda h: h // (pairs // GQA_KV)
    in_specs = [pl.BlockSpec((1, tq, LANES), lambda i, h, j: (i, j, h)),
                pl.BlockSpec((1, 1, 1, GQA_DIM, sc), lambda i, h, j: (i, grp(h), 0, 0, 0)),
                pl.BlockSpec((1, 1, sc, GQA_DIM), lambda i, h, j: (i, grp(h), 0, 0))]
    args = [q, ktc, vc]
    if n_lat:
        in_specs += [pl.BlockSpec((1, 1, n_lat, GQA_DIM, tk), lambda i, h, j: (i, grp(h), 0, 0, 0)),
                     pl.BlockSpec((1, 1, n_lat * tk, GQA_DIM), lambda i, h, j: (i, grp(h), 0, 0))]
        args += [ktl, vl]
    return pl.pallas_call(
        functools.partial(_gqa_attn_kernel, n_lat=n_lat, tk=tk),
        grid=(b, pairs, t // tq),
        in_specs=in_specs,
        out_specs=pl.BlockSpec((1, tq, LANES), lambda i, h, j: (i, j, h)),
        out_shape=jax.ShapeDtypeStruct((b, t, GQA_HEADS * GQA_DIM), BF16),
        scratch_shapes=[pltpu.VMEM((2, tq, LANES), F32), pltpu.VMEM((2, tq, LANES), F32),
                        pltpu.VMEM((2, tq, GQA_DIM), F32)],
        compiler_params=_cparams(("parallel", "parallel", "parallel")),
        name="gqa_attn",
    )(*args)


ML_STEP = 256
HALO = 16


def _halo_rows(prev_ref, next_ref, j, nblk):
    prev = jnp.where(j > 0, prev_ref[0, HALO - 1:HALO, :].astype(F32), 0.0)
    nxt = jnp.where(j < nblk - 1, next_ref[0, 0:1, :].astype(F32), 0.0)
    return prev, nxt


def _conv3(x, prev, nxt, w, bias):
    tt = x.shape[0]
    row = lax.broadcasted_iota(jnp.int32, x.shape, 0)
    xm = jnp.where(row == 0, prev, pltpu.roll(x, 1, axis=0))
    xp = jnp.where(row == tt - 1, nxt, pltpu.roll(x, tt - 1, axis=0))
    return xm * w[0:1] + x * w[1:2] + xp * w[2:3] + bias


def _halo_specs(tt, width, col_block, t):
    nb = t // HALO
    prev = pl.BlockSpec((1, HALO, width), lambda i, j: (i, jnp.maximum(j * (tt // HALO) - 1, 0), col_block))
    nxt = pl.BlockSpec((1, HALO, width), lambda i, j: (i, jnp.minimum((j + 1) * (tt // HALO), nb - 1), col_block))
    return prev, nxt


def _ml_prep_kernel(q_ref, qp_ref, qn_ref, k_ref, kp_ref, kn_ref, g_ref, w_ref, b_ref, gb_ref, q_o, k_o, g_o):
    j = pl.program_id(1)
    nblk = pl.num_programs(1)
    w = ML_HEADS * ML_DIM
    qp, qn = _halo_rows(qp_ref, qn_ref, j, nblk)
    kp, kn = _halo_rows(kp_ref, kn_ref, j, nblk)
    q = _conv3(q_ref[0].astype(F32), qp, qn, w_ref[:, :w], b_ref[:, :w])
    k = _conv3(k_ref[0].astype(F32), kp, kn, w_ref[:, w:], b_ref[:, w:])
    q_o[0] = q * jax.nn.sigmoid(q)
    k_o[0] = (k * jax.nn.sigmoid(k)) * ML_DIM ** -0.5
    x = g_ref[0].astype(F32) + gb_ref[...]
    lane = lax.broadcasted_iota(jnp.int32, x.shape, 1)
    log_sig = jnp.minimum(x, 0.0) - jnp.log(1.0 + jnp.exp(-jnp.abs(x)))
    g_o[0] = jnp.where((lane % 8) >= ML_HEADS, log_sig, x)


def ml_prep(p, conv_w, conv_b, gate_b):
    b, t, _ = p.shape
    tt = _tile(t, ROW_TILE)
    w = ML_HEADS * ML_DIM
    blk = lambda c: pl.BlockSpec((1, tt, w), lambda i, j, c=c: (i, j, c // w))
    qp, qn = _halo_specs(tt, w, C_MLQ // w, t)
    kp, kn = _halo_specs(tt, w, C_MLK // w, t)
    gb = jnp.zeros((1, LANES), F32).at[0, :4 * ML_HEADS].set(gate_b.reshape(-1))
    return pl.pallas_call(
        _ml_prep_kernel,
        grid=(b, t // tt),
        in_specs=[blk(C_MLQ), qp, qn, blk(C_MLK), kp, kn,
                  pl.BlockSpec((1, tt, LANES), lambda i, j: (i, j, C_MLG // LANES)),
                  pl.BlockSpec((3, 2 * w), lambda i, j: (0, 0)),
                  pl.BlockSpec((1, 2 * w), lambda i, j: (0, 0)),
                  pl.BlockSpec((1, LANES), lambda i, j: (0, 0))],
        out_specs=[pl.BlockSpec((1, tt, w), lambda i, j: (i, j, 0)),
                   pl.BlockSpec((1, tt, w), lambda i, j: (i, j, 0)),
                   pl.BlockSpec((1, tt, LANES), lambda i, j: (i, j, 0))],
        out_shape=[jax.ShapeDtypeStruct((b, t, w), F32), jax.ShapeDtypeStruct((b, t, w), F32),
                   jax.ShapeDtypeStruct((b, t, LANES), F32)],
        compiler_params=_cparams(("parallel", "parallel")),
        name="ml_prep",
    )(p, p, p, p, p, p, p, conv_w, conv_b.reshape(1, 2 * w), gb)


_NT = (((1,), (1,)), ((), ()))


def _ml_chunk_head(q, k, v, li_c, bc_c, bc_r, b_last, seen, ct_ref, n_ref, m_ref):
    m = m_ref[:, 0:1]
    ct = ct_ref[...]
    n_rows = n_ref[...]
    qb, kb, vb = q.astype(BF16), k.astype(BF16), v.astype(BF16)
    d_t = jnp.where(seen, bc_r + (li_c - bc_c), -jnp.inf)
    inter = bc_r + m
    m_t = jnp.maximum(inter, jnp.max(d_t, axis=0, keepdims=True))
    w_inter = jnp.exp(inter - m_t)
    s_t = lax.dot_general(kb, qb, _NT, preferred_element_type=F32) * jnp.exp(d_t - m_t)
    num_t = (jnp.dot(vb.T, s_t.astype(BF16), preferred_element_type=F32)
             + w_inter * lax.dot_general(ct.astype(BF16), qb, _NT, preferred_element_type=F32))
    qn = lax.dot_general(n_rows.astype(BF16), qb, _NT, preferred_element_type=F32)[0:1]
    den = jnp.sum(s_t, axis=0, keepdims=True) + w_inter * qn
    h_t = num_t / jnp.maximum(jnp.abs(den), jnp.exp(-m_t))
    g = b_last - bc_c + li_c
    m_new = jnp.maximum(b_last + m, jnp.max(g, axis=0, keepdims=True))
    kw = k * jnp.exp(g - m_new)
    wc = jnp.exp(b_last + m - m_new)
    ct_ref[...] = wc * ct + jnp.dot(vb.T, kw.astype(BF16), preferred_element_type=F32)
    n_ref[...] = wc * n_rows + jnp.sum(kw, axis=0, keepdims=True)
    m_ref[...] = jnp.broadcast_to(m_new, m_ref.shape)
    return h_t.T


def _ml_scan_kernel(q_ref, k_ref, v_ref, g_ref, c0_ref, n0_ref, m0_ref, h_o, c1_o, n1_o, m1_o,
                    c_sc, n_sc, m_sc, *, direction):
    j = pl.program_id(0)

    @pl.when(j == 0)
    def _():
        c_sc[...] = c0_ref[...]
        n_sc[...] = n0_ref[...]
        m_sc[...] = m0_ref[...]

    r = lax.broadcasted_iota(jnp.int32, (ML_CHUNK, ML_CHUNK), 0)
    s = lax.broadcasted_iota(jnp.int32, (ML_CHUNK, ML_CHUNK), 1)
    tri_f = ((s >= r) if direction else (s <= r)).astype(F32)
    seen = (r >= s) if direction else (r <= s)
    n_chunks = q_ref.shape[1] // ML_CHUNK
    order = range(n_chunks - 1, -1, -1) if direction else range(n_chunks)
    last = 0 if direction else ML_CHUNK - 1
    for c in order:
        rows = slice(c * ML_CHUNK, (c + 1) * ML_CHUNK)
        for bi in range(q_ref.shape[0]):
            gch = g_ref[bi, rows, :]
            bc = jnp.dot(tri_f, gch, preferred_element_type=F32, precision=lax.Precision.HIGHEST)
            bct = bc.T
            for hd in range(ML_HEADS):
                ci = 2 * ML_HEADS * direction + hd
                cf = ci + ML_HEADS
                cols = slice(hd * ML_DIM, (hd + 1) * ML_DIM)
                h = _ml_chunk_head(q_ref[bi, rows, cols], k_ref[bi, rows, cols], v_ref[bi, rows, cols],
                                   gch[:, ci:ci + 1], bc[:, cf:cf + 1], bct[cf:cf + 1, :],
                                   bc[last:last + 1, cf:cf + 1], seen,
                                   c_sc.at[bi, hd], n_sc.at[bi, hd], m_sc.at[bi, hd])
                h_o[bi, rows, cols] = h

    @pl.when(j == pl.num_programs(0) - 1)
    def _():
        c1_o[...] = c_sc[...]
        n1_o[...] = n_sc[...]
        m1_o[...] = m_sc[...]


def _ml_state_shapes(b):
    return [(b, ML_HEADS, ML_DIM, ML_DIM), (b, ML_HEADS, 8, ML_DIM), (b, ML_HEADS, 1, ML_DIM)]


def ml_scan(q, k, p, g, state, direction):
    b, t, w = q.shape
    ts = _tile(t, ML_STEP)
    nst = t // ts
    tok = (lambda j: (0, nst - 1 - j, 0)) if direction else (lambda j: (0, j, 0))
    tokv = (lambda j: (0, nst - 1 - j, C_MLV // w)) if direction else (lambda j: (0, j, C_MLV // w))
    st_shapes = _ml_state_shapes(b)
    st_specs = [pl.BlockSpec(s, lambda j: (0, 0, 0, 0)) for s in st_shapes]
    h, c1, n1, m1 = pl.pallas_call(
        functools.partial(_ml_scan_kernel, direction=direction),
        grid=(nst,),
        in_specs=[pl.BlockSpec((b, ts, w), tok), pl.BlockSpec((b, ts, w), tok), pl.BlockSpec((b, ts, w), tokv),
                  pl.BlockSpec((b, ts, LANES), tok)] + st_specs,
        out_specs=[pl.BlockSpec((b, ts, w), tok)] + st_specs,
        out_shape=[jax.ShapeDtypeStruct((b, t, w), F32)] + [jax.ShapeDtypeStruct(s, F32) for s in st_shapes],
        scratch_shapes=[pltpu.VMEM(s, F32) for s in st_shapes],
        compiler_params=_cparams(("arbitrary",)),
        name="ml_scan",
    )(q, k, p, g, *state)
    return h, (c1, n1, m1)


def mlstm_branch(p_c, p_l, conv_w, conv_b, gate_b, need_ctx):
    b = p_l.shape[0]
    qc, kc, gc = ml_prep(p_c, conv_w, conv_b, gate_b)
    ql, kl, gl = ml_prep(p_l, conv_w, conv_b, gate_b)
    zero = tuple(jnp.zeros(s, F32) for s in _ml_state_shapes(b))
    hs_c, hs_l = [], []
    for direction in (0, 1):
        hc, st = ml_scan(qc, kc, p_c, gc, zero, direction)
        hl, _ = ml_scan(ql, kl, p_l, gl, st, direction)
        hs_c.append(hc)
        hs_l.append(hl)
    return (tuple(hs_c) if need_ctx else None), tuple(hs_l)


def _merge_kernel(ya_ref, hf_ref, hb_ref, yc_ref, yd_ref, og_ref, g_ref, mg_ref, wb_ref, wo_ref, x_ref, gate_ref,
                  o_ref):
    d = x_ref.shape[-1]
    mg = mg_ref[...]
    yb = []
    for hd in range(ML_HEADS):
        cols = slice(hd * ML_DIM, (hd + 1) * ML_DIM)
        h = hf_ref[0, :, cols] + hb_ref[0, :, cols]
        h = h * lax.rsqrt(jnp.mean(h * h, axis=-1, keepdims=True) + EPS) * mg
        yb.append((h * jax.nn.sigmoid(og_ref[0, :, cols].astype(F32))).astype(BF16))
    ys = (ya_ref[0].astype(BF16), jnp.concatenate(yb, axis=-1), yc_ref[0].astype(BF16), yd_ref[0].astype(BF16))
    acc = None
    for n, y in enumerate(ys):
        t = jnp.dot(y, wb_ref[n], preferred_element_type=F32)
        t = jax.nn.sigmoid(g_ref[0, :, n * d:(n + 1) * d].astype(F32)) * t
        acc = t if acc is None else acc + t
    z = jnp.dot(acc.astype(BF16), wo_ref[...], preferred_element_type=F32)
    o_ref[0] = x_ref[0] + gate_ref[0] * z


def merge(ya, hs, yc, yd, p, ml_norm_g, w_branch, w_out, x, gate):
    b, t, d = x.shape
    tm = _tile(t, ROW_TILE)
    w = ML_HEADS * ML_DIM
    ysp = pl.BlockSpec((1, tm, w), lambda i, j: (i, j, 0))
    return pl.pallas_call(
        _merge_kernel,
        grid=(b, t // tm),
        in_specs=[ysp, ysp, ysp, ysp, ysp,
                  pl.BlockSpec((1, tm, w), lambda i, j: (i, j, C_MLO // w)),
                  pl.BlockSpec((1, tm, N_BRANCH * d), lambda i, j: (i, j, C_GATE // (N_BRANCH * d))),
                  pl.BlockSpec((1, ML_DIM), lambda i, j: (0, 0)),
                  pl.BlockSpec((N_BRANCH, w, d), lambda i, j: (0, 0, 0)),
                  pl.BlockSpec((d, d), lambda i, j: (0, 0)),
                  pl.BlockSpec((1, tm, d), lambda i, j: (i, j, 0)),
                  pl.BlockSpec((1, 1, d), lambda i, j: (i, 0, 0))],
        out_specs=pl.BlockSpec((1, tm, d), lambda i, j: (i, j, 0)),
        out_shape=jax.ShapeDtypeStruct((b, t, d), F32),
        compiler_params=_cparams(("parallel", "parallel")),
        name="merge",
    )(ya, hs[0], hs[1], yc, yd, p, p, ml_norm_g.reshape(1, ML_DIM), w_branch, w_out, x, gate.reshape(b, 1, d))


def _expert_kernel(be_ref, x_ref, wg_ref, wu_ref, wd_ref, o_ref, wg_sc, wu_sc, wd_sc):
    i = pl.program_id(0)

    @pl.when(jnp.logical_or(i == 0, be_ref[i] != be_ref[jnp.maximum(i - 1, 0)]))
    def _():
        wg_sc[...] = wg_ref[0, 0].astype(BF16)
        wu_sc[...] = wu_ref[0, 0].astype(BF16)
        wd_sc[...] = wd_ref[0, 0].astype(BF16)

    x = x_ref[...]
    a = jnp.dot(x, wg_sc[...], preferred_element_type=F32)
    u = jnp.dot(x, wu_sc[...], preferred_element_type=F32)
    h = (a * jax.nn.sigmoid(a)) * u
    o_ref[...] = jnp.dot(h.astype(BF16), wd_sc[...], preferred_element_type=F32).astype(o_ref.dtype)


def expert_blocks(xb, blk_exp, layer, w_gate, w_up, w_down):
    m, d = xb.shape
    hdim = w_gate.shape[-1]
    n_blocks = m // MOE_BLOCK
    grid_spec = pltpu.PrefetchScalarGridSpec(
        num_scalar_prefetch=1,
        grid=(n_blocks,),
        in_specs=[pl.BlockSpec((MOE_BLOCK, d), lambda i, be: (i, 0)),
                  pl.BlockSpec((1, 1, d, hdim), lambda i, be: (layer, be[i], 0, 0)),
                  pl.BlockSpec((1, 1, d, hdim), lambda i, be: (layer, be[i], 0, 0)),
                  pl.BlockSpec((1, 1, hdim, d), lambda i, be: (layer, be[i], 0, 0))],
        out_specs=pl.BlockSpec((MOE_BLOCK, d), lambda i, be: (i, 0)),
        scratch_shapes=[pltpu.VMEM((d, hdim), BF16), pltpu.VMEM((d, hdim), BF16), pltpu.VMEM((hdim, d), BF16)],
    )
    return pl.pallas_call(
        _expert_kernel,
        grid_spec=grid_spec,
        out_shape=jax.ShapeDtypeStruct((m, d), BF16),
        compiler_params=_cparams(("arbitrary",)),
        name="moe_experts",
    )(blk_exp, xb, w_gate, w_up, w_down)


def _rope_tables(n_tok):
    rows = n_tok // GRID_W
    row = jnp.repeat(jnp.arange(rows, dtype=F32), GRID_W)
    col = jnp.broadcast_to(jnp.arange(GRID_W, dtype=F32), (rows, GRID_W)).reshape(-1)
    n_freq = DIFF_DIM // 4
    inv = ROPE_BASE ** (-jnp.arange(n_freq, dtype=F32) / n_freq)
    ar = row[:, None] * inv
    ac = col[:, None] * inv
    ang = jnp.concatenate([ar, ar, ac, ac], axis=-1)
    cos, sin = jnp.cos(ang), jnp.sin(ang)
    first = (jnp.arange(DIFF_DIM) % 32) < 16
    sa = jnp.where(first, -sin, 0.0)
    sb = jnp.where(first, 0.0, sin)
    return tuple(jnp.tile(a, (1, 2)) for a in (cos, sa, sb))


DFT_N2 = 256
DFT_J = 8
DFT_P = 4


def _hy_prep_kernel(*refs):
    ins, outs = refs[:9], refs[11:]
    w_ref, b_ref = refs[9], refs[10]
    j = pl.program_id(1)
    nblk = pl.num_programs(1)
    for n in range(HY_ORDER + 1):
        x_ref, p_ref, n_ref = ins[3 * n:3 * n + 3]
        cols = slice(n * HY_CH, (n + 1) * HY_CH)
        prev, nxt = _halo_rows(p_ref, n_ref, j, nblk)
        outs[n][0] = _conv3(x_ref[0].astype(F32), prev, nxt, w_ref[:, cols], b_ref[:, cols])


def hy_prep(p, conv_w, conv_b):
    b, t, _ = p.shape
    tt = _tile(t, ROW_TILE)
    in_specs, args = [], []
    for n in range(HY_ORDER + 1):
        cb = C_HY // HY_CH + n
        prev, nxt = _halo_specs(tt, HY_CH, cb, t)
        in_specs += [pl.BlockSpec((1, tt, HY_CH), lambda i, j, cb=cb: (i, j, cb)), prev, nxt]
        args += [p, p, p]
    nch = (HY_ORDER + 1) * HY_CH
    in_specs += [pl.BlockSpec((3, nch), lambda i, j: (0, 0)), pl.BlockSpec((1, nch), lambda i, j: (0, 0))]
    osp = pl.BlockSpec((1, tt, HY_CH), lambda i, j: (i, j, 0))
    return pl.pallas_call(
        _hy_prep_kernel,
        grid=(b, t // tt),
        in_specs=in_specs,
        out_specs=[osp] * (HY_ORDER + 1),
        out_shape=[jax.ShapeDtypeStruct((b, t, HY_CH), F32)] * (HY_ORDER + 1),
        compiler_params=_cparams(("parallel", "parallel")),
        name="hy_prep",
    )(*args, conv_w, conv_b.reshape(1, nch))


def _hy_filter_kernel(emb_ref, w1_ref, b1_ref, w2_ref, b2_ref, w3_ref, fr_ref, al_ref, f_o, ss_o, *, length, half):
    j = pl.program_id(0)
    tt = emb_ref.shape[0]
    a = jnp.dot(emb_ref[...].astype(BF16), w1_ref[...].astype(BF16), preferred_element_type=F32) + b1_ref[...]
    a = jnp.sin(fr_ref[0:1, :] * a)
    a = jnp.dot(a.astype(BF16), w2_ref[...].astype(BF16), preferred_element_type=F32) + b2_ref[...]
    a = jnp.sin(fr_ref[1:2, :] * a)
    filt = jnp.dot(a.astype(BF16), w3_ref[...].astype(BF16), preferred_element_type=F32)
    r = lax.broadcasted_iota(jnp.int32, (tt, HY_CH), 0)
    if half:
        row = DFT_N2 * (r & (half - 1)) + j * (tt // half) + (r >> (half.bit_length() - 1))
    else:
        row = r + j * tt
    window = jnp.exp(-(row.astype(F32) / length) * al_ref[...]) + FILTER_SHIFT

    @pl.when(j == 0)
    def _():
        ss_o[...] = jnp.zeros(ss_o.shape, F32)

    for o in range(HY_ORDER):
        for d in range(2):
            idx = 2 * o + d
            f = filt[:, idx * HY_CH:(idx + 1) * HY_CH] * window
            if d == 1:
                f = jnp.where(row == 0, 0.0, f)
            if half:
                for jj in range(tt // half):
                    f_o[idx, :, jj * HY_CH:(jj + 1) * HY_CH] = f[jj * half:(jj + 1) * half]
            else:
                f_o[idx] = f
            ss_o[o:o + 1, :] += jnp.sum(f * f, axis=0, keepdims=True)


def hy_filters(length, w1, b1, w2, b2, w3, freq, wide):
    t = jnp.arange(length, dtype=F32) / length
    bands = jnp.arange(1, FILTER_BANDS + 1, dtype=F32)
    ang = 2.0 * math.pi * t[:, None] * bands
    emb = jnp.concatenate([t[:, None], jnp.cos(ang), jnp.sin(ang)], axis=-1)
    pad = LANES - emb.shape[1]
    emb = jnp.pad(emb, ((0, 0), (0, pad)))
    w1 = jnp.pad(w1, ((0, pad), (0, 0)))
    ne, nh = emb.shape[1], w1.shape[1]
    alpha = jnp.linspace(abs(math.log(DECAY_TARGET)) / SLOW_DECAY_PCT,
                         abs(math.log(DECAY_TARGET)) / FAST_DECAY_PCT, HY_CH).reshape(1, HY_CH)
    tt = _tile(length, ROW_TILE)
    half = length // DFT_N2
    wide = wide and half > 0 and half & (half - 1) == 0 and tt % half == 0
    full = lambda shape: pl.BlockSpec(shape, lambda j: (0,) * len(shape))
    if wide:
        emb = jnp.swapaxes(emb.reshape(half, DFT_N2, ne), 0, 1).reshape(length, ne)
        f_spec = pl.BlockSpec((2 * HY_ORDER, half, (tt // half) * HY_CH), lambda j: (0, 0, j))
        f_shape = (2 * HY_ORDER, half, DFT_N2 * HY_CH)
    else:
        f_spec = pl.BlockSpec((2 * HY_ORDER, tt, HY_CH), lambda j: (0, j, 0))
        f_shape = (2 * HY_ORDER, length, HY_CH)
    f, ss = pl.pallas_call(
        functools.partial(_hy_filter_kernel, length=length, half=half if wide else 0),
        grid=(length // tt,),
        in_specs=[pl.BlockSpec((tt, ne), lambda j: (j, 0)), full((ne, nh)), full((1, nh)), full((nh, nh)),
                  full((1, nh)), full((nh, 2 * HY_ORDER * HY_CH)), full((2, nh)), full((1, HY_CH))],
        out_specs=[f_spec, full((HY_ORDER, HY_CH))],
        out_shape=[jax.ShapeDtypeStruct(f_shape, F32), jax.ShapeDtypeStruct((HY_ORDER, HY_CH), F32)],
        compiler_params=_cparams(("arbitrary",)),
        name="hy_filter",
    )(emb, w1, b1.reshape(1, nh), w2, b2.reshape(1, nh), w3, freq, alpha)
    return f, lax.rsqrt(ss + EPS)


def _dft_tables(length):
    n = 2 * length
    n1 = n // DFT_N2
    half = n1 // 2
    n1h = -(-(half + 1) // 16) * 16
    kv = jnp.arange(n1h, dtype=jnp.int32)
    valid = (kv <= half).astype(F32)[None, :, None]
    pair = jnp.where((kv == 0) | (kv == half), 1.0, 2.0)[None, :, None] * valid
    k1 = kv[None, :, None]
    tn = (DFT_N2 * jnp.arange(half, dtype=jnp.int32)[None, None, :]
          + jnp.arange(DFT_N2, dtype=jnp.int32)[:, None, None])
    th = (2.0 * math.pi / n) * ((k1 * tn) % n).astype(F32)
    ga = jnp.concatenate([jnp.cos(th) * valid, -jnp.sin(th) * valid], axis=1).astype(BF16)
    gi = jnp.swapaxes(jnp.concatenate([jnp.cos(th) * pair, -jnp.sin(th) * pair], axis=1), 1, 2).astype(BF16)
    kk = jnp.arange(DFT_N2, dtype=jnp.int32)
    t2 = (2.0 * math.pi / DFT_N2) * ((kk[:, None] * kk[None, :]) % DFT_N2).astype(F32)
    c2, s2 = jnp.cos(t2), jnp.sin(t2)
    mf = jnp.block([[c2, s2], [-s2, c2]]).astype(BF16)
    mi = jnp.block([[c2, -s2], [s2, c2]]).astype(BF16)
    return ga, gi, mf, mi


def _dft_a_kernel(z_ref, g_ref, o_ref):
    c = HY_CH
    for j in range(DFT_J):
        slab = z_ref[0, :, j * c:(j + 1) * c].astype(BF16)
        r = jnp.dot(g_ref[j], slab, preferred_element_type=F32)
        o_ref[0, :, :, j * c:(j + 1) * c] = r.reshape(2, r.shape[0] // 2, c).astype(o_ref.dtype)


def dft_a(z, ga):
    bz, half, wid = z.shape
    n1 = ga.shape[1] // 2
    jc = DFT_J * HY_CH
    return pl.pallas_call(
        _dft_a_kernel,
        grid=(bz, DFT_N2 // DFT_J),
        in_specs=[pl.BlockSpec((1, half, jc), lambda i, j: (i, 0, j)),
                  pl.BlockSpec((DFT_J, 2 * n1, half), lambda i, j: (j, 0, 0))],
        out_specs=pl.BlockSpec((1, 2, n1, jc), lambda i, j: (i, 0, 0, j)),
        out_shape=jax.ShapeDtypeStruct((bz, 2, n1, wid), BF16),
        compiler_params=_cparams(("parallel", "parallel")),
        name="dft_a",
    )(z, ga)


def _stack_ri(ref, b, k):
    return jnp.concatenate([ref[b, 0, k], ref[b, 1, k]], axis=0)


def _spec_filter_kernel(f_ref, mf_ref, sc_ref, h_o):
    sc = sc_ref[0]
    for k in range(DFT_P):
        xf = jnp.dot(mf_ref[...], _stack_ri(f_ref, 0, k), preferred_element_type=F32)
        xb = jnp.dot(mf_ref[...], _stack_ri(f_ref, 1, k), preferred_element_type=F32)
        h_o[0, k, 0] = ((xf[:DFT_N2] + xb[:DFT_N2]) * sc).astype(h_o.dtype)
        h_o[0, k, 1] = ((xf[DFT_N2:] - xb[DFT_N2:]) * sc).astype(h_o.dtype)


def spec_filter(fa, mf, scale):
    nb, _, n1, _, c = fa.shape
    order = nb // 2
    return pl.pallas_call(
        _spec_filter_kernel,
        grid=(order, n1 // DFT_P),
        in_specs=[pl.BlockSpec((2, 2, DFT_P, DFT_N2, c), lambda o, k: (o, 0, k, 0, 0)),
                  pl.BlockSpec((2 * DFT_N2, 2 * DFT_N2), lambda o, k: (0, 0)),
                  pl.BlockSpec((1, 1, c), lambda o, k: (o, 0, 0))],
        out_specs=pl.BlockSpec((1, DFT_P, 2, DFT_N2, c), lambda o, k: (o, k, 0, 0, 0)),
        out_shape=jax.ShapeDtypeStruct((order, n1, 2, DFT_N2, c), BF16),
        compiler_params=_cparams(("parallel", "parallel")),
        name="spec_filter",
    )(fa, mf, scale.reshape(order, 1, c))


def _spec_conv_kernel(a_ref, h_ref, mf_ref, mi_ref, o_ref):
    for k in range(DFT_P):
        x = jnp.dot(mf_ref[...], _stack_ri(a_ref, 0, k), preferred_element_type=F32)
        xr, xi = x[:DFT_N2], x[DFT_N2:]
        hr, hi = h_ref[0, k, 0].astype(F32), h_ref[0, k, 1].astype(F32)
        y = jnp.concatenate([xr * hr - xi * hi, xr * hi + xi * hr], axis=0).astype(BF16)
        z = jnp.dot(mi_ref[...], y, preferred_element_type=F32)
        o_ref[0, 0, k] = z[:DFT_N2].astype(o_ref.dtype)
        o_ref[0, 1, k] = z[DFT_N2:].astype(o_ref.dtype)


def spec_conv(a, h, order, mf, mi):
    b, _, n1, _, c = a.shape
    blk = pl.BlockSpec((1, 2, DFT_P, DFT_N2, c), lambda i, k: (i, 0, k, 0, 0))
    mat = pl.BlockSpec((2 * DFT_N2, 2 * DFT_N2), lambda i, k: (0, 0))
    return pl.pallas_call(
        _spec_conv_kernel,
        grid=(b, n1 // DFT_P),
        in_specs=[blk, pl.BlockSpec((1, DFT_P, 2, DFT_N2, c), lambda i, k: (order, k, 0, 0, 0)), mat, mat],
        out_specs=blk,
        out_shape=jax.ShapeDtypeStruct(a.shape, BF16),
        compiler_params=_cparams(("parallel", "parallel")),
        name="spec_conv",
    )(a, h, mf, mi)


def _dft_ainv_kernel(z_ref, g_ref, xg_ref, zin_ref, bias_ref, o_ref):
    c = HY_CH
    n1 = z_ref.shape[2]
    for j in range(DFT_J):
        cols = slice(j * c, (j + 1) * c)
        zz = z_ref[0, :, :, cols].reshape(2 * n1, c)
        y = jnp.dot(g_ref[j], zz, preferred_element_type=F32)
        o_ref[0, :, cols] = xg_ref[0, :, cols] * (y + bias_ref[...] * zin_ref[0, :, cols])


def dft_ainv(z, gi, xg, zin, bias):
    b, _, n1, wid = z.shape
    half = gi.shape[1]
    jc = DFT_J * HY_CH
    tok = pl.BlockSpec((1, half, jc), lambda i, j: (i, 0, j))
    return pl.pallas_call(
        _dft_ainv_kernel,
        grid=(b, DFT_N2 // DFT_J),
        in_specs=[pl.BlockSpec((1, 2, n1, jc), lambda i, j: (i, 0, 0, j)),
                  pl.BlockSpec((DFT_J, half, 2 * n1), lambda i, j: (j, 0, 0)),
                  tok, tok, pl.BlockSpec((1, HY_CH), lambda i, j: (0, 0))],
        out_specs=tok,
        out_shape=jax.ShapeDtypeStruct((b, half, wid), F32),
        compiler_params=_cparams(("parallel", "parallel")),
        name="dft_ainv",
    )(z, gi, xg, zin, bias.reshape(1, HY_CH))


def _ctx_conv_kernel(z_ref, xg_ref, f_ref, mf_ref, mi_ref, sc_ref, bias_ref, o_ref):
    nf = mf_ref.shape[0] // 2
    mf = mf_ref[...]
    xf = jnp.dot(mf, f_ref[0].astype(BF16), preferred_element_type=F32)
    xb = jnp.dot(mf, f_ref[1].astype(BF16), preferred_element_type=F32)
    sc = sc_ref[0]
    hr = (xf[:nf] + xb[:nf]) * sc
    hi = (xf[nf:] - xb[nf:]) * sc
    z = z_ref[0]
    x = jnp.dot(mf, z.astype(BF16), preferred_element_type=F32)
    xr, xi = x[:nf], x[nf:]
    y = jnp.concatenate([xr * hr - xi * hi, xr * hi + xi * hr], axis=0).astype(BF16)
    o_ref[0] = xg_ref[0] * (jnp.dot(mi_ref[...], y, preferred_element_type=F32) + bias_ref[...] * z)


def ctx_conv(z, xg, f, order, scale, bias):
    b, length, c = z.shape
    n = 2 * length
    kk = jnp.arange(n, dtype=jnp.int32)[:, None]
    tn = jnp.arange(length, dtype=jnp.int32)[None, :]
    th = (2.0 * math.pi / n) * ((kk * tn) % n).astype(F32)
    mf = jnp.concatenate([jnp.cos(th), -jnp.sin(th)], axis=0).astype(BF16)
    mi = jnp.concatenate([jnp.cos(th.T), -jnp.sin(th.T)], axis=1).astype(BF16)
    tok = pl.BlockSpec((1, length, c), lambda i: (i, 0, 0))
    return pl.pallas_call(
        _ctx_conv_kernel,
        grid=(b,),
        in_specs=[tok, tok, pl.BlockSpec((2, length, c), lambda i: (order, 0, 0)),
                  pl.BlockSpec((2 * n, length), lambda i: (0, 0)), pl.BlockSpec((length, 2 * n), lambda i: (0, 0)),
                  pl.BlockSpec((1, 1, c), lambda i: (order, 0, 0)), pl.BlockSpec((1, c), lambda i: (0, 0))],
        out_specs=tok,
        out_shape=jax.ShapeDtypeStruct((b, length, c), F32),
        compiler_params=_cparams(("parallel",)),
        name="ctx_conv",
    )(z, xg, f, mf, mi, scale.reshape(-1, 1, c), bias.reshape(1, c))


def hyena(p, conv_w, conv_b, w1, b1, w2, b2, w3, freq, bias, tables):
    b, length, _ = p.shape
    parts = hy_prep(p, conv_w, conv_b)
    f, rnorm = hy_filters(length, w1, b1, w2, b2, w3, freq, wide=tables is not None)
    scale = rnorm / (2 * length)
    z = parts[0]
    if tables is None:
        for o in range(HY_ORDER):
            z = ctx_conv(z, parts[o + 1], f, o, scale, bias[o])
        return z
    ga, gi, mf, mi = tables
    half = ga.shape[2]
    wid = DFT_N2 * HY_CH
    fa = dft_a(f.reshape(2 * HY_ORDER, half, wid), ga)
    n1 = fa.shape[2]
    h = spec_filter(fa.reshape(2 * HY_ORDER, 2, n1, DFT_N2, HY_CH), mf, scale)
    for o in range(HY_ORDER):
        a = dft_a(z.reshape(b, half, wid), ga).reshape(b, 2, n1, DFT_N2, HY_CH)
        zc = spec_conv(a, h, o, mf, mi).reshape(b, 2, n1, wid)
        z = dft_ainv(zc, gi, parts[o + 1].reshape(b, half, wid), z.reshape(b, half, wid), bias[o])
        z = z.reshape(b, length, HY_CH)
    return z


ROUTE_ROWS = 8


def _router_kernel(x_ref, g_ref, sh_ref, sc_ref, w_ref, b_ref, h_o, id_o, gate_o):
    x = x_ref[0]
    y = x * lax.rsqrt(jnp.mean(x * x, axis=-1, keepdims=True) + EPS)
    y = ((y * g_ref[...]) * (1.0 + sc_ref[0]) + sh_ref[0]).astype(BF16)
    h_o[0] = y
    logits = jnp.dot(y, w_ref[...], preferred_element_type=F32) + b_ref[...]
    lane = lax.broadcasted_iota(jnp.int32, logits.shape, 1)
    lane_f = lane.astype(F32)
    none = float(LANES)

    def top(vals):
        v = jnp.max(vals, axis=1, keepdims=True)
        return v, jnp.min(jnp.where(vals == v, lane_f, none), axis=1, keepdims=True)

    is_grp = lane < MOE_GROUPS
    mg, grp = top(jnp.where(is_grp, logits, -jnp.inf))
    p_grp = 1.0 / jnp.sum(jnp.where(is_grp, jnp.exp(logits - mg), 0.0), axis=1, keepdims=True)
    lo = MOE_GROUPS + MOE_EPG * grp
    el = jnp.where((lane_f >= lo) & (lane_f < lo + MOE_EPG), logits, -jnp.inf)
    v1, i1 = top(el)
    v2, i2 = top(jnp.where(lane_f == i1, -jnp.inf, el))
    t = jnp.exp(v2 - v1)
    g1 = p_grp / (1.0 + t)
    ids = jnp.where(lane == 0, i1 - MOE_GROUPS, jnp.where(lane == 1, i2 - MOE_GROUPS, 0.0))
    id_o[0, 0] = ids.T[:ROUTE_ROWS].astype(jnp.int32)
    gate_o[0, 0] = jnp.where(lane == 0, g1, jnp.where(lane == 1, g1 * t, 0.0)).T[:ROUTE_ROWS]


def moe_router(x, g, shift, scale, w_group, b_group, w_router, b_router):
    b, t, d = x.shape
    tt = _tile(t, ROW_TILE)
    rows = pl.BlockSpec((1, 1, ROUTE_ROWS, tt), lambda i, j: (i, j, 0, 0))
    npad = LANES - MOE_GROUPS - MOE_EXPERTS
    w = jnp.concatenate([w_group, w_router, jnp.zeros((d, npad), F32)], axis=1).astype(BF16)
    bias = jnp.concatenate([b_group, b_router, jnp.zeros((npad,), F32)]).reshape(1, LANES)
    tok = lambda width: pl.BlockSpec((1, tt, width), lambda i, j: (i, j, 0))
    mod = pl.BlockSpec((1, 1, d), lambda i, j: (i, 0, 0))
    return pl.pallas_call(
        _router_kernel,
        grid=(b, t // tt),
        in_specs=[tok(d), pl.BlockSpec((1, d), lambda i, j: (0, 0)), mod, mod,
                  pl.BlockSpec((d, LANES), lambda i, j: (0, 0)), pl.BlockSpec((1, LANES), lambda i, j: (0, 0))],
        out_specs=[tok(d), rows, rows],
        out_shape=[jax.ShapeDtypeStruct((b, t, d), BF16),
                   jax.ShapeDtypeStruct((b, t // tt, ROUTE_ROWS, tt), jnp.int32),
                   jax.ShapeDtypeStruct((b, t // tt, ROUTE_ROWS, tt), F32)],
        compiler_params=_cparams(("parallel", "parallel")),
        name="moe_router",
    )(x, g.reshape(1, d), shift.reshape(b, 1, d), scale.reshape(b, 1, d), w, bias)


def _blocked_cumsum(onehot, blk=256):
    m, e = onehot.shape
    if m % blk:
        return jnp.cumsum(onehot, axis=0)
    oh = onehot.astype(BF16).reshape(m // blk, blk, e)
    tril = jnp.tril(jnp.ones((blk, blk), BF16))
    within = jnp.einsum('ts,bse->bte', tril, oh, preferred_element_type=F32)
    tot = within[:, -1, :]
    off = jnp.cumsum(tot, axis=0) - tot
    return (within + off[:, None, :]).reshape(m, e).astype(jnp.int32)


def _hier_moe(h, ids, layer, w_gate, w_up, w_down):
    n_tok, d = h.shape
    e_flat = ids.reshape(-1)
    m_slots = n_tok * MOE_TOP_K
    onehot = (e_flat[:, None] == jnp.arange(MOE_EXPERTS, dtype=jnp.int32)[None, :]).astype(jnp.int32)
    csum = _blocked_cumsum(onehot)
    rank = jnp.sum(onehot * csum, axis=1) - 1
    counts = csum[-1]
    padded = (counts + MOE_BLOCK - 1) // MOE_BLOCK * MOE_BLOCK
    p_end = jnp.cumsum(padded)
    dest = (p_end - padded)[e_flat] + rank
    n_blocks = -(-(m_slots + MOE_EXPERTS * (MOE_BLOCK - 1)) // MOE_BLOCK)
    slot_tok = jnp.arange(m_slots, dtype=jnp.int32) // MOE_TOP_K
    buf_tok = jnp.zeros((n_blocks * MOE_BLOCK,), jnp.int32).at[dest].set(slot_tok)
    first_row = jnp.arange(n_blocks, dtype=jnp.int32)[:, None] * MOE_BLOCK
    blk_exp = jnp.minimum(jnp.sum((p_end[None, :] <= first_row).astype(jnp.int32), axis=1), MOE_EXPERTS - 1)
    yb = expert_blocks(h[buf_tok], blk_exp, layer, w_gate, w_up, w_down)
    return yb, dest.reshape(n_tok, MOE_TOP_K)


def _moe_combine_kernel(x_ref, y0_ref, y1_ref, g_ref, m_ref, o_ref):
    g = g_ref[0, 0].T
    f = g[:, 0:1] * y0_ref[...].astype(F32) + g[:, 1:2] * y1_ref[...].astype(F32)
    o_ref[0] = x_ref[0] + m_ref[0] * f


def moe_combine(x, yb, dest, gates, mod):
    b, t, d = x.shape
    y0 = yb[dest[:, 0]]
    y1 = yb[dest[:, 1]]
    nt, tt = gates.shape[1], gates.shape[3]
    row = pl.BlockSpec((tt, d), lambda i, j: (i * nt + j, 0))
    return pl.pallas_call(
        _moe_combine_kernel,
        grid=(b, nt),
        in_specs=[pl.BlockSpec((1, tt, d), lambda i, j: (i, j, 0)), row, row,
                  pl.BlockSpec((1, 1, ROUTE_ROWS, tt), lambda i, j: (i, j, 0, 0)),
                  pl.BlockSpec((1, 1, d), lambda i, j: (i, 0, 0))],
        out_specs=pl.BlockSpec((1, tt, d), lambda i, j: (i, j, 0)),
        out_shape=jax.ShapeDtypeStruct((b, t, d), F32),
        compiler_params=_cparams(("parallel", "parallel")),
        name="moe_combine",
    )(x, y0, y1, gates, mod.reshape(b, 1, d))


def _permute_w_in(w):
    d = w.shape[0]
    sizes = (DIFF_HEADS * 2 * DIFF_DIM, DIFF_HEADS * 2 * DIFF_DIM, DIFF_HEADS * DIFF_VDIM, 3 * ML_HEADS * ML_DIM,
             ML_HEADS * ML_DIM, 4 * ML_HEADS, GQA_HEADS * GQA_DIM, 2 * GQA_KV * GQA_DIM, (HY_ORDER + 1) * HY_CH,
             N_BRANCH * d)
    offs = np.cumsum((0,) + sizes)
    dq, dk, dv, mlqkv, mlo, mlg, gq, gkv, hy, gate = [w[:, offs[i]:offs[i + 1]] for i in range(10)]
    pad = jnp.zeros((d, N_P - C_MLG - 4 * ML_HEADS), w.dtype)
    return jnp.concatenate([gate, dq, dk, dv, mlqkv, mlo, gq, hy, gkv, mlg, pad], axis=1)


def kernel(x, c, ctx, c_ctx, w_ada, b_ada, norm1_g, norm2_g, w_in, diff_lam, diff_norm_g, ml_conv_w, ml_conv_b, ml_gate_b, ml_norm_g, gqa_qnorm_g, gqa_knorm_g, hy_conv_w, hy_conv_b, hy_f_w1, hy_f_b1, hy_f_w2, hy_f_b2, hy_f_w3, hy_f_freq, hy_bias, w_branch, w_out, moe_w_group, moe_b_group, moe_w_router, moe_b_router, moe_w_gate, moe_w_up, moe_w_down, final_norm_g):
    b, n, d = x.shape
    n_ctx = ctx.shape[1]
    depth = w_in.shape[0]
    tk = _tile(n, 2048)
    tables = _rope_tables(n)
    dft_tables = _dft_tables(n)
    sc = jax.nn.silu(c)
    scx = jax.nn.silu(c_ctx)
    xs, cs = x, ctx
    for l in range(depth):
        need_ctx = l < depth - 1
        mod_l = jnp.split(sc @ w_ada[l] + b_ada[l], 6, axis=-1)
        mod_c = [jnp.broadcast_to(m, (b, d)) for m in jnp.split(scx @ w_ada[l] + b_ada[l], 6, axis=-1)]
        w_p = _permute_w_in(w_in[l]).astype(BF16)
        hl = norm_mod(xs, norm1_g[l], mod_l[0], mod_l[1], BF16)
        hc = norm_mod(cs, norm1_g[l], mod_c[0], mod_c[1], BF16)
        p_l = matmul(hl.reshape(b * n, d), w_p, BF16, tm=1024).reshape(b, n, N_P)
        p_c = matmul(hc.reshape(b * n_ctx, d), w_p, BF16, tm=1024).reshape(b, n_ctx, N_P)

        dq_l, dkt_l, dv_l, gq_l, gkt_l, gv_l = attn_prep(p_l, tables, gqa_qnorm_g[l], gqa_knorm_g[l], tk)
        dq_c, dkt_c, dv_c, gq_c, gkt_c, gv_c = attn_prep(p_c, None, gqa_qnorm_g[l], gqa_knorm_g[l], n_ctx)
        lam_init = 0.8 - 0.6 * math.exp(-0.3 * l)
        lp = diff_lam[l].astype(F32)
        lam = jnp.exp(jnp.sum(lp[0] * lp[1])) - jnp.exp(jnp.sum(lp[2] * lp[3])) + lam_init
        yl_a = diff_attention(dq_l, dkt_c, dv_c, dkt_l, dv_l, lam, diff_norm_g[l], 1.0 - lam_init, 1024)
        yl_c = gqa_attention(gq_l, gkt_c, gv_c, gkt_l, gv_l, 1024)
        if need_ctx:
            yc_a = diff_attention(dq_c, dkt_c, dv_c, None, None, lam, diff_norm_g[l], 1.0 - lam_init, 256)
            yc_c = gqa_attention(gq_c, gkt_c, gv_c, None, None, 256)

        hs_c, hs_l = mlstm_branch(p_c, p_l, ml_conv_w[l], ml_conv_b[l], ml_gate_b[l], need_ctx)
        hy_args = (hy_conv_w[l], hy_conv_b[l], hy_f_w1[l], hy_f_b1[l], hy_f_w2[l], hy_f_b2[l],
                   hy_f_w3[l], hy_f_freq[l], hy_bias[l])
        yl_d = hyena(p_l, *hy_args, dft_tables)

        wb = w_branch[l].astype(BF16)
        wo = w_out[l].astype(BF16)
        xs = merge(yl_a, hs_l, yl_c, yl_d, p_l, ml_norm_g[l], wb, wo, xs, mod_l[2])
        route = (moe_w_group[l], moe_b_group[l], moe_w_router[l], moe_b_router[l])
        experts = (l, moe_w_gate, moe_w_up, moe_w_down)
        routed = [moe_router(xs, norm2_g[l], mod_l[3], mod_l[4], *route)]
        if need_ctx:
            yc_d = hyena(p_c, *hy_args, None)
            cs = merge(yc_a, hs_c, yc_c, yc_d, p_c, ml_norm_g[l], wb, wo, cs, mod_c[2])
            routed.insert(0, moe_router(cs, norm2_g[l], mod_c[3], mod_c[4], *route))
        h2 = jnp.concatenate([r[0].reshape(-1, d) for r in routed], axis=0)
        ids = jnp.concatenate([jnp.stack([r[1][:, :, k, :].reshape(-1) for k in range(MOE_TOP_K)], axis=1)
                               for r in routed], axis=0)
        yb, dest = _hier_moe(h2, ids, *experts)
        if need_ctx:
            nc = b * n_ctx
            cs = moe_combine(cs, yb, dest[:nc], routed[0][2], mod_c[5])
            dest = dest[nc:]
        xs = moe_combine(xs, yb, dest, routed[-1][2], mod_l[5])
    zero = jnp.zeros((b, d), F32)
    return norm_mod(xs, final_norm_g, zero, zero, F32)
```

```python
import functools
import math

import jax
import jax.numpy as jnp
import numpy as np
from jax import lax
from jax.experimental import pallas as pl
from jax.experimental.pallas import tpu as pltpu

F32 = jnp.float32
BF16 = jnp.bfloat16

EPS = 1e-6
ROPE_BASE = 10000.0
GRID_W = 64

DIFF_HEADS = 4
DIFF_DIM = 64
DIFF_VDIM = 128
ML_HEADS = 4
ML_DIM = 128
ML_CHUNK = 256
GQA_HEADS = 8
GQA_KV = 2
GQA_DIM = 64
HY_CH = 512
HY_ORDER = 2
FILTER_BANDS = 16
FILTER_SHIFT = 0.05
DECAY_TARGET = 1e-2
FAST_DECAY_PCT = 0.3
SLOW_DECAY_PCT = 1.5
N_BRANCH = 4
MOE_GROUPS = 4
MOE_EPG = 8
MOE_EXPERTS = MOE_GROUPS * MOE_EPG
MOE_TOP_K = 2
MOE_BLOCK = 256

LANES = 128
VMEM_LIMIT = 48 * 1024 * 1024
ROW_TILE = 512
GROUP_W = 512

C_GATE = 0
C_DQ = 4096
C_DK = 4608
C_DV = 5120
C_MLQ = 5632
C_MLK = 6144
C_MLV = 6656
C_MLO = 7168
C_GQ = 7680
C_HY = 8192
C_GK = 9728
C_GV = 9856
C_MLG = 9984
N_P = 10240

QSCALE = (DIFF_DIM ** -0.5) * math.log2(math.e)


def _cparams(sem):
    return pltpu.CompilerParams(dimension_semantics=sem, vmem_limit_bytes=VMEM_LIMIT)


def _tile(n, target):
    if n <= target:
        return n
    for t in range(target, 7, -1):
        if n % t == 0 and t % 8 == 0:
            return t
    return n


def _norm_mod_kernel(x_ref, g_ref, sh_ref, sc_ref, o_ref):
    x = x_ref[0]
    y = x * lax.rsqrt(jnp.mean(x * x, axis=-1, keepdims=True) + EPS)
    y = y * g_ref[...]
    o_ref[0] = (y * (1.0 + sc_ref[0]) + sh_ref[0]).astype(o_ref.dtype)


def norm_mod(x, g, shift, scale, out_dtype):
    b, t, d = x.shape
    tt = _tile(t, ROW_TILE)
    return pl.pallas_call(
        _norm_mod_kernel,
        grid=(b, t // tt),
        in_specs=[pl.BlockSpec((1, tt, d), lambda i, j: (i, j, 0)),
                  pl.BlockSpec((1, d), lambda i, j: (0, 0)),
                  pl.BlockSpec((1, 1, d), lambda i, j: (i, 0, 0)),
                  pl.BlockSpec((1, 1, d), lambda i, j: (i, 0, 0))],
        out_specs=pl.BlockSpec((1, tt, d), lambda i, j: (i, j, 0)),
        out_shape=jax.ShapeDtypeStruct((b, t, d), out_dtype),
        compiler_params=_cparams(("parallel", "parallel")),
        name="norm_mod",
    )(x, g.reshape(1, d), shift.reshape(b, 1, d), scale.reshape(b, 1, d))


def _mm_kernel(a_ref, w_ref, o_ref):
    o_ref[...] = jnp.dot(a_ref[...], w_ref[...], preferred_element_type=F32).astype(o_ref.dtype)


def matmul(a, w, out_dtype, tm=512, tn=1024):
    m, k = a.shape
    n = w.shape[1]
    tm = _tile(m, tm)
    tn = _tile(n, tn)
    return pl.pallas_call(
        _mm_kernel,
        grid=(m // tm, n // tn),
        in_specs=[pl.BlockSpec((tm, k), lambda i, j: (i, 0)),
                  pl.BlockSpec((k, tn), lambda i, j: (0, j))],
        out_specs=pl.BlockSpec((tm, tn), lambda i, j: (i, j)),
        out_shape=jax.ShapeDtypeStruct((m, n), out_dtype),
        compiler_params=_cparams(("parallel", "parallel")),
        name="matmul",
    )(a, w)


def _rope(x, cos, sa, sb):
    xa = pltpu.roll(x, LANES - 16, axis=1)
    xb = pltpu.roll(x, 16, axis=1)
    return x * cos + xa * sa + xb * sb


def _seg_rmsnorm(x, g):
    lane = lax.broadcasted_iota(jnp.int32, x.shape, 1)
    lo = lane < GQA_DIM
    ss = x * x
    s_lo = jnp.sum(jnp.where(lo, ss, 0.0), axis=-1, keepdims=True)
    s_hi = jnp.sum(jnp.where(lo, 0.0, ss), axis=-1, keepdims=True)
    r = jnp.where(lo, lax.rsqrt(s_lo * (1.0 / GQA_DIM) + EPS), lax.rsqrt(s_hi * (1.0 / GQA_DIM) + EPS))
    return x * r * g


def _prep_kernel(dq_ref, dk_ref, dv_ref, gq_ref, gkv_ref, cos_ref, sa_ref, sb_ref, qg_ref, kg_ref,
                 dq_o, dkt_o, dv_o, gq_o, gkt_o, gv_o, *, rope):
    if rope:
        cos, sa, sb = cos_ref[...], sa_ref[...], sb_ref[...]
        rot = lambda x: _rope(x, cos, sa, sb)
    else:
        rot = lambda x: x
    qg = qg_ref[...]
    kg = kg_ref[...]
    for j in range(DIFF_HEADS):
        sl = slice(j * LANES, (j + 1) * LANES)
        dq_o[0, :, sl] = (rot(dq_ref[0, :, sl].astype(F32)) * QSCALE).astype(BF16)
        kt = rot(dk_ref[0, :, sl].astype(F32)).T
        dkt_o[0, j, 0, 0] = kt[:DIFF_DIM].astype(BF16)
        dkt_o[0, j, 1, 0] = kt[DIFF_DIM:].astype(BF16)
        gq_o[0, :, sl] = (rot(_seg_rmsnorm(gq_ref[0, :, sl].astype(F32), qg)) * QSCALE).astype(BF16)
    dv_o[0] = dv_ref[0].astype(BF16)
    kt = rot(_seg_rmsnorm(gkv_ref[0, :, :LANES].astype(F32), kg)).T
    gkt_o[0, 0, 0] = kt[:GQA_DIM].astype(BF16)
    gkt_o[0, 1, 0] = kt[GQA_DIM:].astype(BF16)
    v = gkv_ref[0, :, LANES:].astype(BF16)
    gv_o[0, 0] = v[:, :GQA_DIM]
    gv_o[0, 1] = v[:, GQA_DIM:]


def attn_prep(p, tables, q_g, k_g, tk):
    b, t, _ = p.shape
    rope = tables is not None
    if rope:
        cos, sa, sb = tables
    else:
        cos = sa = sb = jnp.zeros((t, LANES), F32)
    nck = t // tk
    tp = _tile(tk, ROW_TILE)
    sub = tk // tp
    gw = GROUP_W
    kvw = 2 * GQA_KV * GQA_DIM
    grp = lambda c: pl.BlockSpec((1, tp, gw), lambda i, j, c=c: (i, j, c // gw))
    tab = pl.BlockSpec((tp, LANES), lambda i, j: (j, 0))
    vec = pl.BlockSpec((1, LANES), lambda i, j: (0, 0))
    outs = pl.pallas_call(
        functools.partial(_prep_kernel, rope=rope),
        grid=(b, t // tp),
        in_specs=[grp(C_DQ), grp(C_DK), grp(C_DV), grp(C_GQ),
                  pl.BlockSpec((1, tp, kvw), lambda i, j: (i, j, C_GK // kvw)),
                  tab, tab, tab, vec, vec],
        out_specs=[pl.BlockSpec((1, tp, gw), lambda i, j: (i, j, 0)),
                   pl.BlockSpec((1, DIFF_HEADS, 2, 1, DIFF_DIM, tp), lambda i, j: (i, 0, 0, j // sub, 0, j % sub)),
                   pl.BlockSpec((1, tp, gw), lambda i, j: (i, j, 0)),
                   pl.BlockSpec((1, tp, gw), lambda i, j: (i, j, 0)),
                   pl.BlockSpec((1, GQA_KV, 1, GQA_DIM, tp), lambda i, j: (i, 0, j // sub, 0, j % sub)),
                   pl.BlockSpec((1, GQA_KV, tp, GQA_DIM), lambda i, j: (i, 0, j, 0))],
        out_shape=[jax.ShapeDtypeStruct((b, t, gw), BF16),
                   jax.ShapeDtypeStruct((b, DIFF_HEADS, 2, nck, DIFF_DIM, tk), BF16),
                   jax.ShapeDtypeStruct((b, t, gw), BF16),
                   jax.ShapeDtypeStruct((b, t, gw), BF16),
                   jax.ShapeDtypeStruct((b, GQA_KV, nck, GQA_DIM, tk), BF16),
                   jax.ShapeDtypeStruct((b, GQA_KV, t, GQA_DIM), BF16)],
        compiler_params=_cparams(("parallel", "parallel")),
        name="attn_prep",
    )(p, p, p, p, p, cos, sa, sb,
      jnp.tile(q_g, 2).reshape(1, LANES), jnp.tile(k_g, 2).reshape(1, LANES))
    return outs


def _flash_step(q, kt, v, m_ref, l_ref, acc_ref):
    s = jnp.dot(q, kt, preferred_element_type=F32)
    tk = s.shape[1]
    m_prev = m_ref[...]
    m_next = jnp.maximum(m_prev, jnp.max(s, axis=1, keepdims=True))
    alpha = jnp.exp2(m_prev - m_next)
    p = jnp.exp2(s - jnp.concatenate([m_next] * (tk // LANES), axis=1))
    l_ref[...] = alpha * l_ref[...] + jnp.sum(p, axis=1, keepdims=True)
    dv = acc_ref.shape[-1]
    acc_ref[...] = acc_ref[...] * alpha[:, :dv] + jnp.dot(p.astype(BF16), v, preferred_element_type=F32)
    m_ref[...] = m_next


def _attn_body(q_ref, ktc, vc, ktl, vl, m_sc, l_sc, acc_sc, *, n_lat, tk):
    m_sc[...] = jnp.full(m_sc.shape, -jnp.inf, F32)
    l_sc[...] = jnp.zeros(l_sc.shape, F32)
    acc_sc[...] = jnp.zeros(acc_sc.shape, F32)
    q = q_ref[0]
    qs = (q[:, :DIFF_DIM], q[:, DIFF_DIM:])
    for c in range(2):
        _flash_step(qs[c], ktc(c), vc(c), m_sc.at[c], l_sc.at[c], acc_sc.at[c])
    if n_lat:
        def body(i, carry):
            for c in range(2):
                _flash_step(qs[c], ktl(c, i), vl(c, i), m_sc.at[c], l_sc.at[c], acc_sc.at[c])
            return carry
        lax.fori_loop(0, n_lat, body, 0)


def _diff_attn_kernel(*refs, n_lat, tk, out_scale):
    if n_lat:
        q_ref, ktc_ref, vc_ref, ktl_ref, vl_ref, lam_ref, g_ref, o_ref, m_sc, l_sc, acc_sc = refs
        ktl = lambda c, i: ktl_ref[0, 0, c, i]
        vl = lambda c, i: vl_ref[0, pl.ds(pl.multiple_of(i * tk, tk), tk), :]
    else:
        q_ref, ktc_ref, vc_ref, lam_ref, g_ref, o_ref, m_sc, l_sc, acc_sc = refs
        ktl = vl = None
    _attn_body(q_ref, lambda c: ktc_ref[0, 0, c, 0], lambda c: vc_ref[0], ktl, vl,
               m_sc, l_sc, acc_sc, n_lat=n_lat, tk=tk)
    o0 = acc_sc[0] / l_sc[0]
    o1 = acc_sc[1] / l_sc[1]
    o = o0 - lam_ref[...] * o1
    o = o * lax.rsqrt(jnp.mean(o * o, axis=-1, keepdims=True) + EPS)
    o_ref[0] = (o * g_ref[...] * out_scale).astype(o_ref.dtype)


def diff_attention(q, ktc, vc, ktl, vl, lam, norm_g, out_scale, tq):
    b, t, _ = q.shape
    sc = vc.shape[1]
    tq = _tile(t, tq)
    n_lat, tk = (ktl.shape[3], ktl.shape[5]) if ktl is not None else (0, 0)
    in_specs = [pl.BlockSpec((1, tq, LANES), lambda i, h, j: (i, j, h)),
                pl.BlockSpec((1, 1, 2, 1, DIFF_DIM, sc), lambda i, h, j: (i, h, 0, 0, 0, 0)),
                pl.BlockSpec((1, sc, LANES), lambda i, h, j: (i, 0, h))]
    args = [q, ktc, vc]
    if n_lat:
        in_specs += [pl.BlockSpec((1, 1, 2, n_lat, DIFF_DIM, tk), lambda i, h, j: (i, h, 0, 0, 0, 0)),
                     pl.BlockSpec((1, n_lat * tk, LANES), lambda i, h, j: (i, 0, h))]
        args += [ktl, vl]
    vec = pl.BlockSpec((1, LANES), lambda i, h, j: (0, 0))
    in_specs += [vec, vec]
    args += [jnp.full((1, LANES), lam, F32), norm_g.reshape(1, LANES)]
    return pl.pallas_call(
        functools.partial(_diff_attn_kernel, n_lat=n_lat, tk=tk, out_scale=out_scale),
        grid=(b, DIFF_HEADS, t // tq),
        in_specs=in_specs,
        out_specs=pl.BlockSpec((1, tq, LANES), lambda i, h, j: (i, j, h)),
        out_shape=jax.ShapeDtypeStruct((b, t, DIFF_HEADS * DIFF_VDIM), BF16),
        scratch_shapes=[pltpu.VMEM((2, tq, LANES), F32), pltpu.VMEM((2, tq, LANES), F32),
                        pltpu.VMEM((2, tq, DIFF_VDIM), F32)],
        compiler_params=_cparams(("parallel", "parallel", "parallel")),
        name="diff_attn",
    )(*args)


def _gqa_attn_kernel(*refs, n_lat, tk):
    if n_lat:
        q_ref, ktc_ref, vc_ref, ktl_ref, vl_ref, o_ref, m_sc, l_sc, acc_sc = refs
        ktl = lambda c, i: ktl_ref[0, 0, i]
        vl = lambda c, i: vl_ref[0, 0, pl.ds(pl.multiple_of(i * tk, tk), tk), :]
    else:
        q_ref, ktc_ref, vc_ref, o_ref, m_sc, l_sc, acc_sc = refs
        ktl = vl = None
    _attn_body(q_ref, lambda c: ktc_ref[0, 0, 0], lambda c: vc_ref[0, 0], ktl, vl,
               m_sc, l_sc, acc_sc, n_lat=n_lat, tk=tk)
    o0 = acc_sc[0] / l_sc[0][:, :GQA_DIM]
    o1 = acc_sc[1] / l_sc[1][:, :GQA_DIM]
    o_ref[0] = jnp.concatenate([o0, o1], axis=-1).astype(o_ref.dtype)


def gqa_attention(q, ktc, vc, ktl, vl, tq):
    b, t, _ = q.shape
    sc = vc.shape[2]
    tq = _tile(t, tq)
    n_lat, tk = (ktl.shape[2], ktl.shape[4]) if ktl is not None else (0, 0)
    pairs = GQA_HEADS // 2
    grp = lambda h: h // (pairs // GQA_KV)
    in_specs = [pl.BlockSpec((1, tq, LANES), lambda i, h, j: (i, j, h)),
                pl.BlockSpec((1, 1, 1, GQA_DIM, sc), lambda i, h, j: (i, grp(h), 0, 0, 0)),
                pl.BlockSpec((1, 1, sc, GQA_DIM), lambda i, h, j: (i, grp(h), 0, 0))]
    args = [q, ktc, vc]
    if n_lat:
        in_specs += [pl.BlockSpec((1, 1, n_lat, GQA_DIM, tk), lambda i, h, j: (i, grp(h), 0, 0, 0)),
                     pl.BlockSpec((1, 1, n_lat * tk, GQA_DIM), lambda i, h, j: (i, grp(h), 0, 0))]
        args += [ktl, vl]
    return pl.pallas_call(
        functools.partial(_gqa_attn_kernel, n_lat=n_lat, tk=tk),
        grid=(b, pairs, t // tq),
        in_specs=in_specs,
        out_specs=pl.BlockSpec((1, tq, LANES), lambda i, h, j: (i, j, h)),
        out_shape=jax.ShapeDtypeStruct((b, t, GQA_HEADS * GQA_DIM), BF16),
        scratch_shapes=[pltpu.VMEM((2, tq, LANES), F32), pltpu.VMEM((2, tq, LANES), F32),
                        pltpu.VMEM((2, tq, GQA_DIM), F32)],
        compiler_params=_cparams(("parallel", "parallel", "parallel")),
        name="gqa_attn",
    )(*args)


ML_STEP = 256
HALO = 16


def _halo_rows(prev_ref, next_ref, j, nblk):
    prev = jnp.where(j > 0, prev_ref[0, HALO - 1:HALO, :].astype(F32), 0.0)
    nxt = jnp.where(j < nblk - 1, next_ref[0, 0:1, :].astype(F32), 0.0)
    return prev, nxt


def _conv3(x, prev, nxt, w, bias):
    tt = x.shape[0]
    row = lax.broadcasted_iota(jnp.int32, x.shape, 0)
    xm = jnp.where(row == 0, prev, pltpu.roll(x, 1, axis=0))
    xp = jnp.where(row == tt - 1, nxt, pltpu.roll(x, tt - 1, axis=0))
    return xm * w[0:1] + x * w[1:2] + xp * w[2:3] + bias


def _halo_specs(tt, width, col_block, t):
    nb = t // HALO
    prev = pl.BlockSpec((1, HALO, width), lambda i, j: (i, jnp.maximum(j * (tt // HALO) - 1, 0), col_block))
    nxt = pl.BlockSpec((1, HALO, width), lambda i, j: (i, jnp.minimum((j + 1) * (tt // HALO), nb - 1), col_block))
    return prev, nxt


def _ml_prep_kernel(q_ref, qp_ref, qn_ref, k_ref, kp_ref, kn_ref, g_ref, w_ref, b_ref, gb_ref, q_o, k_o, g_o):
    j = pl.program_id(1)
    nblk = pl.num_programs(1)
    w = ML_HEADS * ML_DIM
    qp, qn = _halo_rows(qp_ref, qn_ref, j, nblk)
    kp, kn = _halo_rows(kp_ref, kn_ref, j, nblk)
    q = _conv3(q_ref[0].astype(F32), qp, qn, w_ref[:, :w], b_ref[:, :w])
    k = _conv3(k_ref[0].astype(F32), kp, kn, w_ref[:, w:], b_ref[:, w:])
    q_o[0] = q * jax.nn.sigmoid(q)
    k_o[0] = (k * jax.nn.sigmoid(k)) * ML_DIM ** -0.5
    x = g_ref[0].astype(F32) + gb_ref[...]
    lane = lax.broadcasted_iota(jnp.int32, x.shape, 1)
    log_sig = jnp.minimum(x, 0.0) - jnp.log(1.0 + jnp.exp(-jnp.abs(x)))
    g_o[0] = jnp.where((lane % 8) >= ML_HEADS, log_sig, x)


def ml_prep(p, conv_w, conv_b, gate_b):
    b, t, _ = p.shape
    tt = _tile(t, ROW_TILE)
    w = ML_HEADS * ML_DIM
    blk = lambda c: pl.BlockSpec((1, tt, w), lambda i, j, c=c: (i, j, c // w))
    qp, qn = _halo_specs(tt, w, C_MLQ // w, t)
    kp, kn = _halo_specs(tt, w, C_MLK // w, t)
    gb = jnp.zeros((1, LANES), F32).at[0, :4 * ML_HEADS].set(gate_b.reshape(-1))
    return pl.pallas_call(
        _ml_prep_kernel,
        grid=(b, t // tt),
        in_specs=[blk(C_MLQ), qp, qn, blk(C_MLK), kp, kn,
                  pl.BlockSpec((1, tt, LANES), lambda i, j: (i, j, C_MLG // LANES)),
                  pl.BlockSpec((3, 2 * w), lambda i, j: (0, 0)),
                  pl.BlockSpec((1, 2 * w), lambda i, j: (0, 0)),
                  pl.BlockSpec((1, LANES), lambda i, j: (0, 0))],
        out_specs=[pl.BlockSpec((1, tt, w), lambda i, j: (i, j, 0)),
                   pl.BlockSpec((1, tt, w), lambda i, j: (i, j, 0)),
                   pl.BlockSpec((1, tt, LANES), lambda i, j: (i, j, 0))],
        out_shape=[jax.ShapeDtypeStruct((b, t, w), F32), jax.ShapeDtypeStruct((b, t, w), F32),
                   jax.ShapeDtypeStruct((b, t, LANES), F32)],
        compiler_params=_cparams(("parallel", "parallel")),
        name="ml_prep",
    )(p, p, p, p, p, p, p, conv_w, conv_b.reshape(1, 2 * w), gb)


_NT = (((1,), (1,)), ((), ()))


def _ml_chunk_head(q, k, v, li_c, bc_c, bc_r, b_last, seen, ct_ref, n_ref, m_ref):
    m = m_ref[:, 0:1]
    ct = ct_ref[...]
    n_rows = n_ref[...]
    qb, kb, vb = q.astype(BF16), k.astype(BF16), v.astype(BF16)
    d_t = jnp.where(seen, bc_r + (li_c - bc_c), -jnp.inf)
    inter = bc_r + m
    m_t = jnp.maximum(inter, jnp.max(d_t, axis=0, keepdims=True))
    w_inter = jnp.exp(inter - m_t)
    s_t = lax.dot_general(kb, qb, _NT, preferred_element_type=F32) * jnp.exp(d_t - m_t)
    num_t = (jnp.dot(vb.T, s_t.astype(BF16), preferred_element_type=F32)
             + w_inter * lax.dot_general(ct.astype(BF16), qb, _NT, preferred_element_type=F32))
    qn = lax.dot_general(n_rows.astype(BF16), qb, _NT, preferred_element_type=F32)[0:1]
    den = jnp.sum(s_t, axis=0, keepdims=True) + w_inter * qn
    h_t = num_t / jnp.maximum(jnp.abs(den), jnp.exp(-m_t))
    g = b_last - bc_c + li_c
    m_new = jnp.maximum(b_last + m, jnp.max(g, axis=0, keepdims=True))
    kw = k * jnp.exp(g - m_new)
    wc = jnp.exp(b_last + m - m_new)
    ct_ref[...] = wc * ct + jnp.dot(vb.T, kw.astype(BF16), preferred_element_type=F32)
    n_ref[...] = wc * n_rows + jnp.sum(kw, axis=0, keepdims=True)
    m_ref[...] = jnp.broadcast_to(m_new, m_ref.shape)
    return h_t.T


def _ml_scan_kernel(q_ref, k_ref, v_ref, g_ref, c0_ref, n0_ref, m0_ref, h_o, c1_o, n1_o, m1_o,
                    c_sc, n_sc, m_sc, *, direction):
    j = pl.program_id(0)

    @pl.when(j == 0)
    def _():
        c_sc[...] = c0_ref[...]
        n_sc[...] = n0_ref[...]
        m_sc[...] = m0_ref[...]

    r = lax.broadcasted_iota(jnp.int32, (ML_CHUNK, ML_CHUNK), 0)
    s = lax.broadcasted_iota(jnp.int32, (ML_CHUNK, ML_CHUNK), 1)
    tri_f = ((s >= r) if direction else (s <= r)).astype(F32)
    seen = (r >= s) if direction else (r <= s)
    n_chunks = q_ref.shape[1] // ML_CHUNK
    order = range(n_chunks - 1, -1, -1) if direction else range(n_chunks)
    last = 0 if direction else ML_CHUNK - 1
    for c in order:
        rows = slice(c * ML_CHUNK, (c + 1) * ML_CHUNK)
        for bi in range(q_ref.shape[0]):
            gch = g_ref[bi, rows, :]
            bc = jnp.dot(tri_f, gch, preferred_element_type=F32, precision=lax.Precision.HIGHEST)
            bct = bc.T
            for hd in range(ML_HEADS):
                ci = 2 * ML_HEADS * direction + hd
                cf = ci + ML_HEADS
                cols = slice(hd * ML_DIM, (hd + 1) * ML_DIM)
                h = _ml_chunk_head(q_ref[bi, rows, cols], k_ref[bi, rows, cols], v_ref[bi, rows, cols],
                                   gch[:, ci:ci + 1], bc[:, cf:cf + 1], bct[cf:cf + 1, :],
                                   bc[last:last + 1, cf:cf + 1], seen,
                                   c_sc.at[bi, hd], n_sc.at[bi, hd], m_sc.at[bi, hd])
                h_o[bi, rows, cols] = h

    @pl.when(j == pl.num_programs(0) - 1)
    def _():
        c1_o[...] = c_sc[...]
        n1_o[...] = n_sc[...]
        m1_o[...] = m_sc[...]


def _ml_state_shapes(b):
    return [(b, ML_HEADS, ML_DIM, ML_DIM), (b, ML_HEADS, 8, ML_DIM), (b, ML_HEADS, 1, ML_DIM)]


def ml_scan(q, k, p, g, state, direction):
    b, t, w = q.shape
    ts = _tile(t, ML_STEP)
    nst = t // ts
    tok = (lambda j: (0, nst - 1 - j, 0)) if direction else (lambda j: (0, j, 0))
    tokv = (lambda j: (0, nst - 1 - j, C_MLV // w)) if direction else (lambda j: (0, j, C_MLV // w))
    st_shapes = _ml_state_shapes(b)
    st_specs = [pl.BlockSpec(s, lambda j: (0, 0, 0, 0)) for s in st_shapes]
    h, c1, n1, m1 = pl.pallas_call(
        functools.partial(_ml_scan_kernel, direction=direction),
        grid=(nst,),
        in_specs=[pl.BlockSpec((b, ts, w), tok), pl.BlockSpec((b, ts, w), tok), pl.BlockSpec((b, ts, w), tokv),
                  pl.BlockSpec((b, ts, LANES), tok)] + st_specs,
        out_specs=[pl.BlockSpec((b, ts, w), tok)] + st_specs,
        out_shape=[jax.ShapeDtypeStruct((b, t, w), F32)] + [jax.ShapeDtypeStruct(s, F32) for s in st_shapes],
        scratch_shapes=[pltpu.VMEM(s, F32) for s in st_shapes],
        compiler_params=_cparams(("arbitrary",)),
        name="ml_scan",
    )(q, k, p, g, *state)
    return h, (c1, n1, m1)


def mlstm_branch(p_c, p_l, conv_w, conv_b, gate_b, need_ctx):
    b = p_l.shape[0]
    qc, kc, gc = ml_prep(p_c, conv_w, conv_b, gate_b)
    ql, kl, gl = ml_prep(p_l, conv_w, conv_b, gate_b)
    zero = tuple(jnp.zeros(s, F32) for s in _ml_state_shapes(b))
    hs_c, hs_l = [], []
    for direction in (0, 1):
        hc, st = ml_scan(qc, kc, p_c, gc, zero, direction)
        hl, _ = ml_scan(ql, kl, p_l, gl, st, direction)
        hs_c.append(hc)
        hs_l.append(hl)
    return (tuple(hs_c) if need_ctx else None), tuple(hs_l)


def _merge_kernel(ya_ref, hf_ref, hb_ref, yc_ref, yd_ref, og_ref, g_ref, mg_ref, wb_ref, wo_ref, x_ref, gate_ref,
                  o_ref):
    d = x_ref.shape[-1]
    mg = mg_ref[...]
    yb = []
    for hd in range(ML_HEADS):
        cols = slice(hd * ML_DIM, (hd + 1) * ML_DIM)
        h = hf_ref[0, :, cols] + hb_ref[0, :, cols]
        h = h * lax.rsqrt(jnp.mean(h * h, axis=-1, keepdims=True) + EPS) * mg
        yb.append((h * jax.nn.sigmoid(og_ref[0, :, cols].astype(F32))).astype(BF16))
    ys = (ya_ref[0].astype(BF16), jnp.concatenate(yb, axis=-1), yc_ref[0].astype(BF16), yd_ref[0].astype(BF16))
    acc = None
    for n, y in enumerate(ys):
        t = jnp.dot(y, wb_ref[n], preferred_element_type=F32)
        t = jax.nn.sigmoid(g_ref[0, :, n * d:(n + 1) * d].astype(F32)) * t
        acc = t if acc is None else acc + t
    z = jnp.dot(acc.astype(BF16), wo_ref[...], preferred_element_type=F32)
    o_ref[0] = x_ref[0] + gate_ref[0] * z


def merge(ya, hs, yc, yd, p, ml_norm_g, w_branch, w_out, x, gate):
    b, t, d = x.shape
    tm = _tile(t, ROW_TILE)
    w = ML_HEADS * ML_DIM
    ysp = pl.BlockSpec((1, tm, w), lambda i, j: (i, j, 0))
    return pl.pallas_call(
        _merge_kernel,
        grid=(b, t // tm),
        in_specs=[ysp, ysp, ysp, ysp, ysp,
                  pl.BlockSpec((1, tm, w), lambda i, j: (i, j, C_MLO // w)),
                  pl.BlockSpec((1, tm, N_BRANCH * d), lambda i, j: (i, j, C_GATE // (N_BRANCH * d))),
                  pl.BlockSpec((1, ML_DIM), lambda i, j: (0, 0)),
                  pl.BlockSpec((N_BRANCH, w, d), lambda i, j: (0, 0, 0)),
                  pl.BlockSpec((d, d), lambda i, j: (0, 0)),
                  pl.BlockSpec((1, tm, d), lambda i, j: (i, j, 0)),
                  pl.BlockSpec((1, 1, d), lambda i, j: (i, 0, 0))],
        out_specs=pl.BlockSpec((1, tm, d), lambda i, j: (i, j, 0)),
        out_shape=jax.ShapeDtypeStruct((b, t, d), F32),
        compiler_params=_cparams(("parallel", "parallel")),
        name="merge",
    )(ya, hs[0], hs[1], yc, yd, p, p, ml_norm_g.reshape(1, ML_DIM), w_branch, w_out, x, gate.reshape(b, 1, d))


def _expert_kernel(be_ref, x_ref, wg_ref, wu_ref, wd_ref, o_ref, wg_sc, wu_sc, wd_sc):
    i = pl.program_id(0)

    @pl.when(jnp.logical_or(i == 0, be_ref[i] != be_ref[jnp.maximum(i - 1, 0)]))
    def _():
        wg_sc[...] = wg_ref[0, 0].astype(BF16)
        wu_sc[...] = wu_ref[0, 0].astype(BF16)
        wd_sc[...] = wd_ref[0, 0].astype(BF16)

    x = x_ref[...]
    a = jnp.dot(x, wg_sc[...], preferred_element_type=F32)
    u = jnp.dot(x, wu_sc[...], preferred_element_type=F32)
    h = (a * jax.nn.sigmoid(a)) * u
    o_ref[...] = jnp.dot(h.astype(BF16), wd_sc[...], preferred_element_type=F32).astype(o_ref.dtype)


def expert_blocks(xb, blk_exp, layer, w_gate, w_up, w_down):
    m, d = xb.shape
    hdim = w_gate.shape[-1]
    n_blocks = m // MOE_BLOCK
    grid_spec = pltpu.PrefetchScalarGridSpec(
        num_scalar_prefetch=1,
        grid=(n_blocks,),
        in_specs=[pl.BlockSpec((MOE_BLOCK, d), lambda i, be: (i, 0)),
                  pl.BlockSpec((1, 1, d, hdim), lambda i, be: (layer, be[i], 0, 0)),
                  pl.BlockSpec((1, 1, d, hdim), lambda i, be: (layer, be[i], 0, 0)),
                  pl.BlockSpec((1, 1, hdim, d), lambda i, be: (layer, be[i], 0, 0))],
        out_specs=pl.BlockSpec((MOE_BLOCK, d), lambda i, be: (i, 0)),
        scratch_shapes=[pltpu.VMEM((d, hdim), BF16), pltpu.VMEM((d, hdim), BF16), pltpu.VMEM((hdim, d), BF16)],
    )
    return pl.pallas_call(
        _expert_kernel,
        grid_spec=grid_spec,
        out_shape=jax.ShapeDtypeStruct((m, d), BF16),
        compiler_params=_cparams(("arbitrary",)),
        name="moe_experts",
    )(blk_exp, xb, w_gate, w_up, w_down)


def _rope_tables(n_tok):
    rows = n_tok // GRID_W
    row = jnp.repeat(jnp.arange(rows, dtype=F32), GRID_W)
    col = jnp.broadcast_to(jnp.arange(GRID_W, dtype=F32), (rows, GRID_W)).reshape(-1)
    n_freq = DIFF_DIM // 4
    inv = ROPE_BASE ** (-jnp.arange(n_freq, dtype=F32) / n_freq)
    ar = row[:, None] * inv
    ac = col[:, None] * inv
    ang = jnp.concatenate([ar, ar, ac, ac], axis=-1)
    cos, sin = jnp.cos(ang), jnp.sin(ang)
    first = (jnp.arange(DIFF_DIM) % 32) < 16
    sa = jnp.where(first, -sin, 0.0)
    sb = jnp.where(first, 0.0, sin)
    return tuple(jnp.tile(a, (1, 2)) for a in (cos, sa, sb))


DFT_N2 = 256
DFT_J = 8
DFT_P = 4


def _hy_prep_kernel(*refs):
    ins, outs = refs[:9], refs[11:]
    w_ref, b_ref = refs[9], refs[10]
    j = pl.program_id(1)
    nblk = pl.num_programs(1)
    for n in range(HY_ORDER + 1):
        x_ref, p_ref, n_ref = ins[3 * n:3 * n + 3]
        cols = slice(n * HY_CH, (n + 1) * HY_CH)
        prev, nxt = _halo_rows(p_ref, n_ref, j, nblk)
        outs[n][0] = _conv3(x_ref[0].astype(F32), prev, nxt, w_ref[:, cols], b_ref[:, cols]).astype(outs[n].dtype)


def hy_prep(p, conv_w, conv_b):
    b, t, _ = p.shape
    tt = _tile(t, ROW_TILE)
    in_specs, args = [], []
    for n in range(HY_ORDER + 1):
        cb = C_HY // HY_CH + n
        prev, nxt = _halo_specs(tt, HY_CH, cb, t)
        in_specs += [pl.BlockSpec((1, tt, HY_CH), lambda i, j, cb=cb: (i, j, cb)), prev, nxt]
        args += [p, p, p]
    nch = (HY_ORDER + 1) * HY_CH
    in_specs += [pl.BlockSpec((3, nch), lambda i, j: (0, 0)), pl.BlockSpec((1, nch), lambda i, j: (0, 0))]
    osp = pl.BlockSpec((1, tt, HY_CH), lambda i, j: (i, j, 0))
    return pl.pallas_call(
        _hy_prep_kernel,
        grid=(b, t // tt),
        in_specs=in_specs,
        out_specs=[osp] * (HY_ORDER + 1),
        out_shape=[jax.ShapeDtypeStruct((b, t, HY_CH), BF16)] * (HY_ORDER + 1),
        compiler_params=_cparams(("parallel", "parallel")),
        name="hy_prep",
    )(*args, conv_w, conv_b.reshape(1, nch))


def _hy_filter_kernel(emb_ref, w1_ref, b1_ref, w2_ref, b2_ref, w3_ref, fr_ref, al_ref, f_o, ss_o, *, length, half):
    j = pl.program_id(0)
    tt = emb_ref.shape[0]
    a = jnp.dot(emb_ref[...].astype(BF16), w1_ref[...].astype(BF16), preferred_element_type=F32) + b1_ref[...]
    a = jnp.sin(fr_ref[0:1, :] * a)
    a = jnp.dot(a.astype(BF16), w2_ref[...].astype(BF16), preferred_element_type=F32) + b2_ref[...]
    a = jnp.sin(fr_ref[1:2, :] * a)
    filt = jnp.dot(a.astype(BF16), w3_ref[...].astype(BF16), preferred_element_type=F32)
    r = lax.broadcasted_iota(jnp.int32, (tt, HY_CH), 0)
    if half:
        row = DFT_N2 * (r & (half - 1)) + j * (tt // half) + (r >> (half.bit_length() - 1))
    else:
        row = r + j * tt
    window = jnp.exp(-(row.astype(F32) / length) * al_ref[...]) + FILTER_SHIFT

    @pl.when(j == 0)
    def _():
        ss_o[...] = jnp.zeros(ss_o.shape, F32)

    for o in range(HY_ORDER):
        for d in range(2):
            idx = 2 * o + d
            f = filt[:, idx * HY_CH:(idx + 1) * HY_CH] * window
            if d == 1:
                f = jnp.where(row == 0, 0.0, f)
            if half:
                for jj in range(tt // half):
                    f_o[idx, :, jj * HY_CH:(jj + 1) * HY_CH] = f[jj * half:(jj + 1) * half]
            else:
                f_o[idx] = f
            ss_o[o:o + 1, :] += jnp.sum(f * f, axis=0, keepdims=True)


def hy_filters(length, w1, b1, w2, b2, w3, freq, wide):
    t = jnp.arange(length, dtype=F32) / length
    bands = jnp.arange(1, FILTER_BANDS + 1, dtype=F32)
    ang = 2.0 * math.pi * t[:, None] * bands
    emb = jnp.concatenate([t[:, None], jnp.cos(ang), jnp.sin(ang)], axis=-1)
    pad = LANES - emb.shape[1]
    emb = jnp.pad(emb, ((0, 0), (0, pad)))
    w1 = jnp.pad(w1, ((0, pad), (0, 0)))
    ne, nh = emb.shape[1], w1.shape[1]
    alpha = jnp.linspace(abs(math.log(DECAY_TARGET)) / SLOW_DECAY_PCT,
                         abs(math.log(DECAY_TARGET)) / FAST_DECAY_PCT, HY_CH).reshape(1, HY_CH)
    tt = _tile(length, ROW_TILE)
    half = length // DFT_N2
    wide = wide and half > 0 and half & (half - 1) == 0 and tt % half == 0
    full = lambda shape: pl.BlockSpec(shape, lambda j: (0,) * len(shape))
    if wide:
        emb = jnp.swapaxes(emb.reshape(half, DFT_N2, ne), 0, 1).reshape(length, ne)
        f_spec = pl.BlockSpec((2 * HY_ORDER, half, (tt // half) * HY_CH), lambda j: (0, 0, j))
        f_shape = (2 * HY_ORDER, half, DFT_N2 * HY_CH)
    else:
        f_spec = pl.BlockSpec((2 * HY_ORDER, tt, HY_CH), lambda j: (0, j, 0))
        f_shape = (2 * HY_ORDER, length, HY_CH)
    f, ss = pl.pallas_call(
        functools.partial(_hy_filter_kernel, length=length, half=half if wide else 0),
        grid=(length // tt,),
        in_specs=[pl.BlockSpec((tt, ne), lambda j: (j, 0)), full((ne, nh)), full((1, nh)), full((nh, nh)),
                  full((1, nh)), full((nh, 2 * HY_ORDER * HY_CH)), full((2, nh)), full((1, HY_CH))],
        out_specs=[f_spec, full((HY_ORDER, HY_CH))],
        out_shape=[jax.ShapeDtypeStruct(f_shape, F32), jax.ShapeDtypeStruct((HY_ORDER, HY_CH), F32)],
        compiler_params=_cparams(("arbitrary",)),
        name="hy_filter",
    )(emb, w1, b1.reshape(1, nh), w2, b2.reshape(1, nh), w3, freq, alpha)
    return f, lax.rsqrt(ss + EPS)


def _dft_tables(length):
    n = 2 * length
    n1 = n // DFT_N2
    half = n1 // 2
    n1h = -(-(half + 1) // 16) * 16
    kv = jnp.arange(n1h, dtype=jnp.int32)
    valid = (kv <= half).astype(F32)[None, :, None]
    pair = jnp.where((kv == 0) | (kv == half), 1.0, 2.0)[None, :, None] * valid
    k1 = kv[None, :, None]
    tn = (DFT_N2 * jnp.arange(half, dtype=jnp.int32)[None, None, :]
          + jnp.arange(DFT_N2, dtype=jnp.int32)[:, None, None])
    th = (2.0 * math.pi / n) * ((k1 * tn) % n).astype(F32)
    ga = jnp.concatenate([jnp.cos(th) * valid, -jnp.sin(th) * valid], axis=1).astype(BF16)
    gi = jnp.swapaxes(jnp.concatenate([jnp.cos(th) * pair, -jnp.sin(th) * pair], axis=1), 1, 2).astype(BF16)
    kk = jnp.arange(DFT_N2, dtype=jnp.int32)
    t2 = (2.0 * math.pi / DFT_N2) * ((kk[:, None] * kk[None, :]) % DFT_N2).astype(F32)
    c2, s2 = jnp.cos(t2), jnp.sin(t2)
    mf = jnp.block([[c2, s2], [-s2, c2]]).astype(BF16)
    mi = jnp.block([[c2, -s2], [s2, c2]]).astype(BF16)
    return ga, gi, mf, mi


def _dft_a_kernel(z_ref, g_ref, o_ref):
    c = HY_CH
    for j in range(DFT_J):
        slab = z_ref[0, :, j * c:(j + 1) * c].astype(BF16)
        r = jnp.dot(g_ref[j], slab, preferred_element_type=F32)
        o_ref[0, :, :, j * c:(j + 1) * c] = r.reshape(2, r.shape[0] // 2, c).astype(o_ref.dtype)


def dft_a(z, ga):
    bz, half, wid = z.shape
    n1 = ga.shape[1] // 2
    jc = DFT_J * HY_CH
    return pl.pallas_call(
        _dft_a_kernel,
        grid=(bz, DFT_N2 // DFT_J),
        in_specs=[pl.BlockSpec((1, half, jc), lambda i, j: (i, 0, j)),
                  pl.BlockSpec((DFT_J, 2 * n1, half), lambda i, j: (j, 0, 0))],
        out_specs=pl.BlockSpec((1, 2, n1, jc), lambda i, j: (i, 0, 0, j)),
        out_shape=jax.ShapeDtypeStruct((bz, 2, n1, wid), BF16),
        compiler_params=_cparams(("parallel", "parallel")),
        name="dft_a",
    )(z, ga)


def _stack_ri(ref, b, k):
    return jnp.concatenate([ref[b, 0, k], ref[b, 1, k]], axis=0)


def _spec_filter_kernel(f_ref, mf_ref, sc_ref, h_o):
    sc = sc_ref[0]
    for k in range(DFT_P):
        xf = jnp.dot(mf_ref[...], _stack_ri(f_ref, 0, k), preferred_element_type=F32)
        xb = jnp.dot(mf_ref[...], _stack_ri(f_ref, 1, k), preferred_element_type=F32)
        h_o[0, k, 0] = ((xf[:DFT_N2] + xb[:DFT_N2]) * sc).astype(h_o.dtype)
        h_o[0, k, 1] = ((xf[DFT_N2:] - xb[DFT_N2:]) * sc).astype(h_o.dtype)


def spec_filter(fa, mf, scale):
    nb, _, n1, _, c = fa.shape
    order = nb // 2
    return pl.pallas_call(
        _spec_filter_kernel,
        grid=(order, n1 // DFT_P),
        in_specs=[pl.BlockSpec((2, 2, DFT_P, DFT_N2, c), lambda o, k: (o, 0, k, 0, 0)),
                  pl.BlockSpec((2 * DFT_N2, 2 * DFT_N2), lambda o, k: (0, 0)),
                  pl.BlockSpec((1, 1, c), lambda o, k: (o, 0, 0))],
        out_specs=pl.BlockSpec((1, DFT_P, 2, DFT_N2, c), lambda o, k: (o, k, 0, 0, 0)),
        out_shape=jax.ShapeDtypeStruct((order, n1, 2, DFT_N2, c), BF16),
        compiler_params=_cparams(("parallel", "parallel")),
        name="spec_filter",
    )(fa, mf, scale.reshape(order, 1, c))


def _spec_conv_kernel(a_ref, h_ref, mf_ref, mi_ref, o_ref):
    for k in range(DFT_P):
        x = jnp.dot(mf_ref[...], _stack_ri(a_ref, 0, k), preferred_element_type=F32)
        xr, xi = x[:DFT_N2], x[DFT_N2:]
        hr, hi = h_ref[0, k, 0].astype(F32), h_ref[0, k, 1].astype(F32)
        y = jnp.concatenate([xr * hr - xi * hi, xr * hi + xi * hr], axis=0).astype(BF16)
        z = jnp.dot(mi_ref[...], y, preferred_element_type=F32)
        o_ref[0, 0, k] = z[:DFT_N2].astype(o_ref.dtype)
        o_ref[0, 1, k] = z[DFT_N2:].astype(o_ref.dtype)


def spec_conv(a, h, order, mf, mi):
    b, _, n1, _, c = a.shape
    blk = pl.BlockSpec((1, 2, DFT_P, DFT_N2, c), lambda i, k: (i, 0, k, 0, 0))
    mat = pl.BlockSpec((2 * DFT_N2, 2 * DFT_N2), lambda i, k: (0, 0))
    return pl.pallas_call(
        _spec_conv_kernel,
        grid=(b, n1 // DFT_P),
        in_specs=[blk, pl.BlockSpec((1, DFT_P, 2, DFT_N2, c), lambda i, k: (order, k, 0, 0, 0)), mat, mat],
        out_specs=blk,
        out_shape=jax.ShapeDtypeStruct(a.shape, BF16),
        compiler_params=_cparams(("parallel", "parallel")),
        name="spec_conv",
    )(a, h, mf, mi)


def _dft_ainv_kernel(z_ref, g_ref, xg_ref, zin_ref, bias_ref, o_ref):
    c = HY_CH
    n1 = z_ref.shape[2]
    for j in range(DFT_J):
        cols = slice(j * c, (j + 1) * c)
        zz = z_ref[0, :, :, cols].reshape(2 * n1, c)
        y = jnp.dot(g_ref[j], zz, preferred_element_type=F32)
        gated = xg_ref[0, :, cols].astype(F32) * (y + bias_ref[...] * zin_ref[0, :, cols].astype(F32))
        o_ref[0, :, cols] = gated.astype(o_ref.dtype)


def dft_ainv(z, gi, xg, zin, bias):
    b, _, n1, wid = z.shape
    half = gi.shape[1]
    jc = DFT_J * HY_CH
    tok = pl.BlockSpec((1, half, jc), lambda i, j: (i, 0, j))
    return pl.pallas_call(
        _dft_ainv_kernel,
        grid=(b, DFT_N2 // DFT_J),
        in_specs=[pl.BlockSpec((1, 2, n1, jc), lambda i, j: (i, 0, 0, j)),
                  pl.BlockSpec((DFT_J, half, 2 * n1), lambda i, j: (j, 0, 0)),
                  tok, tok, pl.BlockSpec((1, HY_CH), lambda i, j: (0, 0))],
        out_specs=tok,
        out_shape=jax.ShapeDtypeStruct((b, half, wid), BF16),
        compiler_params=_cparams(("parallel", "parallel")),
        name="dft_ainv",
    )(z, gi, xg, zin, bias.reshape(1, HY_CH))


def _ctx_conv_kernel(z_ref, xg_ref, f_ref, mf_ref, mi_ref, sc_ref, bias_ref, o_ref):
    nf = mf_ref.shape[0] // 2
    mf = mf_ref[...]
    xf = jnp.dot(mf, f_ref[0].astype(BF16), preferred_element_type=F32)
    xb = jnp.dot(mf, f_ref[1].astype(BF16), preferred_element_type=F32)
    sc = sc_ref[0]
    hr = (xf[:nf] + xb[:nf]) * sc
    hi = (xf[nf:] - xb[nf:]) * sc
    z = z_ref[0].astype(F32)
    x = jnp.dot(mf, z.astype(BF16), preferred_element_type=F32)
    xr, xi = x[:nf], x[nf:]
    y = jnp.concatenate([xr * hr - xi * hi, xr * hi + xi * hr], axis=0).astype(BF16)
    gated = xg_ref[0].astype(F32) * (jnp.dot(mi_ref[...], y, preferred_element_type=F32) + bias_ref[...] * z)
    o_ref[0] = gated.astype(o_ref.dtype)


def ctx_conv(z, xg, f, order, scale, bias):
    b, length, c = z.shape
    n = 2 * length
    kk = jnp.arange(n, dtype=jnp.int32)[:, None]
    tn = jnp.arange(length, dtype=jnp.int32)[None, :]
    th = (2.0 * math.pi / n) * ((kk * tn) % n).astype(F32)
    mf = jnp.concatenate([jnp.cos(th), -jnp.sin(th)], axis=0).astype(BF16)
    mi = jnp.concatenate([jnp.cos(th.T), -jnp.sin(th.T)], axis=1).astype(BF16)
    tok = pl.BlockSpec((1, length, c), lambda i: (i, 0, 0))
    return pl.pallas_call(
        _ctx_conv_kernel,
        grid=(b,),
        in_specs=[tok, tok, pl.BlockSpec((2, length, c), lambda i: (order, 0, 0)),
                  pl.BlockSpec((2 * n, length), lambda i: (0, 0)), pl.BlockSpec((length, 2 * n), lambda i: (0, 0)),
                  pl.BlockSpec((1, 1, c), lambda i: (order, 0, 0)), pl.BlockSpec((1, c), lambda i: (0, 0))],
        out_specs=tok,
        out_shape=jax.ShapeDtypeStruct((b, length, c), BF16),
        compiler_params=_cparams(("parallel",)),
        name="ctx_conv",
    )(z, xg, f, mf, mi, scale.reshape(-1, 1, c), bias.reshape(1, c))


def hyena(p, conv_w, conv_b, w1, b1, w2, b2, w3, freq, bias, tables):
    b, length, _ = p.shape
    parts = hy_prep(p, conv_w, conv_b)
    f, rnorm = hy_filters(length, w1, b1, w2, b2, w3, freq, wide=tables is not None)
    scale = rnorm / (2 * length)
    z = parts[0]
    if tables is None:
        for o in range(HY_ORDER):
            z = ctx_conv(z, parts[o + 1], f, o, scale, bias[o])
        return z
    ga, gi, mf, mi = tables
    half = ga.shape[2]
    wid = DFT_N2 * HY_CH
    fa = dft_a(f.reshape(2 * HY_ORDER, half, wid), ga)
    n1 = fa.shape[2]
    h = spec_filter(fa.reshape(2 * HY_ORDER, 2, n1, DFT_N2, HY_CH), mf, scale)
    for o in range(HY_ORDER):
        a = dft_a(z.reshape(b, half, wid), ga).reshape(b, 2, n1, DFT_N2, HY_CH)
        zc = spec_conv(a, h, o, mf, mi).reshape(b, 2, n1, wid)
        z = dft_ainv(zc, gi, parts[o + 1].reshape(b, half, wid), z.reshape(b, half, wid), bias[o])
        z = z.reshape(b, length, HY_CH)
    return z


ROUTE_ROWS = 8


def _router_kernel(x_ref, g_ref, sh_ref, sc_ref, w_ref, b_ref, h_o, id_o, gate_o):
    x = x_ref[0]
    y = x * lax.rsqrt(jnp.mean(x * x, axis=-1, keepdims=True) + EPS)
    y = ((y * g_ref[...]) * (1.0 + sc_ref[0]) + sh_ref[0]).astype(BF16)
    h_o[0] = y
    logits = jnp.dot(y, w_ref[...], preferred_element_type=F32) + b_ref[...]
    lane = lax.broadcasted_iota(jnp.int32, logits.shape, 1)
    lane_f = lane.astype(F32)
    none = float(LANES)

    def top(vals):
        v = jnp.max(vals, axis=1, keepdims=True)
        return v, jnp.min(jnp.where(vals == v, lane_f, none), axis=1, keepdims=True)

    is_grp = lane < MOE_GROUPS
    mg, grp = top(jnp.where(is_grp, logits, -jnp.inf))
    p_grp = 1.0 / jnp.sum(jnp.where(is_grp, jnp.exp(logits - mg), 0.0), axis=1, keepdims=True)
    lo = MOE_GROUPS + MOE_EPG * grp
    el = jnp.where((lane_f >= lo) & (lane_f < lo + MOE_EPG), logits, -jnp.inf)
    v1, i1 = top(el)
    v2, i2 = top(jnp.where(lane_f == i1, -jnp.inf, el))
    t = jnp.exp(v2 - v1)
    g1 = p_grp / (1.0 + t)
    ids = jnp.where(lane == 0, i1 - MOE_GROUPS, jnp.where(lane == 1, i2 - MOE_GROUPS, 0.0))
    id_o[0, 0] = ids.T[:ROUTE_ROWS].astype(jnp.int32)
    gate_o[0, 0] = jnp.where(lane == 0, g1, jnp.where(lane == 1, g1 * t, 0.0)).T[:ROUTE_ROWS]


def moe_router(x, g, shift, scale, w_group, b_group, w_router, b_router):
    b, t, d = x.shape
    tt = _tile(t, ROW_TILE)
    rows = pl.BlockSpec((1, 1, ROUTE_ROWS, tt), lambda i, j: (i, j, 0, 0))
    npad = LANES - MOE_GROUPS - MOE_EXPERTS
    w = jnp.concatenate([w_group, w_router, jnp.zeros((d, npad), F32)], axis=1).astype(BF16)
    bias = jnp.concatenate([b_group, b_router, jnp.zeros((npad,), F32)]).reshape(1, LANES)
    tok = lambda width: pl.BlockSpec((1, tt, width), lambda i, j: (i, j, 0))
    mod = pl.BlockSpec((1, 1, d), lambda i, j: (i, 0, 0))
    return pl.pallas_call(
        _router_kernel,
        grid=(b, t // tt),
        in_specs=[tok(d), pl.BlockSpec((1, d), lambda i, j: (0, 0)), mod, mod,
                  pl.BlockSpec((d, LANES), lambda i, j: (0, 0)), pl.BlockSpec((1, LANES), lambda i, j: (0, 0))],
        out_specs=[tok(d), rows, rows],
        out_shape=[jax.ShapeDtypeStruct((b, t, d), BF16),
                   jax.ShapeDtypeStruct((b, t // tt, ROUTE_ROWS, tt), jnp.int32),
                   jax.ShapeDtypeStruct((b, t // tt, ROUTE_ROWS, tt), F32)],
        compiler_params=_cparams(("parallel", "parallel")),
        name="moe_router",
    )(x, g.reshape(1, d), shift.reshape(b, 1, d), scale.reshape(b, 1, d), w, bias)


def _blocked_cumsum(onehot, blk=256):
    m, e = onehot.shape
    if m % blk:
        return jnp.cumsum(onehot, axis=0)
    oh = onehot.astype(BF16).reshape(m // blk, blk, e)
    tril = jnp.tril(jnp.ones((blk, blk), BF16))
    within = jnp.einsum('ts,bse->bte', tril, oh, preferred_element_type=F32)
    tot = within[:, -1, :]
    off = jnp.cumsum(tot, axis=0) - tot
    return (within + off[:, None, :]).reshape(m, e).astype(jnp.int32)


def _hier_moe(h, ids, layer, w_gate, w_up, w_down):
    n_tok, d = h.shape
    e_flat = ids.reshape(-1)
    m_slots = n_tok * MOE_TOP_K
    onehot = (e_flat[:, None] == jnp.arange(MOE_EXPERTS, dtype=jnp.int32)[None, :]).astype(jnp.int32)
    csum = _blocked_cumsum(onehot)
    rank = jnp.sum(onehot * csum, axis=1) - 1
    counts = csum[-1]
    padded = (counts + MOE_BLOCK - 1) // MOE_BLOCK * MOE_BLOCK
    p_end = jnp.cumsum(padded)
    dest = (p_end - padded)[e_flat] + rank
    n_blocks = -(-(m_slots + MOE_EXPERTS * (MOE_BLOCK - 1)) // MOE_BLOCK)
    slot_tok = jnp.arange(m_slots, dtype=jnp.int32) // MOE_TOP_K
    buf_tok = jnp.zeros((n_blocks * MOE_BLOCK,), jnp.int32).at[dest].set(slot_tok)
    first_row = jnp.arange(n_blocks, dtype=jnp.int32)[:, None] * MOE_BLOCK
    blk_exp = jnp.minimum(jnp.sum((p_end[None, :] <= first_row).astype(jnp.int32), axis=1), MOE_EXPERTS - 1)
    yb = expert_blocks(h[buf_tok], blk_exp, layer, w_gate, w_up, w_down)
    return yb, dest.reshape(n_tok, MOE_TOP_K)


def _moe_combine_kernel(x_ref, y0_ref, y1_ref, g_ref, m_ref, o_ref):
    g = g_ref[0, 0].T
    f = g[:, 0:1] * y0_ref[...].astype(F32) + g[:, 1:2] * y1_ref[...].astype(F32)
    o_ref[0] = x_ref[0] + m_ref[0] * f


def moe_combine(x, yb, dest, gates, mod):
    b, t, d = x.shape
    y0 = yb[dest[:, 0]]
    y1 = yb[dest[:, 1]]
    nt, tt = gates.shape[1], gates.shape[3]
    row = pl.BlockSpec((tt, d), lambda i, j: (i * nt + j, 0))
    return pl.pallas_call(
        _moe_combine_kernel,
        grid=(b, nt),
        in_specs=[pl.BlockSpec((1, tt, d), lambda i, j: (i, j, 0)), row, row,
                  pl.BlockSpec((1, 1, ROUTE_ROWS, tt), lambda i, j: (i, j, 0, 0)),
                  pl.BlockSpec((1, 1, d), lambda i, j: (i, 0, 0))],
        out_specs=pl.BlockSpec((1, tt, d), lambda i, j: (i, j, 0)),
        out_shape=jax.ShapeDtypeStruct((b, t, d), F32),
        compiler_params=_cparams(("parallel", "parallel")),
        name="moe_combine",
    )(x, y0, y1, gates, mod.reshape(b, 1, d))


def _permute_w_in(w):
    d = w.shape[0]
    sizes = (DIFF_HEADS * 2 * DIFF_DIM, DIFF_HEADS * 2 * DIFF_DIM, DIFF_HEADS * DIFF_VDIM, 3 * ML_HEADS * ML_DIM,
             ML_HEADS * ML_DIM, 4 * ML_HEADS, GQA_HEADS * GQA_DIM, 2 * GQA_KV * GQA_DIM, (HY_ORDER + 1) * HY_CH,
             N_BRANCH * d)
    offs = np.cumsum((0,) + sizes)
    dq, dk, dv, mlqkv, mlo, mlg, gq, gkv, hy, gate = [w[:, offs[i]:offs[i + 1]] for i in range(10)]
    pad = jnp.zeros((d, N_P - C_MLG - 4 * ML_HEADS), w.dtype)
    return jnp.concatenate([gate, dq, dk, dv, mlqkv, mlo, gq, hy, gkv, mlg, pad], axis=1)


def kernel(x, c, ctx, c_ctx, w_ada, b_ada, norm1_g, norm2_g, w_in, diff_lam, diff_norm_g, ml_conv_w, ml_conv_b, ml_gate_b, ml_norm_g, gqa_qnorm_g, gqa_knorm_g, hy_conv_w, hy_conv_b, hy_f_w1, hy_f_b1, hy_f_w2, hy_f_b2, hy_f_w3, hy_f_freq, hy_bias, w_branch, w_out, moe_w_group, moe_b_group, moe_w_router, moe_b_router, moe_w_gate, moe_w_up, moe_w_down, final_norm_g):
    b, n, d = x.shape
    n_ctx = ctx.shape[1]
    depth = w_in.shape[0]
    tk = _tile(n, 2048)
    tables = _rope_tables(n)
    dft_tables = _dft_tables(n)
    sc = jax.nn.silu(c)
    scx = jax.nn.silu(c_ctx)
    xs, cs = x, ctx
    for l in range(depth):
        need_ctx = l < depth - 1
        mod_l = jnp.split(sc @ w_ada[l] + b_ada[l], 6, axis=-1)
        mod_c = [jnp.broadcast_to(m, (b, d)) for m in jnp.split(scx @ w_ada[l] + b_ada[l], 6, axis=-1)]
        w_p = _permute_w_in(w_in[l]).astype(BF16)
        hl = norm_mod(xs, norm1_g[l], mod_l[0], mod_l[1], BF16)
        hc = norm_mod(cs, norm1_g[l], mod_c[0], mod_c[1], BF16)
        p_l = matmul(hl.reshape(b * n, d), w_p, BF16, tm=1024).reshape(b, n, N_P)
        p_c = matmul(hc.reshape(b * n_ctx, d), w_p, BF16, tm=1024).reshape(b, n_ctx, N_P)

        dq_l, dkt_l, dv_l, gq_l, gkt_l, gv_l = attn_prep(p_l, tables, gqa_qnorm_g[l], gqa_knorm_g[l], tk)
        dq_c, dkt_c, dv_c, gq_c, gkt_c, gv_c = attn_prep(p_c, None, gqa_qnorm_g[l], gqa_knorm_g[l], n_ctx)
        lam_init = 0.8 - 0.6 * math.exp(-0.3 * l)
        lp = diff_lam[l].astype(F32)
        lam = jnp.exp(jnp.sum(lp[0] * lp[1])) - jnp.exp(jnp.sum(lp[2] * lp[3])) + lam_init
        yl_a = diff_attention(dq_l, dkt_c, dv_c, dkt_l, dv_l, lam, diff_norm_g[l], 1.0 - lam_init, 1024)
        yl_c = gqa_attention(gq_l, gkt_c, gv_c, gkt_l, gv_l, 1024)
        if need_ctx:
            yc_a = diff_attention(dq_c, dkt_c, dv_c, None, None, lam, diff_norm_g[l], 1.0 - lam_init, 256)
            yc_c = gqa_attention(gq_c, gkt_c, gv_c, None, None, 256)

        hs_c, hs_l = mlstm_branch(p_c, p_l, ml_conv_w[l], ml_conv_b[l], ml_gate_b[l], need_ctx)
        hy_args = (hy_conv_w[l], hy_conv_b[l], hy_f_w1[l], hy_f_b1[l], hy_f_w2[l], hy_f_b2[l],
                   hy_f_w3[l], hy_f_freq[l], hy_bias[l])
        yl_d = hyena(p_l, *hy_args, dft_tables)

        wb = w_branch[l].astype(BF16)
        wo = w_out[l].astype(BF16)
        xs = merge(yl_a, hs_l, yl_c, yl_d, p_l, ml_norm_g[l], wb, wo, xs, mod_l[2])
        route = (moe_w_group[l], moe_b_group[l], moe_w_router[l], moe_b_router[l])
        experts = (l, moe_w_gate, moe_w_up, moe_w_down)
        routed = [moe_router(xs, norm2_g[l], mod_l[3], mod_l[4], *route)]
        if need_ctx:
            yc_d = hyena(p_c, *hy_args, None)
            cs = merge(yc_a, hs_c, yc_c, yc_d, p_c, ml_norm_g[l], wb, wo, cs, mod_c[2])
            routed.insert(0, moe_router(cs, norm2_g[l], mod_c[3], mod_c[4], *route))
        h2 = jnp.concatenate([r[0].reshape(-1, d) for r in routed], axis=0)
        ids = jnp.concatenate([jnp.stack([r[1][:, :, k, :].reshape(-1) for k in range(MOE_TOP_K)], axis=1)
                               for r in routed], axis=0)
        yb, dest = _hier_moe(h2, ids, *experts)
        if need_ctx:
            nc = b * n_ctx
            cs = moe_combine(cs, yb, dest[:nc], routed[0][2], mod_c[5])
            dest = dest[nc:]
        xs = moe_combine(xs, yb, dest, routed[-1][2], mod_l[5])
    zero = jnp.zeros((b, d), F32)
    return norm_mod(xs, final_norm_g, zero, zero, F32)
```

```python
import functools
import math

import jax
import jax.numpy as jnp
import numpy as np
from jax import lax
from jax.experimental import pallas as pl
from jax.experimental.pallas import tpu as pltpu

F32 = jnp.float32
BF16 = jnp.bfloat16

EPS = 1e-6
ROPE_BASE = 10000.0
GRID_W = 64

DIFF_HEADS = 4
DIFF_DIM = 64
DIFF_VDIM = 128
ML_HEADS = 4
ML_DIM = 128
ML_CHUNK = 256
GQA_HEADS = 8
GQA_KV = 2
GQA_DIM = 64
HY_CH = 512
HY_ORDER = 2
FILTER_BANDS = 16
FILTER_SHIFT = 0.05
DECAY_TARGET = 1e-2
FAST_DECAY_PCT = 0.3
SLOW_DECAY_PCT = 1.5
N_BRANCH = 4
MOE_GROUPS = 4
MOE_EPG = 8
MOE_EXPERTS = MOE_GROUPS * MOE_EPG
MOE_TOP_K = 2
MOE_BLOCK = 256

LANES = 128
VMEM_LIMIT = 48 * 1024 * 1024
ROW_TILE = 512
GROUP_W = 512

C_GATE = 0
C_DQ = 4096
C_DK = 4608
C_DV = 5120
C_MLQ = 5632
C_MLK = 6144
C_MLV = 6656
C_MLO = 7168
C_GQ = 7680
C_HY = 8192
C_GK = 9728
C_GV = 9856
C_MLG = 9984
N_P = 10240

QSCALE = (DIFF_DIM ** -0.5) * math.log2(math.e)


def _cparams(sem):
    return pltpu.CompilerParams(dimension_semantics=sem, vmem_limit_bytes=VMEM_LIMIT)


def _tile(n, target):
    if n <= target:
        return n
    for t in range(target, 7, -1):
        if n % t == 0 and t % 8 == 0:
            return t
    return n


def _norm_mod_kernel(x_ref, g_ref, sh_ref, sc_ref, o_ref):
    x = x_ref[0]
    y = x * lax.rsqrt(jnp.mean(x * x, axis=-1, keepdims=True) + EPS)
    y = y * g_ref[...]
    o_ref[0] = (y * (1.0 + sc_ref[0]) + sh_ref[0]).astype(o_ref.dtype)


def norm_mod(x, g, shift, scale, out_dtype):
    b, t, d = x.shape
    tt = _tile(t, ROW_TILE)
    return pl.pallas_call(
        _norm_mod_kernel,
        grid=(b, t // tt),
        in_specs=[pl.BlockSpec((1, tt, d), lambda i, j: (i, j, 0)),
                  pl.BlockSpec((1, d), lambda i, j: (0, 0)),
                  pl.BlockSpec((1, 1, d), lambda i, j: (i, 0, 0)),
                  pl.BlockSpec((1, 1, d), lambda i, j: (i, 0, 0))],
        out_specs=pl.BlockSpec((1, tt, d), lambda i, j: (i, j, 0)),
        out_shape=jax.ShapeDtypeStruct((b, t, d), out_dtype),
        compiler_params=_cparams(("parallel", "parallel")),
        name="norm_mod",
    )(x, g.reshape(1, d), shift.reshape(b, 1, d), scale.reshape(b, 1, d))


def _norm_mm_kernel(x_ref, g_ref, sh_ref, sc_ref, w_ref, o_ref, h_sc):
    @pl.when(pl.program_id(2) == 0)
    def _():
        x = x_ref[0]
        y = x * lax.rsqrt(jnp.mean(x * x, axis=-1, keepdims=True) + EPS) * g_ref[...]
        h_sc[...] = (y * (1.0 + sc_ref[0]) + sh_ref[0]).astype(BF16)

    o_ref[0] = jnp.dot(h_sc[...], w_ref[...], preferred_element_type=F32).astype(o_ref.dtype)


def norm_matmul(x, g, shift, scale, w, out_dtype, tm=1024, tn=1024):
    b, t, d = x.shape
    n = w.shape[1]
    tm = _tile(t, tm)
    tn = _tile(n, tn)
    mod = pl.BlockSpec((1, 1, d), lambda i, r, j: (i, 0, 0))
    return pl.pallas_call(
        _norm_mm_kernel,
        grid=(b, t // tm, n // tn),
        in_specs=[pl.BlockSpec((1, tm, d), lambda i, r, j: (i, r, 0)),
                  pl.BlockSpec((1, d), lambda i, r, j: (0, 0)), mod, mod,
                  pl.BlockSpec((d, tn), lambda i, r, j: (0, j))],
        out_specs=pl.BlockSpec((1, tm, tn), lambda i, r, j: (i, r, j)),
        out_shape=jax.ShapeDtypeStruct((b, t, n), out_dtype),
        scratch_shapes=[pltpu.VMEM((tm, d), BF16)],
        compiler_params=_cparams(("parallel", "parallel", "arbitrary")),
        name="norm_matmul",
    )(x, g.reshape(1, d), shift.reshape(b, 1, d), scale.reshape(b, 1, d), w)


def _rope(x, cos, sa, sb):
    xa = pltpu.roll(x, LANES - 16, axis=1)
    xb = pltpu.roll(x, 16, axis=1)
    return x * cos + xa * sa + xb * sb


def _seg_rmsnorm(x, g):
    lane = lax.broadcasted_iota(jnp.int32, x.shape, 1)
    lo = lane < GQA_DIM
    ss = x * x
    s_lo = jnp.sum(jnp.where(lo, ss, 0.0), axis=-1, keepdims=True)
    s_hi = jnp.sum(jnp.where(lo, 0.0, ss), axis=-1, keepdims=True)
    r = jnp.where(lo, lax.rsqrt(s_lo * (1.0 / GQA_DIM) + EPS), lax.rsqrt(s_hi * (1.0 / GQA_DIM) + EPS))
    return x * r * g


def _prep_kernel(dq_ref, dk_ref, dv_ref, gq_ref, gkv_ref, cos_ref, sa_ref, sb_ref, qg_ref, kg_ref,
                 dq_o, dkt_o, dv_o, gq_o, gkt_o, gv_o, *, rope):
    if rope:
        cos, sa, sb = cos_ref[...], sa_ref[...], sb_ref[...]
        rot = lambda x: _rope(x, cos, sa, sb)
    else:
        rot = lambda x: x
    qg = qg_ref[...]
    kg = kg_ref[...]
    for j in range(DIFF_HEADS):
        sl = slice(j * LANES, (j + 1) * LANES)
        dq_o[0, :, sl] = (rot(dq_ref[0, :, sl].astype(F32)) * QSCALE).astype(BF16)
        kt = rot(dk_ref[0, :, sl].astype(F32)).T
        dkt_o[0, j, 0, 0] = kt[:DIFF_DIM].astype(BF16)
        dkt_o[0, j, 1, 0] = kt[DIFF_DIM:].astype(BF16)
        gq_o[0, :, sl] = (rot(_seg_rmsnorm(gq_ref[0, :, sl].astype(F32), qg)) * QSCALE).astype(BF16)
    dv_o[0] = dv_ref[0].astype(BF16)
    kt = rot(_seg_rmsnorm(gkv_ref[0, :, :LANES].astype(F32), kg)).T
    gkt_o[0, 0, 0] = kt[:GQA_DIM].astype(BF16)
    gkt_o[0, 1, 0] = kt[GQA_DIM:].astype(BF16)
    v = gkv_ref[0, :, LANES:].astype(BF16)
    gv_o[0, 0] = v[:, :GQA_DIM]
    gv_o[0, 1] = v[:, GQA_DIM:]


def attn_prep(p, tables, q_g, k_g, tk):
    b, t, _ = p.shape
    rope = tables is not None
    if rope:
        cos, sa, sb = tables
    else:
        cos = sa = sb = jnp.zeros((t, LANES), F32)
    nck = t // tk
    tp = _tile(tk, ROW_TILE)
    sub = tk // tp
    gw = GROUP_W
    kvw = 2 * GQA_KV * GQA_DIM
    grp = lambda c: pl.BlockSpec((1, tp, gw), lambda i, j, c=c: (i, j, c // gw))
    tab = pl.BlockSpec((tp, LANES), lambda i, j: (j, 0))
    vec = pl.BlockSpec((1, LANES), lambda i, j: (0, 0))
    outs = pl.pallas_call(
        functools.partial(_prep_kernel, rope=rope),
        grid=(b, t // tp),
        in_specs=[grp(C_DQ), grp(C_DK), grp(C_DV), grp(C_GQ),
                  pl.BlockSpec((1, tp, kvw), lambda i, j: (i, j, C_GK // kvw)),
                  tab, tab, tab, vec, vec],
        out_specs=[pl.BlockSpec((1, tp, gw), lambda i, j: (i, j, 0)),
                   pl.BlockSpec((1, DIFF_HEADS, 2, 1, DIFF_DIM, tp), lambda i, j: (i, 0, 0, j // sub, 0, j % sub)),
                   pl.BlockSpec((1, tp, gw), lambda i, j: (i, j, 0)),
                   pl.BlockSpec((1, tp, gw), lambda i, j: (i, j, 0)),
                   pl.BlockSpec((1, GQA_KV, 1, GQA_DIM, tp), lambda i, j: (i, 0, j // sub, 0, j % sub)),
                   pl.BlockSpec((1, GQA_KV, tp, GQA_DIM), lambda i, j: (i, 0, j, 0))],
        out_shape=[jax.ShapeDtypeStruct((b, t, gw), BF16),
                   jax.ShapeDtypeStruct((b, DIFF_HEADS, 2, nck, DIFF_DIM, tk), BF16),
                   jax.ShapeDtypeStruct((b, t, gw), BF16),
                   jax.ShapeDtypeStruct((b, t, gw), BF16),
                   jax.ShapeDtypeStruct((b, GQA_KV, nck, GQA_DIM, tk), BF16),
                   jax.ShapeDtypeStruct((b, GQA_KV, t, GQA_DIM), BF16)],
        compiler_params=_cparams(("parallel", "parallel")),
        name="attn_prep",
    )(p, p, p, p, p, cos, sa, sb,
      jnp.tile(q_g, 2).reshape(1, LANES), jnp.tile(k_g, 2).reshape(1, LANES))
    return outs


def _flash_step(q, kt, v, m_ref, l_ref, acc_ref):
    s = jnp.dot(q, kt, preferred_element_type=F32)
    tk = s.shape[1]
    m_prev = m_ref[...]
    m_next = jnp.maximum(m_prev, jnp.max(s, axis=1, keepdims=True))
    alpha = jnp.exp2(m_prev - m_next)
    p = jnp.exp2(s - jnp.concatenate([m_next] * (tk // LANES), axis=1))
    l_ref[...] = alpha * l_ref[...] + jnp.sum(p, axis=1, keepdims=True)
    dv = acc_ref.shape[-1]
    acc_ref[...] = acc_ref[...] * alpha[:, :dv] + jnp.dot(p.astype(BF16), v, preferred_element_type=F32)
    m_ref[...] = m_next


def _attn_body(q_ref, ktc, vc, ktl, vl, m_sc, l_sc, acc_sc, *, n_lat, tk):
    m_sc[...] = jnp.full(m_sc.shape, -jnp.inf, F32)
    l_sc[...] = jnp.zeros(l_sc.shape, F32)
    acc_sc[...] = jnp.zeros(acc_sc.shape, F32)
    q = q_ref[0]
    qs = (q[:, :DIFF_DIM], q[:, DIFF_DIM:])
    for c in range(2):
        _flash_step(qs[c], ktc(c), vc(c), m_sc.at[c], l_sc.at[c], acc_sc.at[c])
    if n_lat:
        def body(i, carry):
            for c in range(2):
                _flash_step(qs[c], ktl(c, i), vl(c, i), m_sc.at[c], l_sc.at[c], acc_sc.at[c])
            return carry
        lax.fori_loop(0, n_lat, body, 0)


def _diff_attn_kernel(*refs, n_lat, tk, out_scale):
    if n_lat:
        q_ref, ktc_ref, vc_ref, ktl_ref, vl_ref, lam_ref, g_ref, o_ref, m_sc, l_sc, acc_sc = refs
        ktl = lambda c, i: ktl_ref[0, 0, c, i]
        vl = lambda c, i: vl_ref[0, pl.ds(pl.multiple_of(i * tk, tk), tk), :]
    else:
        q_ref, ktc_ref, vc_ref, lam_ref, g_ref, o_ref, m_sc, l_sc, acc_sc = refs
        ktl = vl = None
    _attn_body(q_ref, lambda c: ktc_ref[0, 0, c, 0], lambda c: vc_ref[0], ktl, vl,
               m_sc, l_sc, acc_sc, n_lat=n_lat, tk=tk)
    o0 = acc_sc[0] / l_sc[0]
    o1 = acc_sc[1] / l_sc[1]
    o = o0 - lam_ref[...] * o1
    o = o * lax.rsqrt(jnp.mean(o * o, axis=-1, keepdims=True) + EPS)
    o_ref[0] = (o * g_ref[...] * out_scale).astype(o_ref.dtype)


def diff_attention(q, ktc, vc, ktl, vl, lam, norm_g, out_scale, tq):
    b, t, _ = q.shape
    sc = vc.shape[1]
    tq = _tile(t, tq)
    n_lat, tk = (ktl.shape[3], ktl.shape[5]) if ktl is not None else (0, 0)
    in_specs = [pl.BlockSpec((1, tq, LANES), lambda i, h, j: (i, j, h)),
                pl.BlockSpec((1, 1, 2, 1, DIFF_DIM, sc), lambda i, h, j: (i, h, 0, 0, 0, 0)),
                pl.BlockSpec((1, sc, LANES), lambda i, h, j: (i, 0, h))]
    args = [q, ktc, vc]
    if n_lat:
        in_specs += [pl.BlockSpec((1, 1, 2, n_lat, DIFF_DIM, tk), lambda i, h, j: (i, h, 0, 0, 0, 0)),
                     pl.BlockSpec((1, n_lat * tk, LANES), lambda i, h, j: (i, 0, h))]
        args += [ktl, vl]
    vec = pl.BlockSpec((1, LANES), lambda i, h, j: (0, 0))
    in_specs += [vec, vec]
    args += [jnp.full((1, LANES), lam, F32), norm_g.reshape(1, LANES)]
    return pl.pallas_call(
        functools.partial(_diff_attn_kernel, n_lat=n_lat, tk=tk, out_scale=out_scale),
        grid=(b, DIFF_HEADS, t // tq),
        in_specs=in_specs,
        out_specs=pl.BlockSpec((1, tq, LANES), lambda i, h, j: (i, j, h)),
        out_shape=jax.ShapeDtypeStruct((b, t, DIFF_HEADS * DIFF_VDIM), BF16),
        scratch_shapes=[pltpu.VMEM((2, tq, LANES), F32), pltpu.VMEM((2, tq, LANES), F32),
                        pltpu.VMEM((2, tq, DIFF_VDIM), F32)],
        compiler_params=_cparams(("parallel", "parallel", "parallel")),
        name="diff_attn",
    )(*args)


def _gqa_attn_kernel(*refs, n_lat, tk):
    if n_lat:
        q_ref, ktc_ref, vc_ref, ktl_ref, vl_ref, o_ref, m_sc, l_sc, acc_sc = refs
        ktl = lambda c, i: ktl_ref[0, 0, i]
        vl = lambda c, i: vl_ref[0, 0, pl.ds(pl.multiple_of(i * tk, tk), tk), :]
    else:
        q_ref, ktc_ref, vc_ref, o_ref, m_sc, l_sc, acc_sc = refs
        ktl = vl = None
    _attn_body(q_ref, lambda c: ktc_ref[0, 0, 0], lambda c: vc_ref[0, 0], ktl, vl,
               m_sc, l_sc, acc_sc, n_lat=n_lat, tk=tk)
    o0 = acc_sc[0] / l_sc[0][:, :GQA_DIM]
    o1 = acc_sc[1] / l_sc[1][:, :GQA_DIM]
    o_ref[0] = jnp.concatenate([o0, o1], axis=-1).astype(o_ref.dtype)


def gqa_attention(q, ktc, vc, ktl, vl, tq):
    b, t, _ = q.shape
    sc = vc.shape[2]
    tq = _tile(t, tq)
    n_lat, tk = (ktl.shape[2], ktl.shape[4]) if ktl is not None else (0, 0)
    pairs = GQA_HEADS // 2
    grp = lambda h: h // (pairs // GQA_KV)
    in_specs = [pl.BlockSpec((1, tq, LANES), lambda i, h, j: (i, j, h)),
                pl.BlockSpec((1, 1, 1, GQA_DIM, sc), lambda i, h, j: (i, grp(h), 0, 0, 0)),
                pl.BlockSpec((1, 1, sc, GQA_DIM), lambda i, h, j: (i, grp(h), 0, 0))]
    args = [q, ktc, vc]
    if n_lat:
        in_specs += [pl.BlockSpec((1, 1, n_lat, GQA_DIM, tk), lambda i, h, j: (i, grp(h), 0, 0, 0)),
                     pl.BlockSpec((1, 1, n_lat * tk, GQA_DIM), lambda i, h, j: (i, grp(h), 0, 0))]
        args += [ktl, vl]
    return pl.pallas_call(
        functools.partial(_gqa_attn_kernel, n_lat=n_lat, tk=tk),
        grid=(b, pairs, t // tq),
        in_specs=in_specs,
        out_specs=pl.BlockSpec((1, tq, LANES), lambda i, h, j: (i, j, h)),
        out_shape=jax.ShapeDtypeStruct((b, t, GQA_HEADS * GQA_DIM), BF16),
        scratch_shapes=[pltpu.VMEM((2, tq, LANES), F32), pltpu.VMEM((2, tq, LANES), F32),
                        pltpu.VMEM((2, tq, GQA_DIM), F32)],
        compiler_params=_cparams(("parallel", "parallel", "parallel")),
        name="gqa_attn",
    )(*args)


ML_STEP = 256
HALO = 16


def _halo_rows(prev_ref, next_ref, j, nblk):
    prev = jnp.where(j > 0, prev_ref[0, HALO - 1:HALO, :].astype(F32), 0.0)
    nxt = jnp.where(j < nblk - 1, next_ref[0, 0:1, :].astype(F32), 0.0)
    return prev, nxt


def _conv3(x, prev, nxt, w, bias):
    tt = x.shape[0]
    row = lax.broadcasted_iota(jnp.int32, x.shape, 0)
    xm = jnp.where(row == 0, prev, pltpu.roll(x, 1, axis=0))
    xp = jnp.where(row == tt - 1, nxt, pltpu.roll(x, tt - 1, axis=0))
    return xm * w[0:1] + x * w[1:2] + xp * w[2:3] + bias


def _halo_specs(tt, width, col_block, t):
    nb = t // HALO
    prev = pl.BlockSpec((1, HALO, width), lambda i, j: (i, jnp.maximum(j * (tt // HALO) - 1, 0), col_block))
    nxt = pl.BlockSpec((1, HALO, width), lambda i, j: (i, jnp.minimum((j + 1) * (tt // HALO), nb - 1), col_block))
    return prev, nxt


def _ml_prep_kernel(q_ref, qp_ref, qn_ref, k_ref, kp_ref, kn_ref, g_ref, w_ref, b_ref, gb_ref, q_o, k_o, g_o):
    j = pl.program_id(1)
    nblk = pl.num_programs(1)
    w = ML_HEADS * ML_DIM
    qp, qn = _halo_rows(qp_ref, qn_ref, j, nblk)
    kp, kn = _halo_rows(kp_ref, kn_ref, j, nblk)
    q = _conv3(q_ref[0].astype(F32), qp, qn, w_ref[:, :w], b_ref[:, :w])
    k = _conv3(k_ref[0].astype(F32), kp, kn, w_ref[:, w:], b_ref[:, w:])
    q_o[0] = q * jax.nn.sigmoid(q)
    k_o[0] = (k * jax.nn.sigmoid(k)) * ML_DIM ** -0.5
    x = g_ref[0].astype(F32) + gb_ref[...]
    lane = lax.broadcasted_iota(jnp.int32, x.shape, 1)
    log_sig = jnp.minimum(x, 0.0) - jnp.log(1.0 + jnp.exp(-jnp.abs(x)))
    g_o[0] = jnp.where((lane % 8) >= ML_HEADS, log_sig, x)


def ml_prep(p, conv_w, conv_b, gate_b):
    b, t, _ = p.shape
    tt = _tile(t, ROW_TILE)
    w = ML_HEADS * ML_DIM
    blk = lambda c: pl.BlockSpec((1, tt, w), lambda i, j, c=c: (i, j, c // w))
    qp, qn = _halo_specs(tt, w, C_MLQ // w, t)
    kp, kn = _halo_specs(tt, w, C_MLK // w, t)
    gb = jnp.zeros((1, LANES), F32).at[0, :4 * ML_HEADS].set(gate_b.reshape(-1))
    return pl.pallas_call(
        _ml_prep_kernel,
        grid=(b, t // tt),
        in_specs=[blk(C_MLQ), qp, qn, blk(C_MLK), kp, kn,
                  pl.BlockSpec((1, tt, LANES), lambda i, j: (i, j, C_MLG // LANES)),
                  pl.BlockSpec((3, 2 * w), lambda i, j: (0, 0)),
                  pl.BlockSpec((1, 2 * w), lambda i, j: (0, 0)),
                  pl.BlockSpec((1, LANES), lambda i, j: (0, 0))],
        out_specs=[pl.BlockSpec((1, tt, w), lambda i, j: (i, j, 0)),
                   pl.BlockSpec((1, tt, w), lambda i, j: (i, j, 0)),
                   pl.BlockSpec((1, tt, LANES), lambda i, j: (i, j, 0))],
        out_shape=[jax.ShapeDtypeStruct((b, t, w), F32), jax.ShapeDtypeStruct((b, t, w), F32),
                   jax.ShapeDtypeStruct((b, t, LANES), F32)],
        compiler_params=_cparams(("parallel", "parallel")),
        name="ml_prep",
    )(p, p, p, p, p, p, p, conv_w, conv_b.reshape(1, 2 * w), gb)


_NT = (((1,), (1,)), ((), ()))


def _ml_chunk_head(q, k, v, li_c, bc_c, bc_r, b_last, seen, ct_ref, n_ref, m_ref):
    m = m_ref[:, 0:1]
    ct = ct_ref[...]
    n_rows = n_ref[...]
    qb, kb, vb = q.astype(BF16), k.astype(BF16), v.astype(BF16)
    d_t = jnp.where(seen, bc_r + (li_c - bc_c), -jnp.inf)
    inter = bc_r + m
    m_t = jnp.maximum(inter, jnp.max(d_t, axis=0, keepdims=True))
    w_inter = jnp.exp(inter - m_t)
    s_t = lax.dot_general(kb, qb, _NT, preferred_element_type=F32) * jnp.exp(d_t - m_t)
    num_t = (jnp.dot(vb.T, s_t.astype(BF16), preferred_element_type=F32)
             + w_inter * lax.dot_general(ct.astype(BF16), qb, _NT, preferred_element_type=F32))
    qn = lax.dot_general(n_rows.astype(BF16), qb, _NT, preferred_element_type=F32)[0:1]
    den = jnp.sum(s_t, axis=0, keepdims=True) + w_inter * qn
    h_t = num_t / jnp.maximum(jnp.abs(den), jnp.exp(-m_t))
    g = b_last - bc_c + li_c
    m_new = jnp.maximum(b_last + m, jnp.max(g, axis=0, keepdims=True))
    kw = k * jnp.exp(g - m_new)
    wc = jnp.exp(b_last + m - m_new)
    ct_ref[...] = wc * ct + jnp.dot(vb.T, kw.astype(BF16), preferred_element_type=F32)
    n_ref[...] = wc * n_rows + jnp.sum(kw, axis=0, keepdims=True)
    m_ref[...] = jnp.broadcast_to(m_new, m_ref.shape)
    return h_t.T


def _ml_scan_kernel(q_ref, k_ref, v_ref, g_ref, c0_ref, n0_ref, m0_ref, h_o, c1_o, n1_o, m1_o,
                    c_sc, n_sc, m_sc, *, direction):
    j = pl.program_id(0)

    @pl.when(j == 0)
    def _():
        c_sc[...] = c0_ref[...]
        n_sc[...] = n0_ref[...]
        m_sc[...] = m0_ref[...]

    r = lax.broadcasted_iota(jnp.int32, (ML_CHUNK, ML_CHUNK), 0)
    s = lax.broadcasted_iota(jnp.int32, (ML_CHUNK, ML_CHUNK), 1)
    tri_f = ((s >= r) if direction else (s <= r)).astype(F32)
    seen = (r >= s) if direction else (r <= s)
    n_chunks = q_ref.shape[1] // ML_CHUNK
    order = range(n_chunks - 1, -1, -1) if direction else range(n_chunks)
    last = 0 if direction else ML_CHUNK - 1
    for c in order:
        rows = slice(c * ML_CHUNK, (c + 1) * ML_CHUNK)
        for bi in range(q_ref.shape[0]):
            gch = g_ref[bi, rows, :]
            bc = jnp.dot(tri_f, gch, preferred_element_type=F32, precision=lax.Precision.HIGHEST)
            bct = bc.T
            for hd in range(ML_HEADS):
                ci = 2 * ML_HEADS * direction + hd
                cf = ci + ML_HEADS
                cols = slice(hd * ML_DIM, (hd + 1) * ML_DIM)
                h = _ml_chunk_head(q_ref[bi, rows, cols], k_ref[bi, rows, cols], v_ref[bi, rows, cols],
                                   gch[:, ci:ci + 1], bc[:, cf:cf + 1], bct[cf:cf + 1, :],
                                   bc[last:last + 1, cf:cf + 1], seen,
                                   c_sc.at[bi, hd], n_sc.at[bi, hd], m_sc.at[bi, hd])
                h_o[bi, rows, cols] = h

    @pl.when(j == pl.num_programs(0) - 1)
    def _():
        c1_o[...] = c_sc[...]
        n1_o[...] = n_sc[...]
        m1_o[...] = m_sc[...]


def _ml_state_shapes(b):
    return [(b, ML_HEADS, ML_DIM, ML_DIM), (b, ML_HEADS, 8, ML_DIM), (b, ML_HEADS, 1, ML_DIM)]


def ml_scan(q, k, p, g, state, direction):
    b, t, w = q.shape
    ts = _tile(t, ML_STEP)
    nst = t // ts
    tok = (lambda j: (0, nst - 1 - j, 0)) if direction else (lambda j: (0, j, 0))
    tokv = (lambda j: (0, nst - 1 - j, C_MLV // w)) if direction else (lambda j: (0, j, C_MLV // w))
    st_shapes = _ml_state_shapes(b)
    st_specs = [pl.BlockSpec(s, lambda j: (0, 0, 0, 0)) for s in st_shapes]
    h, c1, n1, m1 = pl.pallas_call(
        functools.partial(_ml_scan_kernel, direction=direction),
        grid=(nst,),
        in_specs=[pl.BlockSpec((b, ts, w), tok), pl.BlockSpec((b, ts, w), tok), pl.BlockSpec((b, ts, w), tokv),
                  pl.BlockSpec((b, ts, LANES), tok)] + st_specs,
        out_specs=[pl.BlockSpec((b, ts, w), tok)] + st_specs,
        out_shape=[jax.ShapeDtypeStruct((b, t, w), F32)] + [jax.ShapeDtypeStruct(s, F32) for s in st_shapes],
        scratch_shapes=[pltpu.VMEM(s, F32) for s in st_shapes],
        compiler_params=_cparams(("arbitrary",)),
        name="ml_scan",
    )(q, k, p, g, *state)
    return h, (c1, n1, m1)


def mlstm_branch(p_c, p_l, conv_w, conv_b, gate_b, need_ctx):
    b = p_l.shape[0]
    qc, kc, gc = ml_prep(p_c, conv_w, conv_b, gate_b)
    ql, kl, gl = ml_prep(p_l, conv_w, conv_b, gate_b)
    zero = tuple(jnp.zeros(s, F32) for s in _ml_state_shapes(b))
    hs_c, hs_l = [], []
    for direction in (0, 1):
        hc, st = ml_scan(qc, kc, p_c, gc, zero, direction)
        hl, _ = ml_scan(ql, kl, p_l, gl, st, direction)
        hs_c.append(hc)
        hs_l.append(hl)
    return (tuple(hs_c) if need_ctx else None), tuple(hs_l)


def _merge_kernel(ya_ref, hf_ref, hb_ref, yc_ref, yd_ref, og_ref, g_ref, mg_ref, wb_ref, wo_ref, x_ref, gate_ref,
                  o_ref):
    d = x_ref.shape[-1]
    mg = mg_ref[...]
    yb = []
    for hd in range(ML_HEADS):
        cols = slice(hd * ML_DIM, (hd + 1) * ML_DIM)
        h = hf_ref[0, :, cols] + hb_ref[0, :, cols]
        h = h * lax.rsqrt(jnp.mean(h * h, axis=-1, keepdims=True) + EPS) * mg
        yb.append((h * jax.nn.sigmoid(og_ref[0, :, cols].astype(F32))).astype(BF16))
    ys = (ya_ref[0].astype(BF16), jnp.concatenate(yb, axis=-1), yc_ref[0].astype(BF16), yd_ref[0].astype(BF16))
    acc = None
    for n, y in enumerate(ys):
        t = jnp.dot(y, wb_ref[n], preferred_element_type=F32)
        t = jax.nn.sigmoid(g_ref[0, :, n * d:(n + 1) * d].astype(F32)) * t
        acc = t if acc is None else acc + t
    z = jnp.dot(acc.astype(BF16), wo_ref[...], preferred_element_type=F32)
    o_ref[0] = x_ref[0] + gate_ref[0] * z


def merge(ya, hs, yc, yd, p, ml_norm_g, w_branch, w_out, x, gate):
    b, t, d = x.shape
    tm = _tile(t, ROW_TILE)
    w = ML_HEADS * ML_DIM
    ysp = pl.BlockSpec((1, tm, w), lambda i, j: (i, j, 0))
    return pl.pallas_call(
        _merge_kernel,
        grid=(b, t // tm),
        in_specs=[ysp, ysp, ysp, ysp, ysp,
                  pl.BlockSpec((1, tm, w), lambda i, j: (i, j, C_MLO // w)),
                  pl.BlockSpec((1, tm, N_BRANCH * d), lambda i, j: (i, j, C_GATE // (N_BRANCH * d))),
                  pl.BlockSpec((1, ML_DIM), lambda i, j: (0, 0)),
                  pl.BlockSpec((N_BRANCH, w, d), lambda i, j: (0, 0, 0)),
                  pl.BlockSpec((d, d), lambda i, j: (0, 0)),
                  pl.BlockSpec((1, tm, d), lambda i, j: (i, j, 0)),
                  pl.BlockSpec((1, 1, d), lambda i, j: (i, 0, 0))],
        out_specs=pl.BlockSpec((1, tm, d), lambda i, j: (i, j, 0)),
        out_shape=jax.ShapeDtypeStruct((b, t, d), F32),
        compiler_params=_cparams(("parallel", "parallel")),
        name="merge",
    )(ya, hs[0], hs[1], yc, yd, p, p, ml_norm_g.reshape(1, ML_DIM), w_branch, w_out, x, gate.reshape(b, 1, d))


def _expert_kernel(be_ref, x_ref, wg_ref, wu_ref, wd_ref, o_ref, wg_sc, wu_sc, wd_sc):
    i = pl.program_id(0)

    @pl.when(jnp.logical_or(i == 0, be_ref[i] != be_ref[jnp.maximum(i - 1, 0)]))
    def _():
        wg_sc[...] = wg_ref[0, 0].astype(BF16)
        wu_sc[...] = wu_ref[0, 0].astype(BF16)
        wd_sc[...] = wd_ref[0, 0].astype(BF16)

    x = x_ref[...]
    a = jnp.dot(x, wg_sc[...], preferred_element_type=F32)
    u = jnp.dot(x, wu_sc[...], preferred_element_type=F32)
    h = (a * jax.nn.sigmoid(a)) * u
    o_ref[...] = jnp.dot(h.astype(BF16), wd_sc[...], preferred_element_type=F32).astype(o_ref.dtype)


def expert_blocks(xb, blk_exp, layer, w_gate, w_up, w_down):
    m, d = xb.shape
    hdim = w_gate.shape[-1]
    n_blocks = m // MOE_BLOCK
    grid_spec = pltpu.PrefetchScalarGridSpec(
        num_scalar_prefetch=1,
        grid=(n_blocks,),
        in_specs=[pl.BlockSpec((MOE_BLOCK, d), lambda i, be: (i, 0)),
                  pl.BlockSpec((1, 1, d, hdim), lambda i, be: (layer, be[i], 0, 0)),
                  pl.BlockSpec((1, 1, d, hdim), lambda i, be: (layer, be[i], 0, 0)),
                  pl.BlockSpec((1, 1, hdim, d), lambda i, be: (layer, be[i], 0, 0))],
        out_specs=pl.BlockSpec((MOE_BLOCK, d), lambda i, be: (i, 0)),
        scratch_shapes=[pltpu.VMEM((d, hdim), BF16), pltpu.VMEM((d, hdim), BF16), pltpu.VMEM((hdim, d), BF16)],
    )
    return pl.pallas_call(
        _expert_kernel,
        grid_spec=grid_spec,
        out_shape=jax.ShapeDtypeStruct((m, d), BF16),
        compiler_params=_cparams(("arbitrary",)),
        name="moe_experts",
    )(blk_exp, xb, w_gate, w_up, w_down)


def _rope_tables(n_tok):
    rows = n_tok // GRID_W
    row = jnp.repeat(jnp.arange(rows, dtype=F32), GRID_W)
    col = jnp.broadcast_to(jnp.arange(GRID_W, dtype=F32), (rows, GRID_W)).reshape(-1)
    n_freq = DIFF_DIM // 4
    inv = ROPE_BASE ** (-jnp.arange(n_freq, dtype=F32) / n_freq)
    ar = row[:, None] * inv
    ac = col[:, None] * inv
    ang = jnp.concatenate([ar, ar, ac, ac], axis=-1)
    cos, sin = jnp.cos(ang), jnp.sin(ang)
    first = (jnp.arange(DIFF_DIM) % 32) < 16
    sa = jnp.where(first, -sin, 0.0)
    sb = jnp.where(first, 0.0, sin)
    return tuple(jnp.tile(a, (1, 2)) for a in (cos, sa, sb))


DFT_N2 = 256
DFT_J = 8
DFT_P = 4


def _hy_prep_kernel(*refs):
    ins, outs = refs[:9], refs[11:]
    w_ref, b_ref = refs[9], refs[10]
    j = pl.program_id(1)
    nblk = pl.num_programs(1)
    for n in range(HY_ORDER + 1):
        x_ref, p_ref, n_ref = ins[3 * n:3 * n + 3]
        cols = slice(n * HY_CH, (n + 1) * HY_CH)
        prev, nxt = _halo_rows(p_ref, n_ref, j, nblk)
        outs[n][0] = _conv3(x_ref[0].astype(F32), prev, nxt, w_ref[:, cols], b_ref[:, cols]).astype(outs[n].dtype)


def hy_prep(p, conv_w, conv_b):
    b, t, _ = p.shape
    tt = _tile(t, ROW_TILE)
    in_specs, args = [], []
    for n in range(HY_ORDER + 1):
        cb = C_HY // HY_CH + n
        prev, nxt = _halo_specs(tt, HY_CH, cb, t)
        in_specs += [pl.BlockSpec((1, tt, HY_CH), lambda i, j, cb=cb: (i, j, cb)), prev, nxt]
        args += [p, p, p]
    nch = (HY_ORDER + 1) * HY_CH
    in_specs += [pl.BlockSpec((3, nch), lambda i, j: (0, 0)), pl.BlockSpec((1, nch), lambda i, j: (0, 0))]
    osp = pl.BlockSpec((1, tt, HY_CH), lambda i, j: (i, j, 0))
    return pl.pallas_call(
        _hy_prep_kernel,
        grid=(b, t // tt),
        in_specs=in_specs,
        out_specs=[osp] * (HY_ORDER + 1),
        out_shape=[jax.ShapeDtypeStruct((b, t, HY_CH), BF16)] * (HY_ORDER + 1),
        compiler_params=_cparams(("parallel", "parallel")),
        name="hy_prep",
    )(*args, conv_w, conv_b.reshape(1, nch))


def _hy_filter_kernel(emb_ref, w1_ref, b1_ref, w2_ref, b2_ref, w3_ref, fr_ref, al_ref, f_o, ss_o, *, length, half):
    j = pl.program_id(0)
    tt = emb_ref.shape[0]
    a = jnp.dot(emb_ref[...].astype(BF16), w1_ref[...].astype(BF16), preferred_element_type=F32) + b1_ref[...]
    a = jnp.sin(fr_ref[0:1, :] * a)
    a = jnp.dot(a.astype(BF16), w2_ref[...].astype(BF16), preferred_element_type=F32) + b2_ref[...]
    a = jnp.sin(fr_ref[1:2, :] * a)
    filt = jnp.dot(a.astype(BF16), w3_ref[...].astype(BF16), preferred_element_type=F32)
    r = lax.broadcasted_iota(jnp.int32, (tt, HY_CH), 0)
    if half:
        row = DFT_N2 * (r & (half - 1)) + j * (tt // half) + (r >> (half.bit_length() - 1))
    else:
        row = r + j * tt
    window = jnp.exp(-(row.astype(F32) / length) * al_ref[...]) + FILTER_SHIFT

    @pl.when(j == 0)
    def _():
        ss_o[...] = jnp.zeros(ss_o.shape, F32)

    for o in range(HY_ORDER):
        for d in range(2):
            idx = 2 * o + d
            f = filt[:, idx * HY_CH:(idx + 1) * HY_CH] * window
            if d == 1:
                f = jnp.where(row == 0, 0.0, f)
            if half:
                for jj in range(tt // half):
                    f_o[idx, :, jj * HY_CH:(jj + 1) * HY_CH] = f[jj * half:(jj + 1) * half]
            else:
                f_o[idx] = f
            ss_o[o:o + 1, :] += jnp.sum(f * f, axis=0, keepdims=True)


def hy_filters(length, w1, b1, w2, b2, w3, freq, wide):
    t = jnp.arange(length, dtype=F32) / length
    bands = jnp.arange(1, FILTER_BANDS + 1, dtype=F32)
    ang = 2.0 * math.pi * t[:, None] * bands
    emb = jnp.concatenate([t[:, None], jnp.cos(ang), jnp.sin(ang)], axis=-1)
    pad = LANES - emb.shape[1]
    emb = jnp.pad(emb, ((0, 0), (0, pad)))
    w1 = jnp.pad(w1, ((0, pad), (0, 0)))
    ne, nh = emb.shape[1], w1.shape[1]
    alpha = jnp.linspace(abs(math.log(DECAY_TARGET)) / SLOW_DECAY_PCT,
                         abs(math.log(DECAY_TARGET)) / FAST_DECAY_PCT, HY_CH).reshape(1, HY_CH)
    tt = _tile(length, ROW_TILE)
    half = length // DFT_N2
    wide = wide and half > 0 and half & (half - 1) == 0 and tt % half == 0
    full = lambda shape: pl.BlockSpec(shape, lambda j: (0,) * len(shape))
    if wide:
        emb = jnp.swapaxes(emb.reshape(half, DFT_N2, ne), 0, 1).reshape(length, ne)
        f_spec = pl.BlockSpec((2 * HY_ORDER, half, (tt // half) * HY_CH), lambda j: (0, 0, j))
        f_shape = (2 * HY_ORDER, half, DFT_N2 * HY_CH)
    else:
        f_spec = pl.BlockSpec((2 * HY_ORDER, tt, HY_CH), lambda j: (0, j, 0))
        f_shape = (2 * HY_ORDER, length, HY_CH)
    f, ss = pl.pallas_call(
        functools.partial(_hy_filter_kernel, length=length, half=half if wide else 0),
        grid=(length // tt,),
        in_specs=[pl.BlockSpec((tt, ne), lambda j: (j, 0)), full((ne, nh)), full((1, nh)), full((nh, nh)),
                  full((1, nh)), full((nh, 2 * HY_ORDER * HY_CH)), full((2, nh)), full((1, HY_CH))],
        out_specs=[f_spec, full((HY_ORDER, HY_CH))],
        out_shape=[jax.ShapeDtypeStruct(f_shape, F32), jax.ShapeDtypeStruct((HY_ORDER, HY_CH), F32)],
        compiler_params=_cparams(("arbitrary",)),
        name="hy_filter",
    )(emb, w1, b1.reshape(1, nh), w2, b2.reshape(1, nh), w3, freq, alpha)
    return f, lax.rsqrt(ss + EPS)


def _dft_tables(length):
    n = 2 * length
    n1 = n // DFT_N2
    half = n1 // 2
    n1h = -(-(half + 1) // 16) * 16
    kv = jnp.arange(n1h, dtype=jnp.int32)
    valid = (kv <= half).astype(F32)[None, :, None]
    pair = jnp.where((kv == 0) | (kv == half), 1.0, 2.0)[None, :, None] * valid
    k1 = kv[None, :, None]
    tn = (DFT_N2 * jnp.arange(half, dtype=jnp.int32)[None, None, :]
          + jnp.arange(DFT_N2, dtype=jnp.int32)[:, None, None])
    th = (2.0 * math.pi / n) * ((k1 * tn) % n).astype(F32)
    ga = jnp.concatenate([jnp.cos(th) * valid, -jnp.sin(th) * valid], axis=1).astype(BF16)
    gi = jnp.swapaxes(jnp.concatenate([jnp.cos(th) * pair, -jnp.sin(th) * pair], axis=1), 1, 2).astype(BF16)
    kk = jnp.arange(DFT_N2, dtype=jnp.int32)
    t2 = (2.0 * math.pi / DFT_N2) * ((kk[:, None] * kk[None, :]) % DFT_N2).astype(F32)
    c2, s2 = jnp.cos(t2), jnp.sin(t2)
    mf = jnp.block([[c2, s2], [-s2, c2]]).astype(BF16)
    mi = jnp.block([[c2, -s2], [s2, c2]]).astype(BF16)
    return ga, gi, mf, mi


def _dft_a_kernel(z_ref, g_ref, o_ref):
    c = HY_CH
    for j in range(DFT_J):
        slab = z_ref[0, :, j * c:(j + 1) * c].astype(BF16)
        r = jnp.dot(g_ref[j], slab, preferred_element_type=F32)
        o_ref[0, :, :, j * c:(j + 1) * c] = r.reshape(2, r.shape[0] // 2, c).astype(o_ref.dtype)


def dft_a(z, ga):
    bz, half, wid = z.shape
    n1 = ga.shape[1] // 2
    jc = DFT_J * HY_CH
    return pl.pallas_call(
        _dft_a_kernel,
        grid=(bz, DFT_N2 // DFT_J),
        in_specs=[pl.BlockSpec((1, half, jc), lambda i, j: (i, 0, j)),
                  pl.BlockSpec((DFT_J, 2 * n1, half), lambda i, j: (j, 0, 0))],
        out_specs=pl.BlockSpec((1, 2, n1, jc), lambda i, j: (i, 0, 0, j)),
        out_shape=jax.ShapeDtypeStruct((bz, 2, n1, wid), BF16),
        compiler_params=_cparams(("parallel", "parallel")),
        name="dft_a",
    )(z, ga)


def _stack_ri(ref, b, k):
    return jnp.concatenate([ref[b, 0, k], ref[b, 1, k]], axis=0)


def _spec_filter_kernel(f_ref, mf_ref, sc_ref, h_o):
    sc = sc_ref[0]
    for k in range(DFT_P):
        xf = jnp.dot(mf_ref[...], _stack_ri(f_ref, 0, k), preferred_element_type=F32)
        xb = jnp.dot(mf_ref[...], _stack_ri(f_ref, 1, k), preferred_element_type=F32)
        h_o[0, k, 0] = ((xf[:DFT_N2] + xb[:DFT_N2]) * sc).astype(h_o.dtype)
        h_o[0, k, 1] = ((xf[DFT_N2:] - xb[DFT_N2:]) * sc).astype(h_o.dtype)


def spec_filter(fa, mf, scale):
    nb, _, n1, _, c = fa.shape
    order = nb // 2
    return pl.pallas_call(
        _spec_filter_kernel,
        grid=(order, n1 // DFT_P),
        in_specs=[pl.BlockSpec((2, 2, DFT_P, DFT_N2, c), lambda o, k: (o, 0, k, 0, 0)),
                  pl.BlockSpec((2 * DFT_N2, 2 * DFT_N2), lambda o, k: (0, 0)),
                  pl.BlockSpec((1, 1, c), lambda o, k: (o, 0, 0))],
        out_specs=pl.BlockSpec((1, DFT_P, 2, DFT_N2, c), lambda o, k: (o, k, 0, 0, 0)),
        out_shape=jax.ShapeDtypeStruct((order, n1, 2, DFT_N2, c), BF16),
        compiler_params=_cparams(("parallel", "parallel")),
        name="spec_filter",
    )(fa, mf, scale.reshape(order, 1, c))


def _spec_conv_kernel(a_ref, h_ref, mf_ref, mi_ref, o_ref):
    for k in range(DFT_P):
        x = jnp.dot(mf_ref[...], _stack_ri(a_ref, 0, k), preferred_element_type=F32)
        xr, xi = x[:DFT_N2], x[DFT_N2:]
        hr, hi = h_ref[0, k, 0].astype(F32), h_ref[0, k, 1].astype(F32)
        y = jnp.concatenate([xr * hr - xi * hi, xr * hi + xi * hr], axis=0).astype(BF16)
        z = jnp.dot(mi_ref[...], y, preferred_element_type=F32)
        o_ref[0, 0, k] = z[:DFT_N2].astype(o_ref.dtype)
        o_ref[0, 1, k] = z[DFT_N2:].astype(o_ref.dtype)


def spec_conv(a, h, order, mf, mi):
    b, _, n1, _, c = a.shape
    blk = pl.BlockSpec((1, 2, DFT_P, DFT_N2, c), lambda i, k: (i, 0, k, 0, 0))
    mat = pl.BlockSpec((2 * DFT_N2, 2 * DFT_N2), lambda i, k: (0, 0))
    return pl.pallas_call(
        _spec_conv_kernel,
        grid=(b, n1 // DFT_P),
        in_specs=[blk, pl.BlockSpec((1, DFT_P, 2, DFT_N2, c), lambda i, k: (order, k, 0, 0, 0)), mat, mat],
        out_specs=blk,
        out_shape=jax.ShapeDtypeStruct(a.shape, BF16),
        compiler_params=_cparams(("parallel", "parallel")),
        name="spec_conv",
    )(a, h, mf, mi)


def _dft_ainv_kernel(z_ref, g_ref, xg_ref, zin_ref, bias_ref, o_ref):
    c = HY_CH
    n1 = z_ref.shape[2]
    for j in range(DFT_J):
        cols = slice(j * c, (j + 1) * c)
        zz = z_ref[0, :, :, cols].reshape(2 * n1, c)
        y = jnp.dot(g_ref[j], zz, preferred_element_type=F32)
        gated = xg_ref[0, :, cols].astype(F32) * (y + bias_ref[...] * zin_ref[0, :, cols].astype(F32))
        o_ref[0, :, cols] = gated.astype(o_ref.dtype)


def dft_ainv(z, gi, xg, zin, bias):
    b, _, n1, wid = z.shape
    half = gi.shape[1]
    jc = DFT_J * HY_CH
    tok = pl.BlockSpec((1, half, jc), lambda i, j: (i, 0, j))
    return pl.pallas_call(
        _dft_ainv_kernel,
        grid=(b, DFT_N2 // DFT_J),
        in_specs=[pl.BlockSpec((1, 2, n1, jc), lambda i, j: (i, 0, 0, j)),
                  pl.BlockSpec((DFT_J, half, 2 * n1), lambda i, j: (j, 0, 0)),
                  tok, tok, pl.BlockSpec((1, HY_CH), lambda i, j: (0, 0))],
        out_specs=tok,
        out_shape=jax.ShapeDtypeStruct((b, half, wid), BF16),
        compiler_params=_cparams(("parallel", "parallel")),
        name="dft_ainv",
    )(z, gi, xg, zin, bias.reshape(1, HY_CH))


def _ctx_conv_kernel(z_ref, xg_ref, f_ref, mf_ref, mi_ref, sc_ref, bias_ref, o_ref):
    nf = mf_ref.shape[0] // 2
    mf = mf_ref[...]
    xf = jnp.dot(mf, f_ref[0].astype(BF16), preferred_element_type=F32)
    xb = jnp.dot(mf, f_ref[1].astype(BF16), preferred_element_type=F32)
    sc = sc_ref[0]
    hr = (xf[:nf] + xb[:nf]) * sc
    hi = (xf[nf:] - xb[nf:]) * sc
    z = z_ref[0].astype(F32)
    x = jnp.dot(mf, z.astype(BF16), preferred_element_type=F32)
    xr, xi = x[:nf], x[nf:]
    y = jnp.concatenate([xr * hr - xi * hi, xr * hi + xi * hr], axis=0).astype(BF16)
    gated = xg_ref[0].astype(F32) * (jnp.dot(mi_ref[...], y, preferred_element_type=F32) + bias_ref[...] * z)
    o_ref[0] = gated.astype(o_ref.dtype)


def ctx_conv(z, xg, f, order, scale, bias):
    b, length, c = z.shape
    n = 2 * length
    kk = jnp.arange(n, dtype=jnp.int32)[:, None]
    tn = jnp.arange(length, dtype=jnp.int32)[None, :]
    th = (2.0 * math.pi / n) * ((kk * tn) % n).astype(F32)
    mf = jnp.concatenate([jnp.cos(th), -jnp.sin(th)], axis=0).astype(BF16)
    mi = jnp.concatenate([jnp.cos(th.T), -jnp.sin(th.T)], axis=1).astype(BF16)
    tok = pl.BlockSpec((1, length, c), lambda i: (i, 0, 0))
    return pl.pallas_call(
        _ctx_conv_kernel,
        grid=(b,),
        in_specs=[tok, tok, pl.BlockSpec((2, length, c), lambda i: (order, 0, 0)),
                  pl.BlockSpec((2 * n, length), lambda i: (0, 0)), pl.BlockSpec((length, 2 * n), lambda i: (0, 0)),
                  pl.BlockSpec((1, 1, c), lambda i: (order, 0, 0)), pl.BlockSpec((1, c), lambda i: (0, 0))],
        out_specs=tok,
        out_shape=jax.ShapeDtypeStruct((b, length, c), BF16),
        compiler_params=_cparams(("parallel",)),
        name="ctx_conv",
    )(z, xg, f, mf, mi, scale.reshape(-1, 1, c), bias.reshape(1, c))


def hyena(p, conv_w, conv_b, w1, b1, w2, b2, w3, freq, bias, tables):
    b, length, _ = p.shape
    parts = hy_prep(p, conv_w, conv_b)
    f, rnorm = hy_filters(length, w1, b1, w2, b2, w3, freq, wide=tables is not None)
    scale = rnorm / (2 * length)
    z = parts[0]
    if tables is None:
        for o in range(HY_ORDER):
            z = ctx_conv(z, parts[o + 1], f, o, scale, bias[o])
        return z
    ga, gi, mf, mi = tables
    half = ga.shape[2]
    wid = DFT_N2 * HY_CH
    fa = dft_a(f.reshape(2 * HY_ORDER, half, wid), ga)
    n1 = fa.shape[2]
    h = spec_filter(fa.reshape(2 * HY_ORDER, 2, n1, DFT_N2, HY_CH), mf, scale)
    for o in range(HY_ORDER):
        a = dft_a(z.reshape(b, half, wid), ga).reshape(b, 2, n1, DFT_N2, HY_CH)
        zc = spec_conv(a, h, o, mf, mi).reshape(b, 2, n1, wid)
        z = dft_ainv(zc, gi, parts[o + 1].reshape(b, half, wid), z.reshape(b, half, wid), bias[o])
        z = z.reshape(b, length, HY_CH)
    return z


ROUTE_ROWS = 8


def _router_kernel(x_ref, g_ref, sh_ref, sc_ref, w_ref, b_ref, h_o, id_o, gate_o):
    x = x_ref[0]
    y = x * lax.rsqrt(jnp.mean(x * x, axis=-1, keepdims=True) + EPS)
    y = ((y * g_ref[...]) * (1.0 + sc_ref[0]) + sh_ref[0]).astype(BF16)
    h_o[0] = y
    logits = jnp.dot(y, w_ref[...], preferred_element_type=F32) + b_ref[...]
    lane = lax.broadcasted_iota(jnp.int32, logits.shape, 1)
    lane_f = lane.astype(F32)
    none = float(LANES)

    def top(vals):
        v = jnp.max(vals, axis=1, keepdims=True)
        return v, jnp.min(jnp.where(vals == v, lane_f, none), axis=1, keepdims=True)

    is_grp = lane < MOE_GROUPS
    mg, grp = top(jnp.where(is_grp, logits, -jnp.inf))
    p_grp = 1.0 / jnp.sum(jnp.where(is_grp, jnp.exp(logits - mg), 0.0), axis=1, keepdims=True)
    lo = MOE_GROUPS + MOE_EPG * grp
    el = jnp.where((lane_f >= lo) & (lane_f < lo + MOE_EPG), logits, -jnp.inf)
    v1, i1 = top(el)
    v2, i2 = top(jnp.where(lane_f == i1, -jnp.inf, el))
    t = jnp.exp(v2 - v1)
    g1 = p_grp / (1.0 + t)
    ids = jnp.where(lane == 0, i1 - MOE_GROUPS, jnp.where(lane == 1, i2 - MOE_GROUPS, 0.0))
    id_o[0, 0] = ids.T[:ROUTE_ROWS].astype(jnp.int32)
    gate_o[0, 0] = jnp.where(lane == 0, g1, jnp.where(lane == 1, g1 * t, 0.0)).T[:ROUTE_ROWS]


def moe_router(x, g, shift, scale, w_group, b_group, w_router, b_router):
    b, t, d = x.shape
    tt = _tile(t, ROW_TILE)
    rows = pl.BlockSpec((1, 1, ROUTE_ROWS, tt), lambda i, j: (i, j, 0, 0))
    npad = LANES - MOE_GROUPS - MOE_EXPERTS
    w = jnp.concatenate([w_group, w_router, jnp.zeros((d, npad), F32)], axis=1).astype(BF16)
    bias = jnp.concatenate([b_group, b_router, jnp.zeros((npad,), F32)]).reshape(1, LANES)
    tok = lambda width: pl.BlockSpec((1, tt, width), lambda i, j: (i, j, 0))
    mod = pl.BlockSpec((1, 1, d), lambda i, j: (i, 0, 0))
    return pl.pallas_call(
        _router_kernel,
        grid=(b, t // tt),
        in_specs=[tok(d), pl.BlockSpec((1, d), lambda i, j: (0, 0)), mod, mod,
                  pl.BlockSpec((d, LANES), lambda i, j: (0, 0)), pl.BlockSpec((1, LANES), lambda i, j: (0, 0))],
        out_specs=[tok(d), rows, rows],
        out_shape=[jax.ShapeDtypeStruct((b, t, d), BF16),
                   jax.ShapeDtypeStruct((b, t // tt, ROUTE_ROWS, tt), jnp.int32),
                   jax.ShapeDtypeStruct((b, t // tt, ROUTE_ROWS, tt), F32)],
        compiler_params=_cparams(("parallel", "parallel")),
        name="moe_router",
    )(x, g.reshape(1, d), shift.reshape(b, 1, d), scale.reshape(b, 1, d), w, bias)


def _blocked_cumsum(onehot, blk=256):
    m, e = onehot.shape
    if m % blk:
        return jnp.cumsum(onehot, axis=0)
    oh = onehot.astype(BF16).reshape(m // blk, blk, e)
    tril = jnp.tril(jnp.ones((blk, blk), BF16))
    within = jnp.einsum('ts,bse->bte', tril, oh, preferred_element_type=F32)
    tot = within[:, -1, :]
    off = jnp.cumsum(tot, axis=0) - tot
    return (within + off[:, None, :]).reshape(m, e).astype(jnp.int32)


def _hier_moe(h, ids, layer, w_gate, w_up, w_down):
    n_tok, d = h.shape
    e_flat = ids.reshape(-1)
    m_slots = n_tok * MOE_TOP_K
    onehot = (e_flat[:, None] == jnp.arange(MOE_EXPERTS, dtype=jnp.int32)[None, :]).astype(jnp.int32)
    csum = _blocked_cumsum(onehot)
    rank = jnp.sum(onehot * csum, axis=1) - 1
    counts = csum[-1]
    padded = (counts + MOE_BLOCK - 1) // MOE_BLOCK * MOE_BLOCK
    p_end = jnp.cumsum(padded)
    dest = (p_end - padded)[e_flat] + rank
    n_blocks = -(-(m_slots + MOE_EXPERTS * (MOE_BLOCK - 1)) // MOE_BLOCK)
    slot_tok = jnp.arange(m_slots, dtype=jnp.int32) // MOE_TOP_K
    buf_tok = jnp.zeros((n_blocks * MOE_BLOCK,), jnp.int32).at[dest].set(slot_tok)
    first_row = jnp.arange(n_blocks, dtype=jnp.int32)[:, None] * MOE_BLOCK
    blk_exp = jnp.minimum(jnp.sum((p_end[None, :] <= first_row).astype(jnp.int32), axis=1), MOE_EXPERTS - 1)
    yb = expert_blocks(h[buf_tok], blk_exp, layer, w_gate, w_up, w_down)
    return yb, dest.reshape(n_tok, MOE_TOP_K)


def _moe_combine_kernel(x_ref, y0_ref, y1_ref, g_ref, m_ref, o_ref):
    g = g_ref[0, 0].T
    f = g[:, 0:1] * y0_ref[...].astype(F32) + g[:, 1:2] * y1_ref[...].astype(F32)
    o_ref[0] = x_ref[0] + m_ref[0] * f


def moe_combine(x, yb, dest, gates, mod):
    b, t, d = x.shape
    y0 = yb[dest[:, 0]]
    y1 = yb[dest[:, 1]]
    nt, tt = gates.shape[1], gates.shape[3]
    row = pl.BlockSpec((tt, d), lambda i, j: (i * nt + j, 0))
    return pl.pallas_call(
        _moe_combine_kernel,
        grid=(b, nt),
        in_specs=[pl.BlockSpec((1, tt, d), lambda i, j: (i, j, 0)), row, row,
                  pl.BlockSpec((1, 1, ROUTE_ROWS, tt), lambda i, j: (i, j, 0, 0)),
                  pl.BlockSpec((1, 1, d), lambda i, j: (i, 0, 0))],
        out_specs=pl.BlockSpec((1, tt, d), lambda i, j: (i, j, 0)),
        out_shape=jax.ShapeDtypeStruct((b, t, d), F32),
        compiler_params=_cparams(("parallel", "parallel")),
        name="moe_combine",
    )(x, y0, y1, gates, mod.reshape(b, 1, d))


def _permute_w_in(w):
    d = w.shape[0]
    sizes = (DIFF_HEADS * 2 * DIFF_DIM, DIFF_HEADS * 2 * DIFF_DIM, DIFF_HEADS * DIFF_VDIM, 3 * ML_HEADS * ML_DIM,
             ML_HEADS * ML_DIM, 4 * ML_HEADS, GQA_HEADS * GQA_DIM, 2 * GQA_KV * GQA_DIM, (HY_ORDER + 1) * HY_CH,
             N_BRANCH * d)
    offs = np.cumsum((0,) + sizes)
    dq, dk, dv, mlqkv, mlo, mlg, gq, gkv, hy, gate = [w[:, offs[i]:offs[i + 1]] for i in range(10)]
    pad = jnp.zeros((d, N_P - C_MLG - 4 * ML_HEADS), w.dtype)
    return jnp.concatenate([gate, dq, dk, dv, mlqkv, mlo, gq, hy, gkv, mlg, pad], axis=1)


def kernel(x, c, ctx, c_ctx, w_ada, b_ada, norm1_g, norm2_g, w_in, diff_lam, diff_norm_g, ml_conv_w, ml_conv_b, ml_gate_b, ml_norm_g, gqa_qnorm_g, gqa_knorm_g, hy_conv_w, hy_conv_b, hy_f_w1, hy_f_b1, hy_f_w2, hy_f_b2, hy_f_w3, hy_f_freq, hy_bias, w_branch, w_out, moe_w_group, moe_b_group, moe_w_router, moe_b_router, moe_w_gate, moe_w_up, moe_w_down, final_norm_g):
    b, n, d = x.shape
    n_ctx = ctx.shape[1]
    depth = w_in.shape[0]
    tk = _tile(n, 2048)
    tables = _rope_tables(n)
    dft_tables = _dft_tables(n)
    sc = jax.nn.silu(c)
    scx = jax.nn.silu(c_ctx)
    xs, cs = x, ctx
    for l in range(depth):
        need_ctx = l < depth - 1
        mod_l = jnp.split(sc @ w_ada[l] + b_ada[l], 6, axis=-1)
        mod_c = [jnp.broadcast_to(m, (b, d)) for m in jnp.split(scx @ w_ada[l] + b_ada[l], 6, axis=-1)]
        w_p = _permute_w_in(w_in[l]).astype(BF16)
        p_l = norm_matmul(xs, norm1_g[l], mod_l[0], mod_l[1], w_p, BF16)
        p_c = norm_matmul(cs, norm1_g[l], mod_c[0], mod_c[1], w_p, BF16)

        dq_l, dkt_l, dv_l, gq_l, gkt_l, gv_l = attn_prep(p_l, tables, gqa_qnorm_g[l], gqa_knorm_g[l], tk)
        dq_c, dkt_c, dv_c, gq_c, gkt_c, gv_c = attn_prep(p_c, None, gqa_qnorm_g[l], gqa_knorm_g[l], n_ctx)
        lam_init = 0.8 - 0.6 * math.exp(-0.3 * l)
        lp = diff_lam[l].astype(F32)
        lam = jnp.exp(jnp.sum(lp[0] * lp[1])) - jnp.exp(jnp.sum(lp[2] * lp[3])) + lam_init
        yl_a = diff_attention(dq_l, dkt_c, dv_c, dkt_l, dv_l, lam, diff_norm_g[l], 1.0 - lam_init, 1024)
        yl_c = gqa_attention(gq_l, gkt_c, gv_c, gkt_l, gv_l, 1024)
        if need_ctx:
            yc_a = diff_attention(dq_c, dkt_c, dv_c, None, None, lam, diff_norm_g[l], 1.0 - lam_init, 256)
            yc_c = gqa_attention(gq_c, gkt_c, gv_c, None, None, 256)

        hs_c, hs_l = mlstm_branch(p_c, p_l, ml_conv_w[l], ml_conv_b[l], ml_gate_b[l], need_ctx)
        hy_args = (hy_conv_w[l], hy_conv_b[l], hy_f_w1[l], hy_f_b1[l], hy_f_w2[l], hy_f_b2[l],
                   hy_f_w3[l], hy_f_freq[l], hy_bias[l])
        yl_d = hyena(p_l, *hy_args, dft_tables)

        wb = w_branch[l].astype(BF16)
        wo = w_out[l].astype(BF16)
        xs = merge(yl_a, hs_l, yl_c, yl_d, p_l, ml_norm_g[l], wb, wo, xs, mod_l[2])
        route = (moe_w_group[l], moe_b_group[l], moe_w_router[l], moe_b_router[l])
        experts = (l, moe_w_gate, moe_w_up, moe_w_down)
        routed = [moe_router(xs, norm2_g[l], mod_l[3], mod_l[4], *route)]
        if need_ctx:
            yc_d = hyena(p_c, *hy_args, None)
            cs = merge(yc_a, hs_c, yc_c, yc_d, p_c, ml_norm_g[l], wb, wo, cs, mod_c[2])
            routed.insert(0, moe_router(cs, norm2_g[l], mod_c[3], mod_c[4], *route))
        h2 = jnp.concatenate([r[0].reshape(-1, d) for r in routed], axis=0)
        ids = jnp.concatenate([jnp.stack([r[1][:, :, k, :].reshape(-1) for k in range(MOE_TOP_K)], axis=1)
                               for r in routed], axis=0)
        yb, dest = _hier_moe(h2, ids, *experts)
        if need_ctx:
            nc = b * n_ctx
            cs = moe_combine(cs, yb, dest[:nc], routed[0][2], mod_c[5])
            dest = dest[nc:]
        xs = moe_combine(xs, yb, dest, routed[-1][2], mod_l[5])
    zero = jnp.zeros((b, d), F32)
    return norm_mod(xs, final_norm_g, zero, zero, F32)
```
